```python
import math
import jax
import jax.numpy as jnp
from jax import lax
import numpy as np

D_MODEL = 1024
BATCH = 4
SEQ = 4096
DEPTH = 1

HEAD_DIM = 64
MOBA_HEADS = 8
NSA_HEADS = 8
NSA_GROUPS = 2
NSA_HPG = NSA_HEADS // NSA_GROUPS
N_HEADS_TOTAL = MOBA_HEADS + NSA_HEADS
MOBA_WIDTH = MOBA_HEADS * HEAD_DIM
NSA_WIDTH = NSA_HEADS * HEAD_DIM
NSA_KV_WIDTH = NSA_GROUPS * HEAD_DIM
MOBA_BLOCK = 256
MOBA_TOPK = 3
CMP_LEN = 32
CMP_STRIDE = 16
CMP_HIDDEN = 256
SEL_BLOCK = 64
SEL_TOPN = 16
WINDOW = 512
NUM_BUCKETS = 32
MAX_DISTANCE = 1024
Q_CHUNK = 64
EPS = 1e-6
NEG = -1e30
BIG = 1e30
SPLIT_SIZES = (MOBA_WIDTH, MOBA_WIDTH, MOBA_WIDTH, MOBA_WIDTH, NSA_WIDTH, NSA_KV_WIDTH, NSA_KV_WIDTH, NSA_KV_WIDTH, NSA_KV_WIDTH, NSA_KV_WIDTH, NSA_KV_WIDTH, 3 * NSA_HEADS, NSA_WIDTH, 2 * D_MODEL)
IN_WIDTH = sum(SPLIT_SIZES)

kernel_name = 'hybrid_moba_nsa_gated_block'


def rms_norm(x, g):
    xf = x.astype(jnp.float32)
    y = xf * lax.rsqrt(jnp.mean(xf * xf, axis=-1, keepdims=True) + EPS)
    return (y * g.astype(jnp.float32)).astype(x.dtype)


def t5_bucket(dist):
    n = jnp.maximum(dist, 0)
    max_exact = NUM_BUCKETS // 2
    nf = jnp.maximum(n, max_exact).astype(jnp.float32)
    large = max_exact + (jnp.log(nf / max_exact) / math.log(MAX_DISTANCE / max_exact) * (NUM_BUCKETS - max_exact)).astype(jnp.int32)
    return jnp.where(n < max_exact, n, jnp.minimum(large, NUM_BUCKETS - 1))


def moba_mixer(q, k, v, bias_tab):
    B, S, H, Dh = q.shape
    scale = 1.0 / math.sqrt(Dh)
    nb = -(-S // MOBA_BLOCK)
    pad = nb * MOBA_BLOCK - S
    padw = ((0, 0), (0, pad), (0, 0), (0, 0))
    kb = jnp.pad(k, padw).reshape(B, nb, MOBA_BLOCK, H, Dh).transpose(0, 3, 1, 2, 4)
    vb = jnp.pad(v, padw).reshape(B, nb, MOBA_BLOCK, H, Dh).transpose(0, 3, 1, 2, 4)
    qh = q.transpose(0, 2, 1, 3)
    kmean = jnp.mean(kb.astype(jnp.float32), axis=3)
    score = jnp.einsum('bhsd,bhjd->bhsj', qh.astype(jnp.float32), kmean)
    qblk = jnp.arange(S)[:, None] // MOBA_BLOCK
    blk = jnp.arange(nb)[None, :]
    score = jnp.where(blk == qblk, BIG, jnp.where(blk < qblk, score, NEG))
    _, sel = lax.top_k(score, min(MOBA_TOPK + 1, nb))
    bi = jnp.arange(B)[:, None, None, None]
    hi = jnp.arange(H)[None, :, None, None]
    offs = jnp.arange(MOBA_BLOCK)

    def one_chunk(c):
        t0 = c * Q_CHUNK
        tq = t0 + jnp.arange(Q_CHUNK)
        qc = lax.dynamic_slice_in_dim(qh, t0, Q_CHUNK, axis=2)
        sc = lax.dynamic_slice_in_dim(sel, t0, Q_CHUNK, axis=2)
        kg = kb[bi, hi, sc]
        vg = vb[bi, hi, sc]
        logits = jnp.einsum('bhqd,bhqnkd->bhqnk', qc, kg).astype(jnp.float32) * scale
        dist = tq[:, None, None] - (sc[..., None] * MOBA_BLOCK + offs)
        bias = bias_tab[hi[..., None], t5_bucket(dist)]
        logits = jnp.where(dist >= 0, logits + bias, NEG)
        p = jax.nn.softmax(logits.reshape(B, H, Q_CHUNK, -1), axis=-1).astype(v.dtype)
        return jnp.einsum('bhqm,bhqmd->bhqd', p, vg.reshape(B, H, Q_CHUNK, -1, Dh))

    out = lax.map(one_chunk, jnp.arange(S // Q_CHUNK))
    return out.transpose(1, 0, 3, 2, 4).reshape(B, S, H * Dh)


def compress(kv, pos, w1, w2):
    B, S, G, Dh = kv.shape
    nc = (S - CMP_LEN) // CMP_STRIDE + 1
    idx = np.arange(nc)[:, None] * CMP_STRIDE + np.arange(CMP_LEN)[None, :]
    win = kv[:, idx] + pos[None, None, :, None, :]
    win = win.transpose(0, 1, 3, 2, 4).reshape(B, nc, G, CMP_LEN * Dh)
    return jax.nn.gelu(win @ w1) @ w2


def nsa_mixer(q, k_cmp, v_cmp, k_sel, v_sel, k_win, v_win, branch_gate, bias_tab):
    B, S, H, Dh = q.shape
    G = NSA_GROUPS
    HPG = H // G
    scale = 1.0 / math.sqrt(Dh)
    qg = q.reshape(B, S, G, HPG, Dh).transpose(0, 2, 3, 1, 4)
    nc = k_cmp.shape[1]
    tpos = jnp.arange(S)
    lc = jnp.einsum('bghsd,bcgd->bghsc', qg, k_cmp).astype(jnp.float32) * scale
    cend = jnp.arange(nc) * CMP_STRIDE + CMP_LEN - 1
    cvalid = cend[None, :] <= tpos[:, None]
    p_cmp = jax.nn.softmax(jnp.where(cvalid, lc, NEG), axis=-1) * cvalid
    o_cmp = jnp.einsum('bghsc,bcgd->bghsd', p_cmp.astype(v_cmp.dtype), v_cmp)
    nsel = S // SEL_BLOCK
    cs = np.arange(nc) * CMP_STRIDE
    bs = np.arange(nsel) * SEL_BLOCK
    ov = ((cs[None, :] < bs[:, None] + SEL_BLOCK) & (cs[None, :] + CMP_LEN > bs[:, None])).astype(np.float32)
    imp = jnp.einsum('bghsc,jc->bgsj', p_cmp, jnp.asarray(ov))
    qblk = tpos[:, None] // SEL_BLOCK
    blk = jnp.arange(nsel)[None, :]
    imp = jnp.where((blk == qblk) | (blk == 0), BIG, jnp.where(blk < qblk, imp, NEG))
    _, sel = lax.top_k(imp, min(SEL_TOPN, nsel))
    ksb = k_sel.reshape(B, nsel, SEL_BLOCK, G, Dh).transpose(0, 3, 1, 2, 4)
    vsb = v_sel.reshape(B, nsel, SEL_BLOCK, G, Dh).transpose(0, 3, 1, 2, 4)
    padw = ((0, 0), (WINDOW, 0), (0, 0), (0, 0))
    kwp = jnp.pad(k_win, padw).transpose(0, 2, 1, 3)
    vwp = jnp.pad(v_win, padw).transpose(0, 2, 1, 3)
    tab_g = bias_tab.reshape(G, HPG, NUM_BUCKETS)
    bi = jnp.arange(B)[:, None, None, None]
    gi = jnp.arange(G)[None, :, None, None]
    g6 = jnp.arange(G)[None, :, None, None, None, None]
    h6 = jnp.arange(HPG)[None, None, :, None, None, None]
    offs = jnp.arange(SEL_BLOCK)

    def one_chunk(c):
        t0 = c * Q_CHUNK
        tq = t0 + jnp.arange(Q_CHUNK)
        qc = lax.dynamic_slice_in_dim(qg, t0, Q_CHUNK, axis=3)
        sc = lax.dynamic_slice_in_dim(sel, t0, Q_CHUNK, axis=2)
        kg = ksb[bi, gi, sc]
        vg = vsb[bi, gi, sc]
        ls = jnp.einsum('bghqd,bgqnkd->bghqnk', qc, kg).astype(jnp.float32) * scale
        dist = tq[:, None, None] - (sc[..., None] * SEL_BLOCK + offs)
        bias = tab_g[g6, h6, t5_bucket(dist)[:, :, None]]
        ls = jnp.where(dist[:, :, None] >= 0, ls + bias, NEG)
        ps = jax.nn.softmax(ls.reshape(B, G, HPG, Q_CHUNK, -1), axis=-1).astype(v_sel.dtype)
        o_sel = jnp.einsum('bghqm,bgqmd->bghqd', ps, vg.reshape(B, G, Q_CHUNK, -1, Dh))
        kwc = lax.dynamic_slice_in_dim(kwp, t0, Q_CHUNK + WINDOW, axis=2)
        vwc = lax.dynamic_slice_in_dim(vwp, t0, Q_CHUNK + WINDOW, axis=2)
        kpos = t0 - WINDOW + jnp.arange(Q_CHUNK + WINDOW)
        dw = tq[:, None] - kpos[None, :]
        wvalid = (dw >= 0) & (dw < WINDOW) & (kpos[None, :] >= 0)
        lw = jnp.einsum('bghqd,bgkd->bghqk', qc, kwc).astype(jnp.float32) * scale + tab_g[:, :, t5_bucket(dw)]
        pw = jax.nn.softmax(jnp.where(wvalid, lw, NEG), axis=-1).astype(v_win.dtype)
        o_win = jnp.einsum('bghqk,bgkd->bghqd', pw, vwc)
        return o_sel, o_win

    o_sel, o_win = lax.map(one_chunk, jnp.arange(S // Q_CHUNK))
    o_sel = o_sel.transpose(1, 2, 3, 0, 4, 5).reshape(B, G, HPG, S, Dh)
    o_win = o_win.transpose(1, 2, 3, 0, 4, 5).reshape(B, G, HPG, S, Dh)
    g = jax.nn.sigmoid(branch_gate.astype(jnp.float32)).astype(q.dtype)
    g = g.reshape(B, S, G, HPG, 3).transpose(0, 2, 3, 1, 4)
    o = g[..., 0:1] * o_cmp + g[..., 1:2] * o_sel + g[..., 2:3] * o_win
    return o.transpose(0, 3, 1, 2, 4).reshape(B, S, H * Dh)


def setup_inputs(seed: int = 0) -> dict:
    key = jax.random.key(seed)
    ks = jax.random.split(key, 19)

    def normal(k, shape, s):
        return jax.random.normal(k, shape, jnp.float32) * s

    def gain(k, shape):
        return 1.0 + 0.1 * jax.random.normal(k, shape, jnp.float32)

    L = DEPTH
    flat = CMP_LEN * HEAD_DIM
    return {
        'x': normal(ks[0], (BATCH, SEQ, D_MODEL), 1.0),
        'norm_w': gain(ks[1], (L, D_MODEL)),
        'w_in': normal(ks[2], (L, D_MODEL, IN_WIDTH), D_MODEL ** -0.5),
        'q_norm_a': gain(ks[3], (L, HEAD_DIM)),
        'k_norm_a': gain(ks[4], (L, HEAD_DIM)),
        'q_norm_b': gain(ks[5], (L, HEAD_DIM)),
        'k_norm_cmp': gain(ks[6], (L, HEAD_DIM)),
        'k_norm_sel': gain(ks[7], (L, HEAD_DIM)),
        'k_norm_win': gain(ks[8], (L, HEAD_DIM)),
        'cmp_pos_k': normal(ks[9], (L, CMP_LEN, HEAD_DIM), 0.1),
        'cmp_w1_k': normal(ks[10], (L, flat, CMP_HIDDEN), flat ** -0.5),
        'cmp_w2_k': normal(ks[11], (L, CMP_HIDDEN, HEAD_DIM), CMP_HIDDEN ** -0.5),
        'cmp_pos_v': normal(ks[12], (L, CMP_LEN, HEAD_DIM), 0.1),
        'cmp_w1_v': normal(ks[13], (L, flat, CMP_HIDDEN), flat ** -0.5),
        'cmp_w2_v': normal(ks[14], (L, CMP_HIDDEN, HEAD_DIM), CMP_HIDDEN ** -0.5),
        'rel_bias': normal(ks[15], (NUM_BUCKETS, N_HEADS_TOTAL), 0.5),
        'w_branch_a': normal(ks[16], (L, MOBA_WIDTH, D_MODEL), MOBA_WIDTH ** -0.5),
        'w_branch_b': normal(ks[17], (L, NSA_WIDTH, D_MODEL), NSA_WIDTH ** -0.5),
        'w_out': normal(ks[18], (L, D_MODEL, D_MODEL), D_MODEL ** -0.5),
    }


def reference(x, norm_w, w_in, q_norm_a, k_norm_a, q_norm_b, k_norm_cmp, k_norm_sel, k_norm_win,
              cmp_pos_k, cmp_w1_k, cmp_w2_k, cmp_pos_v, cmp_w1_v, cmp_w2_v, rel_bias,
              w_branch_a, w_branch_b, w_out):
    B, S, _ = x.shape
    split_points = [int(p) for p in np.cumsum(SPLIT_SIZES)[:-1]]
    bias_tab = rel_bias.T
    bias_a = bias_tab[:MOBA_HEADS]
    bias_b = bias_tab[MOBA_HEADS:]

    def heads(t):
        return t.reshape(B, S, -1, HEAD_DIM)

    for l in range(DEPTH):
        h = rms_norm(x, norm_w[l])
        proj = h @ w_in[l]
        (q_a, k_a, v_a, z_a, q_b, kc, vc, ksl, vsl, kwn, vwn, gate_b, z_b, gate_m) = jnp.split(proj, split_points, axis=-1)
        o_a = moba_mixer(rms_norm(heads(q_a), q_norm_a[l]), rms_norm(heads(k_a), k_norm_a[l]), heads(v_a), bias_a)
        k_cmp = rms_norm(compress(heads(kc), cmp_pos_k[l], cmp_w1_k[l], cmp_w2_k[l]), k_norm_cmp[l])
        v_cmp = compress(heads(vc), cmp_pos_v[l], cmp_w1_v[l], cmp_w2_v[l])
        o_b = nsa_mixer(rms_norm(heads(q_b), q_norm_b[l]), k_cmp, v_cmp,
                        rms_norm(heads(ksl), k_norm_sel[l]), heads(vsl),
                        rms_norm(heads(kwn), k_norm_win[l]), heads(vwn),
                        gate_b.reshape(B, S, NSA_HEADS, 3), bias_b)
        y_a = o_a * jax.nn.silu(z_a)
        y_b = o_b * jax.nn.silu(z_b)
        g_m = jax.nn.sigmoid(gate_m)
        merged = g_m[..., :D_MODEL] * (y_a @ w_branch_a[l]) + g_m[..., D_MODEL:] * (y_b @ w_branch_b[l])
        x = x + merged @ w_out[l]
    return x
```

```python
import functools
import math

import jax
import jax.numpy as jnp
import numpy as np
from jax import lax
from jax.experimental import pallas as pl
from jax.experimental.pallas import tpu as pltpu

D_MODEL = 1024
HEAD_DIM = 64
MOBA_HEADS = 8
NSA_HEADS = 8
NSA_GROUPS = 2
NSA_HPG = NSA_HEADS // NSA_GROUPS
MOBA_BLOCK = 256
MOBA_TOPK = 3
CMP_LEN = 32
CMP_STRIDE = 16
CMP_HIDDEN = 256
SEL_BLOCK = 64
SEL_TOPN = 16
WINDOW = 512
NUM_BUCKETS = 32
MAX_DISTANCE = 1024
EPS = 1e-6
NEG = -1e30

LANES = 128
TQ = 256
N_FAR = 5
N_WIN_TILES = WINDOW // TQ + 1
VMEM_LIMIT = 56 * 1024 * 1024

_C_QA, _C_KA, _C_VA, _C_QB = 0, 512, 1024, 1536
_C_KC, _C_VC, _C_KSL, _C_VSL, _C_KWN, _C_VWN, _C_GATE = 2048, 2176, 2304, 2432, 2560, 2688, 2816
_N1 = 2944


def _nt(a, b):
    return lax.dot_general(a, b, (((1,), (1,)), ((), ())), preferred_element_type=jnp.float32)


def _nn(a, b):
    return jnp.dot(a, b, preferred_element_type=jnp.float32)


def _split(x):
    hi = x.astype(jnp.bfloat16)
    lo = (x - hi.astype(jnp.float32)).astype(jnp.bfloat16)
    return hi, lo


def _bucket_table(n):
    d = np.arange(n)
    nf = np.maximum(d, NUM_BUCKETS // 2).astype(np.float64)
    large = NUM_BUCKETS // 2 + (np.log(nf / (NUM_BUCKETS // 2)) / math.log(MAX_DISTANCE / (NUM_BUCKETS // 2))
                                * (NUM_BUCKETS - NUM_BUCKETS // 2)).astype(np.int64)
    return np.where(d < NUM_BUCKETS // 2, d, np.minimum(large, NUM_BUCKETS - 1))


_N_TILE_KINDS = (N_FAR + 1) + N_WIN_TILES


def _bias_onehot():
    oh = np.zeros((40, _N_TILE_KINDS * 2 * TQ), np.float32)
    buckets = _bucket_table((N_FAR + 2) * TQ + 1)
    for t in range(_N_TILE_KINDS):
        window = t > N_FAR
        delta = t - (N_FAR + 1) if window else t
        for u in range(2 * TQ):
            d = TQ * delta + TQ - u
            masked = d < 0 or (window and d >= WINDOW)
            oh[NUM_BUCKETS if masked else buckets[d], t * 2 * TQ + u] = 1.0
    return oh


def _bias_body(relt_ref, oh_ref, out_ref):
    g = jnp.dot(jnp.broadcast_to(relt_ref[0], (8, 40)), oh_ref[...], preferred_element_type=jnp.float32,
                precision=lax.Precision.HIGHEST)
    for t in range(_N_TILE_KINDS):
        full = jnp.broadcast_to(g[0:1, t * 2 * TQ:(t + 1) * 2 * TQ], (TQ, 2 * TQ))
        rolled = pltpu.roll(full, TQ, 1, stride=1, stride_axis=0)
        out_ref[0, t] = rolled[:, :TQ]


def _bias_tiles(rel_bias):
    nh = MOBA_HEADS + NSA_HEADS
    relt = jnp.concatenate([rel_bias.T.astype(jnp.float32), jnp.full((nh, 1), NEG, jnp.float32),
                            jnp.zeros((nh, 7), jnp.float32)], axis=1).reshape(nh, 1, 40)
    oh = jnp.asarray(_bias_onehot())
    return pl.pallas_call(
        _bias_body,
        grid=(nh,),
        in_specs=[pl.BlockSpec((1, 1, 40), lambda h: (h, 0, 0)), pl.BlockSpec(oh.shape, lambda h: (0, 0))],
        out_specs=pl.BlockSpec((1, _N_TILE_KINDS, TQ, TQ), lambda h: (h, 0, 0, 0)),
        out_shape=jax.ShapeDtypeStruct((nh, _N_TILE_KINDS, TQ, TQ), jnp.float32),
        compiler_params=pltpu.CompilerParams(dimension_semantics=("parallel",), vmem_limit_bytes=VMEM_LIMIT),
        name="bias_tiles",
    )(relt, oh)


def _seg_norm(x, bd, gain):
    hi, lo = _split(x * x)
    ss = _nn(hi, bd) + _nn(lo, bd)
    return x * lax.rsqrt(ss * (1.0 / HEAD_DIM) + EPS) * gain


def _in_proj_body(x_ref, nw_ref, w_ref, bd_ref, gain_ref,
                  qa_ref, ka_ref, va_ref, qb_ref, kmean_ref, kc_ref, vc_ref,
                  ksl_ref, vsl_ref, kwn_ref, vwn_ref, gate_ref):
    x = x_ref[0]
    ms = jnp.mean(x * x, axis=-1, keepdims=True)
    h = (x * lax.rsqrt(ms + EPS) * nw_ref[...]).astype(jnp.bfloat16)
    bd = bd_ref[...]
    lane = lax.broadcasted_iota(jnp.int32, (TQ, LANES), 1)
    lo_half = lane < HEAD_DIM

    def normed(col, width, scale=None):
        p = _nn(h, w_ref[:, col:col + width])
        outs = []
        for s in range(width // LANES):
            c = col + s * LANES
            y = _seg_norm(p[:, s * LANES:(s + 1) * LANES], bd, gain_ref[:, c:c + LANES])
            outs.append(y if scale is None else y * scale)
        return outs

    qa = normed(_C_QA, 512, 1.0 / math.sqrt(HEAD_DIM))
    for s in range(4):
        qa_ref[0, :, s * LANES:(s + 1) * LANES] = qa[s].astype(jnp.bfloat16)
    ka = normed(_C_KA, 512)
    for s in range(4):
        ka_ref[0, :, s * LANES:(s + 1) * LANES] = ka[s].astype(jnp.bfloat16)
        kmean_ref[0, 0, :, s * LANES:(s + 1) * LANES] = jnp.broadcast_to(
            jnp.mean(ka[s], axis=0, keepdims=True), (8, LANES))
    va_ref[0] = _nn(h, w_ref[:, _C_VA:_C_VA + 512]).astype(jnp.bfloat16)
    qb = normed(_C_QB, 512, 1.0 / math.sqrt(HEAD_DIM))
    for s in range(4):
        qb_ref[0, :, s * LANES:(s + 1) * LANES] = qb[s].astype(jnp.bfloat16)

    rest = _nn(h, w_ref[:, _C_KC:_N1])

    def slab(c):
        return rest[:, c - _C_KC:c - _C_KC + LANES]

    kc_ref[0] = slab(_C_KC).astype(jnp.bfloat16)
    vc_ref[0] = slab(_C_VC).astype(jnp.bfloat16)

    def dup_store(ref, y):
        r = pltpu.roll(y, HEAD_DIM, 1)
        ref[0, 0] = jnp.where(lo_half, y, r).astype(jnp.bfloat16)
        ref[0, 1] = jnp.where(lo_half, r, y).astype(jnp.bfloat16)

    dup_store(ksl_ref, _seg_norm(slab(_C_KSL), bd, gain_ref[:, _C_KSL:_C_KSL + LANES]))
    dup_store(vsl_ref, slab(_C_VSL))
    dup_store(kwn_ref, _seg_norm(slab(_C_KWN), bd, gain_ref[:, _C_KWN:_C_KWN + LANES]))
    dup_store(vwn_ref, slab(_C_VWN))
    gate_ref[0] = slab(_C_GATE)


def _in_proj(x, norm_w, w1, gain_row):
    B, S, _ = x.shape
    nt = S // TQ
    bd = np.kron(np.eye(2, dtype=np.float32), np.ones((HEAD_DIM, HEAD_DIM), np.float32))
    bf = jnp.bfloat16
    tok = lambda w, dt: jax.ShapeDtypeStruct((B, S, w), dt)
    grp = jax.ShapeDtypeStruct((B, NSA_GROUPS, S, LANES), bf)
    tok_spec = lambda w: pl.BlockSpec((1, TQ, w), lambda b, i: (b, i, 0))
    grp_spec = pl.BlockSpec((1, NSA_GROUPS, TQ, LANES), lambda b, i: (b, 0, i, 0))
    full = lambda a: pl.BlockSpec(a.shape, lambda b, i: (0,) * a.ndim)
    bd = jnp.asarray(bd, bf)
    nw = norm_w.reshape(1, D_MODEL)
    return pl.pallas_call(
        _in_proj_body,
        grid=(B, nt),
        in_specs=[tok_spec(D_MODEL), full(nw), full(w1), full(bd), full(gain_row)],
        out_specs=[tok_spec(512), tok_spec(512), tok_spec(512), tok_spec(512),
                   pl.BlockSpec((1, 1, 8, 512), lambda b, i: (b, i, 0, 0)),
                   tok_spec(LANES), tok_spec(LANES), grp_spec, grp_spec, grp_spec, grp_spec, tok_spec(LANES)],
        out_shape=[tok(512, bf), tok(512, bf), tok(512, bf), tok(512, bf),
                   jax.ShapeDtypeStruct((B, nt, 8, 512), jnp.float32),
                   tok(LANES, bf), tok(LANES, bf), grp, grp, grp, grp, tok(LANES, jnp.float32)],
        compiler_params=pltpu.CompilerParams(dimension_semantics=("parallel", "parallel"),
                                             vmem_limit_bytes=VMEM_LIMIT),
        name="in_proj",
    )(x, nw, w1, bd, gain_row)


def _gelu_tanh(x):
    return 0.5 * x * (1.0 + jnp.tanh(math.sqrt(2.0 / math.pi) * (x + 0.044715 * (x * x * x))))


def _compress_body(xk_ref, xv_ref, w1k_ref, w1v_ref, pk_ref, pv_ref, w1kn_ref, w1vn_ref,
                   w2k_ref, w2v_ref, gk_ref, kcmp_ref, vcmp_ref):
    nchunk = xk_ref.shape[1]

    def branch(x_ref, w1_ref, pos_ref, w1n_ref, w2_ref, gain):
        ab = _nn(x_ref[0], w1_ref[...])
        posw = _nn(jnp.broadcast_to(pos_ref[...], (8, CMP_LEN * HEAD_DIM)), w1n_ref[...])[0:1]
        outs = []
        for g in range(NSA_GROUPS):
            top = ab[:, (2 * g) * CMP_HIDDEN:(2 * g + 1) * CMP_HIDDEN]
            bot = ab[:, (2 * g + 1) * CMP_HIDDEN:(2 * g + 2) * CMP_HIDDEN]
            hid = top + pltpu.roll(bot, nchunk - 1, 0) + posw
            y = _nn(_gelu_tanh(hid).astype(jnp.bfloat16), w2_ref[...])
            if gain is not None:
                ms = jnp.sum(y * y, axis=-1, keepdims=True) * (0.5 / HEAD_DIM)
                y = y * lax.rsqrt(ms + EPS) * gain
            outs.append(y.astype(jnp.bfloat16))
        return outs

    k0, k1 = branch(xk_ref, w1k_ref, pk_ref, w1kn_ref, w2k_ref, gk_ref[...])
    kcmp_ref[0, 0], kcmp_ref[0, 1] = k0, k1
    v0, v1 = branch(xv_ref, w1v_ref, pv_ref, w1vn_ref, w2v_ref, None)
    vcmp_ref[0, 0], vcmp_ref[0, 1] = v0, v1


def _pack_w1(w1):
    half = CMP_STRIDE * HEAD_DIM
    cols = []
    for g in range(NSA_GROUPS):
        for part in range(2):
            w = w1[part * half:(part + 1) * half].reshape(CMP_STRIDE, HEAD_DIM, CMP_HIDDEN)
            z = jnp.zeros_like(w)
            pieces = [w, z] if g == 0 else [z, w]
            cols.append(jnp.concatenate(pieces, axis=1).reshape(CMP_STRIDE * LANES, CMP_HIDDEN))
    return jnp.concatenate(cols, axis=1).astype(jnp.bfloat16)


def _compress(kc, vc, pos_k, w1_k, w2_k, pos_v, w1_v, w2_v, gain_cmp):
    B, S, _ = kc.shape
    nchunk = S // CMP_STRIDE
    bf = jnp.bfloat16
    xk = kc.reshape(B, nchunk, CMP_STRIDE * LANES)
    xv = vc.reshape(B, nchunk, CMP_STRIDE * LANES)
    args = [xk, xv, _pack_w1(w1_k), _pack_w1(w1_v),
            pos_k.reshape(1, -1).astype(bf), pos_v.reshape(1, -1).astype(bf),
            w1_k.astype(bf), w1_v.astype(bf),
            jnp.concatenate([w2_k, w2_k], axis=1).astype(bf), jnp.concatenate([w2_v, w2_v], axis=1).astype(bf),
            jnp.concatenate([gain_cmp, gain_cmp]).reshape(1, LANES)]
    x_spec = pl.BlockSpec((1, nchunk, CMP_STRIDE * LANES), lambda b: (b, 0, 0))
    full = lambda a: pl.BlockSpec(a.shape, lambda b: (0,) * a.ndim)
    out = jax.ShapeDtypeStruct((B, NSA_GROUPS, nchunk, LANES), bf)
    o_spec = pl.BlockSpec((1, NSA_GROUPS, nchunk, LANES), lambda b: (b, 0, 0, 0))
    return pl.pallas_call(
        _compress_body,
        grid=(B,),
        in_specs=[x_spec, x_spec] + [full(a) for a in args[2:]],
        out_specs=[o_spec, o_spec],
        out_shape=[out, out],
        compiler_params=pltpu.CompilerParams(dimension_semantics=("parallel",), vmem_limit_bytes=VMEM_LIMIT),
        name="compress",
    )(*args)


def _softmax_step(s, v, m_ref, l_ref, acc_ref, first):
    rmax = jnp.max(s, axis=-1, keepdims=True)
    if first:
        m_new = jnp.broadcast_to(rmax, m_ref.shape)
        p = jnp.exp(s - rmax)
        l_ref[...] = p[:, :LANES] + p[:, LANES:]
        acc_ref[...] = _nn(p.astype(jnp.bfloat16), v)
    else:
        m_old = m_ref[...]
        m_new = jnp.maximum(m_old, rmax)
        alpha = jnp.exp(m_old - m_new)
        p = jnp.exp(s - m_new[:, 0:1])
        l_ref[...] = alpha * l_ref[...] + (p[:, :LANES] + p[:, LANES:])
        acc_ref[...] = alpha * acc_ref[...] + _nn(p.astype(jnp.bfloat16), v)
    m_ref[...] = m_new


def _finish(l_ref, acc_ref):
    return acc_ref[...] / jnp.sum(l_ref[...], axis=-1, keepdims=True)


def _transpose_flags(flag_t):
    r = lax.broadcasted_iota(jnp.int32, (TQ, TQ), 0)
    c = lax.broadcasted_iota(jnp.int32, (TQ, TQ), 1)
    eye = jnp.where(r == c, 1.0, 0.0).astype(jnp.bfloat16)
    return _nt(eye, flag_t.astype(jnp.bfloat16))


def _moba_body(q_ref, k_ref, v_ref, kmean_ref, bias_ref, o_ref, m_ref, l_ref, acc_ref):
    qi = pl.program_id(2)
    nblk = kmean_ref.shape[1]
    lane = lax.broadcasted_iota(jnp.int32, (TQ, LANES), 1)
    lane16 = lax.broadcasted_iota(jnp.int32, (nblk, LANES), 1)
    jrow = lax.broadcasted_iota(jnp.int32, (nblk, TQ), 0)
    q2 = q_ref[0]
    q2f = q2.astype(jnp.float32)
    kmean2 = kmean_ref[0]
    outs = []
    for hh in range(2):
        inhead = (lane < HEAD_DIM) if hh == 0 else (lane >= HEAD_DIM)
        inhead16 = (lane16 < HEAD_DIM) if hh == 0 else (lane16 >= HEAD_DIM)
        km_hi, km_lo = _split(jnp.where(inhead16, kmean2, 0.0))
        sc = _nt(km_hi, q2) + _nt(km_lo, q2)
        cnt = jnp.zeros((nblk, TQ), jnp.int32)
        for jp in range(nblk):
            row = sc[jp:jp + 1, :]
            ahead = (row > sc) | ((row == sc) & (jp < jrow))
            cnt = cnt + jnp.where(ahead & (jp < qi), 1, 0)
        keep = ((jrow < qi) & (cnt < MOBA_TOPK)) | (jrow == qi)
        flag = jnp.where(keep, 0.0, 1.0)
        zeros = lambda n: jnp.zeros((n, TQ), jnp.float32)
        if hh == 0:
            placed = jnp.concatenate([zeros(HEAD_DIM), flag, zeros(LANES - HEAD_DIM - nblk)], axis=0)
        else:
            placed = jnp.concatenate([flag, zeros(LANES - nblk)], axis=0)
        maskpart = _transpose_flags(placed) * NEG
        qaug = jnp.where(inhead, q2f, maskpart).astype(jnp.bfloat16)
        oh_base = HEAD_DIM if hh == 0 else 0

        def kaug(kt):
            k2 = k_ref[0, pl.ds(pl.multiple_of(kt * TQ, TQ), TQ), :].astype(jnp.float32)
            return jnp.where(inhead, k2, jnp.where(lane == oh_base + kt, 1.0, 0.0)).astype(jnp.bfloat16)

        def vtile(kt):
            return v_ref[0, pl.ds(pl.multiple_of(kt * TQ, TQ), TQ), :]

        s = _nt(qaug, kaug(qi)) + bias_ref[hh, 0]
        _softmax_step(s, vtile(qi), m_ref, l_ref, acc_ref, True)

        def body(kt, carry):
            s = _nt(qaug, kaug(kt)) + bias_ref[hh, jnp.minimum(qi - kt, N_FAR)]
            _softmax_step(s, vtile(kt), m_ref, l_ref, acc_ref, False)
            return carry

        lax.fori_loop(0, qi, body, 0)
        outs.append(_finish(l_ref, acc_ref))
    o_ref[0] = jnp.where(lane < HEAD_DIM, outs[0], outs[1])


def _moba(qa, ka, va, kmean, bias):
    B, S, _ = qa.shape
    nt = S // TQ
    npair = MOBA_HEADS // 2
    return pl.pallas_call(
        _moba_body,
        grid=(B, npair, nt),
        in_specs=[pl.BlockSpec((1, TQ, LANES), lambda b, p, i: (b, i, p)),
                  pl.BlockSpec((1, S, LANES), lambda b, p, i: (b, 0, p)),
                  pl.BlockSpec((1, S, LANES), lambda b, p, i: (b, 0, p)),
                  pl.BlockSpec((1, nt, LANES), lambda b, p, i: (b, 0, p)),
                  pl.BlockSpec((2, _N_TILE_KINDS, TQ, TQ), lambda b, p, i: (p, 0, 0, 0))],
        out_specs=pl.BlockSpec((1, TQ, LANES), lambda b, p, i: (b, i, p)),
        out_shape=jax.ShapeDtypeStruct((B, S, MOBA_HEADS * HEAD_DIM), jnp.float32),
        scratch_shapes=[pltpu.VMEM((TQ, LANES), jnp.float32)] * 3,
        compiler_params=pltpu.CompilerParams(dimension_semantics=("parallel", "parallel", "arbitrary"),
                                             vmem_limit_bytes=VMEM_LIMIT),
        name="moba",
    )(qa, ka, va, kmean, bias)


def _nsa_body(q_ref, kcmp_ref, vcmp_ref, ksl_ref, vsl_ref, kwn_ref, vwn_ref, gate_ref, ov_ref, koh_ref,
              gexp_ref, bias_ref, o_ref, imp_ref, m_ref, l_ref, acc_ref):
    qi = pl.program_id(2)
    ncmp = kcmp_ref.shape[2]
    nsel = ov_ref.shape[0]
    R = NSA_HPG * TQ
    lane = lax.broadcasted_iota(jnp.int32, (TQ, LANES), 1)
    lo_half = lane < HEAD_DIM
    qf = [q_ref[0, :, 0:LANES].astype(jnp.float32), q_ref[0, :, LANES:2 * LANES].astype(jnp.float32)]
    qmask = [jnp.where(lo_half if hh % 2 == 0 else ~lo_half, qf[hh // 2], 0.0).astype(jnp.bfloat16)
             for hh in range(NSA_HPG)]
    q4 = jnp.concatenate(qmask, axis=0)

    lc = _nt(q4, kcmp_ref[0, 0])
    tpos = qi * TQ + lax.broadcasted_iota(jnp.int32, (TQ, ncmp), 0)
    cidx = lax.broadcasted_iota(jnp.int32, (TQ, ncmp), 1)
    valid1 = (cidx * CMP_STRIDE + (CMP_LEN - 1) <= tpos) & (cidx < ncmp - 1)
    valid = jnp.concatenate([valid1] * NSA_HPG, axis=0)
    mx = jnp.max(jnp.where(valid, lc, NEG), axis=-1, keepdims=True)
    e = jnp.where(valid, jnp.exp(lc - mx), 0.0)
    den = jnp.sum(e, axis=-1, keepdims=True)
    pc = e * (1.0 / jnp.where(den > 0.0, den, 1.0))
    o_cmp = _nn(pc.astype(jnp.bfloat16), vcmp_ref[0, 0])
    psum = pc[0:TQ] + pc[TQ:2 * TQ] + pc[2 * TQ:3 * TQ] + pc[3 * TQ:4 * TQ]
    p_hi, p_lo = _split(psum)
    ov = ov_ref[...]
    imp_ref[...] = _nt(ov, p_hi) + _nt(ov, p_lo)

    imp = imp_ref[...]
    jrow = lax.broadcasted_iota(jnp.int32, (nsel, TQ), 0)
    qblk = (qi * TQ + lax.broadcasted_iota(jnp.int32, (nsel, TQ), 1)) // SEL_BLOCK
    cand = (jrow >= 1) & (jrow < qblk)

    def rank_body(jp, cnt):
        row = imp_ref[pl.ds(jp, 1), :]
        ahead = (row > imp) | ((row == imp) & (jp < jrow))
        return cnt + jnp.where(ahead & (jp < qblk), 1, 0)

    cnt = lax.fori_loop(1, qi * (TQ // SEL_BLOCK) + (TQ // SEL_BLOCK - 1), rank_body,
                        jnp.zeros((nsel, TQ), jnp.int32))
    keep = (jrow == 0) | (jrow == qblk) | (cand & (cnt < SEL_TOPN - 2))
    flag = jnp.where(keep, 0.0, 1.0)
    if nsel < HEAD_DIM:
        flag = jnp.concatenate([flag, jnp.zeros((HEAD_DIM - nsel, TQ), jnp.float32)], axis=0)
    maskpart = _transpose_flags(jnp.concatenate([flag, flag], axis=0)) * NEG

    def ktile(ref, kt):
        return ref[0, 0, pl.ds(pl.multiple_of(kt * TQ, TQ), TQ), :]

    o_sel = []
    for par in range(2):
        inhead = lo_half if par == 0 else ~lo_half
        qa2 = jnp.concatenate([jnp.where(inhead, qf[s], maskpart).astype(jnp.bfloat16) for s in range(2)], axis=0)

        def sel_scores(kt, dd):
            kaug = jnp.where(inhead, ktile(ksl_ref, kt).astype(jnp.float32),
                             ktile(koh_ref, kt).astype(jnp.float32)).astype(jnp.bfloat16)
            s = _nt(qa2, kaug)
            b = jnp.concatenate([bias_ref[par, dd], bias_ref[par + 2, dd]], axis=0)
            return s + b

        ms, ls, accs = m_ref.at[0:2 * TQ], l_ref.at[0:2 * TQ], acc_ref.at[0:2 * TQ]
        _softmax_step(sel_scores(qi, 0), ktile(vsl_ref, qi), ms, ls, accs, True)

        def body(kt, carry):
            _softmax_step(sel_scores(kt, jnp.minimum(qi - kt, N_FAR)), ktile(vsl_ref, kt), ms, ls, accs, False)
            return carry

        lax.fori_loop(0, qi, body, 0)
        o_sel.append(_finish(ls, accs))

    def win_scores(kt, dd):
        s = _nt(q4, ktile(kwn_ref, kt))
        b = jnp.concatenate([bias_ref[hh, N_FAR + 1 + dd] for hh in range(NSA_HPG)], axis=0)
        return s + b

    _softmax_step(win_scores(qi, 0), ktile(vwn_ref, qi), m_ref, l_ref, acc_ref, True)

    def wbody(kt, carry):
        _softmax_step(win_scores(kt, qi - kt), ktile(vwn_ref, kt), m_ref, l_ref, acc_ref, False)
        return carry

    lax.fori_loop(jnp.maximum(qi - (N_WIN_TILES - 1), 0), qi, wbody, 0)
    o_win = _finish(l_ref, acc_ref)

    g = jax.nn.sigmoid(gate_ref[0])
    g_hi, g_lo = _split(g)
    gx = _nn(g_hi, gexp_ref[0]) + _nn(g_lo, gexp_ref[0])
    for s in range(2):
        h_even, h_odd = 2 * s, 2 * s + 1
        cmp2 = jnp.where(lo_half, o_cmp[h_even * TQ:(h_even + 1) * TQ], o_cmp[h_odd * TQ:(h_odd + 1) * TQ])
        sel2 = jnp.where(lo_half, o_sel[0][s * TQ:(s + 1) * TQ], o_sel[1][s * TQ:(s + 1) * TQ])
        win2 = jnp.where(lo_half, o_win[h_even * TQ:(h_even + 1) * TQ], o_win[h_odd * TQ:(h_odd + 1) * TQ])
        base = s * 3 * LANES
        o_ref[0, :, s * LANES:(s + 1) * LANES] = (gx[:, base:base + LANES] * cmp2
                                                  + gx[:, base + LANES:base + 2 * LANES] * sel2
                                                  + gx[:, base + 2 * LANES:base + 3 * LANES] * win2)


def _gate_expand():
    e = np.zeros((NSA_GROUPS, LANES, 2 * 3 * LANES), np.float32)
    for g in range(NSA_GROUPS):
        for s in range(2):
            for c in range(3):
                for half in range(2):
                    h = g * NSA_HPG + 2 * s + half
                    col = (s * 3 + c) * LANES + half * HEAD_DIM
                    e[g, 3 * h + c, col:col + HEAD_DIM] = 1.0
    return e


def _nsa(qb, kcmp, vcmp, ksl, vsl, kwn, vwn, gate, bias):
    B, S, _ = qb.shape
    nt = S // TQ
    ncmp = kcmp.shape[2]
    nsel = S // SEL_BLOCK
    bf = jnp.bfloat16
    cs = np.arange(ncmp) * CMP_STRIDE
    bs = np.arange(nsel) * SEL_BLOCK
    ov = ((cs[None, :] < bs[:, None] + SEL_BLOCK) & (cs[None, :] + CMP_LEN > bs[:, None])
          & (np.arange(ncmp)[None, :] < ncmp - 1)).astype(np.float32)
    koh = (np.arange(S)[:, None] // SEL_BLOCK == np.arange(LANES)[None, :] % HEAD_DIM).astype(np.float32)
    koh = jnp.asarray(koh, bf).reshape(1, 1, S, LANES)
    grp = lambda n: pl.BlockSpec((1, 1, n, LANES), lambda b, g, i: (b, g, 0, 0))
    return pl.pallas_call(
        _nsa_body,
        grid=(B, NSA_GROUPS, nt),
        in_specs=[pl.BlockSpec((1, TQ, 2 * LANES), lambda b, g, i: (b, i, g)),
                  grp(ncmp), grp(ncmp), grp(S), grp(S), grp(S), grp(S),
                  pl.BlockSpec((1, TQ, LANES), lambda b, g, i: (b, i, 0)),
                  pl.BlockSpec((nsel, ncmp), lambda b, g, i: (0, 0)),
                  pl.BlockSpec((1, 1, S, LANES), lambda b, g, i: (0, 0, 0, 0)),
                  pl.BlockSpec((1, LANES, 6 * LANES), lambda b, g, i: (g, 0, 0)),
                  pl.BlockSpec((NSA_HPG, _N_TILE_KINDS, TQ, TQ), lambda b, g, i: (2 + g, 0, 0, 0))],
        out_specs=pl.BlockSpec((1, TQ, 2 * LANES), lambda b, g, i: (b, i, g)),
        out_shape=jax.ShapeDtypeStruct((B, S, NSA_HEADS * HEAD_DIM), jnp.float32),
        scratch_shapes=[pltpu.VMEM((nsel, TQ), jnp.float32)] + [pltpu.VMEM((NSA_HPG * TQ, LANES), jnp.float32)] * 3,
        compiler_params=pltpu.CompilerParams(dimension_semantics=("parallel", "parallel", "arbitrary"),
                                             vmem_limit_bytes=VMEM_LIMIT),
        name="nsa",
    )(qb, kcmp, vcmp, ksl, vsl, kwn, vwn, gate, jnp.asarray(ov, bf), koh,
      jnp.asarray(_gate_expand(), bf), bias)


def _out_proj_body(x_ref, nw_ref, oa_ref, ob_ref, wz_ref, wa_ref, wb_ref, wo_ref, out_ref):
    x = x_ref[0]
    ms = jnp.mean(x * x, axis=-1, keepdims=True)
    h = (x * lax.rsqrt(ms + EPS) * nw_ref[...]).astype(jnp.bfloat16)
    z = _nn(h, wz_ref[:, 0:1024])
    ya = (oa_ref[0] * jax.nn.silu(z[:, 0:512])).astype(jnp.bfloat16)
    yb = (ob_ref[0] * jax.nn.silu(z[:, 512:1024])).astype(jnp.bfloat16)
    gm = jax.nn.sigmoid(_nn(h, wz_ref[:, 1024:3072]))
    merged = gm[:, 0:D_MODEL] * _nn(ya, wa_ref[...]) + gm[:, D_MODEL:] * _nn(yb, wb_ref[...])
    out_ref[0] = x + _nn(merged.astype(jnp.bfloat16), wo_ref[...])


def _out_proj(x, norm_w, oa, ob, wz, wa, wb, wo):
    B, S, _ = x.shape
    nw = norm_w.reshape(1, D_MODEL)
    tok = lambda w: pl.BlockSpec((1, TQ, w), lambda b, i: (b, i, 0))
    full = lambda a: pl.BlockSpec(a.shape, lambda b, i: (0,) * a.ndim)
    return pl.pallas_call(
        _out_proj_body,
        grid=(B, S // TQ),
        in_specs=[tok(D_MODEL), full(nw), tok(512), tok(512), full(wz), full(wa), full(wb), full(wo)],
        out_specs=tok(D_MODEL),
        out_shape=jax.ShapeDtypeStruct((B, S, D_MODEL), jnp.float32),
        compiler_params=pltpu.CompilerParams(dimension_semantics=("parallel", "parallel"),
                                             vmem_limit_bytes=VMEM_LIMIT),
        name="out_proj",
    )(x, nw, oa, ob, wz, wa, wb, wo)


def _pack_in_weights(w_in, q_norm_a, k_norm_a, q_norm_b, k_norm_sel, k_norm_win):
    bf = jnp.bfloat16
    w1 = jnp.concatenate([w_in[:, 0:1536], w_in[:, 2048:2560], w_in[:, 2560:3328],
                          w_in[:, 3328:3352], jnp.zeros((D_MODEL, LANES - 3 * NSA_HEADS), w_in.dtype)], axis=1)
    wz = jnp.concatenate([w_in[:, 1536:2048], w_in[:, 3352:3864], w_in[:, 3864:5912]], axis=1)
    ones = lambda n: jnp.ones((n,), jnp.float32)
    gain = jnp.concatenate([jnp.tile(q_norm_a, MOBA_HEADS), jnp.tile(k_norm_a, MOBA_HEADS), ones(512),
                            jnp.tile(q_norm_b, NSA_HEADS), ones(256), jnp.tile(k_norm_sel, NSA_GROUPS), ones(128),
                            jnp.tile(k_norm_win, NSA_GROUPS), ones(256)]).reshape(1, _N1)
    return w1.astype(bf), wz.astype(bf), gain


def kernel(x, norm_w, w_in, q_norm_a, k_norm_a, q_norm_b, k_norm_cmp, k_norm_sel, k_norm_win, cmp_pos_k, cmp_w1_k, cmp_w2_k, cmp_pos_v, cmp_w1_v, cmp_w2_v, rel_bias, w_branch_a, w_branch_b, w_out):
    bf = jnp.bfloat16
    w1, wz, gain = _pack_in_weights(w_in[0], q_norm_a[0], k_norm_a[0], q_norm_b[0], k_norm_sel[0], k_norm_win[0])
    bias = _bias_tiles(rel_bias)
    qa, ka, va, qb, kmean, kc, vc, ksl, vsl, kwn, vwn, gate = _in_proj(x, norm_w[0], w1, gain)
    kcmp, vcmp = _compress(kc, vc, cmp_pos_k[0], cmp_w1_k[0], cmp_w2_k[0],
                           cmp_pos_v[0], cmp_w1_v[0], cmp_w2_v[0], k_norm_cmp[0])
    oa = _moba(qa, ka, va, kmean[:, :, 0, :], bias)
    ob = _nsa(qb, kcmp, vcmp, ksl, vsl, kwn, vwn, gate, bias)
    return _out_proj(x, norm_w[0], oa, ob, wz, w_branch_a[0].astype(bf), w_branch_b[0].astype(bf),
                     w_out[0].astype(bf))
```

```python
import math

import jax
import jax.numpy as jnp
import numpy as np
from jax import lax
from jax.experimental import pallas as pl
from jax.experimental.pallas import tpu as pltpu

D_MODEL = 1024
HEAD_DIM = 64
MOBA_HEADS = 8
NSA_HEADS = 8
NSA_GROUPS = 2
NSA_HPG = NSA_HEADS // NSA_GROUPS
MOBA_BLOCK = 256
MOBA_TOPK = 3
CMP_LEN = 32
CMP_STRIDE = 16
CMP_HIDDEN = 256
SEL_BLOCK = 64
SEL_TOPN = 16
WINDOW = 512
NUM_BUCKETS = 32
MAX_DISTANCE = 1024
EPS = 1e-6
NEG = -1e30

LANES = 128
TQ = 256
TK = 2 * TQ
N_FAR = 5
N_WIN_TILES = WINDOW // TQ + 1
VMEM_LIMIT = 56 * 1024 * 1024
LOG2E = math.log2(math.e)
QSCALE = LOG2E / math.sqrt(HEAD_DIM)

_K_WIN = N_FAR + 1
_K_MASKED = _K_WIN + N_WIN_TILES
_N_TILE_KINDS = _K_MASKED + 1

_C_QA, _C_KA, _C_VA, _C_QB = 0, 512, 1024, 1536
_C_KC, _C_VC, _C_KSL, _C_VSL, _C_KWN, _C_VWN, _C_GATE = 2048, 2176, 2304, 2432, 2560, 2688, 2816
_N1 = 2944


def _nt(a, b):
    return lax.dot_general(a, b, (((1,), (1,)), ((), ())), preferred_element_type=jnp.float32)


def _nn(a, b):
    return jnp.dot(a, b, preferred_element_type=jnp.float32)


def _split(x):
    hi = x.astype(jnp.bfloat16)
    lo = (x - hi.astype(jnp.float32)).astype(jnp.bfloat16)
    return hi, lo


def _bucket_table(n):
    d = np.arange(n)
    nf = np.maximum(d, NUM_BUCKETS // 2).astype(np.float64)
    large = NUM_BUCKETS // 2 + (np.log(nf / (NUM_BUCKETS // 2)) / math.log(MAX_DISTANCE / (NUM_BUCKETS // 2))
                                * (NUM_BUCKETS - NUM_BUCKETS // 2)).astype(np.int64)
    return np.where(d < NUM_BUCKETS // 2, d, np.minimum(large, NUM_BUCKETS - 1))


def _bias_onehot():
    oh = np.zeros((40, _N_TILE_KINDS * 2 * TQ), np.float32)
    buckets = _bucket_table((N_FAR + 2) * TQ + 1)
    for t in range(_N_TILE_KINDS):
        window = _K_WIN <= t < _K_MASKED
        delta = t - _K_WIN if window else t
        for u in range(2 * TQ):
            d = TQ * delta + TQ - u
            masked = t == _K_MASKED or d < 0 or (window and d >= WINDOW)
            oh[NUM_BUCKETS if masked else buckets[d], t * 2 * TQ + u] = 1.0
    return oh


def _bias_body(relt_ref, oh_ref, out_ref):
    g = jnp.dot(jnp.broadcast_to(relt_ref[0], (8, 40)), oh_ref[...], preferred_element_type=jnp.float32,
                precision=lax.Precision.HIGHEST)
    for t in range(_N_TILE_KINDS):
        full = jnp.broadcast_to(g[0:1, t * 2 * TQ:(t + 1) * 2 * TQ], (TQ, 2 * TQ))
        rolled = pltpu.roll(full, TQ, 1, stride=1, stride_axis=0)
        out_ref[0, t] = rolled[:, :TQ]


def _bias_tiles(rel_bias):
    nh = MOBA_HEADS + NSA_HEADS
    relt = jnp.concatenate([rel_bias.T.astype(jnp.float32) * LOG2E, jnp.full((nh, 1), NEG, jnp.float32),
                            jnp.zeros((nh, 7), jnp.float32)], axis=1).reshape(nh, 1, 40)
    oh = jnp.asarray(_bias_onehot())
    return pl.pallas_call(
        _bias_body,
        grid=(nh,),
        in_specs=[pl.BlockSpec((1, 1, 40), lambda h: (h, 0, 0)), pl.BlockSpec(oh.shape, lambda h: (0, 0))],
        out_specs=pl.BlockSpec((1, _N_TILE_KINDS, TQ, TQ), lambda h: (h, 0, 0, 0)),
        out_shape=jax.ShapeDtypeStruct((nh, _N_TILE_KINDS, TQ, TQ), jnp.float32),
        compiler_params=pltpu.CompilerParams(dimension_semantics=("parallel",), vmem_limit_bytes=VMEM_LIMIT),
        name="bias_tiles",
    )(relt, oh)


def _seg_norm(x, bd, gain):
    hi, lo = _split(x * x)
    ss = _nn(hi, bd) + _nn(lo, bd)
    return x * lax.rsqrt(ss * (1.0 / HEAD_DIM) + EPS) * gain


def _in_proj_body(x_ref, nw_ref, w_ref, bd_ref, gain_ref,
                  qa_ref, kae_ref, kao_ref, va_ref, qb_ref, kmean_ref, kc_ref, vc_ref,
                  kse_ref, kso_ref, vsl_ref, kwn_ref, vwn_ref, gate_ref):
    i = pl.program_id(1)
    x = x_ref[0]
    ms = jnp.mean(x * x, axis=-1, keepdims=True)
    h = (x * lax.rsqrt(ms + EPS) * nw_ref[...]).astype(jnp.bfloat16)
    bd = bd_ref[...]
    bf = jnp.bfloat16
    lane = lax.broadcasted_iota(jnp.int32, (TQ, LANES), 1)
    row = lax.broadcasted_iota(jnp.int32, (TQ, LANES), 0)
    lo_half = lane < HEAD_DIM

    def normed(col, width, scale=None):
        p = _nn(h, w_ref[:, col:col + width])
        outs = []
        for s in range(width // LANES):
            c = col + s * LANES
            y = _seg_norm(p[:, s * LANES:(s + 1) * LANES], bd, gain_ref[:, c:c + LANES])
            outs.append(y if scale is None else y * scale)
        return outs

    qa = normed(_C_QA, 512, QSCALE)
    for s in range(4):
        qa_ref[0, :, s * LANES:(s + 1) * LANES] = qa[s].astype(bf)
    ka = normed(_C_KA, 512)
    oh_hi = jnp.where(lane == HEAD_DIM + i, 1.0, 0.0)
    oh_lo = jnp.where(lane == i, 1.0, 0.0)
    for s in range(4):
        kae_ref[0, :, s * LANES:(s + 1) * LANES] = jnp.where(lo_half, ka[s], oh_hi).astype(bf)
        kao_ref[0, :, s * LANES:(s + 1) * LANES] = jnp.where(lo_half, oh_lo, ka[s]).astype(bf)
        kmean_ref[0, 0, :, s * LANES:(s + 1) * LANES] = jnp.broadcast_to(
            jnp.mean(ka[s], axis=0, keepdims=True), (8, LANES))
    va_ref[0] = _nn(h, w_ref[:, _C_VA:_C_VA + 512]).astype(bf)
    qb = normed(_C_QB, 512, QSCALE)
    for s in range(4):
        qb_ref[0, :, s * LANES:(s + 1) * LANES] = qb[s].astype(bf)

    rest = _nn(h, w_ref[:, _C_KC:_N1])

    def slab(c):
        return rest[:, c - _C_KC:c - _C_KC + LANES]

    kc_ref[0] = slab(_C_KC).astype(bf)
    vc_ref[0] = slab(_C_VC).astype(bf)

    def dup_store(ref, y):
        r = pltpu.roll(y, HEAD_DIM, 1)
        ref[0, 0] = jnp.where(lo_half, y, r).astype(bf)
        ref[0, 1] = jnp.where(lo_half, r, y).astype(bf)

    y = _seg_norm(slab(_C_KSL), bd, gain_ref[:, _C_KSL:_C_KSL + LANES])
    r = pltpu.roll(y, HEAD_DIM, 1)
    oh = jnp.where((lane & (HEAD_DIM - 1)) == i * (TQ // SEL_BLOCK) + row // SEL_BLOCK, 1.0, 0.0)
    kse_ref[0, 0] = jnp.where(lo_half, y, oh).astype(bf)
    kso_ref[0, 0] = jnp.where(lo_half, oh, r).astype(bf)
    kse_ref[0, 1] = jnp.where(lo_half, r, oh).astype(bf)
    kso_ref[0, 1] = jnp.where(lo_half, oh, y).astype(bf)
    dup_store(vsl_ref, slab(_C_VSL))
    dup_store(kwn_ref, _seg_norm(slab(_C_KWN), bd, gain_ref[:, _C_KWN:_C_KWN + LANES]))
    dup_store(vwn_ref, slab(_C_VWN))
    gate_ref[0] = slab(_C_GATE)


def _in_proj(x, norm_w, w1, gain_row):
    B, S, _ = x.shape
    nt = S // TQ
    bf = jnp.bfloat16
    bd = jnp.asarray(np.kron(np.eye(2, dtype=np.float32), np.ones((HEAD_DIM, HEAD_DIM), np.float32)), bf)
    tok = lambda w, dt: jax.ShapeDtypeStruct((B, S, w), dt)
    grp = jax.ShapeDtypeStruct((B, NSA_GROUPS, S, LANES), bf)
    tok_spec = lambda w: pl.BlockSpec((1, TQ, w), lambda b, i: (b, i, 0))
    grp_spec = pl.BlockSpec((1, NSA_GROUPS, TQ, LANES), lambda b, i: (b, 0, i, 0))
    full = lambda a: pl.BlockSpec(a.shape, lambda b, i: (0,) * a.ndim)
    nw = norm_w.reshape(1, D_MODEL)
    return pl.pallas_call(
        _in_proj_body,
        grid=(B, nt),
        in_specs=[tok_spec(D_MODEL), full(nw), full(w1), full(bd), full(gain_row)],
        out_specs=[tok_spec(512), tok_spec(512), tok_spec(512), tok_spec(512), tok_spec(512),
                   pl.BlockSpec((1, 1, 8, 512), lambda b, i: (b, i, 0, 0)),
                   tok_spec(LANES), tok_spec(LANES),
                   grp_spec, grp_spec, grp_spec, grp_spec, grp_spec, tok_spec(LANES)],
        out_shape=[tok(512, bf), tok(512, bf), tok(512, bf), tok(512, bf), tok(512, bf),
                   jax.ShapeDtypeStruct((B, nt, 8, 512), jnp.float32),
                   tok(LANES, bf), tok(LANES, bf), grp, grp, grp, grp, grp, tok(LANES, jnp.float32)],
        compiler_params=pltpu.CompilerParams(dimension_semantics=("parallel", "parallel"),
                                             vmem_limit_bytes=VMEM_LIMIT),
        name="in_proj",
    )(x, nw, w1, bd, gain_row)


def _gelu_tanh(x):
    return 0.5 * x * (1.0 + jnp.tanh(math.sqrt(2.0 / math.pi) * (x + 0.044715 * (x * x * x))))


def _compress_body(xk_ref, xv_ref, w1k_ref, w1v_ref, pk_ref, pv_ref, w1kn_ref, w1vn_ref,
                   w2k_ref, w2v_ref, gk_ref, kcmp_ref, vcmp_ref):
    nchunk = xk_ref.shape[1]

    def branch(x_ref, w1_ref, pos_ref, w1n_ref, w2_ref, gain):
        ab = _nn(x_ref[0], w1_ref[...])
        posw = _nn(jnp.broadcast_to(pos_ref[...], (8, CMP_LEN * HEAD_DIM)), w1n_ref[...])[0:1]
        outs = []
        for g in range(NSA_GROUPS):
            top = ab[:, (2 * g) * CMP_HIDDEN:(2 * g + 1) * CMP_HIDDEN]
            bot = ab[:, (2 * g + 1) * CMP_HIDDEN:(2 * g + 2) * CMP_HIDDEN]
            hid = top + pltpu.roll(bot, nchunk - 1, 0) + posw
            y = _nn(_gelu_tanh(hid).astype(jnp.bfloat16), w2_ref[...])
            if gain is not None:
                ms = jnp.sum(y * y, axis=-1, keepdims=True) * (0.5 / HEAD_DIM)
                y = y * lax.rsqrt(ms + EPS) * gain
            outs.append(y.astype(jnp.bfloat16))
        return outs

    k0, k1 = branch(xk_ref, w1k_ref, pk_ref, w1kn_ref, w2k_ref, gk_ref[...])
    kcmp_ref[0, 0], kcmp_ref[0, 1] = k0, k1
    v0, v1 = branch(xv_ref, w1v_ref, pv_ref, w1vn_ref, w2v_ref, None)
    vcmp_ref[0, 0], vcmp_ref[0, 1] = v0, v1


def _pack_w1(w1):
    half = CMP_STRIDE * HEAD_DIM
    cols = []
    for g in range(NSA_GROUPS):
        for part in range(2):
            w = w1[part * half:(part + 1) * half].reshape(CMP_STRIDE, HEAD_DIM, CMP_HIDDEN)
            z = jnp.zeros_like(w)
            pieces = [w, z] if g == 0 else [z, w]
            cols.append(jnp.concatenate(pieces, axis=1).reshape(CMP_STRIDE * LANES, CMP_HIDDEN))
    return jnp.concatenate(cols, axis=1).astype(jnp.bfloat16)


def _compress(kc, vc, pos_k, w1_k, w2_k, pos_v, w1_v, w2_v, gain_cmp):
    B, S, _ = kc.shape
    nchunk = S // CMP_STRIDE
    bf = jnp.bfloat16
    xk = kc.reshape(B, nchunk, CMP_STRIDE * LANES)
    xv = vc.reshape(B, nchunk, CMP_STRIDE * LANES)
    args = [xk, xv, _pack_w1(w1_k), _pack_w1(w1_v),
            pos_k.reshape(1, -1).astype(bf), pos_v.reshape(1, -1).astype(bf),
            w1_k.astype(bf), w1_v.astype(bf),
            jnp.concatenate([w2_k, w2_k], axis=1).astype(bf), jnp.concatenate([w2_v, w2_v], axis=1).astype(bf),
            jnp.concatenate([gain_cmp, gain_cmp]).reshape(1, LANES)]
    x_spec = pl.BlockSpec((1, nchunk, CMP_STRIDE * LANES), lambda b: (b, 0, 0))
    full = lambda a: pl.BlockSpec(a.shape, lambda b: (0,) * a.ndim)
    out = jax.ShapeDtypeStruct((B, NSA_GROUPS, nchunk, LANES), bf)
    o_spec = pl.BlockSpec((1, NSA_GROUPS, nchunk, LANES), lambda b: (b, 0, 0, 0))
    return pl.pallas_call(
        _compress_body,
        grid=(B,),
        in_specs=[x_spec, x_spec] + [full(a) for a in args[2:]],
        out_specs=[o_spec, o_spec],
        out_shape=[out, out],
        compiler_params=pltpu.CompilerParams(dimension_semantics=("parallel",), vmem_limit_bytes=VMEM_LIMIT),
        name="compress",
    )(*args)


def _lane_partial_sum(p):
    t = p[:, 0:LANES]
    for c in range(1, p.shape[1] // LANES):
        t = t + p[:, c * LANES:(c + 1) * LANES]
    return t


def _softmax_step(s, v, m_ref, l_ref, acc_ref, first):
    rmax = jnp.max(s, axis=-1, keepdims=True)
    if first:
        m_new = jnp.broadcast_to(rmax, m_ref.shape)
        p = jnp.exp2(s - rmax)
        l_ref[...] = _lane_partial_sum(p)
        acc_ref[...] = _nn(p.astype(jnp.bfloat16), v)
    else:
        m_old = m_ref[...]
        m_new = jnp.maximum(m_old, rmax)
        alpha = jnp.exp2(m_old - m_new)
        p = jnp.exp2(s - m_new[:, 0:1])
        l_ref[...] = alpha * l_ref[...] + _lane_partial_sum(p)
        acc_ref[...] = alpha * acc_ref[...] + _nn(p.astype(jnp.bfloat16), v)
    m_ref[...] = m_new


def _finish(l_ref, acc_ref):
    return acc_ref[...] / jnp.sum(l_ref[...], axis=-1, keepdims=True)


def _chunk_kinds(qi, c):
    d0 = qi - 2 * c
    kind0 = jnp.minimum(d0, N_FAR)
    kind1 = jnp.where(d0 >= 1, jnp.minimum(d0 - 1, N_FAR), _K_MASKED)
    return kind0, kind1


def _moba_body(q_ref, ke_ref, ko_ref, v_ref, kmean_ref, eye_ref, bias_ref, o_ref, m_ref, l_ref, acc_ref):
    qi = pl.program_id(2)
    nblk = kmean_ref.shape[1]
    lane = lax.broadcasted_iota(jnp.int32, (TQ, LANES), 1)
    lo_half = lane < HEAD_DIM
    lane16 = lax.broadcasted_iota(jnp.int32, (nblk, LANES), 1)
    jrow = lax.broadcasted_iota(jnp.int32, (nblk, TQ), 0)
    q2 = q_ref[0]
    kmean2 = kmean_ref[0]
    flags = []
    for hh in range(2):
        inhead16 = (lane16 < HEAD_DIM) if hh == 0 else (lane16 >= HEAD_DIM)
        km_hi, km_lo = _split(jnp.where(inhead16, kmean2, 0.0))
        sc = _nt(km_hi, q2) + _nt(km_lo, q2)
        cnt = jnp.zeros((nblk, TQ), jnp.int32)
        for jp in range(nblk):
            r = sc[jp:jp + 1, :]
            ahead = (r > sc) | ((r == sc) & (jp < jrow))
            cnt = cnt + jnp.where(ahead & (jp < qi), 1, 0)
        keep = ((jrow < qi) & (cnt < MOBA_TOPK)) | (jrow == qi)
        flags.append(jnp.where(keep, 0.0, 1.0))
    pad = jnp.zeros((HEAD_DIM - nblk, TQ), jnp.float32)
    placed = jnp.concatenate([flags[1], pad, flags[0], pad], axis=0).astype(jnp.bfloat16)
    maskpart = _nt(eye_ref[...], placed) * NEG
    q2f = q2.astype(jnp.float32)
    qaug = [jnp.where(lo_half, q2f, maskpart).astype(jnp.bfloat16),
            jnp.where(lo_half, maskpart, q2f).astype(jnp.bfloat16)]
    k_refs = [ke_ref, ko_ref]

    def step(c, first):
        start = pl.multiple_of(c * TK, TK)
        v = v_ref[0, pl.ds(start, TK), :]
        kind0, kind1 = _chunk_kinds(qi, c)
        for hh in range(2):
            s = _nt(qaug[hh], k_refs[hh][0, pl.ds(start, TK), :])
            s = s + jnp.concatenate([bias_ref[hh, kind0], bias_ref[hh, kind1]], axis=1)
            _softmax_step(s, v, m_ref.at[hh], l_ref.at[hh], acc_ref.at[hh], first)

    last = lax.shift_right_logical(qi, 1)
    step(last, True)

    def body(c, carry):
        step(c, False)
        return carry

    lax.fori_loop(0, last, body, 0)
    o_ref[0] = jnp.where(lo_half, _finish(l_ref.at[0], acc_ref.at[0]), _finish(l_ref.at[1], acc_ref.at[1]))


def _eye():
    return jnp.asarray(np.eye(TQ, dtype=np.float32), jnp.bfloat16)


def _moba(qa, kae, kao, va, kmean, bias):
    B, S, _ = qa.shape
    nt = S // TQ
    npair = MOBA_HEADS // 2
    kv_spec = pl.BlockSpec((1, S, LANES), lambda b, p, i: (b, 0, p))
    return pl.pallas_call(
        _moba_body,
        grid=(B, npair, nt),
        in_specs=[pl.BlockSpec((1, TQ, LANES), lambda b, p, i: (b, i, p)),
                  kv_spec, kv_spec, kv_spec,
                  pl.BlockSpec((1, nt, LANES), lambda b, p, i: (b, 0, p)),
                  pl.BlockSpec((TQ, TQ), lambda b, p, i: (0, 0)),
                  pl.BlockSpec((2, _N_TILE_KINDS, TQ, TQ), lambda b, p, i: (p, 0, 0, 0))],
        out_specs=pl.BlockSpec((1, TQ, LANES), lambda b, p, i: (b, i, p)),
        out_shape=jax.ShapeDtypeStruct((B, S, MOBA_HEADS * HEAD_DIM), jnp.float32),
        scratch_shapes=[pltpu.VMEM((2, TQ, LANES), jnp.float32)] * 3,
        compiler_params=pltpu.CompilerParams(dimension_semantics=("parallel", "parallel", "arbitrary"),
                                             vmem_limit_bytes=VMEM_LIMIT),
        name="moba",
    )(qa, kae, kao, va, kmean, _eye(), bias)


def _nsa_body(q_ref, kcmp_ref, vcmp_ref, kse_ref, kso_ref, vsl_ref, kwn_ref, vwn_ref, gate_ref, ov_ref, eye_ref,
              gexp_ref, bias_ref, o_ref, imp_ref, m_ref, l_ref, acc_ref):
    qi = pl.program_id(2)
    ncmp = kcmp_ref.shape[2]
    nsel = ov_ref.shape[0]
    bf = jnp.bfloat16
    lane = lax.broadcasted_iota(jnp.int32, (TQ, LANES), 1)
    lo_half = lane < HEAD_DIM
    qf = [q_ref[0, :, 0:LANES].astype(jnp.float32), q_ref[0, :, LANES:2 * LANES].astype(jnp.float32)]
    q4 = jnp.concatenate([jnp.where(lo_half if hh % 2 == 0 else ~lo_half, qf[hh // 2], 0.0).astype(bf)
                          for hh in range(NSA_HPG)], axis=0)

    lc = _nt(q4, kcmp_ref[0, 0])
    tpos = qi * TQ + lax.broadcasted_iota(jnp.int32, (TQ, ncmp), 0)
    cidx = lax.broadcasted_iota(jnp.int32, (TQ, ncmp), 1)
    valid1 = (cidx * CMP_STRIDE + (CMP_LEN - 1) <= tpos) & (cidx < ncmp - 1)
    valid = jnp.concatenate([valid1] * NSA_HPG, axis=0)
    mx = jnp.max(jnp.where(valid, lc, NEG), axis=-1, keepdims=True)
    e = jnp.where(valid, jnp.exp2(lc - mx), 0.0)
    den = jnp.sum(e, axis=-1, keepdims=True)
    pc = e * (1.0 / jnp.where(den > 0.0, den, 1.0))
    o_cmp = _nn(pc.astype(bf), vcmp_ref[0, 0])
    psum = pc[0:TQ] + pc[TQ:2 * TQ] + pc[2 * TQ:3 * TQ] + pc[3 * TQ:4 * TQ]
    p_hi, p_lo = _split(psum)
    ov = ov_ref[...]
    imp_ref[...] = _nt(ov, p_hi) + _nt(ov, p_lo)

    imp = imp_ref[...]
    jrow = lax.broadcasted_iota(jnp.int32, (nsel, TQ), 0)
    qblk = (qi * TQ + lax.broadcasted_iota(jnp.int32, (nsel, TQ), 1)) // SEL_BLOCK
    cand = (jrow >= 1) & (jrow < qblk)

    def rank_body(jp, cnt):
        r = imp_ref[pl.ds(jp, 1), :]
        ahead = (r > imp) | ((r == imp) & (jp < jrow))
        return cnt + jnp.where(ahead & (jp < qblk), 1, 0)

    cnt = lax.fori_loop(1, qi * (TQ // SEL_BLOCK) + (TQ // SEL_BLOCK - 1), rank_body,
                        jnp.zeros((nsel, TQ), jnp.int32))
    keep = (jrow == 0) | (jrow == qblk) | (cand & (cnt < SEL_TOPN - 2))
    flag = jnp.where(keep, 0.0, 1.0)
    if nsel < HEAD_DIM:
        flag = jnp.concatenate([flag, jnp.zeros((HEAD_DIM - nsel, TQ), jnp.float32)], axis=0)
    maskpart = _nt(eye_ref[...], jnp.concatenate([flag, flag], axis=0).astype(bf)) * NEG

    qa2 = [jnp.concatenate([jnp.where(lo_half if par == 0 else ~lo_half, qf[s], maskpart).astype(bf)
                            for s in range(2)], axis=0) for par in range(2)]
    ks_refs = [kse_ref, kso_ref]

    def sel_step(c, first):
        start = pl.multiple_of(c * TK, TK)
        v = vsl_ref[0, 0, pl.ds(start, TK), :]
        kind0, kind1 = _chunk_kinds(qi, c)
        for par in range(2):
            s = _nt(qa2[par], ks_refs[par][0, 0, pl.ds(start, TK), :])
            b = jnp.concatenate(
                [jnp.concatenate([bias_ref[h, kind0], bias_ref[h, kind1]], axis=1) for h in (par, par + 2)], axis=0)
            _softmax_step(s + b, v, m_ref.at[par], l_ref.at[par], acc_ref.at[par], first)

    last = lax.shift_right_logical(qi, 1)
    sel_step(last, True)

    def sel_body(c, carry):
        sel_step(c, False)
        return carry

    lax.fori_loop(0, last, sel_body, 0)
    o_sel = [_finish(l_ref.at[par], acc_ref.at[par]) for par in range(2)]

    t0 = jnp.maximum(qi - (N_WIN_TILES - 1), 0)
    wstart = pl.multiple_of(t0 * TQ, TQ)
    sw = _nt(q4, kwn_ref[0, 0, pl.ds(wstart, N_WIN_TILES * TQ), :])
    kinds = []
    for r in range(N_WIN_TILES):
        dd = qi - (t0 + r)
        kinds.append(jnp.where(dd >= 0, _K_WIN + dd, _K_MASKED))
    bw = jnp.concatenate([jnp.concatenate([bias_ref[hh, k] for k in kinds], axis=1) for hh in range(NSA_HPG)], axis=0)
    sw = sw + bw
    pw = jnp.exp2(sw - jnp.max(sw, axis=-1, keepdims=True))
    o_win = _nn(pw.astype(bf), vwn_ref[0, 0, pl.ds(wstart, N_WIN_TILES * TQ), :]) \
        / jnp.sum(pw, axis=-1, keepdims=True)

    g = jax.nn.sigmoid(gate_ref[0])
    g_hi, g_lo = _split(g)
    gx = _nn(g_hi, gexp_ref[0]) + _nn(g_lo, gexp_ref[0])
    for s in range(2):
        h_even, h_odd = 2 * s, 2 * s + 1
        cmp2 = jnp.where(lo_half, o_cmp[h_even * TQ:(h_even + 1) * TQ], o_cmp[h_odd * TQ:(h_odd + 1) * TQ])
        sel2 = jnp.where(lo_half, o_sel[0][s * TQ:(s + 1) * TQ], o_sel[1][s * TQ:(s + 1) * TQ])
        win2 = jnp.where(lo_half, o_win[h_even * TQ:(h_even + 1) * TQ], o_win[h_odd * TQ:(h_odd + 1) * TQ])
        base = s * 3 * LANES
        o_ref[0, :, s * LANES:(s + 1) * LANES] = (gx[:, base:base + LANES] * cmp2
                                                  + gx[:, base + LANES:base + 2 * LANES] * sel2
                                                  + gx[:, base + 2 * LANES:base + 3 * LANES] * win2)


def _gate_expand():
    e = np.zeros((NSA_GROUPS, LANES, 2 * 3 * LANES), np.float32)
    for g in range(NSA_GROUPS):
        for s in range(2):
            for c in range(3):
                for half in range(2):
                    h = g * NSA_HPG + 2 * s + half
                    col = (s * 3 + c) * LANES + half * HEAD_DIM
                    e[g, 3 * h + c, col:col + HEAD_DIM] = 1.0
    return e


def _nsa(qb, kcmp, vcmp, kse, kso, vsl, kwn, vwn, gate, bias):
    B, S, _ = qb.shape
    nt = S // TQ
    ncmp = kcmp.shape[2]
    nsel = S // SEL_BLOCK
    bf = jnp.bfloat16
    cs = np.arange(ncmp) * CMP_STRIDE
    bs = np.arange(nsel) * SEL_BLOCK
    ov = ((cs[None, :] < bs[:, None] + SEL_BLOCK) & (cs[None, :] + CMP_LEN > bs[:, None])
          & (np.arange(ncmp)[None, :] < ncmp - 1)).astype(np.float32)
    grp = lambda n: pl.BlockSpec((1, 1, n, LANES), lambda b, g, i: (b, g, 0, 0))
    return pl.pallas_call(
        _nsa_body,
        grid=(B, NSA_GROUPS, nt),
        in_specs=[pl.BlockSpec((1, TQ, 2 * LANES), lambda b, g, i: (b, i, g)),
                  grp(ncmp), grp(ncmp), grp(S), grp(S), grp(S), grp(S), grp(S),
                  pl.BlockSpec((1, TQ, LANES), lambda b, g, i: (b, i, 0)),
                  pl.BlockSpec((nsel, ncmp), lambda b, g, i: (0, 0)),
                  pl.BlockSpec((TQ, TQ), lambda b, g, i: (0, 0)),
                  pl.BlockSpec((1, LANES, 6 * LANES), lambda b, g, i: (g, 0, 0)),
                  pl.BlockSpec((NSA_HPG, _N_TILE_KINDS, TQ, TQ), lambda b, g, i: (2 + g, 0, 0, 0))],
        out_specs=pl.BlockSpec((1, TQ, 2 * LANES), lambda b, g, i: (b, i, g)),
        out_shape=jax.ShapeDtypeStruct((B, S, NSA_HEADS * HEAD_DIM), jnp.float32),
        scratch_shapes=[pltpu.VMEM((nsel, TQ), jnp.float32)] + [pltpu.VMEM((2, 2 * TQ, LANES), jnp.float32)] * 3,
        compiler_params=pltpu.CompilerParams(dimension_semantics=("parallel", "parallel", "arbitrary"),
                                             vmem_limit_bytes=VMEM_LIMIT),
        name="nsa",
    )(qb, kcmp, vcmp, kse, kso, vsl, kwn, vwn, gate, jnp.asarray(ov, bf), _eye(),
      jnp.asarray(_gate_expand(), bf), bias)


def _out_proj_body(x_ref, nw_ref, oa_ref, ob_ref, wz_ref, wa_ref, wb_ref, wo_ref, out_ref):
    x = x_ref[0]
    ms = jnp.mean(x * x, axis=-1, keepdims=True)
    h = (x * lax.rsqrt(ms + EPS) * nw_ref[...]).astype(jnp.bfloat16)
    z = _nn(h, wz_ref[:, 0:1024])
    ya = (oa_ref[0] * jax.nn.silu(z[:, 0:512])).astype(jnp.bfloat16)
    yb = (ob_ref[0] * jax.nn.silu(z[:, 512:1024])).astype(jnp.bfloat16)
    gm = jax.nn.sigmoid(_nn(h, wz_ref[:, 1024:3072]))
    merged = gm[:, 0:D_MODEL] * _nn(ya, wa_ref[...]) + gm[:, D_MODEL:] * _nn(yb, wb_ref[...])
    out_ref[0] = x + _nn(merged.astype(jnp.bfloat16), wo_ref[...])


def _out_proj(x, norm_w, oa, ob, wz, wa, wb, wo):
    B, S, _ = x.shape
    nw = norm_w.reshape(1, D_MODEL)
    tok = lambda w: pl.BlockSpec((1, TQ, w), lambda b, i: (b, i, 0))
    full = lambda a: pl.BlockSpec(a.shape, lambda b, i: (0,) * a.ndim)
    return pl.pallas_call(
        _out_proj_body,
        grid=(B, S // TQ),
        in_specs=[tok(D_MODEL), full(nw), tok(512), tok(512), full(wz), full(wa), full(wb), full(wo)],
        out_specs=tok(D_MODEL),
        out_shape=jax.ShapeDtypeStruct((B, S, D_MODEL), jnp.float32),
        compiler_params=pltpu.CompilerParams(dimension_semantics=("parallel", "parallel"),
                                             vmem_limit_bytes=VMEM_LIMIT),
        name="out_proj",
    )(x, nw, oa, ob, wz, wa, wb, wo)


def _pack_in_weights(w_in, q_norm_a, k_norm_a, q_norm_b, k_norm_sel, k_norm_win):
    bf = jnp.bfloat16
    w1 = jnp.concatenate([w_in[:, 0:1536], w_in[:, 2048:2560], w_in[:, 2560:3328],
                          w_in[:, 3328:3352], jnp.zeros((D_MODEL, LANES - 3 * NSA_HEADS), w_in.dtype)], axis=1)
    wz = jnp.concatenate([w_in[:, 1536:2048], w_in[:, 3352:3864], w_in[:, 3864:5912]], axis=1)
    ones = lambda n: jnp.ones((n,), jnp.float32)
    gain = jnp.concatenate([jnp.tile(q_norm_a, MOBA_HEADS), jnp.tile(k_norm_a, MOBA_HEADS), ones(512),
                            jnp.tile(q_norm_b, NSA_HEADS), ones(256), jnp.tile(k_norm_sel, NSA_GROUPS), ones(128),
                            jnp.tile(k_norm_win, NSA_GROUPS), ones(256)]).reshape(1, _N1)
    return w1.astype(bf), wz.astype(bf), gain


def kernel(x, norm_w, w_in, q_norm_a, k_norm_a, q_norm_b, k_norm_cmp, k_norm_sel, k_norm_win, cmp_pos_k, cmp_w1_k, cmp_w2_k, cmp_pos_v, cmp_w1_v, cmp_w2_v, rel_bias, w_branch_a, w_branch_b, w_out):
    bf = jnp.bfloat16
    w1, wz, gain = _pack_in_weights(w_in[0], q_norm_a[0], k_norm_a[0], q_norm_b[0], k_norm_sel[0], k_norm_win[0])
    bias = _bias_tiles(rel_bias)
    qa, kae, kao, va, qb, kmean, kc, vc, kse, kso, vsl, kwn, vwn, gate = _in_proj(x, norm_w[0], w1, gain)
    kcmp, vcmp = _compress(kc, vc, cmp_pos_k[0], cmp_w1_k[0], cmp_w2_k[0],
                           cmp_pos_v[0], cmp_w1_v[0], cmp_w2_v[0], k_norm_cmp[0])
    oa = _moba(qa, kae, kao, va, kmean[:, :, 0, :], bias)
    ob = _nsa(qb, kcmp, vcmp, kse, kso, vsl, kwn, vwn, gate, bias)
    return _out_proj(x, norm_w[0], oa, ob, wz, w_branch_a[0].astype(bf), w_branch_b[0].astype(bf),
                     w_out[0].astype(bf))
```

```python
import math

import jax
import jax.numpy as jnp
import numpy as np
from jax import lax
from jax.experimental import pallas as pl
from jax.experimental.pallas import tpu as pltpu

D_MODEL = 1024
HEAD_DIM = 64
MOBA_HEADS = 8
NSA_HEADS = 8
NSA_GROUPS = 2
NSA_HPG = NSA_HEADS // NSA_GROUPS
MOBA_BLOCK = 256
MOBA_TOPK = 3
CMP_LEN = 32
CMP_STRIDE = 16
CMP_HIDDEN = 256
SEL_BLOCK = 64
SEL_TOPN = 16
WINDOW = 512
NUM_BUCKETS = 32
MAX_DISTANCE = 1024
EPS = 1e-6
NEG = -1e30

LANES = 128
TQ = 256
TK = 2 * TQ
N_FAR = 5
N_WIN_TILES = WINDOW // TQ + 1
VMEM_LIMIT = 56 * 1024 * 1024
LOG2E = math.log2(math.e)
QSCALE = LOG2E / math.sqrt(HEAD_DIM)

_K_WIN = N_FAR + 1
_K_MASKED = _K_WIN + N_WIN_TILES
_N_TILE_KINDS = _K_MASKED + 1

_C_QA, _C_KA, _C_VA, _C_QB = 0, 512, 1024, 1536
_C_KC, _C_VC, _C_KSL, _C_VSL, _C_KWN, _C_VWN, _C_GATE = 2048, 2176, 2304, 2432, 2560, 2688, 2816
_N1 = 2944


def _nt(a, b):
    return lax.dot_general(a, b, (((1,), (1,)), ((), ())), preferred_element_type=jnp.float32)


def _nn(a, b):
    return jnp.dot(a, b, preferred_element_type=jnp.float32)


def _split(x):
    hi = x.astype(jnp.bfloat16)
    lo = (x - hi.astype(jnp.float32)).astype(jnp.bfloat16)
    return hi, lo


def _bucket_table(n):
    d = np.arange(n)
    nf = np.maximum(d, NUM_BUCKETS // 2).astype(np.float64)
    large = NUM_BUCKETS // 2 + (np.log(nf / (NUM_BUCKETS // 2)) / math.log(MAX_DISTANCE / (NUM_BUCKETS // 2))
                                * (NUM_BUCKETS - NUM_BUCKETS // 2)).astype(np.int64)
    return np.where(d < NUM_BUCKETS // 2, d, np.minimum(large, NUM_BUCKETS - 1))


def _bias_onehot():
    oh = np.zeros((40, _N_TILE_KINDS * 2 * TQ), np.float32)
    buckets = _bucket_table((N_FAR + 2) * TQ + 1)
    for t in range(_N_TILE_KINDS):
        window = _K_WIN <= t < _K_MASKED
        delta = t - _K_WIN if window else t
        for u in range(2 * TQ):
            d = TQ * delta + TQ - u
            masked = t == _K_MASKED or d < 0 or (window and d >= WINDOW)
            oh[NUM_BUCKETS if masked else buckets[d], t * 2 * TQ + u] = 1.0
    return oh


def _bias_body(relt_ref, oh_ref, out_ref):
    g = jnp.dot(jnp.broadcast_to(relt_ref[0], (8, 40)), oh_ref[...], preferred_element_type=jnp.float32,
                precision=lax.Precision.HIGHEST)
    for t in range(_N_TILE_KINDS):
        full = jnp.broadcast_to(g[0:1, t * 2 * TQ:(t + 1) * 2 * TQ], (TQ, 2 * TQ))
        rolled = pltpu.roll(full, TQ, 1, stride=1, stride_axis=0)
        out_ref[0, t] = rolled[:, :TQ]


def _bias_tiles(rel_bias):
    nh = MOBA_HEADS + NSA_HEADS
    relt = jnp.concatenate([rel_bias.T.astype(jnp.float32) * LOG2E, jnp.full((nh, 1), NEG, jnp.float32),
                            jnp.zeros((nh, 7), jnp.float32)], axis=1).reshape(nh, 1, 40)
    oh = jnp.asarray(_bias_onehot())
    return pl.pallas_call(
        _bias_body,
        grid=(nh,),
        in_specs=[pl.BlockSpec((1, 1, 40), lambda h: (h, 0, 0)), pl.BlockSpec(oh.shape, lambda h: (0, 0))],
        out_specs=pl.BlockSpec((1, _N_TILE_KINDS, TQ, TQ), lambda h: (h, 0, 0, 0)),
        out_shape=jax.ShapeDtypeStruct((nh, _N_TILE_KINDS, TQ, TQ), jnp.float32),
        compiler_params=pltpu.CompilerParams(dimension_semantics=("parallel",), vmem_limit_bytes=VMEM_LIMIT),
        name="bias_tiles",
    )(relt, oh)


def _seg_norm(x, bd, gain):
    hi, lo = _split(x * x)
    ss = _nn(hi, bd) + _nn(lo, bd)
    return x * lax.rsqrt(ss * (1.0 / HEAD_DIM) + EPS) * gain


def _in_proj_body(x_ref, nw_ref, w_ref, bd_ref, gain_ref,
                  qa_ref, kae_ref, kao_ref, va_ref, qb_ref, kmean_ref, kc_ref, vc_ref,
                  kse_ref, kso_ref, vsl_ref, kwn_ref, vwn_ref, gate_ref):
    i = pl.program_id(1)
    x = x_ref[0]
    ms = jnp.mean(x * x, axis=-1, keepdims=True)
    h = (x * lax.rsqrt(ms + EPS) * nw_ref[...]).astype(jnp.bfloat16)
    bd = bd_ref[...]
    bf = jnp.bfloat16
    lane = lax.broadcasted_iota(jnp.int32, (TQ, LANES), 1)
    row = lax.broadcasted_iota(jnp.int32, (TQ, LANES), 0)
    lo_half = lane < HEAD_DIM

    def normed(col, width, scale=None):
        p = _nn(h, w_ref[:, col:col + width])
        outs = []
        for s in range(width // LANES):
            c = col + s * LANES
            y = _seg_norm(p[:, s * LANES:(s + 1) * LANES], bd, gain_ref[:, c:c + LANES])
            outs.append(y if scale is None else y * scale)
        return outs

    qa = normed(_C_QA, 512, QSCALE)
    for s in range(4):
        qa_ref[0, :, s * LANES:(s + 1) * LANES] = qa[s].astype(bf)
    ka = normed(_C_KA, 512)
    oh_hi = jnp.where(lane == HEAD_DIM + i, 1.0, 0.0)
    oh_lo = jnp.where(lane == i, 1.0, 0.0)
    for s in range(4):
        kae_ref[0, :, s * LANES:(s + 1) * LANES] = jnp.where(lo_half, ka[s], oh_hi).astype(bf)
        kao_ref[0, :, s * LANES:(s + 1) * LANES] = jnp.where(lo_half, oh_lo, ka[s]).astype(bf)
        kmean_ref[0, 0, :, s * LANES:(s + 1) * LANES] = jnp.broadcast_to(
            jnp.mean(ka[s], axis=0, keepdims=True), (8, LANES))
    va_ref[0] = _nn(h, w_ref[:, _C_VA:_C_VA + 512]).astype(bf)
    qb = normed(_C_QB, 512, QSCALE)
    for s in range(4):
        qb_ref[0, :, s * LANES:(s + 1) * LANES] = qb[s].astype(bf)

    rest = _nn(h, w_ref[:, _C_KC:_N1])

    def slab(c):
        return rest[:, c - _C_KC:c - _C_KC + LANES]

    kc_ref[0] = slab(_C_KC).astype(bf)
    vc_ref[0] = slab(_C_VC).astype(bf)

    def dup_store(ref, y):
        r = pltpu.roll(y, HEAD_DIM, 1)
        ref[0, 0] = jnp.where(lo_half, y, r).astype(bf)
        ref[0, 1] = jnp.where(lo_half, r, y).astype(bf)

    y = _seg_norm(slab(_C_KSL), bd, gain_ref[:, _C_KSL:_C_KSL + LANES])
    r = pltpu.roll(y, HEAD_DIM, 1)
    oh = jnp.where((lane & (HEAD_DIM - 1)) == i * (TQ // SEL_BLOCK) + row // SEL_BLOCK, 1.0, 0.0)
    kse_ref[0, 0] = jnp.where(lo_half, y, oh).astype(bf)
    kso_ref[0, 0] = jnp.where(lo_half, oh, r).astype(bf)
    kse_ref[0, 1] = jnp.where(lo_half, r, oh).astype(bf)
    kso_ref[0, 1] = jnp.where(lo_half, oh, y).astype(bf)
    dup_store(vsl_ref, slab(_C_VSL))
    dup_store(kwn_ref, _seg_norm(slab(_C_KWN), bd, gain_ref[:, _C_KWN:_C_KWN + LANES]))
    dup_store(vwn_ref, slab(_C_VWN))
    gate_ref[0] = slab(_C_GATE)


def _in_proj(x, norm_w, w1, gain_row):
    B, S, _ = x.shape
    nt = S // TQ
    bf = jnp.bfloat16
    bd = jnp.asarray(np.kron(np.eye(2, dtype=np.float32), np.ones((HEAD_DIM, HEAD_DIM), np.float32)), bf)
    tok = lambda w, dt: jax.ShapeDtypeStruct((B, S, w), dt)
    grp = jax.ShapeDtypeStruct((B, NSA_GROUPS, S, LANES), bf)
    tok_spec = lambda w: pl.BlockSpec((1, TQ, w), lambda b, i: (b, i, 0))
    grp_spec = pl.BlockSpec((1, NSA_GROUPS, TQ, LANES), lambda b, i: (b, 0, i, 0))
    full = lambda a: pl.BlockSpec(a.shape, lambda b, i: (0,) * a.ndim)
    nw = norm_w.reshape(1, D_MODEL)
    return pl.pallas_call(
        _in_proj_body,
        grid=(B, nt),
        in_specs=[tok_spec(D_MODEL), full(nw), full(w1), full(bd), full(gain_row)],
        out_specs=[tok_spec(512), tok_spec(512), tok_spec(512), tok_spec(512), tok_spec(512),
                   pl.BlockSpec((1, 1, 8, 512), lambda b, i: (b, i, 0, 0)),
                   tok_spec(LANES), tok_spec(LANES),
                   grp_spec, grp_spec, grp_spec, grp_spec, grp_spec, tok_spec(LANES)],
        out_shape=[tok(512, bf), tok(512, bf), tok(512, bf), tok(512, bf), tok(512, bf),
                   jax.ShapeDtypeStruct((B, nt, 8, 512), jnp.float32),
                   tok(LANES, bf), tok(LANES, bf), grp, grp, grp, grp, grp, tok(LANES, jnp.float32)],
        compiler_params=pltpu.CompilerParams(dimension_semantics=("parallel", "parallel"),
                                             vmem_limit_bytes=VMEM_LIMIT),
        name="in_proj",
    )(x, nw, w1, bd, gain_row)


def _gelu_tanh(x):
    return 0.5 * x * (1.0 + jnp.tanh(math.sqrt(2.0 / math.pi) * (x + 0.044715 * (x * x * x))))


def _compress_body(xk_ref, xv_ref, w1k_ref, w1v_ref, pk_ref, pv_ref, w1kn_ref, w1vn_ref,
                   w2k_ref, w2v_ref, gk_ref, kcmp_ref, vcmp_ref):
    nchunk = xk_ref.shape[1]

    def branch(x_ref, w1_ref, pos_ref, w1n_ref, w2_ref, gain):
        ab = _nn(x_ref[0], w1_ref[...])
        posw = _nn(jnp.broadcast_to(pos_ref[...], (8, CMP_LEN * HEAD_DIM)), w1n_ref[...])[0:1]
        outs = []
        for g in range(NSA_GROUPS):
            top = ab[:, (2 * g) * CMP_HIDDEN:(2 * g + 1) * CMP_HIDDEN]
            bot = ab[:, (2 * g + 1) * CMP_HIDDEN:(2 * g + 2) * CMP_HIDDEN]
            hid = top + pltpu.roll(bot, nchunk - 1, 0) + posw
            y = _nn(_gelu_tanh(hid).astype(jnp.bfloat16), w2_ref[...])
            if gain is not None:
                ms = jnp.sum(y * y, axis=-1, keepdims=True) * (0.5 / HEAD_DIM)
                y = y * lax.rsqrt(ms + EPS) * gain
            outs.append(y.astype(jnp.bfloat16))
        return outs

    k0, k1 = branch(xk_ref, w1k_ref, pk_ref, w1kn_ref, w2k_ref, gk_ref[...])
    kcmp_ref[0, 0], kcmp_ref[0, 1] = k0, k1
    v0, v1 = branch(xv_ref, w1v_ref, pv_ref, w1vn_ref, w2v_ref, None)
    vcmp_ref[0, 0], vcmp_ref[0, 1] = v0, v1


def _pack_w1(w1):
    half = CMP_STRIDE * HEAD_DIM
    cols = []
    for g in range(NSA_GROUPS):
        for part in range(2):
            w = w1[part * half:(part + 1) * half].reshape(CMP_STRIDE, HEAD_DIM, CMP_HIDDEN)
            z = jnp.zeros_like(w)
            pieces = [w, z] if g == 0 else [z, w]
            cols.append(jnp.concatenate(pieces, axis=1).reshape(CMP_STRIDE * LANES, CMP_HIDDEN))
    return jnp.concatenate(cols, axis=1).astype(jnp.bfloat16)


def _compress(kc, vc, pos_k, w1_k, w2_k, pos_v, w1_v, w2_v, gain_cmp):
    B, S, _ = kc.shape
    nchunk = S // CMP_STRIDE
    bf = jnp.bfloat16
    xk = kc.reshape(B, nchunk, CMP_STRIDE * LANES)
    xv = vc.reshape(B, nchunk, CMP_STRIDE * LANES)
    args = [xk, xv, _pack_w1(w1_k), _pack_w1(w1_v),
            pos_k.reshape(1, -1).astype(bf), pos_v.reshape(1, -1).astype(bf),
            w1_k.astype(bf), w1_v.astype(bf),
            jnp.concatenate([w2_k, w2_k], axis=1).astype(bf), jnp.concatenate([w2_v, w2_v], axis=1).astype(bf),
            jnp.concatenate([gain_cmp, gain_cmp]).reshape(1, LANES)]
    x_spec = pl.BlockSpec((1, nchunk, CMP_STRIDE * LANES), lambda b: (b, 0, 0))
    full = lambda a: pl.BlockSpec(a.shape, lambda b: (0,) * a.ndim)
    out = jax.ShapeDtypeStruct((B, NSA_GROUPS, nchunk, LANES), bf)
    o_spec = pl.BlockSpec((1, NSA_GROUPS, nchunk, LANES), lambda b: (b, 0, 0, 0))
    return pl.pallas_call(
        _compress_body,
        grid=(B,),
        in_specs=[x_spec, x_spec] + [full(a) for a in args[2:]],
        out_specs=[o_spec, o_spec],
        out_shape=[out, out],
        compiler_params=pltpu.CompilerParams(dimension_semantics=("parallel",), vmem_limit_bytes=VMEM_LIMIT),
        name="compress",
    )(*args)


def _lane_partial_sum(p):
    t = p[:, 0:LANES]
    for c in range(1, p.shape[1] // LANES):
        t = t + p[:, c * LANES:(c + 1) * LANES]
    return t


def _softmax_step(s, v, m_ref, l_ref, acc_ref, first):
    blocks = [s[:, c * LANES:(c + 1) * LANES] for c in range(s.shape[1] // LANES)]
    bmax = blocks[0]
    for blk in blocks[1:]:
        bmax = jnp.maximum(bmax, blk)
    m_new = jnp.broadcast_to(jnp.max(bmax, axis=-1, keepdims=True), m_ref.shape)
    if not first:
        m_old = m_ref[...]
        m_new = jnp.maximum(m_old, m_new)
        alpha = jnp.exp2(m_old - m_new)
    p = jnp.concatenate([jnp.exp2(blk - m_new) for blk in blocks], axis=1)
    if first:
        l_ref[...] = _lane_partial_sum(p)
        acc_ref[...] = _nn(p.astype(jnp.bfloat16), v)
    else:
        l_ref[...] = alpha * l_ref[...] + _lane_partial_sum(p)
        acc_ref[...] = alpha * acc_ref[...] + _nn(p.astype(jnp.bfloat16), v)
    m_ref[...] = m_new


def _finish(l_ref, acc_ref):
    return acc_ref[...] / jnp.sum(l_ref[...], axis=-1, keepdims=True)


def _chunk_kinds(qi, c):
    d0 = qi - 2 * c
    kind0 = jnp.minimum(d0, N_FAR)
    kind1 = jnp.where(d0 >= 1, jnp.minimum(d0 - 1, N_FAR), _K_MASKED)
    return kind0, kind1


def _moba_body(q_ref, ke_ref, ko_ref, v_ref, kmean_ref, eye_ref, bias_ref, o_ref, m_ref, l_ref, acc_ref):
    qi = pl.program_id(2)
    nblk = kmean_ref.shape[1]
    lane = lax.broadcasted_iota(jnp.int32, (TQ, LANES), 1)
    lo_half = lane < HEAD_DIM
    lane16 = lax.broadcasted_iota(jnp.int32, (nblk, LANES), 1)
    jrow = lax.broadcasted_iota(jnp.int32, (nblk, TQ), 0)
    q2 = q_ref[0]
    kmean2 = kmean_ref[0]
    flags = []
    for hh in range(2):
        inhead16 = (lane16 < HEAD_DIM) if hh == 0 else (lane16 >= HEAD_DIM)
        km_hi, km_lo = _split(jnp.where(inhead16, kmean2, 0.0))
        sc = _nt(km_hi, q2) + _nt(km_lo, q2)
        cnt = jnp.zeros((nblk, TQ), jnp.int32)
        for jp in range(nblk):
            r = sc[jp:jp + 1, :]
            ahead = (r > sc) | ((r == sc) & (jp < jrow))
            cnt = cnt + jnp.where(ahead & (jp < qi), 1, 0)
        keep = ((jrow < qi) & (cnt < MOBA_TOPK)) | (jrow == qi)
        flags.append(jnp.where(keep, 0.0, 1.0))
    pad = jnp.zeros((HEAD_DIM - nblk, TQ), jnp.float32)
    placed = jnp.concatenate([flags[1], pad, flags[0], pad], axis=0).astype(jnp.bfloat16)
    maskpart = _nt(eye_ref[...], placed) * NEG
    q2f = q2.astype(jnp.float32)
    zero = jnp.zeros((TQ, LANES), jnp.bfloat16)
    qq = jnp.concatenate(
        [jnp.concatenate([jnp.where(lo_half, q2f, maskpart).astype(jnp.bfloat16), zero], axis=1),
         jnp.concatenate([zero, jnp.where(lo_half, maskpart, q2f).astype(jnp.bfloat16)], axis=1)], axis=0)

    def step(c, first):
        start = pl.multiple_of(c * TK, TK)
        kind0, kind1 = _chunk_kinds(qi, c)
        kk = jnp.concatenate([ke_ref[0, pl.ds(start, TK), :], ko_ref[0, pl.ds(start, TK), :]], axis=1)
        b = jnp.concatenate([jnp.concatenate([bias_ref[hh, kind0], bias_ref[hh, kind1]], axis=1)
                             for hh in range(2)], axis=0)
        _softmax_step(_nt(qq, kk) + b, v_ref[0, pl.ds(start, TK), :], m_ref, l_ref, acc_ref, first)

    last = lax.shift_right_logical(qi, 1)
    step(last, True)

    def body(c, carry):
        step(c, False)
        return carry

    lax.fori_loop(0, last, body, 0)
    o = _finish(l_ref, acc_ref)
    o_ref[0] = jnp.where(lo_half, o[0:TQ], o[TQ:2 * TQ])


def _eye():
    return jnp.asarray(np.eye(TQ, dtype=np.float32), jnp.bfloat16)


def _moba(qa, kae, kao, va, kmean, bias):
    B, S, _ = qa.shape
    nt = S // TQ
    npair = MOBA_HEADS // 2
    kv_spec = pl.BlockSpec((1, S, LANES), lambda b, p, i: (b, 0, p))
    return pl.pallas_call(
        _moba_body,
        grid=(B, npair, nt),
        in_specs=[pl.BlockSpec((1, TQ, LANES), lambda b, p, i: (b, i, p)),
                  kv_spec, kv_spec, kv_spec,
                  pl.BlockSpec((1, nt, LANES), lambda b, p, i: (b, 0, p)),
                  pl.BlockSpec((TQ, TQ), lambda b, p, i: (0, 0)),
                  pl.BlockSpec((2, _N_TILE_KINDS, TQ, TQ), lambda b, p, i: (p, 0, 0, 0))],
        out_specs=pl.BlockSpec((1, TQ, LANES), lambda b, p, i: (b, i, p)),
        out_shape=jax.ShapeDtypeStruct((B, S, MOBA_HEADS * HEAD_DIM), jnp.float32),
        scratch_shapes=[pltpu.VMEM((2 * TQ, LANES), jnp.float32)] * 3,
        compiler_params=pltpu.CompilerParams(dimension_semantics=("parallel", "parallel", "arbitrary"),
                                             vmem_limit_bytes=VMEM_LIMIT),
        name="moba",
    )(qa, kae, kao, va, kmean, _eye(), bias)


def _nsa_body(q_ref, kcmp_ref, vcmp_ref, kse_ref, kso_ref, vsl_ref, kwn_ref, vwn_ref, gate_ref, ov_ref, eye_ref,
              gexp_ref, bias_ref, o_ref, imp_ref, m_ref, l_ref, acc_ref):
    qi = pl.program_id(2)
    ncmp = kcmp_ref.shape[2]
    nsel = ov_ref.shape[0]
    bf = jnp.bfloat16
    lane = lax.broadcasted_iota(jnp.int32, (TQ, LANES), 1)
    lo_half = lane < HEAD_DIM
    qf = [q_ref[0, :, 0:LANES].astype(jnp.float32), q_ref[0, :, LANES:2 * LANES].astype(jnp.float32)]
    q4 = jnp.concatenate([jnp.where(lo_half if hh % 2 == 0 else ~lo_half, qf[hh // 2], 0.0).astype(bf)
                          for hh in range(NSA_HPG)], axis=0)

    lc = _nt(q4, kcmp_ref[0, 0])
    tpos = qi * TQ + lax.broadcasted_iota(jnp.int32, (TQ, ncmp), 0)
    cidx = lax.broadcasted_iota(jnp.int32, (TQ, ncmp), 1)
    valid1 = (cidx * CMP_STRIDE + (CMP_LEN - 1) <= tpos) & (cidx < ncmp - 1)
    valid = jnp.concatenate([valid1] * NSA_HPG, axis=0)
    mx = jnp.max(jnp.where(valid, lc, NEG), axis=-1, keepdims=True)
    e = jnp.where(valid, jnp.exp2(lc - mx), 0.0)
    den = jnp.sum(e, axis=-1, keepdims=True)
    pc = e * (1.0 / jnp.where(den > 0.0, den, 1.0))
    o_cmp = _nn(pc.astype(bf), vcmp_ref[0, 0])
    psum = pc[0:TQ] + pc[TQ:2 * TQ] + pc[2 * TQ:3 * TQ] + pc[3 * TQ:4 * TQ]
    p_hi, p_lo = _split(psum)
    ov = ov_ref[...]
    imp_ref[...] = _nt(ov, p_hi) + _nt(ov, p_lo)

    imp = imp_ref[...]
    jrow = lax.broadcasted_iota(jnp.int32, (nsel, TQ), 0)
    qblk = (qi * TQ + lax.broadcasted_iota(jnp.int32, (nsel, TQ), 1)) // SEL_BLOCK
    cand = (jrow >= 1) & (jrow < qblk)

    def rank_body(jp, cnt):
        r = imp_ref[pl.ds(jp, 1), :]
        ahead = (r > imp) | ((r == imp) & (jp < jrow))
        return cnt + jnp.where(ahead & (jp < qblk), 1, 0)

    cnt = lax.fori_loop(1, qi * (TQ // SEL_BLOCK) + (TQ // SEL_BLOCK - 1), rank_body,
                        jnp.zeros((nsel, TQ), jnp.int32))
    keep = (jrow == 0) | (jrow == qblk) | (cand & (cnt < SEL_TOPN - 2))
    flag = jnp.where(keep, 0.0, 1.0)
    if nsel < HEAD_DIM:
        flag = jnp.concatenate([flag, jnp.zeros((HEAD_DIM - nsel, TQ), jnp.float32)], axis=0)
    maskpart = _nt(eye_ref[...], jnp.concatenate([flag, flag], axis=0).astype(bf)) * NEG

    zero = jnp.zeros((2 * TQ, LANES), bf)
    qa2 = [jnp.concatenate([jnp.where(lo_half if par == 0 else ~lo_half, qf[s], maskpart).astype(bf)
                            for s in range(2)], axis=0) for par in range(2)]
    qq = jnp.concatenate([jnp.concatenate([qa2[0], zero], axis=1),
                          jnp.concatenate([zero, qa2[1]], axis=1)], axis=0)
    sel_heads = (0, 2, 1, 3)

    def sel_step(c, first):
        start = pl.multiple_of(c * TK, TK)
        kind0, kind1 = _chunk_kinds(qi, c)
        kk = jnp.concatenate([kse_ref[0, 0, pl.ds(start, TK), :], kso_ref[0, 0, pl.ds(start, TK), :]], axis=1)
        b = jnp.concatenate(
            [jnp.concatenate([bias_ref[h, kind0], bias_ref[h, kind1]], axis=1) for h in sel_heads], axis=0)
        _softmax_step(_nt(qq, kk) + b, vsl_ref[0, 0, pl.ds(start, TK), :], m_ref, l_ref, acc_ref, first)

    last = lax.shift_right_logical(qi, 1)
    sel_step(last, True)

    def sel_body(c, carry):
        sel_step(c, False)
        return carry

    lax.fori_loop(0, last, sel_body, 0)
    o_all = _finish(l_ref, acc_ref)
    o_sel = [o_all[0:2 * TQ], o_all[2 * TQ:4 * TQ]]

    t0 = jnp.maximum(qi - (N_WIN_TILES - 1), 0)
    wstart = pl.multiple_of(t0 * TQ, TQ)
    sw = _nt(q4, kwn_ref[0, 0, pl.ds(wstart, N_WIN_TILES * TQ), :])
    kinds = []
    for r in range(N_WIN_TILES):
        dd = qi - (t0 + r)
        kinds.append(jnp.where(dd >= 0, _K_WIN + dd, _K_MASKED))
    bw = jnp.concatenate([jnp.concatenate([bias_ref[hh, k] for k in kinds], axis=1) for hh in range(NSA_HPG)], axis=0)
    sw = sw + bw
    pw = jnp.exp2(sw - jnp.max(sw, axis=-1, keepdims=True))
    o_win = _nn(pw.astype(bf), vwn_ref[0, 0, pl.ds(wstart, N_WIN_TILES * TQ), :]) \
        / jnp.sum(pw, axis=-1, keepdims=True)

    g = jax.nn.sigmoid(gate_ref[0])
    g_hi, g_lo = _split(g)
    gx = _nn(g_hi, gexp_ref[0]) + _nn(g_lo, gexp_ref[0])
    for s in range(2):
        h_even, h_odd = 2 * s, 2 * s + 1
        cmp2 = jnp.where(lo_half, o_cmp[h_even * TQ:(h_even + 1) * TQ], o_cmp[h_odd * TQ:(h_odd + 1) * TQ])
        sel2 = jnp.where(lo_half, o_sel[0][s * TQ:(s + 1) * TQ], o_sel[1][s * TQ:(s + 1) * TQ])
        win2 = jnp.where(lo_half, o_win[h_even * TQ:(h_even + 1) * TQ], o_win[h_odd * TQ:(h_odd + 1) * TQ])
        base = s * 3 * LANES
        o_ref[0, :, s * LANES:(s + 1) * LANES] = (gx[:, base:base + LANES] * cmp2
                                                  + gx[:, base + LANES:base + 2 * LANES] * sel2
                                                  + gx[:, base + 2 * LANES:base + 3 * LANES] * win2)


def _gate_expand():
    e = np.zeros((NSA_GROUPS, LANES, 2 * 3 * LANES), np.float32)
    for g in range(NSA_GROUPS):
        for s in range(2):
            for c in range(3):
                for half in range(2):
                    h = g * NSA_HPG + 2 * s + half
                    col = (s * 3 + c) * LANES + half * HEAD_DIM
                    e[g, 3 * h + c, col:col + HEAD_DIM] = 1.0
    return e


def _nsa(qb, kcmp, vcmp, kse, kso, vsl, kwn, vwn, gate, bias):
    B, S, _ = qb.shape
    nt = S // TQ
    ncmp = kcmp.shape[2]
    nsel = S // SEL_BLOCK
    bf = jnp.bfloat16
    cs = np.arange(ncmp) * CMP_STRIDE
    bs = np.arange(nsel) * SEL_BLOCK
    ov = ((cs[None, :] < bs[:, None] + SEL_BLOCK) & (cs[None, :] + CMP_LEN > bs[:, None])
          & (np.arange(ncmp)[None, :] < ncmp - 1)).astype(np.float32)
    grp = lambda n: pl.BlockSpec((1, 1, n, LANES), lambda b, g, i: (b, g, 0, 0))
    return pl.pallas_call(
        _nsa_body,
        grid=(B, NSA_GROUPS, nt),
        in_specs=[pl.BlockSpec((1, TQ, 2 * LANES), lambda b, g, i: (b, i, g)),
                  grp(ncmp), grp(ncmp), grp(S), grp(S), grp(S), grp(S), grp(S),
                  pl.BlockSpec((1, TQ, LANES), lambda b, g, i: (b, i, 0)),
                  pl.BlockSpec((nsel, ncmp), lambda b, g, i: (0, 0)),
                  pl.BlockSpec((TQ, TQ), lambda b, g, i: (0, 0)),
                  pl.BlockSpec((1, LANES, 6 * LANES), lambda b, g, i: (g, 0, 0)),
                  pl.BlockSpec((NSA_HPG, _N_TILE_KINDS, TQ, TQ), lambda b, g, i: (2 + g, 0, 0, 0))],
        out_specs=pl.BlockSpec((1, TQ, 2 * LANES), lambda b, g, i: (b, i, g)),
        out_shape=jax.ShapeDtypeStruct((B, S, NSA_HEADS * HEAD_DIM), jnp.float32),
        scratch_shapes=[pltpu.VMEM((nsel, TQ), jnp.float32)] + [pltpu.VMEM((NSA_HPG * TQ, LANES), jnp.float32)] * 3,
        compiler_params=pltpu.CompilerParams(dimension_semantics=("parallel", "parallel", "arbitrary"),
                                             vmem_limit_bytes=VMEM_LIMIT),
        name="nsa",
    )(qb, kcmp, vcmp, kse, kso, vsl, kwn, vwn, gate, jnp.asarray(ov, bf), _eye(),
      jnp.asarray(_gate_expand(), bf), bias)


def _out_proj_body(x_ref, nw_ref, oa_ref, ob_ref, wz_ref, wa_ref, wb_ref, wo_ref, out_ref):
    x = x_ref[0]
    ms = jnp.mean(x * x, axis=-1, keepdims=True)
    h = (x * lax.rsqrt(ms + EPS) * nw_ref[...]).astype(jnp.bfloat16)
    z = _nn(h, wz_ref[:, 0:1024])
    ya = (oa_ref[0] * jax.nn.silu(z[:, 0:512])).astype(jnp.bfloat16)
    yb = (ob_ref[0] * jax.nn.silu(z[:, 512:1024])).astype(jnp.bfloat16)
    gm = jax.nn.sigmoid(_nn(h, wz_ref[:, 1024:3072]))
    merged = gm[:, 0:D_MODEL] * _nn(ya, wa_ref[...]) + gm[:, D_MODEL:] * _nn(yb, wb_ref[...])
    out_ref[0] = x + _nn(merged.astype(jnp.bfloat16), wo_ref[...])


def _out_proj(x, norm_w, oa, ob, wz, wa, wb, wo):
    B, S, _ = x.shape
    nw = norm_w.reshape(1, D_MODEL)
    tok = lambda w: pl.BlockSpec((1, TQ, w), lambda b, i: (b, i, 0))
    full = lambda a: pl.BlockSpec(a.shape, lambda b, i: (0,) * a.ndim)
    return pl.pallas_call(
        _out_proj_body,
        grid=(B, S // TQ),
        in_specs=[tok(D_MODEL), full(nw), tok(512), tok(512), full(wz), full(wa), full(wb), full(wo)],
        out_specs=tok(D_MODEL),
        out_shape=jax.ShapeDtypeStruct((B, S, D_MODEL), jnp.float32),
        compiler_params=pltpu.CompilerParams(dimension_semantics=("parallel", "parallel"),
                                             vmem_limit_bytes=VMEM_LIMIT),
        name="out_proj",
    )(x, nw, oa, ob, wz, wa, wb, wo)


def _pack_in_weights(w_in, q_norm_a, k_norm_a, q_norm_b, k_norm_sel, k_norm_win):
    bf = jnp.bfloat16
    w1 = jnp.concatenate([w_in[:, 0:1536], w_in[:, 2048:2560], w_in[:, 2560:3328],
                          w_in[:, 3328:3352], jnp.zeros((D_MODEL, LANES - 3 * NSA_HEADS), w_in.dtype)], axis=1)
    wz = jnp.concatenate([w_in[:, 1536:2048], w_in[:, 3352:3864], w_in[:, 3864:5912]], axis=1)
    ones = lambda n: jnp.ones((n,), jnp.float32)
    gain = jnp.concatenate([jnp.tile(q_norm_a, MOBA_HEADS), jnp.tile(k_norm_a, MOBA_HEADS), ones(512),
                            jnp.tile(q_norm_b, NSA_HEADS), ones(256), jnp.tile(k_norm_sel, NSA_GROUPS), ones(128),
                            jnp.tile(k_norm_win, NSA_GROUPS), ones(256)]).reshape(1, _N1)
    return w1.astype(bf), wz.astype(bf), gain


def kernel(x, norm_w, w_in, q_norm_a, k_norm_a, q_norm_b, k_norm_cmp, k_norm_sel, k_norm_win, cmp_pos_k, cmp_w1_k, cmp_w2_k, cmp_pos_v, cmp_w1_v, cmp_w2_v, rel_bias, w_branch_a, w_branch_b, w_out):
    bf = jnp.bfloat16
    w1, wz, gain = _pack_in_weights(w_in[0], q_norm_a[0], k_norm_a[0], q_norm_b[0], k_norm_sel[0], k_norm_win[0])
    bias = _bias_tiles(rel_bias)
    qa, kae, kao, va, qb, kmean, kc, vc, kse, kso, vsl, kwn, vwn, gate = _in_proj(x, norm_w[0], w1, gain)
    kcmp, vcmp = _compress(kc, vc, cmp_pos_k[0], cmp_w1_k[0], cmp_w2_k[0],
                           cmp_pos_v[0], cmp_w1_v[0], cmp_w2_v[0], k_norm_cmp[0])
    oa = _moba(qa, kae, kao, va, kmean[:, :, 0, :], bias)
    ob = _nsa(qb, kcmp, vcmp, kse, kso, vsl, kwn, vwn, gate, bias)
    return _out_proj(x, norm_w[0], oa, ob, wz, w_branch_a[0].astype(bf), w_branch_b[0].astype(bf),
                     w_out[0].astype(bf))
```

```python
import math

import jax
import jax.numpy as jnp
import numpy as np
from jax import lax
from jax.experimental import pallas as pl
from jax.experimental.pallas import tpu as pltpu

D_MODEL = 1024
HEAD_DIM = 64
MOBA_HEADS = 8
NSA_HEADS = 8
NSA_GROUPS = 2
NSA_HPG = NSA_HEADS // NSA_GROUPS
MOBA_BLOCK = 256
MOBA_TOPK = 3
CMP_LEN = 32
CMP_STRIDE = 16
CMP_HIDDEN = 256
SEL_BLOCK = 64
SEL_TOPN = 16
WINDOW = 512
NUM_BUCKETS = 32
MAX_DISTANCE = 1024
EPS = 1e-6
NEG = -1e30

LANES = 128
TQ = 256
TK = 2 * TQ
TR = 2 * TQ
N_FAR = 5
N_WIN_TILES = WINDOW // TQ + 1
VMEM_LIMIT = 56 * 1024 * 1024
LOG2E = math.log2(math.e)
QSCALE = LOG2E / math.sqrt(HEAD_DIM)

_K_WIN = N_FAR + 1
_K_MASKED = _K_WIN + N_WIN_TILES
_N_TILE_KINDS = _K_MASKED + 1

_C_QA, _C_KA, _C_VA, _C_QB = 0, 512, 1024, 1536
_C_KC, _C_VC, _C_KSL, _C_VSL, _C_KWN, _C_VWN, _C_GATE = 2048, 2176, 2304, 2432, 2560, 2688, 2816
_N1 = 2944


def _nt(a, b):
    return lax.dot_general(a, b, (((1,), (1,)), ((), ())), preferred_element_type=jnp.float32)


def _nn(a, b):
    return jnp.dot(a, b, preferred_element_type=jnp.float32)


def _split(x):
    hi = x.astype(jnp.bfloat16)
    lo = (x - hi.astype(jnp.float32)).astype(jnp.bfloat16)
    return hi, lo


def _bucket_table(n):
    d = np.arange(n)
    nf = np.maximum(d, NUM_BUCKETS // 2).astype(np.float64)
    large = NUM_BUCKETS // 2 + (np.log(nf / (NUM_BUCKETS // 2)) / math.log(MAX_DISTANCE / (NUM_BUCKETS // 2))
                                * (NUM_BUCKETS - NUM_BUCKETS // 2)).astype(np.int64)
    return np.where(d < NUM_BUCKETS // 2, d, np.minimum(large, NUM_BUCKETS - 1))


def _bias_onehot():
    oh = np.zeros((40, _N_TILE_KINDS * 2 * TQ), np.float32)
    buckets = _bucket_table((N_FAR + 2) * TQ + 1)
    for t in range(_N_TILE_KINDS):
        window = _K_WIN <= t < _K_MASKED
        delta = t - _K_WIN if window else t
        for u in range(2 * TQ):
            d = TQ * delta + TQ - u
            masked = t == _K_MASKED or d < 0 or (window and d >= WINDOW)
            oh[NUM_BUCKETS if masked else buckets[d], t * 2 * TQ + u] = 1.0
    return oh


def _bias_body(relt_ref, oh_ref, out_ref):
    g = jnp.dot(jnp.broadcast_to(relt_ref[0], (8, 40)), oh_ref[...], preferred_element_type=jnp.float32,
                precision=lax.Precision.HIGHEST)
    for t in range(_N_TILE_KINDS):
        full = jnp.broadcast_to(g[0:1, t * 2 * TQ:(t + 1) * 2 * TQ], (TQ, 2 * TQ))
        rolled = pltpu.roll(full, TQ, 1, stride=1, stride_axis=0)
        out_ref[0, t] = rolled[:, :TQ]


def _bias_tiles(rel_bias):
    nh = MOBA_HEADS + NSA_HEADS
    relt = jnp.concatenate([rel_bias.T.astype(jnp.float32) * LOG2E, jnp.full((nh, 1), NEG, jnp.float32),
                            jnp.zeros((nh, 7), jnp.float32)], axis=1).reshape(nh, 1, 40)
    oh = jnp.asarray(_bias_onehot())
    return pl.pallas_call(
        _bias_body,
        grid=(nh,),
        in_specs=[pl.BlockSpec((1, 1, 40), lambda h: (h, 0, 0)), pl.BlockSpec(oh.shape, lambda h: (0, 0))],
        out_specs=pl.BlockSpec((1, _N_TILE_KINDS, TQ, TQ), lambda h: (h, 0, 0, 0)),
        out_shape=jax.ShapeDtypeStruct((nh, _N_TILE_KINDS, TQ, TQ), jnp.float32),
        compiler_params=pltpu.CompilerParams(dimension_semantics=("parallel",), vmem_limit_bytes=VMEM_LIMIT),
        name="bias_tiles",
    )(relt, oh)


def _seg_norm(x, bd, gain):
    hi, lo = _split(x * x)
    ss = _nn(hi, bd) + _nn(lo, bd)
    return x * lax.rsqrt(ss * (1.0 / HEAD_DIM) + EPS) * gain


def _in_proj_body(x_ref, nw_ref, w_ref, bd_ref, gain_ref,
                  qa_ref, kae_ref, kao_ref, va_ref, qb_ref, kmean_ref, kc_ref, vc_ref,
                  kse_ref, kso_ref, vsl_ref, kwn_ref, vwn_ref, gate_ref):
    i = pl.program_id(1)
    x = x_ref[0]
    ms = jnp.mean(x * x, axis=-1, keepdims=True)
    h = (x * lax.rsqrt(ms + EPS) * nw_ref[...]).astype(jnp.bfloat16)
    bd = bd_ref[...]
    bf = jnp.bfloat16
    lane = lax.broadcasted_iota(jnp.int32, (TQ, LANES), 1)
    row = lax.broadcasted_iota(jnp.int32, (TQ, LANES), 0)
    lo_half = lane < HEAD_DIM

    def normed(col, width, scale=None):
        p = _nn(h, w_ref[:, col:col + width])
        outs = []
        for s in range(width // LANES):
            c = col + s * LANES
            y = _seg_norm(p[:, s * LANES:(s + 1) * LANES], bd, gain_ref[:, c:c + LANES])
            outs.append(y if scale is None else y * scale)
        return outs

    qa = normed(_C_QA, 512, QSCALE)
    for s in range(4):
        qa_ref[0, :, s * LANES:(s + 1) * LANES] = qa[s].astype(bf)
    ka = normed(_C_KA, 512)
    oh_hi = jnp.where(lane == HEAD_DIM + i, 1.0, 0.0)
    oh_lo = jnp.where(lane == i, 1.0, 0.0)
    for s in range(4):
        kae_ref[0, :, s * LANES:(s + 1) * LANES] = jnp.where(lo_half, ka[s], oh_hi).astype(bf)
        kao_ref[0, :, s * LANES:(s + 1) * LANES] = jnp.where(lo_half, oh_lo, ka[s]).astype(bf)
        kmean_ref[0, 0, :, s * LANES:(s + 1) * LANES] = jnp.broadcast_to(
            jnp.mean(ka[s], axis=0, keepdims=True), (8, LANES))
    va_ref[0] = _nn(h, w_ref[:, _C_VA:_C_VA + 512]).astype(bf)
    qb = normed(_C_QB, 512, QSCALE)
    for s in range(4):
        qb_ref[0, :, s * LANES:(s + 1) * LANES] = qb[s].astype(bf)

    rest = _nn(h, w_ref[:, _C_KC:_N1])

    def slab(c):
        return rest[:, c - _C_KC:c - _C_KC + LANES]

    kc_ref[0] = slab(_C_KC).astype(bf)
    vc_ref[0] = slab(_C_VC).astype(bf)

    def dup_store(ref, y):
        r = pltpu.roll(y, HEAD_DIM, 1)
        ref[0, 0] = jnp.where(lo_half, y, r).astype(bf)
        ref[0, 1] = jnp.where(lo_half, r, y).astype(bf)

    y = _seg_norm(slab(_C_KSL), bd, gain_ref[:, _C_KSL:_C_KSL + LANES])
    r = pltpu.roll(y, HEAD_DIM, 1)
    oh = jnp.where((lane & (HEAD_DIM - 1)) == i * (TQ // SEL_BLOCK) + row // SEL_BLOCK, 1.0, 0.0)
    kse_ref[0, 0] = jnp.where(lo_half, y, oh).astype(bf)
    kso_ref[0, 0] = jnp.where(lo_half, oh, r).astype(bf)
    kse_ref[0, 1] = jnp.where(lo_half, r, oh).astype(bf)
    kso_ref[0, 1] = jnp.where(lo_half, oh, y).astype(bf)
    dup_store(vsl_ref, slab(_C_VSL))
    dup_store(kwn_ref, _seg_norm(slab(_C_KWN), bd, gain_ref[:, _C_KWN:_C_KWN + LANES]))
    dup_store(vwn_ref, slab(_C_VWN))
    gate_ref[0] = slab(_C_GATE)


def _in_proj(x, norm_w, w1, gain_row):
    B, S, _ = x.shape
    nt = S // TQ
    bf = jnp.bfloat16
    bd = jnp.asarray(np.kron(np.eye(2, dtype=np.float32), np.ones((HEAD_DIM, HEAD_DIM), np.float32)), bf)
    tok = lambda w, dt: jax.ShapeDtypeStruct((B, S, w), dt)
    grp = jax.ShapeDtypeStruct((B, NSA_GROUPS, S, LANES), bf)
    tok_spec = lambda w: pl.BlockSpec((1, TQ, w), lambda b, i: (b, i, 0))
    grp_spec = pl.BlockSpec((1, NSA_GROUPS, TQ, LANES), lambda b, i: (b, 0, i, 0))
    full = lambda a: pl.BlockSpec(a.shape, lambda b, i: (0,) * a.ndim)
    nw = norm_w.reshape(1, D_MODEL)
    return pl.pallas_call(
        _in_proj_body,
        grid=(B, nt),
        in_specs=[tok_spec(D_MODEL), full(nw), full(w1), full(bd), full(gain_row)],
        out_specs=[tok_spec(512), tok_spec(512), tok_spec(512), tok_spec(512), tok_spec(512),
                   pl.BlockSpec((1, 1, 8, 512), lambda b, i: (b, i, 0, 0)),
                   tok_spec(LANES), tok_spec(LANES),
                   grp_spec, grp_spec, grp_spec, grp_spec, grp_spec, tok_spec(LANES)],
        out_shape=[tok(512, bf), tok(512, bf), tok(512, bf), tok(512, bf), tok(512, bf),
                   jax.ShapeDtypeStruct((B, nt, 8, 512), jnp.float32),
                   tok(LANES, bf), tok(LANES, bf), grp, grp, grp, grp, grp, tok(LANES, jnp.float32)],
        compiler_params=pltpu.CompilerParams(dimension_semantics=("parallel", "parallel"),
                                             vmem_limit_bytes=VMEM_LIMIT),
        name="in_proj",
    )(x, nw, w1, bd, gain_row)


def _gelu_tanh(x):
    return 0.5 * x * (1.0 + jnp.tanh(math.sqrt(2.0 / math.pi) * (x + 0.044715 * (x * x * x))))


def _compress_body(xk_ref, xv_ref, w1k_ref, w1v_ref, pk_ref, pv_ref, w1kn_ref, w1vn_ref,
                   w2k_ref, w2v_ref, gk_ref, kcmp_ref, vcmp_ref):
    nchunk = xk_ref.shape[1]

    def branch(x_ref, w1_ref, pos_ref, w1n_ref, w2_ref, gain):
        ab = _nn(x_ref[0], w1_ref[...])
        posw = _nn(jnp.broadcast_to(pos_ref[...], (8, CMP_LEN * HEAD_DIM)), w1n_ref[...])[0:1]
        outs = []
        for g in range(NSA_GROUPS):
            top = ab[:, (2 * g) * CMP_HIDDEN:(2 * g + 1) * CMP_HIDDEN]
            bot = ab[:, (2 * g + 1) * CMP_HIDDEN:(2 * g + 2) * CMP_HIDDEN]
            hid = top + pltpu.roll(bot, nchunk - 1, 0) + posw
            y = _nn(_gelu_tanh(hid).astype(jnp.bfloat16), w2_ref[...])
            if gain is not None:
                ms = jnp.sum(y * y, axis=-1, keepdims=True) * (0.5 / HEAD_DIM)
                y = y * lax.rsqrt(ms + EPS) * gain
            outs.append(y.astype(jnp.bfloat16))
        return outs

    k0, k1 = branch(xk_ref, w1k_ref, pk_ref, w1kn_ref, w2k_ref, gk_ref[...])
    kcmp_ref[0, 0], kcmp_ref[0, 1] = k0, k1
    v0, v1 = branch(xv_ref, w1v_ref, pv_ref, w1vn_ref, w2v_ref, None)
    vcmp_ref[0, 0], vcmp_ref[0, 1] = v0, v1


def _pack_w1(w1):
    half = CMP_STRIDE * HEAD_DIM
    cols = []
    for g in range(NSA_GROUPS):
        for part in range(2):
            w = w1[part * half:(part + 1) * half].reshape(CMP_STRIDE, HEAD_DIM, CMP_HIDDEN)
            z = jnp.zeros_like(w)
            pieces = [w, z] if g == 0 else [z, w]
            cols.append(jnp.concatenate(pieces, axis=1).reshape(CMP_STRIDE * LANES, CMP_HIDDEN))
    return jnp.concatenate(cols, axis=1).astype(jnp.bfloat16)


def _compress(kc, vc, pos_k, w1_k, w2_k, pos_v, w1_v, w2_v, gain_cmp):
    B, S, _ = kc.shape
    nchunk = S // CMP_STRIDE
    bf = jnp.bfloat16
    xk = kc.reshape(B, nchunk, CMP_STRIDE * LANES)
    xv = vc.reshape(B, nchunk, CMP_STRIDE * LANES)
    args = [xk, xv, _pack_w1(w1_k), _pack_w1(w1_v),
            pos_k.reshape(1, -1).astype(bf), pos_v.reshape(1, -1).astype(bf),
            w1_k.astype(bf), w1_v.astype(bf),
            jnp.concatenate([w2_k, w2_k], axis=1).astype(bf), jnp.concatenate([w2_v, w2_v], axis=1).astype(bf),
            jnp.concatenate([gain_cmp, gain_cmp]).reshape(1, LANES)]
    x_spec = pl.BlockSpec((1, nchunk, CMP_STRIDE * LANES), lambda b: (b, 0, 0))
    full = lambda a: pl.BlockSpec(a.shape, lambda b: (0,) * a.ndim)
    out = jax.ShapeDtypeStruct((B, NSA_GROUPS, nchunk, LANES), bf)
    o_spec = pl.BlockSpec((1, NSA_GROUPS, nchunk, LANES), lambda b: (b, 0, 0, 0))
    return pl.pallas_call(
        _compress_body,
        grid=(B,),
        in_specs=[x_spec, x_spec] + [full(a) for a in args[2:]],
        out_specs=[o_spec, o_spec],
        out_shape=[out, out],
        compiler_params=pltpu.CompilerParams(dimension_semantics=("parallel",), vmem_limit_bytes=VMEM_LIMIT),
        name="compress",
    )(*args)


def _lane_partial_sum(p):
    t = p[:, 0:LANES]
    for c in range(1, p.shape[1] // LANES):
        t = t + p[:, c * LANES:(c + 1) * LANES]
    return t


def _softmax_step(s, v, m_ref, l_ref, acc_ref, first):
    blocks = [s[:, c * LANES:(c + 1) * LANES] for c in range(s.shape[1] // LANES)]
    bmax = blocks[0]
    for blk in blocks[1:]:
        bmax = jnp.maximum(bmax, blk)
    m_new = jnp.broadcast_to(jnp.max(bmax, axis=-1, keepdims=True), m_ref.shape)
    if not first:
        m_old = m_ref[...]
        m_new = jnp.maximum(m_old, m_new)
        alpha = jnp.exp2(m_old - m_new)
    p = jnp.concatenate([jnp.exp2(blk - m_new) for blk in blocks], axis=1)
    pb = p.astype(jnp.bfloat16)
    half = s.shape[0] // 2
    pv = jnp.concatenate([_nn(pb[0:half], v), _nn(pb[half:], v)], axis=0)
    if first:
        l_ref[...] = _lane_partial_sum(p)
        acc_ref[...] = pv
    else:
        l_ref[...] = alpha * l_ref[...] + _lane_partial_sum(p)
        acc_ref[...] = alpha * acc_ref[...] + pv
    m_ref[...] = m_new


def _finish(l_ref, acc_ref):
    return acc_ref[...] / jnp.sum(l_ref[...], axis=-1, keepdims=True)


def _tile_kind(delta):
    return jnp.where(delta >= 0, jnp.minimum(delta, N_FAR), _K_MASKED)


def _chunk_kinds(qi, c):
    d0 = qi - 2 * c
    kind0 = jnp.minimum(d0, N_FAR)
    kind1 = jnp.where(d0 >= 1, jnp.minimum(d0 - 1, N_FAR), _K_MASKED)
    return kind0, kind1


def _moba_body(q_ref, ke_ref, ko_ref, v_ref, kmean_ref, eye_ref, bias_ref, o_ref, m_ref, l_ref, acc_ref):
    i = pl.program_id(2)
    nblk = kmean_ref.shape[1]
    lane = lax.broadcasted_iota(jnp.int32, (TR, LANES), 1)
    lo_half = lane < HEAD_DIM
    lane16 = lax.broadcasted_iota(jnp.int32, (nblk, LANES), 1)
    jrow = lax.broadcasted_iota(jnp.int32, (nblk, TR), 0)
    qblk = 2 * i + jnp.where(lax.broadcasted_iota(jnp.int32, (nblk, TR), 1) >= TQ, 1, 0)
    q2 = q_ref[0]
    kmean2 = kmean_ref[0]
    flags = []
    for hh in range(2):
        inhead16 = (lane16 < HEAD_DIM) if hh == 0 else (lane16 >= HEAD_DIM)
        km_hi, km_lo = _split(jnp.where(inhead16, kmean2, 0.0))
        sc = jnp.where(jrow < qblk, _nt(km_hi, q2) + _nt(km_lo, q2), -jnp.inf)
        flag = jnp.where(jrow == qblk, 0.0, 1.0)
        for _ in range(MOBA_TOPK):
            best = jnp.max(sc, axis=0, keepdims=True)
            first = jnp.min(jnp.where(sc == best, jrow, nblk), axis=0, keepdims=True)
            hit = jrow == first
            flag = jnp.where(hit, 0.0, flag)
            sc = jnp.where(hit, -jnp.inf, sc)
        flags.append(flag)
    pad = jnp.zeros((HEAD_DIM - nblk, TR), jnp.float32)
    placed = jnp.concatenate([flags[1], pad, flags[0], pad], axis=0).astype(jnp.bfloat16)
    maskpart = _nt(eye_ref[...], placed) * NEG
    q2f = q2.astype(jnp.float32)
    zero = jnp.zeros((TR, LANES), jnp.bfloat16)
    qq = jnp.concatenate(
        [jnp.concatenate([jnp.where(lo_half, q2f, maskpart).astype(jnp.bfloat16), zero], axis=1),
         jnp.concatenate([zero, jnp.where(lo_half, maskpart, q2f).astype(jnp.bfloat16)], axis=1)], axis=0)

    def step(c, first):
        start = pl.multiple_of(c * TK, TK)
        kk = jnp.concatenate([ke_ref[0, pl.ds(start, TK), :], ko_ref[0, pl.ds(start, TK), :]], axis=1)
        d = 2 * (i - c)
        b = jnp.concatenate(
            [jnp.concatenate([bias_ref[hh, _tile_kind(d + a - t)] for t in range(2)], axis=1)
             for hh in range(2) for a in range(2)], axis=0)
        _softmax_step(_nt(qq, kk) + b, v_ref[0, pl.ds(start, TK), :], m_ref, l_ref, acc_ref, first)

    step(i, True)

    def body(c, carry):
        step(c, False)
        return carry

    lax.fori_loop(0, i, body, 0)
    o = _finish(l_ref, acc_ref)
    o_ref[0] = jnp.where(lo_half, o[0:TR], o[TR:2 * TR])


def _eye(n):
    return jnp.asarray(np.eye(n, dtype=np.float32), jnp.bfloat16)


def _moba(qa, kae, kao, va, kmean, bias):
    B, S, _ = qa.shape
    nt = S // TQ
    npair = MOBA_HEADS // 2
    kv_spec = pl.BlockSpec((1, S, LANES), lambda b, p, i: (b, 0, p))
    return pl.pallas_call(
        _moba_body,
        grid=(B, npair, S // TR),
        in_specs=[pl.BlockSpec((1, TR, LANES), lambda b, p, i: (b, i, p)),
                  kv_spec, kv_spec, kv_spec,
                  pl.BlockSpec((1, nt, LANES), lambda b, p, i: (b, 0, p)),
                  pl.BlockSpec((TR, TR), lambda b, p, i: (0, 0)),
                  pl.BlockSpec((2, _N_TILE_KINDS, TQ, TQ), lambda b, p, i: (p, 0, 0, 0))],
        out_specs=pl.BlockSpec((1, TR, LANES), lambda b, p, i: (b, i, p)),
        out_shape=jax.ShapeDtypeStruct((B, S, MOBA_HEADS * HEAD_DIM), jnp.float32),
        scratch_shapes=[pltpu.VMEM((2 * TR, LANES), jnp.float32)] * 3,
        compiler_params=pltpu.CompilerParams(dimension_semantics=("parallel", "parallel", "arbitrary"),
                                             vmem_limit_bytes=VMEM_LIMIT),
        name="moba",
    )(qa, kae, kao, va, kmean, _eye(TR), bias)


def _nsa_body(q_ref, kcmp_ref, vcmp_ref, kse_ref, kso_ref, vsl_ref, kwn_ref, vwn_ref, gate_ref, ov_ref, eye_ref,
              gexp_ref, bias_ref, o_ref, imp_ref, m_ref, l_ref, acc_ref):
    qi = pl.program_id(2)
    ncmp = kcmp_ref.shape[2]
    nsel = ov_ref.shape[0]
    bf = jnp.bfloat16
    lane = lax.broadcasted_iota(jnp.int32, (TQ, LANES), 1)
    lo_half = lane < HEAD_DIM
    qf = [q_ref[0, :, 0:LANES].astype(jnp.float32), q_ref[0, :, LANES:2 * LANES].astype(jnp.float32)]
    q4 = jnp.concatenate([jnp.where(lo_half if hh % 2 == 0 else ~lo_half, qf[hh // 2], 0.0).astype(bf)
                          for hh in range(NSA_HPG)], axis=0)

    lc = _nt(q4, kcmp_ref[0, 0])
    tpos = qi * TQ + lax.broadcasted_iota(jnp.int32, (TQ, ncmp), 0)
    cidx = lax.broadcasted_iota(jnp.int32, (TQ, ncmp), 1)
    valid1 = (cidx * CMP_STRIDE + (CMP_LEN - 1) <= tpos) & (cidx < ncmp - 1)
    valid = jnp.concatenate([valid1] * NSA_HPG, axis=0)
    mx = jnp.max(jnp.where(valid, lc, NEG), axis=-1, keepdims=True)
    e = jnp.where(valid, jnp.exp2(lc - mx), 0.0)
    den = jnp.sum(e, axis=-1, keepdims=True)
    pc = e * (1.0 / jnp.where(den > 0.0, den, 1.0))
    o_cmp = _nn(pc.astype(bf), vcmp_ref[0, 0])
    psum = pc[0:TQ] + pc[TQ:2 * TQ] + pc[2 * TQ:3 * TQ] + pc[3 * TQ:4 * TQ]
    p_hi, p_lo = _split(psum)
    ov = ov_ref[...]
    imp = _nt(ov, p_hi) + _nt(ov, p_lo)

    jrow = lax.broadcasted_iota(jnp.int32, (nsel, TQ), 0)
    qblk = lax.shift_right_logical(qi * TQ + lax.broadcasted_iota(jnp.int32, (nsel, TQ), 1),
                                   SEL_BLOCK.bit_length() - 1)
    cand = (jrow >= 1) & (jrow < qblk)
    keys = jnp.where(cand, lax.bitcast_convert_type(imp + 0.0, jnp.int32), -1)
    imp_ref[...] = keys

    def rank_body(jp, cnt):
        r = imp_ref[pl.ds(jp, 1), :] + jnp.where(jrow > jp, 1, 0)
        return cnt + jnp.where(r > keys, 1, 0)

    cnt = lax.fori_loop(1, qi * (TQ // SEL_BLOCK) + (TQ // SEL_BLOCK - 1), rank_body,
                        jnp.zeros((nsel, TQ), jnp.int32))
    keep = (jrow == 0) | (jrow == qblk) | (cand & (cnt < SEL_TOPN - 2))
    flag = jnp.where(keep, 0.0, 1.0)
    if nsel < HEAD_DIM:
        flag = jnp.concatenate([flag, jnp.zeros((HEAD_DIM - nsel, TQ), jnp.float32)], axis=0)
    maskpart = _nt(eye_ref[...], jnp.concatenate([flag, flag], axis=0).astype(bf)) * NEG

    zero = jnp.zeros((2 * TQ, LANES), bf)
    qa2 = [jnp.concatenate([jnp.where(lo_half if par == 0 else ~lo_half, qf[s], maskpart).astype(bf)
                            for s in range(2)], axis=0) for par in range(2)]
    qq = jnp.concatenate([jnp.concatenate([qa2[0], zero], axis=1),
                          jnp.concatenate([zero, qa2[1]], axis=1)], axis=0)
    sel_heads = (0, 2, 1, 3)

    def sel_step(c, first):
        start = pl.multiple_of(c * TK, TK)
        kind0, kind1 = _chunk_kinds(qi, c)
        kk = jnp.concatenate([kse_ref[0, 0, pl.ds(start, TK), :], kso_ref[0, 0, pl.ds(start, TK), :]], axis=1)
        b = jnp.concatenate(
            [jnp.concatenate([bias_ref[h, kind0], bias_ref[h, kind1]], axis=1) for h in sel_heads], axis=0)
        _softmax_step(_nt(qq, kk) + b, vsl_ref[0, 0, pl.ds(start, TK), :], m_ref, l_ref, acc_ref, first)

    last = lax.shift_right_logical(qi, 1)
    sel_step(last, True)

    def sel_body(c, carry):
        sel_step(c, False)
        return carry

    lax.fori_loop(0, last, sel_body, 0)
    o_all = _finish(l_ref, acc_ref)
    o_sel = [o_all[0:2 * TQ], o_all[2 * TQ:4 * TQ]]

    t0 = jnp.maximum(qi - (N_WIN_TILES - 1), 0)
    wstart = pl.multiple_of(t0 * TQ, TQ)
    sw = _nt(q4, kwn_ref[0, 0, pl.ds(wstart, N_WIN_TILES * TQ), :])
    kinds = []
    for r in range(N_WIN_TILES):
        dd = qi - (t0 + r)
        kinds.append(jnp.where(dd >= 0, _K_WIN + dd, _K_MASKED))
    bw = jnp.concatenate([jnp.concatenate([bias_ref[hh, k] for k in kinds], axis=1) for hh in range(NSA_HPG)], axis=0)
    sw = sw + bw
    pw = jnp.exp2(sw - jnp.max(sw, axis=-1, keepdims=True))
    o_win = _nn(pw.astype(bf), vwn_ref[0, 0, pl.ds(wstart, N_WIN_TILES * TQ), :]) \
        / jnp.sum(pw, axis=-1, keepdims=True)

    g = jax.nn.sigmoid(gate_ref[0])
    g_hi, g_lo = _split(g)
    gx = _nn(g_hi, gexp_ref[0]) + _nn(g_lo, gexp_ref[0])
    for s in range(2):
        h_even, h_odd = 2 * s, 2 * s + 1
        cmp2 = jnp.where(lo_half, o_cmp[h_even * TQ:(h_even + 1) * TQ], o_cmp[h_odd * TQ:(h_odd + 1) * TQ])
        sel2 = jnp.where(lo_half, o_sel[0][s * TQ:(s + 1) * TQ], o_sel[1][s * TQ:(s + 1) * TQ])
        win2 = jnp.where(lo_half, o_win[h_even * TQ:(h_even + 1) * TQ], o_win[h_odd * TQ:(h_odd + 1) * TQ])
        base = s * 3 * LANES
        o_ref[0, :, s * LANES:(s + 1) * LANES] = (gx[:, base:base + LANES] * cmp2
                                                  + gx[:, base + LANES:base + 2 * LANES] * sel2
                                                  + gx[:, base + 2 * LANES:base + 3 * LANES] * win2)


def _gate_expand():
    e = np.zeros((NSA_GROUPS, LANES, 2 * 3 * LANES), np.float32)
    for g in range(NSA_GROUPS):
        for s in range(2):
            for c in range(3):
                for half in range(2):
                    h = g * NSA_HPG + 2 * s + half
                    col = (s * 3 + c) * LANES + half * HEAD_DIM
                    e[g, 3 * h + c, col:col + HEAD_DIM] = 1.0
    return e


def _nsa(qb, kcmp, vcmp, kse, kso, vsl, kwn, vwn, gate, bias):
    B, S, _ = qb.shape
    nt = S // TQ
    ncmp = kcmp.shape[2]
    nsel = S // SEL_BLOCK
    bf = jnp.bfloat16
    cs = np.arange(ncmp) * CMP_STRIDE
    bs = np.arange(nsel) * SEL_BLOCK
    ov = ((cs[None, :] < bs[:, None] + SEL_BLOCK) & (cs[None, :] + CMP_LEN > bs[:, None])
          & (np.arange(ncmp)[None, :] < ncmp - 1)).astype(np.float32)
    grp = lambda n: pl.BlockSpec((1, 1, n, LANES), lambda b, g, i: (b, g, 0, 0))
    return pl.pallas_call(
        _nsa_body,
        grid=(B, NSA_GROUPS, nt),
        in_specs=[pl.BlockSpec((1, TQ, 2 * LANES), lambda b, g, i: (b, i, g)),
                  grp(ncmp), grp(ncmp), grp(S), grp(S), grp(S), grp(S), grp(S),
                  pl.BlockSpec((1, TQ, LANES), lambda b, g, i: (b, i, 0)),
                  pl.BlockSpec((nsel, ncmp), lambda b, g, i: (0, 0)),
                  pl.BlockSpec((TQ, TQ), lambda b, g, i: (0, 0)),
                  pl.BlockSpec((1, LANES, 6 * LANES), lambda b, g, i: (g, 0, 0)),
                  pl.BlockSpec((NSA_HPG, _N_TILE_KINDS, TQ, TQ), lambda b, g, i: (2 + g, 0, 0, 0))],
        out_specs=pl.BlockSpec((1, TQ, 2 * LANES), lambda b, g, i: (b, i, g)),
        out_shape=jax.ShapeDtypeStruct((B, S, NSA_HEADS * HEAD_DIM), jnp.float32),
        scratch_shapes=[pltpu.VMEM((nsel, TQ), jnp.int32)] + [pltpu.VMEM((NSA_HPG * TQ, LANES), jnp.float32)] * 3,
        compiler_params=pltpu.CompilerParams(dimension_semantics=("parallel", "parallel", "arbitrary"),
                                             vmem_limit_bytes=VMEM_LIMIT),
        name="nsa",
    )(qb, kcmp, vcmp, kse, kso, vsl, kwn, vwn, gate, jnp.asarray(ov, bf), _eye(TQ),
      jnp.asarray(_gate_expand(), bf), bias)


def _out_proj_body(x_ref, nw_ref, oa_ref, ob_ref, wz_ref, wa_ref, wb_ref, wo_ref, out_ref):
    x = x_ref[0]
    ms = jnp.mean(x * x, axis=-1, keepdims=True)
    h = (x * lax.rsqrt(ms + EPS) * nw_ref[...]).astype(jnp.bfloat16)
    z = _nn(h, wz_ref[:, 0:1024])
    ya = (oa_ref[0] * jax.nn.silu(z[:, 0:512])).astype(jnp.bfloat16)
    yb = (ob_ref[0] * jax.nn.silu(z[:, 512:1024])).astype(jnp.bfloat16)
    gm = jax.nn.sigmoid(_nn(h, wz_ref[:, 1024:3072]))
    merged = gm[:, 0:D_MODEL] * _nn(ya, wa_ref[...]) + gm[:, D_MODEL:] * _nn(yb, wb_ref[...])
    out_ref[0] = x + _nn(merged.astype(jnp.bfloat16), wo_ref[...])


def _out_proj(x, norm_w, oa, ob, wz, wa, wb, wo):
    B, S, _ = x.shape
    nw = norm_w.reshape(1, D_MODEL)
    tok = lambda w: pl.BlockSpec((1, TQ, w), lambda b, i: (b, i, 0))
    full = lambda a: pl.BlockSpec(a.shape, lambda b, i: (0,) * a.ndim)
    return pl.pallas_call(
        _out_proj_body,
        grid=(B, S // TQ),
        in_specs=[tok(D_MODEL), full(nw), tok(512), tok(512), full(wz), full(wa), full(wb), full(wo)],
        out_specs=tok(D_MODEL),
        out_shape=jax.ShapeDtypeStruct((B, S, D_MODEL), jnp.float32),
        compiler_params=pltpu.CompilerParams(dimension_semantics=("parallel", "parallel"),
                                             vmem_limit_bytes=VMEM_LIMIT),
        name="out_proj",
    )(x, nw, oa, ob, wz, wa, wb, wo)


def _pack_in_weights(w_in, q_norm_a, k_norm_a, q_norm_b, k_norm_sel, k_norm_win):
    bf = jnp.bfloat16
    w1 = jnp.concatenate([w_in[:, 0:1536], w_in[:, 2048:2560], w_in[:, 2560:3328],
                          w_in[:, 3328:3352], jnp.zeros((D_MODEL, LANES - 3 * NSA_HEADS), w_in.dtype)], axis=1)
    wz = jnp.concatenate([w_in[:, 1536:2048], w_in[:, 3352:3864], w_in[:, 3864:5912]], axis=1)
    ones = lambda n: jnp.ones((n,), jnp.float32)
    gain = jnp.concatenate([jnp.tile(q_norm_a, MOBA_HEADS), jnp.tile(k_norm_a, MOBA_HEADS), ones(512),
                            jnp.tile(q_norm_b, NSA_HEADS), ones(256), jnp.tile(k_norm_sel, NSA_GROUPS), ones(128),
                            jnp.tile(k_norm_win, NSA_GROUPS), ones(256)]).reshape(1, _N1)
    return w1.astype(bf), wz.astype(bf), gain


def kernel(x, norm_w, w_in, q_norm_a, k_norm_a, q_norm_b, k_norm_cmp, k_norm_sel, k_norm_win, cmp_pos_k, cmp_w1_k, cmp_w2_k, cmp_pos_v, cmp_w1_v, cmp_w2_v, rel_bias, w_branch_a, w_branch_b, w_out):
    bf = jnp.bfloat16
    w1, wz, gain = _pack_in_weights(w_in[0], q_norm_a[0], k_norm_a[0], q_norm_b[0], k_norm_sel[0], k_norm_win[0])
    bias = _bias_tiles(rel_bias)
    qa, kae, kao, va, qb, kmean, kc, vc, kse, kso, vsl, kwn, vwn, gate = _in_proj(x, norm_w[0], w1, gain)
    kcmp, vcmp = _compress(kc, vc, cmp_pos_k[0], cmp_w1_k[0], cmp_w2_k[0],
                           cmp_pos_v[0], cmp_w1_v[0], cmp_w2_v[0], k_norm_cmp[0])
    oa = _moba(qa, kae, kao, va, kmean[:, :, 0, :], bias)
    ob = _nsa(qb, kcmp, vcmp, kse, kso, vsl, kwn, vwn, gate, bias)
    return _out_proj(x, norm_w[0], oa, ob, wz, w_branch_a[0].astype(bf), w_branch_b[0].astype(bf),
                     w_out[0].astype(bf))
```

```python
import math

import jax
import jax.numpy as jnp
import numpy as np
from jax import lax
from jax.experimental import pallas as pl
from jax.experimental.pallas import tpu as pltpu

D_MODEL = 1024
HEAD_DIM = 64
MOBA_HEADS = 8
NSA_HEADS = 8
NSA_GROUPS = 2
NSA_HPG = NSA_HEADS // NSA_GROUPS
MOBA_BLOCK = 256
MOBA_TOPK = 3
CMP_LEN = 32
CMP_STRIDE = 16
CMP_HIDDEN = 256
SEL_BLOCK = 64
SEL_TOPN = 16
WINDOW = 512
NUM_BUCKETS = 32
MAX_DISTANCE = 1024
EPS = 1e-6
NEG = -1e30

LANES = 128
TQ = 256
TK = 2 * TQ
TR = 2 * TQ
TM = 2 * TQ
N_FAR = 5
N_WIN_TILES = WINDOW // TQ + 1
VMEM_LIMIT = 56 * 1024 * 1024
LOG2E = math.log2(math.e)
QSCALE = LOG2E / math.sqrt(HEAD_DIM)

_K_WIN = N_FAR + 1
_K_MASKED = _K_WIN + N_WIN_TILES
_N_TILE_KINDS = _K_MASKED + 1

_C_QA, _C_KA, _C_VA, _C_QB = 0, 512, 1024, 1536
_C_KC, _C_VC, _C_KSL, _C_VSL, _C_KWN, _C_VWN, _C_GATE = 2048, 2176, 2304, 2432, 2560, 2688, 2816
_N1 = 2944


def _nt(a, b):
    return lax.dot_general(a, b, (((1,), (1,)), ((), ())), preferred_element_type=jnp.float32)


def _nn(a, b):
    return jnp.dot(a, b, preferred_element_type=jnp.float32)


def _split(x):
    hi = x.astype(jnp.bfloat16)
    lo = (x - hi.astype(jnp.float32)).astype(jnp.bfloat16)
    return hi, lo


def _bucket_table(n):
    d = np.arange(n)
    nf = np.maximum(d, NUM_BUCKETS // 2).astype(np.float64)
    large = NUM_BUCKETS // 2 + (np.log(nf / (NUM_BUCKETS // 2)) / math.log(MAX_DISTANCE / (NUM_BUCKETS // 2))
                                * (NUM_BUCKETS - NUM_BUCKETS // 2)).astype(np.int64)
    return np.where(d < NUM_BUCKETS // 2, d, np.minimum(large, NUM_BUCKETS - 1))


def _bias_onehot():
    oh = np.zeros((40, _N_TILE_KINDS * 2 * TQ), np.float32)
    buckets = _bucket_table((N_FAR + 2) * TQ + 1)
    for t in range(_N_TILE_KINDS):
        window = _K_WIN <= t < _K_MASKED
        delta = t - _K_WIN if window else t
        for u in range(2 * TQ):
            d = TQ * delta + TQ - u
            masked = t == _K_MASKED or d < 0 or (window and d >= WINDOW)
            oh[NUM_BUCKETS if masked else buckets[d], t * 2 * TQ + u] = 1.0
    return oh


def _bias_body(relt_ref, oh_ref, out_ref):
    g = jnp.dot(jnp.broadcast_to(relt_ref[0], (8, 40)), oh_ref[...], preferred_element_type=jnp.float32,
                precision=lax.Precision.HIGHEST)
    for t in range(_N_TILE_KINDS):
        full = jnp.broadcast_to(g[0:1, t * 2 * TQ:(t + 1) * 2 * TQ], (TQ, 2 * TQ))
        rolled = pltpu.roll(full, TQ, 1, stride=1, stride_axis=0)
        out_ref[0, t] = rolled[:, :TQ]


def _bias_tiles(rel_bias):
    nh = MOBA_HEADS + NSA_HEADS
    relt = jnp.concatenate([rel_bias.T.astype(jnp.float32) * LOG2E, jnp.full((nh, 1), NEG, jnp.float32),
                            jnp.zeros((nh, 7), jnp.float32)], axis=1).reshape(nh, 1, 40)
    oh = jnp.asarray(_bias_onehot())
    return pl.pallas_call(
        _bias_body,
        grid=(nh,),
        in_specs=[pl.BlockSpec((1, 1, 40), lambda h: (h, 0, 0)), pl.BlockSpec(oh.shape, lambda h: (0, 0))],
        out_specs=pl.BlockSpec((1, _N_TILE_KINDS, TQ, TQ), lambda h: (h, 0, 0, 0)),
        out_shape=jax.ShapeDtypeStruct((nh, _N_TILE_KINDS, TQ, TQ), jnp.float32),
        compiler_params=pltpu.CompilerParams(dimension_semantics=("parallel",), vmem_limit_bytes=VMEM_LIMIT),
        name="bias_tiles",
    )(relt, oh)


def _seg_norm(x, lo_half, gain):
    x2 = x * x
    ss_lo = jnp.sum(jnp.where(lo_half, x2, 0.0), axis=-1, keepdims=True)
    ss_hi = jnp.sum(jnp.where(lo_half, 0.0, x2), axis=-1, keepdims=True)
    ss = jnp.where(lo_half, ss_lo, ss_hi)
    return x * lax.rsqrt(ss * (1.0 / HEAD_DIM) + EPS) * gain


def _in_proj_body(x_ref, nw_ref, w_ref, gain_ref,
                  qa_ref, kae_ref, kao_ref, va_ref, qb_ref, kmean_ref, kc_ref, vc_ref,
                  kse_ref, kso_ref, vsl_ref, kwn_ref, vwn_ref, gate_ref):
    i = pl.program_id(1)
    x = x_ref[0]
    ms = jnp.mean(x * x, axis=-1, keepdims=True)
    h = (x * lax.rsqrt(ms + EPS) * nw_ref[...]).astype(jnp.bfloat16)
    bf = jnp.bfloat16
    lane = lax.broadcasted_iota(jnp.int32, (TM, LANES), 1)
    row = lax.broadcasted_iota(jnp.int32, (TM, LANES), 0)
    lo_half = lane < HEAD_DIM
    blk = i * (TM // MOBA_BLOCK) + lax.shift_right_logical(row, MOBA_BLOCK.bit_length() - 1)
    sblk = i * (TM // SEL_BLOCK) + lax.shift_right_logical(row, SEL_BLOCK.bit_length() - 1)

    def normed(col, width, scale=None):
        p = _nn(h, w_ref[:, col:col + width])
        outs = []
        for s in range(width // LANES):
            c = col + s * LANES
            y = _seg_norm(p[:, s * LANES:(s + 1) * LANES], lo_half, gain_ref[:, c:c + LANES])
            outs.append(y if scale is None else y * scale)
        return outs

    qa = normed(_C_QA, 512, QSCALE)
    for s in range(4):
        qa_ref[0, :, s * LANES:(s + 1) * LANES] = qa[s].astype(bf)
    ka = normed(_C_KA, 512)
    oh_hi = jnp.where(lane == HEAD_DIM + blk, 1.0, 0.0)
    oh_lo = jnp.where(lane == blk, 1.0, 0.0)
    for s in range(4):
        kae_ref[0, :, s * LANES:(s + 1) * LANES] = jnp.where(lo_half, ka[s], oh_hi).astype(bf)
        kao_ref[0, :, s * LANES:(s + 1) * LANES] = jnp.where(lo_half, oh_lo, ka[s]).astype(bf)
        for j in range(TM // MOBA_BLOCK):
            kmean_ref[0, j, :, s * LANES:(s + 1) * LANES] = jnp.broadcast_to(
                jnp.mean(ka[s][j * MOBA_BLOCK:(j + 1) * MOBA_BLOCK], axis=0, keepdims=True), (8, LANES))
    va_ref[0] = _nn(h, w_ref[:, _C_VA:_C_VA + 512]).astype(bf)
    qb = normed(_C_QB, 512, QSCALE)
    for s in range(4):
        qb_ref[0, :, s * LANES:(s + 1) * LANES] = qb[s].astype(bf)

    rest = _nn(h, w_ref[:, _C_KC:_N1])

    def slab(c):
        return rest[:, c - _C_KC:c - _C_KC + LANES]

    kc_ref[0] = slab(_C_KC).astype(bf)
    vc_ref[0] = slab(_C_VC).astype(bf)

    def dup_store(ref, y):
        r = pltpu.roll(y, HEAD_DIM, 1)
        ref[0, 0] = jnp.where(lo_half, y, r).astype(bf)
        ref[0, 1] = jnp.where(lo_half, r, y).astype(bf)

    y = _seg_norm(slab(_C_KSL), lo_half, gain_ref[:, _C_KSL:_C_KSL + LANES])
    r = pltpu.roll(y, HEAD_DIM, 1)
    oh = jnp.where((lane & (HEAD_DIM - 1)) == sblk, 1.0, 0.0)
    kse_ref[0, 0] = jnp.where(lo_half, y, oh).astype(bf)
    kso_ref[0, 0] = jnp.where(lo_half, oh, r).astype(bf)
    kse_ref[0, 1] = jnp.where(lo_half, r, oh).astype(bf)
    kso_ref[0, 1] = jnp.where(lo_half, oh, y).astype(bf)
    dup_store(vsl_ref, slab(_C_VSL))
    dup_store(kwn_ref, _seg_norm(slab(_C_KWN), lo_half, gain_ref[:, _C_KWN:_C_KWN + LANES]))
    dup_store(vwn_ref, slab(_C_VWN))
    gate_ref[0] = slab(_C_GATE)


def _in_proj(x, norm_w, w1, gain_row):
    B, S, _ = x.shape
    nt = S // TQ
    bf = jnp.bfloat16
    tok = lambda w, dt: jax.ShapeDtypeStruct((B, S, w), dt)
    grp = jax.ShapeDtypeStruct((B, NSA_GROUPS, S, LANES), bf)
    tok_spec = lambda w: pl.BlockSpec((1, TM, w), lambda b, i: (b, i, 0))
    grp_spec = pl.BlockSpec((1, NSA_GROUPS, TM, LANES), lambda b, i: (b, 0, i, 0))
    full = lambda a: pl.BlockSpec(a.shape, lambda b, i: (0,) * a.ndim)
    nw = norm_w.reshape(1, D_MODEL)
    return pl.pallas_call(
        _in_proj_body,
        grid=(B, S // TM),
        in_specs=[tok_spec(D_MODEL), full(nw), full(w1), full(gain_row)],
        out_specs=[tok_spec(512), tok_spec(512), tok_spec(512), tok_spec(512), tok_spec(512),
                   pl.BlockSpec((1, TM // MOBA_BLOCK, 8, 512), lambda b, i: (b, i, 0, 0)),
                   tok_spec(LANES), tok_spec(LANES),
                   grp_spec, grp_spec, grp_spec, grp_spec, grp_spec, tok_spec(LANES)],
        out_shape=[tok(512, bf), tok(512, bf), tok(512, bf), tok(512, bf), tok(512, bf),
                   jax.ShapeDtypeStruct((B, nt, 8, 512), jnp.float32),
                   tok(LANES, bf), tok(LANES, bf), grp, grp, grp, grp, grp, tok(LANES, jnp.float32)],
        compiler_params=pltpu.CompilerParams(dimension_semantics=("parallel", "parallel"),
                                             vmem_limit_bytes=VMEM_LIMIT),
        name="in_proj",
    )(x, nw, w1, gain_row)


def _gelu_tanh(x):
    return 0.5 * x * (1.0 + jnp.tanh(math.sqrt(2.0 / math.pi) * (x + 0.044715 * (x * x * x))))


def _compress_body(xk_ref, xv_ref, w1k_ref, w1v_ref, pk_ref, pv_ref, w1kn_ref, w1vn_ref,
                   w2k_ref, w2v_ref, gk_ref, kcmp_ref, vcmp_ref):
    nchunk = xk_ref.shape[1]

    def branch(x_ref, w1_ref, pos_ref, w1n_ref, w2_ref, gain):
        ab = _nn(x_ref[0], w1_ref[...])
        posw = _nn(jnp.broadcast_to(pos_ref[...], (8, CMP_LEN * HEAD_DIM)), w1n_ref[...])[0:1]
        outs = []
        for g in range(NSA_GROUPS):
            top = ab[:, (2 * g) * CMP_HIDDEN:(2 * g + 1) * CMP_HIDDEN]
            bot = ab[:, (2 * g + 1) * CMP_HIDDEN:(2 * g + 2) * CMP_HIDDEN]
            hid = top + pltpu.roll(bot, nchunk - 1, 0) + posw
            y = _nn(_gelu_tanh(hid).astype(jnp.bfloat16), w2_ref[...])
            if gain is not None:
                ms = jnp.sum(y * y, axis=-1, keepdims=True) * (0.5 / HEAD_DIM)
                y = y * lax.rsqrt(ms + EPS) * gain
            outs.append(y.astype(jnp.bfloat16))
        return outs

    k0, k1 = branch(xk_ref, w1k_ref, pk_ref, w1kn_ref, w2k_ref, gk_ref[...])
    kcmp_ref[0, 0], kcmp_ref[0, 1] = k0, k1
    v0, v1 = branch(xv_ref, w1v_ref, pv_ref, w1vn_ref, w2v_ref, None)
    vcmp_ref[0, 0], vcmp_ref[0, 1] = v0, v1


def _pack_w1(w1):
    half = CMP_STRIDE * HEAD_DIM
    cols = []
    for g in range(NSA_GROUPS):
        for part in range(2):
            w = w1[part * half:(part + 1) * half].reshape(CMP_STRIDE, HEAD_DIM, CMP_HIDDEN)
            z = jnp.zeros_like(w)
            pieces = [w, z] if g == 0 else [z, w]
            cols.append(jnp.concatenate(pieces, axis=1).reshape(CMP_STRIDE * LANES, CMP_HIDDEN))
    return jnp.concatenate(cols, axis=1).astype(jnp.bfloat16)


def _compress(kc, vc, pos_k, w1_k, w2_k, pos_v, w1_v, w2_v, gain_cmp):
    B, S, _ = kc.shape
    nchunk = S // CMP_STRIDE
    bf = jnp.bfloat16
    xk = kc.reshape(B, nchunk, CMP_STRIDE * LANES)
    xv = vc.reshape(B, nchunk, CMP_STRIDE * LANES)
    args = [xk, xv, _pack_w1(w1_k), _pack_w1(w1_v),
            pos_k.reshape(1, -1).astype(bf), pos_v.reshape(1, -1).astype(bf),
            w1_k.astype(bf), w1_v.astype(bf),
            jnp.concatenate([w2_k, w2_k], axis=1).astype(bf), jnp.concatenate([w2_v, w2_v], axis=1).astype(bf),
            jnp.concatenate([gain_cmp, gain_cmp]).reshape(1, LANES)]
    x_spec = pl.BlockSpec((1, nchunk, CMP_STRIDE * LANES), lambda b: (b, 0, 0))
    full = lambda a: pl.BlockSpec(a.shape, lambda b: (0,) * a.ndim)
    out = jax.ShapeDtypeStruct((B, NSA_GROUPS, nchunk, LANES), bf)
    o_spec = pl.BlockSpec((1, NSA_GROUPS, nchunk, LANES), lambda b: (b, 0, 0, 0))
    return pl.pallas_call(
        _compress_body,
        grid=(B,),
        in_specs=[x_spec, x_spec] + [full(a) for a in args[2:]],
        out_specs=[o_spec, o_spec],
        out_shape=[out, out],
        compiler_params=pltpu.CompilerParams(dimension_semantics=("parallel",), vmem_limit_bytes=VMEM_LIMIT),
        name="compress",
    )(*args)


def _lane_partial_sum(p):
    t = p[:, 0:LANES]
    for c in range(1, p.shape[1] // LANES):
        t = t + p[:, c * LANES:(c + 1) * LANES]
    return t


def _softmax_step(s, v, m_ref, l_ref, acc_ref, first):
    blocks = [s[:, c * LANES:(c + 1) * LANES] for c in range(s.shape[1] // LANES)]
    bmax = blocks[0]
    for blk in blocks[1:]:
        bmax = jnp.maximum(bmax, blk)
    m_new = jnp.broadcast_to(jnp.max(bmax, axis=-1, keepdims=True), m_ref.shape)
    if not first:
        m_old = m_ref[...]
        m_new = jnp.maximum(m_old, m_new)
        alpha = jnp.exp2(m_old - m_new)
    p = jnp.concatenate([jnp.exp2(blk - m_new) for blk in blocks], axis=1)
    pb = p.astype(jnp.bfloat16)
    half = s.shape[0] // 2
    pv = jnp.concatenate([_nn(pb[0:half], v), _nn(pb[half:], v)], axis=0)
    if first:
        l_ref[...] = _lane_partial_sum(p)
        acc_ref[...] = pv
    else:
        l_ref[...] = alpha * l_ref[...] + _lane_partial_sum(p)
        acc_ref[...] = alpha * acc_ref[...] + pv
    m_ref[...] = m_new


def _finish(l_ref, acc_ref):
    return acc_ref[...] / jnp.sum(l_ref[...], axis=-1, keepdims=True)


def _tile_kind(delta):
    return jnp.where(delta >= 0, jnp.minimum(delta, N_FAR), _K_MASKED)


def _chunk_kinds(qi, c):
    d0 = qi - 2 * c
    kind0 = jnp.minimum(d0, N_FAR)
    kind1 = jnp.where(d0 >= 1, jnp.minimum(d0 - 1, N_FAR), _K_MASKED)
    return kind0, kind1


def _moba_body(q_ref, ke_ref, ko_ref, v_ref, kmean_ref, eye_ref, bias_ref, o_ref, m_ref, l_ref, acc_ref):
    i = pl.program_id(2)
    nblk = kmean_ref.shape[1]
    lane = lax.broadcasted_iota(jnp.int32, (TR, LANES), 1)
    lo_half = lane < HEAD_DIM
    lane16 = lax.broadcasted_iota(jnp.int32, (nblk, LANES), 1)
    jrow = lax.broadcasted_iota(jnp.int32, (nblk, TR), 0)
    qblk = 2 * i + jnp.where(lax.broadcasted_iota(jnp.int32, (nblk, TR), 1) >= TQ, 1, 0)
    q2 = q_ref[0]
    kmean2 = kmean_ref[0]
    flags = []
    for hh in range(2):
        inhead16 = (lane16 < HEAD_DIM) if hh == 0 else (lane16 >= HEAD_DIM)
        km_hi, km_lo = _split(jnp.where(inhead16, kmean2, 0.0))
        sc = jnp.where(jrow < qblk, _nt(km_hi, q2) + _nt(km_lo, q2), -jnp.inf)
        flag = jnp.where(jrow == qblk, 0.0, 1.0)
        for _ in range(MOBA_TOPK):
            best = jnp.max(sc, axis=0, keepdims=True)
            first = jnp.min(jnp.where(sc == best, jrow, nblk), axis=0, keepdims=True)
            hit = jrow == first
            flag = jnp.where(hit, 0.0, flag)
            sc = jnp.where(hit, -jnp.inf, sc)
        flags.append(flag)
    pad = jnp.zeros((HEAD_DIM - nblk, TR), jnp.float32)
    placed = jnp.concatenate([flags[1], pad, flags[0], pad], axis=0).astype(jnp.bfloat16)
    maskpart = _nt(eye_ref[...], placed) * NEG
    q2f = q2.astype(jnp.float32)
    zero = jnp.zeros((TR, LANES), jnp.bfloat16)
    qq = jnp.concatenate(
        [jnp.concatenate([jnp.where(lo_half, q2f, maskpart).astype(jnp.bfloat16), zero], axis=1),
         jnp.concatenate([zero, jnp.where(lo_half, maskpart, q2f).astype(jnp.bfloat16)], axis=1)], axis=0)

    def step(c, first):
        start = pl.multiple_of(c * TK, TK)
        kk = jnp.concatenate([ke_ref[0, pl.ds(start, TK), :], ko_ref[0, pl.ds(start, TK), :]], axis=1)
        d = 2 * (i - c)
        b = jnp.concatenate(
            [jnp.concatenate([bias_ref[hh, _tile_kind(d + a - t)] for t in range(2)], axis=1)
             for hh in range(2) for a in range(2)], axis=0)
        _softmax_step(_nt(qq, kk) + b, v_ref[0, pl.ds(start, TK), :], m_ref, l_ref, acc_ref, first)

    step(i, True)

    def body(c, carry):
        step(c, False)
        return carry

    lax.fori_loop(0, i, body, 0)
    o = _finish(l_ref, acc_ref)
    o_ref[0] = jnp.where(lo_half, o[0:TR], o[TR:2 * TR])


def _eye(n):
    return jnp.asarray(np.eye(n, dtype=np.float32), jnp.bfloat16)


def _moba(qa, kae, kao, va, kmean, bias):
    B, S, _ = qa.shape
    nt = S // TQ
    npair = MOBA_HEADS // 2
    kv_spec = pl.BlockSpec((1, S, LANES), lambda b, p, i: (b, 0, p))
    return pl.pallas_call(
        _moba_body,
        grid=(B, npair, S // TR),
        in_specs=[pl.BlockSpec((1, TR, LANES), lambda b, p, i: (b, i, p)),
                  kv_spec, kv_spec, kv_spec,
                  pl.BlockSpec((1, nt, LANES), lambda b, p, i: (b, 0, p)),
                  pl.BlockSpec((TR, TR), lambda b, p, i: (0, 0)),
                  pl.BlockSpec((2, _N_TILE_KINDS, TQ, TQ), lambda b, p, i: (p, 0, 0, 0))],
        out_specs=pl.BlockSpec((1, TR, LANES), lambda b, p, i: (b, i, p)),
        out_shape=jax.ShapeDtypeStruct((B, S, MOBA_HEADS * HEAD_DIM), jnp.float32),
        scratch_shapes=[pltpu.VMEM((2 * TR, LANES), jnp.float32)] * 3,
        compiler_params=pltpu.CompilerParams(dimension_semantics=("parallel", "parallel", "arbitrary"),
                                             vmem_limit_bytes=VMEM_LIMIT),
        name="moba",
    )(qa, kae, kao, va, kmean, _eye(TR), bias)


def _nsa_body(q_ref, kcmp_ref, vcmp_ref, kse_ref, kso_ref, vsl_ref, kwn_ref, vwn_ref, gate_ref, ov_ref, eye_ref,
              gexp_ref, bias_ref, o_ref, imp_ref, m_ref, l_ref, acc_ref):
    i = pl.program_id(2)
    ncmp = kcmp_ref.shape[2]
    nsel = ov_ref.shape[0]
    R = NSA_HPG * TR
    bf = jnp.bfloat16
    lane = lax.broadcasted_iota(jnp.int32, (TR, LANES), 1)
    lo_half = lane < HEAD_DIM
    qf = [q_ref[0, :, 0:LANES].astype(jnp.float32), q_ref[0, :, LANES:2 * LANES].astype(jnp.float32)]
    q4 = jnp.concatenate([jnp.where(lo_half if hh % 2 == 0 else ~lo_half, qf[hh // 2], 0.0).astype(bf)
                          for hh in range(NSA_HPG)], axis=0)

    tpos = i * TR + lax.broadcasted_iota(jnp.int32, (TR, ncmp), 0)
    cidx = lax.broadcasted_iota(jnp.int32, (TR, ncmp), 1)
    cbias1 = jnp.where((cidx * CMP_STRIDE + (CMP_LEN - 1) <= tpos) & (cidx < ncmp - 1), 0.0, NEG)
    lc = _nt(q4, kcmp_ref[0, 0]) + jnp.concatenate([cbias1] * NSA_HPG, axis=0)
    cblocks = [lc[:, c * LANES:(c + 1) * LANES] for c in range(ncmp // LANES)]
    cmax = cblocks[0]
    for blk in cblocks[1:]:
        cmax = jnp.maximum(cmax, blk)
    cm = jnp.maximum(jnp.broadcast_to(jnp.max(cmax, axis=-1, keepdims=True), (R, LANES)), 0.5 * NEG)
    ce = [jnp.exp2(blk - cm) for blk in cblocks]
    cden = ce[0]
    for eb in ce[1:]:
        cden = cden + eb
    cden = jnp.broadcast_to(jnp.sum(cden, axis=-1, keepdims=True), (R, LANES))
    crcp = 1.0 / jnp.where(cden > 0.0, cden, 1.0)
    pc = jnp.concatenate([eb * crcp for eb in ce], axis=1)
    pcb = pc.astype(bf)
    o_cmp = jnp.concatenate([_nn(pcb[0:R // 2], vcmp_ref[0, 0]), _nn(pcb[R // 2:], vcmp_ref[0, 0])], axis=0)
    psum = pc[0:TR] + pc[TR:2 * TR] + pc[2 * TR:3 * TR] + pc[3 * TR:4 * TR]
    p_hi, p_lo = _split(psum)
    ov = ov_ref[...]
    imp = _nt(ov, p_hi) + _nt(ov, p_lo)

    jrow = lax.broadcasted_iota(jnp.int32, (nsel, TR), 0)
    qblk = lax.shift_right_logical(i * TR + lax.broadcasted_iota(jnp.int32, (nsel, TR), 1),
                                   SEL_BLOCK.bit_length() - 1)
    cand = (jrow >= 1) & (jrow < qblk)
    keys = jnp.where(cand, lax.bitcast_convert_type(imp + 0.0, jnp.int32), -1)
    imp_ref[...] = keys

    def rank_body(jp, cnt):
        r = imp_ref[pl.ds(jp, 1), :] + jnp.where(jrow > jp, 1, 0)
        return cnt + jnp.where(r > keys, 1, 0)

    cnt = lax.fori_loop(1, (i + 1) * (TR // SEL_BLOCK) - 1, rank_body, jnp.zeros((nsel, TR), jnp.int32))
    keep = (jrow == 0) | (jrow == qblk) | (cand & (cnt < SEL_TOPN - 2))
    flag = jnp.where(keep, 0.0, 1.0)
    if nsel < HEAD_DIM:
        flag = jnp.concatenate([flag, jnp.zeros((HEAD_DIM - nsel, TR), jnp.float32)], axis=0)
    maskpart = _nt(eye_ref[...], jnp.concatenate([flag, flag], axis=0).astype(bf)) * NEG

    zero = jnp.zeros((2 * TR, LANES), bf)
    qa2 = [jnp.concatenate([jnp.where(lo_half if par == 0 else ~lo_half, qf[s], maskpart).astype(bf)
                            for s in range(2)], axis=0) for par in range(2)]
    qq = jnp.concatenate([jnp.concatenate([qa2[0], zero], axis=1),
                          jnp.concatenate([zero, qa2[1]], axis=1)], axis=0)
    sel_heads = (0, 2, 1, 3)

    def sel_step(c, first):
        start = pl.multiple_of(c * TK, TK)
        kk = jnp.concatenate([kse_ref[0, 0, pl.ds(start, TK), :], kso_ref[0, 0, pl.ds(start, TK), :]], axis=1)
        d = 2 * (i - c)
        b = jnp.concatenate(
            [jnp.concatenate([bias_ref[h, _tile_kind(d + a - t)] for t in range(2)], axis=1)
             for h in sel_heads for a in range(2)], axis=0)
        _softmax_step(_nt(qq, kk) + b, vsl_ref[0, 0, pl.ds(start, TK), :], m_ref, l_ref, acc_ref, first)

    sel_step(i, True)

    def sel_body(c, carry):
        sel_step(c, False)
        return carry

    lax.fori_loop(0, i, sel_body, 0)
    o_all = _finish(l_ref, acc_ref)
    o_sel = [o_all[0:2 * TR], o_all[2 * TR:4 * TR]]

    o_win = []
    for a in range(2):
        qt = 2 * i + a
        t0 = jnp.maximum(qt - (N_WIN_TILES - 1), 0)
        wstart = pl.multiple_of(t0 * TQ, TQ)
        qh = jnp.concatenate([q4[hh * TR + a * TQ:hh * TR + (a + 1) * TQ] for hh in range(NSA_HPG)], axis=0)
        kinds = []
        for r in range(N_WIN_TILES):
            dd = qt - (t0 + r)
            kinds.append(jnp.where(dd >= 0, _K_WIN + dd, _K_MASKED))
        bw = jnp.concatenate([jnp.concatenate([bias_ref[hh, k] for k in kinds], axis=1)
                              for hh in range(NSA_HPG)], axis=0)
        sw = _nt(qh, kwn_ref[0, 0, pl.ds(wstart, N_WIN_TILES * TQ), :]) + bw
        wm, wl, wacc = m_ref.at[0:NSA_HPG * TQ], l_ref.at[0:NSA_HPG * TQ], acc_ref.at[0:NSA_HPG * TQ]
        _softmax_step(sw, vwn_ref[0, 0, pl.ds(wstart, N_WIN_TILES * TQ), :], wm, wl, wacc, True)
        o_win.append(_finish(wl, wacc))

    g = jax.nn.sigmoid(gate_ref[0])
    g_hi, g_lo = _split(g)
    gx = _nn(g_hi, gexp_ref[0]) + _nn(g_lo, gexp_ref[0])
    for s in range(2):
        h_even, h_odd = 2 * s, 2 * s + 1
        cmp2 = jnp.where(lo_half, o_cmp[h_even * TR:(h_even + 1) * TR], o_cmp[h_odd * TR:(h_odd + 1) * TR])
        sel2 = jnp.where(lo_half, o_sel[0][s * TR:(s + 1) * TR], o_sel[1][s * TR:(s + 1) * TR])
        win2 = jnp.where(lo_half,
                         jnp.concatenate([o_win[a][h_even * TQ:(h_even + 1) * TQ] for a in range(2)], axis=0),
                         jnp.concatenate([o_win[a][h_odd * TQ:(h_odd + 1) * TQ] for a in range(2)], axis=0))
        base = s * 3 * LANES
        o_ref[0, :, s * LANES:(s + 1) * LANES] = (gx[:, base:base + LANES] * cmp2
                                                  + gx[:, base + LANES:base + 2 * LANES] * sel2
                                                  + gx[:, base + 2 * LANES:base + 3 * LANES] * win2)


def _gate_expand():
    e = np.zeros((NSA_GROUPS, LANES, 2 * 3 * LANES), np.float32)
    for g in range(NSA_GROUPS):
        for s in range(2):
            for c in range(3):
                for half in range(2):
                    h = g * NSA_HPG + 2 * s + half
                    col = (s * 3 + c) * LANES + half * HEAD_DIM
                    e[g, 3 * h + c, col:col + HEAD_DIM] = 1.0
    return e


def _nsa(qb, kcmp, vcmp, kse, kso, vsl, kwn, vwn, gate, bias):
    B, S, _ = qb.shape
    nt = S // TQ
    ncmp = kcmp.shape[2]
    nsel = S // SEL_BLOCK
    bf = jnp.bfloat16
    cs = np.arange(ncmp) * CMP_STRIDE
    bs = np.arange(nsel) * SEL_BLOCK
    ov = ((cs[None, :] < bs[:, None] + SEL_BLOCK) & (cs[None, :] + CMP_LEN > bs[:, None])
          & (np.arange(ncmp)[None, :] < ncmp - 1)).astype(np.float32)
    grp = lambda n: pl.BlockSpec((1, 1, n, LANES), lambda b, g, i: (b, g, 0, 0))
    return pl.pallas_call(
        _nsa_body,
        grid=(B, NSA_GROUPS, S // TR),
        in_specs=[pl.BlockSpec((1, TR, 2 * LANES), lambda b, g, i: (b, i, g)),
                  grp(ncmp), grp(ncmp), grp(S), grp(S), grp(S), grp(S), grp(S),
                  pl.BlockSpec((1, TR, LANES), lambda b, g, i: (b, i, 0)),
                  pl.BlockSpec((nsel, ncmp), lambda b, g, i: (0, 0)),
                  pl.BlockSpec((TR, TR), lambda b, g, i: (0, 0)),
                  pl.BlockSpec((1, LANES, 6 * LANES), lambda b, g, i: (g, 0, 0)),
                  pl.BlockSpec((NSA_HPG, _N_TILE_KINDS, TQ, TQ), lambda b, g, i: (2 + g, 0, 0, 0),
                               pipeline_mode=pl.Buffered(1))],
        out_specs=pl.BlockSpec((1, TR, 2 * LANES), lambda b, g, i: (b, i, g)),
        out_shape=jax.ShapeDtypeStruct((B, S, NSA_HEADS * HEAD_DIM), jnp.float32),
        scratch_shapes=[pltpu.VMEM((nsel, TR), jnp.int32)] + [pltpu.VMEM((NSA_HPG * TR, LANES), jnp.float32)] * 3,
        compiler_params=pltpu.CompilerParams(dimension_semantics=("parallel", "parallel", "arbitrary"),
                                             vmem_limit_bytes=VMEM_LIMIT),
        name="nsa",
    )(qb, kcmp, vcmp, kse, kso, vsl, kwn, vwn, gate, jnp.asarray(ov, bf), _eye(TR),
      jnp.asarray(_gate_expand(), bf), bias)


def _out_proj_body(x_ref, nw_ref, oa_ref, ob_ref, wz_ref, wa_ref, wb_ref, wo_ref, out_ref):
    x = x_ref[0]
    ms = jnp.mean(x * x, axis=-1, keepdims=True)
    h = (x * lax.rsqrt(ms + EPS) * nw_ref[...]).astype(jnp.bfloat16)
    z = _nn(h, wz_ref[:, 0:1024])
    ya = (oa_ref[0] * jax.nn.silu(z[:, 0:512])).astype(jnp.bfloat16)
    yb = (ob_ref[0] * jax.nn.silu(z[:, 512:1024])).astype(jnp.bfloat16)
    gm = jax.nn.sigmoid(_nn(h, wz_ref[:, 1024:3072]))
    merged = gm[:, 0:D_MODEL] * _nn(ya, wa_ref[...]) + gm[:, D_MODEL:] * _nn(yb, wb_ref[...])
    out_ref[0] = x + _nn(merged.astype(jnp.bfloat16), wo_ref[...])


def _out_proj(x, norm_w, oa, ob, wz, wa, wb, wo):
    B, S, _ = x.shape
    nw = norm_w.reshape(1, D_MODEL)
    tok = lambda w: pl.BlockSpec((1, TM, w), lambda b, i: (b, i, 0))
    full = lambda a: pl.BlockSpec(a.shape, lambda b, i: (0,) * a.ndim)
    return pl.pallas_call(
        _out_proj_body,
        grid=(B, S // TM),
        in_specs=[tok(D_MODEL), full(nw), tok(512), tok(512), full(wz), full(wa), full(wb), full(wo)],
        out_specs=tok(D_MODEL),
        out_shape=jax.ShapeDtypeStruct((B, S, D_MODEL), jnp.float32),
        compiler_params=pltpu.CompilerParams(dimension_semantics=("parallel", "parallel"),
                                             vmem_limit_bytes=VMEM_LIMIT),
        name="out_proj",
    )(x, nw, oa, ob, wz, wa, wb, wo)


def _pack_in_weights(w_in, q_norm_a, k_norm_a, q_norm_b, k_norm_sel, k_norm_win):
    bf = jnp.bfloat16
    w1 = jnp.concatenate([w_in[:, 0:1536], w_in[:, 2048:2560], w_in[:, 2560:3328],
                          w_in[:, 3328:3352], jnp.zeros((D_MODEL, LANES - 3 * NSA_HEADS), w_in.dtype)], axis=1)
    wz = jnp.concatenate([w_in[:, 1536:2048], w_in[:, 3352:3864], w_in[:, 3864:5912]], axis=1)
    ones = lambda n: jnp.ones((n,), jnp.float32)
    gain = jnp.concatenate([jnp.tile(q_norm_a, MOBA_HEADS), jnp.tile(k_norm_a, MOBA_HEADS), ones(512),
                            jnp.tile(q_norm_b, NSA_HEADS), ones(256), jnp.tile(k_norm_sel, NSA_GROUPS), ones(128),
                            jnp.tile(k_norm_win, NSA_GROUPS), ones(256)]).reshape(1, _N1)
    return w1.astype(bf), wz.astype(bf), gain


def kernel(x, norm_w, w_in, q_norm_a, k_norm_a, q_norm_b, k_norm_cmp, k_norm_sel, k_norm_win, cmp_pos_k, cmp_w1_k, cmp_w2_k, cmp_pos_v, cmp_w1_v, cmp_w2_v, rel_bias, w_branch_a, w_branch_b, w_out):
    bf = jnp.bfloat16
    w1, wz, gain = _pack_in_weights(w_in[0], q_norm_a[0], k_norm_a[0], q_norm_b[0], k_norm_sel[0], k_norm_win[0])
    bias = _bias_tiles(rel_bias)
    qa, kae, kao, va, qb, kmean, kc, vc, kse, kso, vsl, kwn, vwn, gate = _in_proj(x, norm_w[0], w1, gain)
    kcmp, vcmp = _compress(kc, vc, cmp_pos_k[0], cmp_w1_k[0], cmp_w2_k[0],
                           cmp_pos_v[0], cmp_w1_v[0], cmp_w2_v[0], k_norm_cmp[0])
    oa = _moba(qa, kae, kao, va, kmean[:, :, 0, :], bias)
    ob = _nsa(qb, kcmp, vcmp, kse, kso, vsl, kwn, vwn, gate, bias)
    return _out_proj(x, norm_w[0], oa, ob, wz, w_branch_a[0].astype(bf), w_branch_b[0].astype(bf),
                     w_out[0].astype(bf))
```

```python
import math

import jax
import jax.numpy as jnp
import numpy as np
from jax import lax
from jax.experimental import pallas as pl
from jax.experimental.pallas import tpu as pltpu

D_MODEL = 1024
HEAD_DIM = 64
MOBA_HEADS = 8
NSA_HEADS = 8
NSA_GROUPS = 2
NSA_HPG = NSA_HEADS // NSA_GROUPS
MOBA_BLOCK = 256
MOBA_TOPK = 3
CMP_LEN = 32
CMP_STRIDE = 16
CMP_HIDDEN = 256
SEL_BLOCK = 64
SEL_TOPN = 16
WINDOW = 512
NUM_BUCKETS = 32
MAX_DISTANCE = 1024
EPS = 1e-6
NEG = -1e30

LANES = 128
TQ = 256
TK = 2 * TQ
TR = 2 * TQ
TM = 2 * TQ
N_FAR = 5
N_WIN_TILES = WINDOW // TQ + 1
VMEM_LIMIT = 56 * 1024 * 1024
LOG2E = math.log2(math.e)
QSCALE = LOG2E / math.sqrt(HEAD_DIM)

_K_WIN = N_FAR + 1
_K_MASKED = _K_WIN + N_WIN_TILES
_N_TILE_KINDS = _K_MASKED + 1

_C_QA, _C_KA, _C_VA, _C_QB = 0, 512, 1024, 1536
_C_KC, _C_VC, _C_KSL, _C_VSL, _C_KWN, _C_VWN, _C_GATE = 2048, 2176, 2304, 2432, 2560, 2688, 2816
_N1 = 2944


def _nt(a, b):
    return lax.dot_general(a, b, (((1,), (1,)), ((), ())), preferred_element_type=jnp.float32)


def _nn(a, b):
    return jnp.dot(a, b, preferred_element_type=jnp.float32)


def _split(x):
    hi = x.astype(jnp.bfloat16)
    lo = (x - hi.astype(jnp.float32)).astype(jnp.bfloat16)
    return hi, lo


def _bucket_table(n):
    d = np.arange(n)
    nf = np.maximum(d, NUM_BUCKETS // 2).astype(np.float64)
    large = NUM_BUCKETS // 2 + (np.log(nf / (NUM_BUCKETS // 2)) / math.log(MAX_DISTANCE / (NUM_BUCKETS // 2))
                                * (NUM_BUCKETS - NUM_BUCKETS // 2)).astype(np.int64)
    return np.where(d < NUM_BUCKETS // 2, d, np.minimum(large, NUM_BUCKETS - 1))


def _bias_onehot():
    oh = np.zeros((40, _N_TILE_KINDS * 2 * TQ), np.float32)
    buckets = _bucket_table((N_FAR + 2) * TQ + 1)
    for t in range(_N_TILE_KINDS):
        window = _K_WIN <= t < _K_MASKED
        delta = t - _K_WIN if window else t
        for u in range(2 * TQ):
            d = TQ * delta + TQ - u
            masked = t == _K_MASKED or d < 0 or (window and d >= WINDOW)
            oh[NUM_BUCKETS if masked else buckets[d], t * 2 * TQ + u] = 1.0
    return oh


def _bias_body(relt_ref, oh_ref, out_ref):
    g = jnp.dot(jnp.broadcast_to(relt_ref[0], (8, 40)), oh_ref[...], preferred_element_type=jnp.float32,
                precision=lax.Precision.HIGHEST)
    for t in range(_N_TILE_KINDS):
        full = jnp.broadcast_to(g[0:1, t * 2 * TQ:(t + 1) * 2 * TQ], (TQ, 2 * TQ))
        rolled = pltpu.roll(full, TQ, 1, stride=1, stride_axis=0)
        out_ref[0, t] = rolled[:, :TQ]


def _bias_tiles(rel_bias):
    nh = MOBA_HEADS + NSA_HEADS
    relt = jnp.concatenate([rel_bias.T.astype(jnp.float32) * LOG2E, jnp.full((nh, 1), NEG, jnp.float32),
                            jnp.zeros((nh, 7), jnp.float32)], axis=1).reshape(nh, 1, 40)
    oh = jnp.asarray(_bias_onehot())
    return pl.pallas_call(
        _bias_body,
        grid=(nh,),
        in_specs=[pl.BlockSpec((1, 1, 40), lambda h: (h, 0, 0)), pl.BlockSpec(oh.shape, lambda h: (0, 0))],
        out_specs=pl.BlockSpec((1, _N_TILE_KINDS, TQ, TQ), lambda h: (h, 0, 0, 0)),
        out_shape=jax.ShapeDtypeStruct((nh, _N_TILE_KINDS, TQ, TQ), jnp.float32),
        compiler_params=pltpu.CompilerParams(dimension_semantics=("parallel",), vmem_limit_bytes=VMEM_LIMIT),
        name="bias_tiles",
    )(relt, oh)


def _seg_norm(x, lo_half, gain):
    x2 = x * x
    ss_lo = jnp.sum(jnp.where(lo_half, x2, 0.0), axis=-1, keepdims=True)
    ss_hi = jnp.sum(jnp.where(lo_half, 0.0, x2), axis=-1, keepdims=True)
    ss = jnp.where(lo_half, ss_lo, ss_hi)
    return x * lax.rsqrt(ss * (1.0 / HEAD_DIM) + EPS) * gain


def _in_proj_body(x_ref, nw_ref, w_ref, gain_ref,
                  qa_ref, kae_ref, kao_ref, vae_ref, vao_ref, qb_ref, kmean_ref, kc_ref, vc_ref,
                  kse_ref, kso_ref, vsl_ref, kwn_ref, vwn_ref, gate_ref):
    i = pl.program_id(1)
    x = x_ref[0]
    ms = jnp.mean(x * x, axis=-1, keepdims=True)
    h = (x * lax.rsqrt(ms + EPS) * nw_ref[...]).astype(jnp.bfloat16)
    bf = jnp.bfloat16
    lane = lax.broadcasted_iota(jnp.int32, (TM, LANES), 1)
    row = lax.broadcasted_iota(jnp.int32, (TM, LANES), 0)
    lo_half = lane < HEAD_DIM
    blk = i * (TM // MOBA_BLOCK) + lax.shift_right_logical(row, MOBA_BLOCK.bit_length() - 1)
    sblk = i * (TM // SEL_BLOCK) + lax.shift_right_logical(row, SEL_BLOCK.bit_length() - 1)

    def normed(col, width, scale=None):
        p = _nn(h, w_ref[:, col:col + width])
        outs = []
        for s in range(width // LANES):
            c = col + s * LANES
            y = _seg_norm(p[:, s * LANES:(s + 1) * LANES], lo_half, gain_ref[:, c:c + LANES])
            outs.append(y if scale is None else y * scale)
        return outs

    qa = normed(_C_QA, 512, QSCALE)
    for s in range(4):
        qa_ref[0, :, s * LANES:(s + 1) * LANES] = qa[s].astype(bf)
    ka = normed(_C_KA, 512)
    oh_hi = jnp.where(lane == HEAD_DIM + blk, 1.0, 0.0)
    oh_lo = jnp.where(lane == blk, 1.0, 0.0)
    for s in range(4):
        kae_ref[0, :, s * LANES:(s + 1) * LANES] = jnp.where(lo_half, ka[s], oh_hi).astype(bf)
        kao_ref[0, :, s * LANES:(s + 1) * LANES] = jnp.where(lo_half, oh_lo, ka[s]).astype(bf)
        for j in range(TM // MOBA_BLOCK):
            kmean_ref[0, j, :, s * LANES:(s + 1) * LANES] = jnp.broadcast_to(
                jnp.mean(ka[s][j * MOBA_BLOCK:(j + 1) * MOBA_BLOCK], axis=0, keepdims=True), (8, LANES))
    one_hi = jnp.where(lane == HEAD_DIM, 1.0, 0.0)
    one_lo = jnp.where(lane == 0, 1.0, 0.0)
    va = _nn(h, w_ref[:, _C_VA:_C_VA + 512])
    for s in range(4):
        vs = va[:, s * LANES:(s + 1) * LANES]
        vae_ref[0, :, s * LANES:(s + 1) * LANES] = jnp.where(lo_half, vs, one_hi).astype(bf)
        vao_ref[0, :, s * LANES:(s + 1) * LANES] = jnp.where(lo_half, one_lo, vs).astype(bf)
    qb = normed(_C_QB, 512, QSCALE)
    for s in range(4):
        qb_ref[0, :, s * LANES:(s + 1) * LANES] = qb[s].astype(bf)

    rest = _nn(h, w_ref[:, _C_KC:_N1])

    def slab(c):
        return rest[:, c - _C_KC:c - _C_KC + LANES]

    kc_ref[0] = slab(_C_KC).astype(bf)
    vc_ref[0] = slab(_C_VC).astype(bf)

    def dup_store(ref, y):
        r = pltpu.roll(y, HEAD_DIM, 1)
        ref[0, 0] = jnp.where(lo_half, y, r).astype(bf)
        ref[0, 1] = jnp.where(lo_half, r, y).astype(bf)

    y = _seg_norm(slab(_C_KSL), lo_half, gain_ref[:, _C_KSL:_C_KSL + LANES])
    r = pltpu.roll(y, HEAD_DIM, 1)
    oh = jnp.where((lane & (HEAD_DIM - 1)) == sblk, 1.0, 0.0)
    kse_ref[0, 0] = jnp.where(lo_half, y, oh).astype(bf)
    kso_ref[0, 0] = jnp.where(lo_half, oh, r).astype(bf)
    kse_ref[0, 1] = jnp.where(lo_half, r, oh).astype(bf)
    kso_ref[0, 1] = jnp.where(lo_half, oh, y).astype(bf)
    def aug_store(ref, y):
        ref[0, 0] = jnp.where(lo_half, y, one_hi).astype(bf)
        ref[0, 1] = jnp.where(lo_half, pltpu.roll(y, HEAD_DIM, 1), one_hi).astype(bf)

    aug_store(vsl_ref, slab(_C_VSL))
    dup_store(kwn_ref, _seg_norm(slab(_C_KWN), lo_half, gain_ref[:, _C_KWN:_C_KWN + LANES]))
    aug_store(vwn_ref, slab(_C_VWN))
    gate_ref[0] = slab(_C_GATE)


def _in_proj(x, norm_w, w1, gain_row):
    B, S, _ = x.shape
    nt = S // TQ
    bf = jnp.bfloat16
    tok = lambda w, dt: jax.ShapeDtypeStruct((B, S, w), dt)
    grp = jax.ShapeDtypeStruct((B, NSA_GROUPS, S, LANES), bf)
    tok_spec = lambda w: pl.BlockSpec((1, TM, w), lambda b, i: (b, i, 0))
    grp_spec = pl.BlockSpec((1, NSA_GROUPS, TM, LANES), lambda b, i: (b, 0, i, 0))
    full = lambda a: pl.BlockSpec(a.shape, lambda b, i: (0,) * a.ndim)
    nw = norm_w.reshape(1, D_MODEL)
    return pl.pallas_call(
        _in_proj_body,
        grid=(B, S // TM),
        in_specs=[tok_spec(D_MODEL), full(nw), full(w1), full(gain_row)],
        out_specs=[tok_spec(512), tok_spec(512), tok_spec(512), tok_spec(512), tok_spec(512), tok_spec(512),
                   pl.BlockSpec((1, TM // MOBA_BLOCK, 8, 512), lambda b, i: (b, i, 0, 0)),
                   tok_spec(LANES), tok_spec(LANES),
                   grp_spec, grp_spec, grp_spec, grp_spec, grp_spec, tok_spec(LANES)],
        out_shape=[tok(512, bf), tok(512, bf), tok(512, bf), tok(512, bf), tok(512, bf), tok(512, bf),
                   jax.ShapeDtypeStruct((B, nt, 8, 512), jnp.float32),
                   tok(LANES, bf), tok(LANES, bf), grp, grp, grp, grp, grp, tok(LANES, jnp.float32)],
        compiler_params=pltpu.CompilerParams(dimension_semantics=("parallel", "parallel"),
                                             vmem_limit_bytes=VMEM_LIMIT),
        name="in_proj",
    )(x, nw, w1, gain_row)


def _gelu_tanh(x):
    return 0.5 * x * (1.0 + jnp.tanh(math.sqrt(2.0 / math.pi) * (x + 0.044715 * (x * x * x))))


def _compress_body(xk_ref, xv_ref, w1k_ref, w1v_ref, pk_ref, pv_ref, w1kn_ref, w1vn_ref,
                   w2k_ref, w2v_ref, gk_ref, kcmp_ref, vcmp_ref):
    nchunk = xk_ref.shape[1]

    def branch(x_ref, w1_ref, pos_ref, w1n_ref, w2_ref, gain):
        ab = _nn(x_ref[0], w1_ref[...])
        posw = _nn(jnp.broadcast_to(pos_ref[...], (8, CMP_LEN * HEAD_DIM)), w1n_ref[...])[0:1]
        outs = []
        for g in range(NSA_GROUPS):
            top = ab[:, (2 * g) * CMP_HIDDEN:(2 * g + 1) * CMP_HIDDEN]
            bot = ab[:, (2 * g + 1) * CMP_HIDDEN:(2 * g + 2) * CMP_HIDDEN]
            hid = top + pltpu.roll(bot, nchunk - 1, 0) + posw
            y = _nn(_gelu_tanh(hid).astype(jnp.bfloat16), w2_ref[...])
            if gain is not None:
                ms = jnp.sum(y * y, axis=-1, keepdims=True) * (0.5 / HEAD_DIM)
                y = y * lax.rsqrt(ms + EPS) * gain
            else:
                lane = lax.broadcasted_iota(jnp.int32, y.shape, 1)
                y = jnp.where(lane < HEAD_DIM, y, jnp.where(lane == HEAD_DIM, 1.0, 0.0))
            outs.append(y.astype(jnp.bfloat16))
        return outs

    k0, k1 = branch(xk_ref, w1k_ref, pk_ref, w1kn_ref, w2k_ref, gk_ref[...])
    kcmp_ref[0, 0], kcmp_ref[0, 1] = k0, k1
    v0, v1 = branch(xv_ref, w1v_ref, pv_ref, w1vn_ref, w2v_ref, None)
    vcmp_ref[0, 0], vcmp_ref[0, 1] = v0, v1


def _pack_w1(w1):
    half = CMP_STRIDE * HEAD_DIM
    cols = []
    for g in range(NSA_GROUPS):
        for part in range(2):
            w = w1[part * half:(part + 1) * half].reshape(CMP_STRIDE, HEAD_DIM, CMP_HIDDEN)
            z = jnp.zeros_like(w)
            pieces = [w, z] if g == 0 else [z, w]
            cols.append(jnp.concatenate(pieces, axis=1).reshape(CMP_STRIDE * LANES, CMP_HIDDEN))
    return jnp.concatenate(cols, axis=1).astype(jnp.bfloat16)


def _compress(kc, vc, pos_k, w1_k, w2_k, pos_v, w1_v, w2_v, gain_cmp):
    B, S, _ = kc.shape
    nchunk = S // CMP_STRIDE
    bf = jnp.bfloat16
    xk = kc.reshape(B, nchunk, CMP_STRIDE * LANES)
    xv = vc.reshape(B, nchunk, CMP_STRIDE * LANES)
    args = [xk, xv, _pack_w1(w1_k), _pack_w1(w1_v),
            pos_k.reshape(1, -1).astype(bf), pos_v.reshape(1, -1).astype(bf),
            w1_k.astype(bf), w1_v.astype(bf),
            jnp.concatenate([w2_k, w2_k], axis=1).astype(bf), jnp.concatenate([w2_v, w2_v], axis=1).astype(bf),
            jnp.concatenate([gain_cmp, gain_cmp]).reshape(1, LANES)]
    x_spec = pl.BlockSpec((1, nchunk, CMP_STRIDE * LANES), lambda b: (b, 0, 0))
    full = lambda a: pl.BlockSpec(a.shape, lambda b: (0,) * a.ndim)
    out = jax.ShapeDtypeStruct((B, NSA_GROUPS, nchunk, LANES), bf)
    o_spec = pl.BlockSpec((1, NSA_GROUPS, nchunk, LANES), lambda b: (b, 0, 0, 0))
    return pl.pallas_call(
        _compress_body,
        grid=(B,),
        in_specs=[x_spec, x_spec] + [full(a) for a in args[2:]],
        out_specs=[o_spec, o_spec],
        out_shape=[out, out],
        compiler_params=pltpu.CompilerParams(dimension_semantics=("parallel",), vmem_limit_bytes=VMEM_LIMIT),
        name="compress",
    )(*args)


def _softmax_step(s, v_top, v_bot, m_ref, acc_ref, first):
    blocks = [s[:, c * LANES:(c + 1) * LANES] for c in range(s.shape[1] // LANES)]
    bmax = blocks[0]
    for blk in blocks[1:]:
        bmax = jnp.maximum(bmax, blk)
    m_new = jnp.broadcast_to(jnp.max(bmax, axis=-1, keepdims=True), m_ref.shape)
    if not first:
        m_old = m_ref[...]
        m_new = jnp.maximum(m_old, m_new)
        alpha = jnp.exp2(m_old - m_new)
    pb = jnp.concatenate([jnp.exp2(blk - m_new) for blk in blocks], axis=1).astype(jnp.bfloat16)
    half = s.shape[0] // 2
    pv = jnp.concatenate([_nn(pb[0:half], v_top), _nn(pb[half:], v_bot)], axis=0)
    acc_ref[...] = pv if first else alpha * acc_ref[...] + pv
    m_ref[...] = m_new


def _finish(acc, ones_lane):
    lane = lax.broadcasted_iota(jnp.int32, acc.shape, 1)
    return acc / jnp.sum(jnp.where(lane == ones_lane, acc, 0.0), axis=-1, keepdims=True)


def _tile_kind(delta):
    return jnp.where(delta >= 0, jnp.minimum(delta, N_FAR), _K_MASKED)


def _chunk_kinds(qi, c):
    d0 = qi - 2 * c
    kind0 = jnp.minimum(d0, N_FAR)
    kind1 = jnp.where(d0 >= 1, jnp.minimum(d0 - 1, N_FAR), _K_MASKED)
    return kind0, kind1


def _moba_body(q_ref, ke_ref, ko_ref, ve_ref, vo_ref, kmean_ref, eye_ref, bias_ref, o_ref, m_ref, acc_ref):
    i = pl.program_id(2)
    nblk = kmean_ref.shape[1]
    lane = lax.broadcasted_iota(jnp.int32, (TR, LANES), 1)
    lo_half = lane < HEAD_DIM
    lane16 = lax.broadcasted_iota(jnp.int32, (nblk, LANES), 1)
    jrow = lax.broadcasted_iota(jnp.int32, (nblk, TR), 0)
    qblk = 2 * i + jnp.where(lax.broadcasted_iota(jnp.int32, (nblk, TR), 1) >= TQ, 1, 0)
    q2 = q_ref[0]
    kmean2 = kmean_ref[0]
    flags = []
    for hh in range(2):
        inhead16 = (lane16 < HEAD_DIM) if hh == 0 else (lane16 >= HEAD_DIM)
        km_hi, km_lo = _split(jnp.where(inhead16, kmean2, 0.0))
        sc = jnp.where(jrow < qblk, _nt(km_hi, q2) + _nt(km_lo, q2), -jnp.inf)
        flag = jnp.where(jrow == qblk, 0.0, 1.0)
        for _ in range(MOBA_TOPK):
            best = jnp.max(sc, axis=0, keepdims=True)
            first = jnp.min(jnp.where(sc == best, jrow, nblk), axis=0, keepdims=True)
            hit = jrow == first
            flag = jnp.where(hit, 0.0, flag)
            sc = jnp.where(hit, -jnp.inf, sc)
        flags.append(flag)
    pad = jnp.zeros((HEAD_DIM - nblk, TR), jnp.float32)
    placed = jnp.concatenate([flags[1], pad, flags[0], pad], axis=0).astype(jnp.bfloat16)
    maskpart = jnp.concatenate([_nt(eye_ref[...], placed[:, a * TQ:(a + 1) * TQ]) for a in range(TR // TQ)],
                               axis=0) * NEG
    q2f = q2.astype(jnp.float32)
    zero = jnp.zeros((TR, LANES), jnp.bfloat16)
    qq = jnp.concatenate(
        [jnp.concatenate([jnp.where(lo_half, q2f, maskpart).astype(jnp.bfloat16), zero], axis=1),
         jnp.concatenate([zero, jnp.where(lo_half, maskpart, q2f).astype(jnp.bfloat16)], axis=1)], axis=0)

    def step(c, first):
        start = pl.multiple_of(c * TK, TK)
        kk = jnp.concatenate([ke_ref[0, pl.ds(start, TK), :], ko_ref[0, pl.ds(start, TK), :]], axis=1)
        d = 2 * (i - c)
        b = jnp.concatenate(
            [jnp.concatenate([bias_ref[hh, _tile_kind(d + a - t)] for t in range(2)], axis=1)
             for hh in range(2) for a in range(2)], axis=0)
        _softmax_step(_nt(qq, kk) + b, ve_ref[0, pl.ds(start, TK), :], vo_ref[0, pl.ds(start, TK), :],
                      m_ref, acc_ref, first)

    step(i, True)

    def body(c, carry):
        step(c, False)
        return carry

    lax.fori_loop(0, i, body, 0)
    o_ref[0] = jnp.where(lo_half, _finish(acc_ref[0:TR], HEAD_DIM), _finish(acc_ref[TR:2 * TR], 0))


def _eye(n):
    return jnp.asarray(np.eye(n, dtype=np.float32), jnp.bfloat16)


def _moba(qa, kae, kao, vae, vao, kmean, bias):
    B, S, _ = qa.shape
    nt = S // TQ
    npair = MOBA_HEADS // 2
    kv_spec = pl.BlockSpec((1, S, LANES), lambda b, p, i: (b, 0, p))
    return pl.pallas_call(
        _moba_body,
        grid=(B, npair, S // TR),
        in_specs=[pl.BlockSpec((1, TR, LANES), lambda b, p, i: (b, i, p)),
                  kv_spec, kv_spec, kv_spec, kv_spec,
                  pl.BlockSpec((1, nt, LANES), lambda b, p, i: (b, 0, p)),
                  pl.BlockSpec((TQ, TQ), lambda b, p, i: (0, 0)),
                  pl.BlockSpec((2, _N_TILE_KINDS, TQ, TQ), lambda b, p, i: (p, 0, 0, 0))],
        out_specs=pl.BlockSpec((1, TR, LANES), lambda b, p, i: (b, i, p)),
        out_shape=jax.ShapeDtypeStruct((B, S, MOBA_HEADS * HEAD_DIM), jnp.float32),
        scratch_shapes=[pltpu.VMEM((2 * TR, LANES), jnp.float32)] * 2,
        compiler_params=pltpu.CompilerParams(dimension_semantics=("parallel", "parallel", "arbitrary"),
                                             vmem_limit_bytes=VMEM_LIMIT),
        name="moba",
    )(qa, kae, kao, vae, vao, kmean, _eye(TQ), bias)


def _nsa_body(q_ref, kcmp_ref, vcmp_ref, kse_ref, kso_ref, vsl_ref, kwn_ref, vwn_ref, gate_ref, ov_ref, eye_ref,
              gexp_ref, bias_ref, o_ref, imp_ref, m_ref, acc_ref):
    i = pl.program_id(2)

    def to_hi(o):
        return pltpu.roll(o, HEAD_DIM, 1)

    ncmp = kcmp_ref.shape[2]
    nsel = ov_ref.shape[0]
    R = NSA_HPG * TR
    bf = jnp.bfloat16
    lane = lax.broadcasted_iota(jnp.int32, (TR, LANES), 1)
    lo_half = lane < HEAD_DIM
    qf = [q_ref[0, :, 0:LANES].astype(jnp.float32), q_ref[0, :, LANES:2 * LANES].astype(jnp.float32)]
    q4 = jnp.concatenate([jnp.where(lo_half if hh % 2 == 0 else ~lo_half, qf[hh // 2], 0.0).astype(bf)
                          for hh in range(NSA_HPG)], axis=0)

    tpos = i * TR + lax.broadcasted_iota(jnp.int32, (TR, ncmp), 0)
    cidx = lax.broadcasted_iota(jnp.int32, (TR, ncmp), 1)
    cbias1 = jnp.where((cidx * CMP_STRIDE + (CMP_LEN - 1) <= tpos) & (cidx < ncmp - 1), 0.0, NEG)
    lc = _nt(q4, kcmp_ref[0, 0]) + jnp.concatenate([cbias1] * NSA_HPG, axis=0)
    cblocks = [lc[:, c * LANES:(c + 1) * LANES] for c in range(ncmp // LANES)]
    cmax = cblocks[0]
    for blk in cblocks[1:]:
        cmax = jnp.maximum(cmax, blk)
    cm = jnp.maximum(jnp.broadcast_to(jnp.max(cmax, axis=-1, keepdims=True), (R, LANES)), 0.5 * NEG)
    ce = [jnp.exp2(blk - cm) for blk in cblocks]
    cden = ce[0]
    for eb in ce[1:]:
        cden = cden + eb
    cden = jnp.broadcast_to(jnp.sum(cden, axis=-1, keepdims=True), (R, LANES))
    crcp = 1.0 / jnp.where(cden > 0.0, cden, 1.0)
    pc = jnp.concatenate([eb * crcp for eb in ce], axis=1)
    pcb = pc.astype(bf)
    o_cmp = jnp.concatenate([_nn(pcb[0:R // 2], vcmp_ref[0, 0]), _nn(pcb[R // 2:], vcmp_ref[0, 0])], axis=0)
    psum = pc[0:TR] + pc[TR:2 * TR] + pc[2 * TR:3 * TR] + pc[3 * TR:4 * TR]
    p_hi, p_lo = _split(psum)
    ov = ov_ref[...]
    imp = _nt(ov, p_hi) + _nt(ov, p_lo)

    jrow = lax.broadcasted_iota(jnp.int32, (nsel, TR), 0)
    qblk = lax.shift_right_logical(i * TR + lax.broadcasted_iota(jnp.int32, (nsel, TR), 1),
                                   SEL_BLOCK.bit_length() - 1)
    cand = (jrow >= 1) & (jrow < qblk)
    keys = jnp.where(cand, lax.bitcast_convert_type(imp + 0.0, jnp.int32), -1)
    imp_ref[...] = keys

    def rank_body(jp, cnt):
        r = imp_ref[pl.ds(jp, 1), :] + jnp.where(jrow > jp, 1, 0)
        return cnt + jnp.where(r > keys, 1, 0)

    cnt = lax.fori_loop(1, (i + 1) * (TR // SEL_BLOCK) - 1, rank_body, jnp.zeros((nsel, TR), jnp.int32))
    keep = (jrow == 0) | (jrow == qblk) | (cand & (cnt < SEL_TOPN - 2))
    flag = jnp.where(keep, 0.0, 1.0)
    if nsel < HEAD_DIM:
        flag = jnp.concatenate([flag, jnp.zeros((HEAD_DIM - nsel, TR), jnp.float32)], axis=0)
    flag2 = jnp.concatenate([flag, flag], axis=0).astype(bf)
    maskpart = jnp.concatenate([_nt(eye_ref[...], flag2[:, a * TQ:(a + 1) * TQ]) for a in range(TR // TQ)],
                               axis=0) * NEG

    zero = jnp.zeros((2 * TR, LANES), bf)
    qa2 = [jnp.concatenate([jnp.where(lo_half if par == 0 else ~lo_half, qf[s], maskpart).astype(bf)
                            for s in range(2)], axis=0) for par in range(2)]
    qq = jnp.concatenate([jnp.concatenate([qa2[0], zero], axis=1),
                          jnp.concatenate([zero, qa2[1]], axis=1)], axis=0)
    sel_heads = (0, 2, 1, 3)

    def sel_step(c, first):
        start = pl.multiple_of(c * TK, TK)
        kk = jnp.concatenate([kse_ref[0, 0, pl.ds(start, TK), :], kso_ref[0, 0, pl.ds(start, TK), :]], axis=1)
        d = 2 * (i - c)
        b = jnp.concatenate(
            [jnp.concatenate([bias_ref[h, _tile_kind(d + a - t)] for t in range(2)], axis=1)
             for h in sel_heads for a in range(2)], axis=0)
        v = vsl_ref[0, 0, pl.ds(start, TK), :]
        _softmax_step(_nt(qq, kk) + b, v, v, m_ref, acc_ref, first)

    sel_step(i, True)

    def sel_body(c, carry):
        sel_step(c, False)
        return carry

    lax.fori_loop(0, i, sel_body, 0)
    o_all = _finish(acc_ref[...], HEAD_DIM)
    o_sel = [o_all[0:2 * TR], to_hi(o_all[2 * TR:4 * TR])]

    o_win = []
    for a in range(2):
        qt = 2 * i + a
        t0 = jnp.maximum(qt - (N_WIN_TILES - 1), 0)
        wstart = pl.multiple_of(t0 * TQ, TQ)
        qh = jnp.concatenate([q4[hh * TR + a * TQ:hh * TR + (a + 1) * TQ] for hh in range(NSA_HPG)], axis=0)
        kinds = []
        for r in range(N_WIN_TILES):
            dd = qt - (t0 + r)
            kinds.append(jnp.where(dd >= 0, _K_WIN + dd, _K_MASKED))
        bw = jnp.concatenate([jnp.concatenate([bias_ref[hh, k] for k in kinds], axis=1)
                              for hh in range(NSA_HPG)], axis=0)
        sw = _nt(qh, kwn_ref[0, 0, pl.ds(wstart, N_WIN_TILES * TQ), :]) + bw
        wm, wacc = m_ref.at[0:NSA_HPG * TQ], acc_ref.at[0:NSA_HPG * TQ]
        vw = vwn_ref[0, 0, pl.ds(wstart, N_WIN_TILES * TQ), :]
        _softmax_step(sw, vw, vw, wm, wacc, True)
        o_win.append(_finish(wacc[...], HEAD_DIM))

    g = jax.nn.sigmoid(gate_ref[0])
    g_hi, g_lo = _split(g)
    gx = _nn(g_hi, gexp_ref[0]) + _nn(g_lo, gexp_ref[0])
    for s in range(2):
        h_even, h_odd = 2 * s, 2 * s + 1
        cmp2 = jnp.where(lo_half, o_cmp[h_even * TR:(h_even + 1) * TR], to_hi(o_cmp[h_odd * TR:(h_odd + 1) * TR]))
        sel2 = jnp.where(lo_half, o_sel[0][s * TR:(s + 1) * TR], o_sel[1][s * TR:(s + 1) * TR])
        win2 = jnp.where(lo_half,
                         jnp.concatenate([o_win[a][h_even * TQ:(h_even + 1) * TQ] for a in range(2)], axis=0),
                         to_hi(jnp.concatenate([o_win[a][h_odd * TQ:(h_odd + 1) * TQ] for a in range(2)], axis=0)))
        base = s * 3 * LANES
        o_ref[0, :, s * LANES:(s + 1) * LANES] = (gx[:, base:base + LANES] * cmp2
                                                  + gx[:, base + LANES:base + 2 * LANES] * sel2
                                                  + gx[:, base + 2 * LANES:base + 3 * LANES] * win2)


def _gate_expand():
    e = np.zeros((NSA_GROUPS, LANES, 2 * 3 * LANES), np.float32)
    for g in range(NSA_GROUPS):
        for s in range(2):
            for c in range(3):
                for half in range(2):
                    h = g * NSA_HPG + 2 * s + half
                    col = (s * 3 + c) * LANES + half * HEAD_DIM
                    e[g, 3 * h + c, col:col + HEAD_DIM] = 1.0
    return e


def _nsa(qb, kcmp, vcmp, kse, kso, vsl, kwn, vwn, gate, bias):
    B, S, _ = qb.shape
    nt = S // TQ
    ncmp = kcmp.shape[2]
    nsel = S // SEL_BLOCK
    bf = jnp.bfloat16
    cs = np.arange(ncmp) * CMP_STRIDE
    bs = np.arange(nsel) * SEL_BLOCK
    ov = ((cs[None, :] < bs[:, None] + SEL_BLOCK) & (cs[None, :] + CMP_LEN > bs[:, None])
          & (np.arange(ncmp)[None, :] < ncmp - 1)).astype(np.float32)
    grp = lambda n: pl.BlockSpec((1, 1, n, LANES), lambda b, g, i: (b, g, 0, 0))
    return pl.pallas_call(
        _nsa_body,
        grid=(B, NSA_GROUPS, S // TR),
        in_specs=[pl.BlockSpec((1, TR, 2 * LANES), lambda b, g, i: (b, i, g)),
                  grp(ncmp), grp(ncmp), grp(S), grp(S), grp(S), grp(S), grp(S),
                  pl.BlockSpec((1, TR, LANES), lambda b, g, i: (b, i, 0)),
                  pl.BlockSpec((nsel, ncmp), lambda b, g, i: (0, 0)),
                  pl.BlockSpec((TQ, TQ), lambda b, g, i: (0, 0)),
                  pl.BlockSpec((1, LANES, 6 * LANES), lambda b, g, i: (g, 0, 0)),
                  pl.BlockSpec((NSA_HPG, _N_TILE_KINDS, TQ, TQ), lambda b, g, i: (2 + g, 0, 0, 0),
                               pipeline_mode=pl.Buffered(1))],
        out_specs=pl.BlockSpec((1, TR, 2 * LANES), lambda b, g, i: (b, i, g)),
        out_shape=jax.ShapeDtypeStruct((B, S, NSA_HEADS * HEAD_DIM), jnp.float32),
        scratch_shapes=[pltpu.VMEM((nsel, TR), jnp.int32)] + [pltpu.VMEM((NSA_HPG * TR, LANES), jnp.float32)] * 2,
        compiler_params=pltpu.CompilerParams(dimension_semantics=("parallel", "parallel", "arbitrary"),
                                             vmem_limit_bytes=VMEM_LIMIT),
        name="nsa",
    )(qb, kcmp, vcmp, kse, kso, vsl, kwn, vwn, gate, jnp.asarray(ov, bf), _eye(TQ),
      jnp.asarray(_gate_expand(), bf), bias)


def _out_proj_body(x_ref, nw_ref, oa_ref, ob_ref, wz_ref, wa_ref, wb_ref, wo_ref, out_ref):
    x = x_ref[0]
    ms = jnp.mean(x * x, axis=-1, keepdims=True)
    h = (x * lax.rsqrt(ms + EPS) * nw_ref[...]).astype(jnp.bfloat16)
    z = _nn(h, wz_ref[:, 0:1024])
    ya = (oa_ref[0] * jax.nn.silu(z[:, 0:512])).astype(jnp.bfloat16)
    yb = (ob_ref[0] * jax.nn.silu(z[:, 512:1024])).astype(jnp.bfloat16)
    gm = jax.nn.sigmoid(_nn(h, wz_ref[:, 1024:3072]))
    merged = gm[:, 0:D_MODEL] * _nn(ya, wa_ref[...]) + gm[:, D_MODEL:] * _nn(yb, wb_ref[...])
    out_ref[0] = x + _nn(merged.astype(jnp.bfloat16), wo_ref[...])


def _out_proj(x, norm_w, oa, ob, wz, wa, wb, wo):
    B, S, _ = x.shape
    nw = norm_w.reshape(1, D_MODEL)
    tok = lambda w: pl.BlockSpec((1, TM, w), lambda b, i: (b, i, 0))
    full = lambda a: pl.BlockSpec(a.shape, lambda b, i: (0,) * a.ndim)
    return pl.pallas_call(
        _out_proj_body,
        grid=(B, S // TM),
        in_specs=[tok(D_MODEL), full(nw), tok(512), tok(512), full(wz), full(wa), full(wb), full(wo)],
        out_specs=tok(D_MODEL),
        out_shape=jax.ShapeDtypeStruct((B, S, D_MODEL), jnp.float32),
        compiler_params=pltpu.CompilerParams(dimension_semantics=("parallel", "parallel"),
                                             vmem_limit_bytes=VMEM_LIMIT),
        name="out_proj",
    )(x, nw, oa, ob, wz, wa, wb, wo)


def _pack_in_weights(w_in, q_norm_a, k_norm_a, q_norm_b, k_norm_sel, k_norm_win):
    bf = jnp.bfloat16
    w1 = jnp.concatenate([w_in[:, 0:1536], w_in[:, 2048:2560], w_in[:, 2560:3328],
                          w_in[:, 3328:3352], jnp.zeros((D_MODEL, LANES - 3 * NSA_HEADS), w_in.dtype)], axis=1)
    wz = jnp.concatenate([w_in[:, 1536:2048], w_in[:, 3352:3864], w_in[:, 3864:5912]], axis=1)
    ones = lambda n: jnp.ones((n,), jnp.float32)
    gain = jnp.concatenate([jnp.tile(q_norm_a, MOBA_HEADS), jnp.tile(k_norm_a, MOBA_HEADS), ones(512),
                            jnp.tile(q_norm_b, NSA_HEADS), ones(256), jnp.tile(k_norm_sel, NSA_GROUPS), ones(128),
                            jnp.tile(k_norm_win, NSA_GROUPS), ones(256)]).reshape(1, _N1)
    return w1.astype(bf), wz.astype(bf), gain


def kernel(x, norm_w, w_in, q_norm_a, k_norm_a, q_norm_b, k_norm_cmp, k_norm_sel, k_norm_win, cmp_pos_k, cmp_w1_k, cmp_w2_k, cmp_pos_v, cmp_w1_v, cmp_w2_v, rel_bias, w_branch_a, w_branch_b, w_out):
    bf = jnp.bfloat16
    w1, wz, gain = _pack_in_weights(w_in[0], q_norm_a[0], k_norm_a[0], q_norm_b[0], k_norm_sel[0], k_norm_win[0])
    bias = _bias_tiles(rel_bias)
    qa, kae, kao, vae, vao, qb, kmean, kc, vc, kse, kso, vsl, kwn, vwn, gate = _in_proj(x, norm_w[0], w1, gain)
    kcmp, vcmp = _compress(kc, vc, cmp_pos_k[0], cmp_w1_k[0], cmp_w2_k[0],
                           cmp_pos_v[0], cmp_w1_v[0], cmp_w2_v[0], k_norm_cmp[0])
    oa = _moba(qa, kae, kao, vae, vao, kmean[:, :, 0, :], bias)
    ob = _nsa(qb, kcmp, vcmp, kse, kso, vsl, kwn, vwn, gate, bias)
    return _out_proj(x, norm_w[0], oa, ob, wz, w_branch_a[0].astype(bf), w_branch_b[0].astype(bf),
                     w_out[0].astype(bf))
```

```python
import math

import jax
import jax.numpy as jnp
import numpy as np
from jax import lax
from jax.experimental import pallas as pl
from jax.experimental.pallas import tpu as pltpu

D_MODEL = 1024
HEAD_DIM = 64
MOBA_HEADS = 8
NSA_HEADS = 8
NSA_GROUPS = 2
NSA_HPG = NSA_HEADS // NSA_GROUPS
MOBA_BLOCK = 256
MOBA_TOPK = 3
CMP_LEN = 32
CMP_STRIDE = 16
CMP_HIDDEN = 256
SEL_BLOCK = 64
SEL_TOPN = 16
WINDOW = 512
NUM_BUCKETS = 32
MAX_DISTANCE = 1024
EPS = 1e-6
NEG = -1e30

LANES = 128
TQ = 256
TK = 2 * TQ
TR = 2 * TQ
TM = 2 * TQ
N_FAR = 5
N_WIN_TILES = WINDOW // TQ + 1
VMEM_LIMIT = 56 * 1024 * 1024
LOG2E = math.log2(math.e)
QSCALE = LOG2E / math.sqrt(HEAD_DIM)

_K_WIN = N_FAR + 1
_K_MASKED = _K_WIN + N_WIN_TILES
_N_TILE_KINDS = _K_MASKED + 1

_C_QA, _C_KA, _C_VA, _C_QB = 0, 512, 1024, 1536
_C_KC, _C_VC, _C_KSL, _C_VSL, _C_KWN, _C_VWN, _C_GATE = 2048, 2176, 2304, 2432, 2560, 2688, 2816
_N1 = 2944


def _nt(a, b):
    return lax.dot_general(a, b, (((1,), (1,)), ((), ())), preferred_element_type=jnp.float32)


def _nn(a, b):
    return jnp.dot(a, b, preferred_element_type=jnp.float32)


def _split(x):
    hi = x.astype(jnp.bfloat16)
    lo = (x - hi.astype(jnp.float32)).astype(jnp.bfloat16)
    return hi, lo


def _bucket_table(n):
    d = np.arange(n)
    nf = np.maximum(d, NUM_BUCKETS // 2).astype(np.float64)
    large = NUM_BUCKETS // 2 + (np.log(nf / (NUM_BUCKETS // 2)) / math.log(MAX_DISTANCE / (NUM_BUCKETS // 2))
                                * (NUM_BUCKETS - NUM_BUCKETS // 2)).astype(np.int64)
    return np.where(d < NUM_BUCKETS // 2, d, np.minimum(large, NUM_BUCKETS - 1))


def _bias_onehot():
    oh = np.zeros((40, _N_TILE_KINDS * 2 * TQ), np.float32)
    buckets = _bucket_table((N_FAR + 2) * TQ + 1)
    for t in range(_N_TILE_KINDS):
        window = _K_WIN <= t < _K_MASKED
        delta = t - _K_WIN if window else t
        for u in range(2 * TQ):
            d = TQ * delta + TQ - u
            masked = t == _K_MASKED or d < 0 or (window and d >= WINDOW)
            oh[NUM_BUCKETS if masked else buckets[d], t * 2 * TQ + u] = 1.0
    return oh


def _bias_body(relt_ref, oh_ref, out_ref):
    g = jnp.dot(jnp.broadcast_to(relt_ref[0], (8, 40)), oh_ref[...], preferred_element_type=jnp.float32,
                precision=lax.Precision.HIGHEST)
    for t in range(_N_TILE_KINDS):
        full = jnp.broadcast_to(g[0:1, t * 2 * TQ:(t + 1) * 2 * TQ], (TQ, 2 * TQ))
        rolled = pltpu.roll(full, TQ, 1, stride=1, stride_axis=0)
        out_ref[0, t] = rolled[:, :TQ]


def _bias_tiles(rel_bias):
    nh = MOBA_HEADS + NSA_HEADS
    relt = jnp.concatenate([rel_bias.T.astype(jnp.float32) * LOG2E, jnp.full((nh, 1), NEG, jnp.float32),
                            jnp.zeros((nh, 7), jnp.float32)], axis=1).reshape(nh, 1, 40)
    oh = jnp.asarray(_bias_onehot())
    return pl.pallas_call(
        _bias_body,
        grid=(nh,),
        in_specs=[pl.BlockSpec((1, 1, 40), lambda h: (h, 0, 0)), pl.BlockSpec(oh.shape, lambda h: (0, 0))],
        out_specs=pl.BlockSpec((1, _N_TILE_KINDS, TQ, TQ), lambda h: (h, 0, 0, 0)),
        out_shape=jax.ShapeDtypeStruct((nh, _N_TILE_KINDS, TQ, TQ), jnp.float32),
        compiler_params=pltpu.CompilerParams(dimension_semantics=("parallel",), vmem_limit_bytes=VMEM_LIMIT),
        name="bias_tiles",
    )(relt, oh)


def _seg_norm(x, lo_half, gain):
    x2 = x * x
    ss_lo = jnp.sum(jnp.where(lo_half, x2, 0.0), axis=-1, keepdims=True)
    ss_hi = jnp.sum(jnp.where(lo_half, 0.0, x2), axis=-1, keepdims=True)
    ss = jnp.where(lo_half, ss_lo, ss_hi)
    return x * lax.rsqrt(ss * (1.0 / HEAD_DIM) + EPS) * gain


def _in_proj_body(x_ref, nw_ref, w_ref, gain_ref,
                  qa_ref, kae_ref, kao_ref, vae_ref, vao_ref, qb_ref, kmean_ref, kc_ref, vc_ref,
                  kse_ref, kso_ref, vsl_ref, kwn_ref, vwn_ref, gate_ref):
    i = pl.program_id(1)
    x = x_ref[0]
    ms = jnp.mean(x * x, axis=-1, keepdims=True)
    h = (x * lax.rsqrt(ms + EPS) * nw_ref[...]).astype(jnp.bfloat16)
    bf = jnp.bfloat16
    lane = lax.broadcasted_iota(jnp.int32, (TM, LANES), 1)
    row = lax.broadcasted_iota(jnp.int32, (TM, LANES), 0)
    lo_half = lane < HEAD_DIM
    blk = i * (TM // MOBA_BLOCK) + lax.shift_right_logical(row, MOBA_BLOCK.bit_length() - 1)
    sblk = i * (TM // SEL_BLOCK) + lax.shift_right_logical(row, SEL_BLOCK.bit_length() - 1)

    def normed(col, width, scale=None):
        p = _nn(h, w_ref[:, col:col + width])
        outs = []
        for s in range(width // LANES):
            c = col + s * LANES
            y = _seg_norm(p[:, s * LANES:(s + 1) * LANES], lo_half, gain_ref[:, c:c + LANES])
            outs.append(y if scale is None else y * scale)
        return outs

    qa = normed(_C_QA, 512, QSCALE)
    for s in range(4):
        qa_ref[0, :, s * LANES:(s + 1) * LANES] = qa[s].astype(bf)
    ka = normed(_C_KA, 512)
    oh_hi = jnp.where(lane == HEAD_DIM + blk, 1.0, 0.0)
    oh_lo = jnp.where(lane == blk, 1.0, 0.0)
    for s in range(4):
        kae_ref[0, :, s * LANES:(s + 1) * LANES] = jnp.where(lo_half, ka[s], oh_hi).astype(bf)
        kao_ref[0, :, s * LANES:(s + 1) * LANES] = jnp.where(lo_half, oh_lo, ka[s]).astype(bf)
        for j in range(TM // MOBA_BLOCK):
            kmean_ref[0, j, :, s * LANES:(s + 1) * LANES] = jnp.broadcast_to(
                jnp.mean(ka[s][j * MOBA_BLOCK:(j + 1) * MOBA_BLOCK], axis=0, keepdims=True), (8, LANES))
    one_hi = jnp.where(lane == HEAD_DIM, 1.0, 0.0)
    one_lo = jnp.where(lane == 0, 1.0, 0.0)
    va = _nn(h, w_ref[:, _C_VA:_C_VA + 512])
    for s in range(4):
        vs = va[:, s * LANES:(s + 1) * LANES]
        vae_ref[0, :, s * LANES:(s + 1) * LANES] = jnp.where(lo_half, vs, one_hi).astype(bf)
        vao_ref[0, :, s * LANES:(s + 1) * LANES] = jnp.where(lo_half, one_lo, vs).astype(bf)
    qb = normed(_C_QB, 512, QSCALE)
    for s in range(4):
        qb_ref[0, :, s * LANES:(s + 1) * LANES] = qb[s].astype(bf)

    rest = _nn(h, w_ref[:, _C_KC:_N1])

    def slab(c):
        return rest[:, c - _C_KC:c - _C_KC + LANES]

    kc_ref[0] = slab(_C_KC).astype(bf)
    vc_ref[0] = slab(_C_VC).astype(bf)

    def dup_store(ref, y):
        r = pltpu.roll(y, HEAD_DIM, 1)
        ref[0, 0] = jnp.where(lo_half, y, r).astype(bf)
        ref[0, 1] = jnp.where(lo_half, r, y).astype(bf)

    y = _seg_norm(slab(_C_KSL), lo_half, gain_ref[:, _C_KSL:_C_KSL + LANES])
    r = pltpu.roll(y, HEAD_DIM, 1)
    oh = jnp.where((lane & (HEAD_DIM - 1)) == sblk, 1.0, 0.0)
    kse_ref[0, 0] = jnp.where(lo_half, y, oh).astype(bf)
    kso_ref[0, 0] = jnp.where(lo_half, oh, r).astype(bf)
    kse_ref[0, 1] = jnp.where(lo_half, r, oh).astype(bf)
    kso_ref[0, 1] = jnp.where(lo_half, oh, y).astype(bf)
    def aug_store(ref, y):
        ref[0, 0] = jnp.where(lo_half, y, one_hi).astype(bf)
        ref[0, 1] = jnp.where(lo_half, pltpu.roll(y, HEAD_DIM, 1), one_hi).astype(bf)

    aug_store(vsl_ref, slab(_C_VSL))
    dup_store(kwn_ref, _seg_norm(slab(_C_KWN), lo_half, gain_ref[:, _C_KWN:_C_KWN + LANES]))
    aug_store(vwn_ref, slab(_C_VWN))
    gate_ref[0] = slab(_C_GATE)


def _in_proj(x, norm_w, w1, gain_row):
    B, S, _ = x.shape
    nt = S // TQ
    bf = jnp.bfloat16
    tok = lambda w, dt: jax.ShapeDtypeStruct((B, S, w), dt)
    grp = jax.ShapeDtypeStruct((B, NSA_GROUPS, S, LANES), bf)
    tok_spec = lambda w: pl.BlockSpec((1, TM, w), lambda b, i: (b, i, 0))
    grp_spec = pl.BlockSpec((1, NSA_GROUPS, TM, LANES), lambda b, i: (b, 0, i, 0))
    full = lambda a: pl.BlockSpec(a.shape, lambda b, i: (0,) * a.ndim)
    nw = norm_w.reshape(1, D_MODEL)
    return pl.pallas_call(
        _in_proj_body,
        grid=(B, S // TM),
        in_specs=[tok_spec(D_MODEL), full(nw), full(w1), full(gain_row)],
        out_specs=[tok_spec(512), tok_spec(512), tok_spec(512), tok_spec(512), tok_spec(512), tok_spec(512),
                   pl.BlockSpec((1, TM // MOBA_BLOCK, 8, 512), lambda b, i: (b, i, 0, 0)),
                   tok_spec(LANES), tok_spec(LANES),
                   grp_spec, grp_spec, grp_spec, grp_spec, grp_spec, tok_spec(LANES)],
        out_shape=[tok(512, bf), tok(512, bf), tok(512, bf), tok(512, bf), tok(512, bf), tok(512, bf),
                   jax.ShapeDtypeStruct((B, nt, 8, 512), jnp.float32),
                   tok(LANES, bf), tok(LANES, bf), grp, grp, grp, grp, grp, tok(LANES, jnp.float32)],
        compiler_params=pltpu.CompilerParams(dimension_semantics=("parallel", "parallel"),
                                             vmem_limit_bytes=VMEM_LIMIT),
        name="in_proj",
    )(x, nw, w1, gain_row)


def _gelu_tanh(x):
    return 0.5 * x * (1.0 + jnp.tanh(math.sqrt(2.0 / math.pi) * (x + 0.044715 * (x * x * x))))


def _compress_body(xk_ref, xv_ref, w1k_ref, w1v_ref, pk_ref, pv_ref, w1kn_ref, w1vn_ref,
                   w2k_ref, w2v_ref, gk_ref, kcmp_ref, vcmp_ref):
    nchunk = xk_ref.shape[1]

    def branch(x_ref, w1_ref, pos_ref, w1n_ref, w2_ref, gain):
        ab = _nn(x_ref[0], w1_ref[...])
        posw = _nn(jnp.broadcast_to(pos_ref[...], (8, CMP_LEN * HEAD_DIM)), w1n_ref[...])[0:1]
        outs = []
        for g in range(NSA_GROUPS):
            top = ab[:, (2 * g) * CMP_HIDDEN:(2 * g + 1) * CMP_HIDDEN]
            bot = ab[:, (2 * g + 1) * CMP_HIDDEN:(2 * g + 2) * CMP_HIDDEN]
            hid = top + pltpu.roll(bot, nchunk - 1, 0) + posw
            y = _nn(_gelu_tanh(hid).astype(jnp.bfloat16), w2_ref[...])
            if gain is not None:
                ms = jnp.sum(y * y, axis=-1, keepdims=True) * (0.5 / HEAD_DIM)
                y = y * lax.rsqrt(ms + EPS) * gain
            else:
                lane = lax.broadcasted_iota(jnp.int32, y.shape, 1)
                y = jnp.where(lane < HEAD_DIM, y, jnp.where(lane == HEAD_DIM, 1.0, 0.0))
            outs.append(y.astype(jnp.bfloat16))
        return outs

    k0, k1 = branch(xk_ref, w1k_ref, pk_ref, w1kn_ref, w2k_ref, gk_ref[...])
    kcmp_ref[0, 0], kcmp_ref[0, 1] = k0, k1
    v0, v1 = branch(xv_ref, w1v_ref, pv_ref, w1vn_ref, w2v_ref, None)
    vcmp_ref[0, 0], vcmp_ref[0, 1] = v0, v1


def _pack_w1(w1):
    half = CMP_STRIDE * HEAD_DIM
    cols = []
    for g in range(NSA_GROUPS):
        for part in range(2):
            w = w1[part * half:(part + 1) * half].reshape(CMP_STRIDE, HEAD_DIM, CMP_HIDDEN)
            z = jnp.zeros_like(w)
            pieces = [w, z] if g == 0 else [z, w]
            cols.append(jnp.concatenate(pieces, axis=1).reshape(CMP_STRIDE * LANES, CMP_HIDDEN))
    return jnp.concatenate(cols, axis=1).astype(jnp.bfloat16)


def _compress(kc, vc, pos_k, w1_k, w2_k, pos_v, w1_v, w2_v, gain_cmp):
    B, S, _ = kc.shape
    nchunk = S // CMP_STRIDE
    bf = jnp.bfloat16
    xk = kc.reshape(B, nchunk, CMP_STRIDE * LANES)
    xv = vc.reshape(B, nchunk, CMP_STRIDE * LANES)
    args = [xk, xv, _pack_w1(w1_k), _pack_w1(w1_v),
            pos_k.reshape(1, -1).astype(bf), pos_v.reshape(1, -1).astype(bf),
            w1_k.astype(bf), w1_v.astype(bf),
            jnp.concatenate([w2_k, w2_k], axis=1).astype(bf), jnp.concatenate([w2_v, w2_v], axis=1).astype(bf),
            jnp.concatenate([gain_cmp, gain_cmp]).reshape(1, LANES)]
    x_spec = pl.BlockSpec((1, nchunk, CMP_STRIDE * LANES), lambda b: (b, 0, 0))
    full = lambda a: pl.BlockSpec(a.shape, lambda b: (0,) * a.ndim)
    out = jax.ShapeDtypeStruct((B, NSA_GROUPS, nchunk, LANES), bf)
    o_spec = pl.BlockSpec((1, NSA_GROUPS, nchunk, LANES), lambda b: (b, 0, 0, 0))
    return pl.pallas_call(
        _compress_body,
        grid=(B,),
        in_specs=[x_spec, x_spec] + [full(a) for a in args[2:]],
        out_specs=[o_spec, o_spec],
        out_shape=[out, out],
        compiler_params=pltpu.CompilerParams(dimension_semantics=("parallel",), vmem_limit_bytes=VMEM_LIMIT),
        name="compress",
    )(*args)


def _softmax_step(s, v_top, v_bot, m_ref, acc_ref, first):
    blocks = [s[:, c * LANES:(c + 1) * LANES] for c in range(s.shape[1] // LANES)]
    bmax = blocks[0]
    for blk in blocks[1:]:
        bmax = jnp.maximum(bmax, blk)
    m_new = jnp.broadcast_to(jnp.max(bmax, axis=-1, keepdims=True), m_ref.shape)
    if not first:
        m_old = m_ref[...]
        m_new = jnp.maximum(m_old, m_new)
        alpha = jnp.exp2(m_old - m_new)
    pb = jnp.concatenate([jnp.exp2(blk - m_new) for blk in blocks], axis=1).astype(jnp.bfloat16)
    half = s.shape[0] // 2
    pv = jnp.concatenate([_nn(pb[0:half], v_top), _nn(pb[half:], v_bot)], axis=0)
    acc_ref[...] = pv if first else alpha * acc_ref[...] + pv
    m_ref[...] = m_new


def _finish(acc, ones_lane):
    lane = lax.broadcasted_iota(jnp.int32, acc.shape, 1)
    return acc / jnp.sum(jnp.where(lane == ones_lane, acc, 0.0), axis=-1, keepdims=True)


def _tile_kind(delta):
    return jnp.where(delta >= 0, jnp.minimum(delta, N_FAR), _K_MASKED)


def _chunk_kinds(qi, c):
    d0 = qi - 2 * c
    kind0 = jnp.minimum(d0, N_FAR)
    kind1 = jnp.where(d0 >= 1, jnp.minimum(d0 - 1, N_FAR), _K_MASKED)
    return kind0, kind1


def _moba_body(q_ref, ke_ref, ko_ref, ve_ref, vo_ref, kmean_ref, eye_ref, bias_ref, o_ref, m_ref, acc_ref):
    i = pl.program_id(2)
    nblk = kmean_ref.shape[1]
    lane = lax.broadcasted_iota(jnp.int32, (TR, LANES), 1)
    lo_half = lane < HEAD_DIM
    lane16 = lax.broadcasted_iota(jnp.int32, (nblk, LANES), 1)
    jrow = lax.broadcasted_iota(jnp.int32, (nblk, TR), 0)
    qblk = 2 * i + jnp.where(lax.broadcasted_iota(jnp.int32, (nblk, TR), 1) >= TQ, 1, 0)
    q2 = q_ref[0]
    kmean2 = kmean_ref[0]
    flags = []
    for hh in range(2):
        inhead16 = (lane16 < HEAD_DIM) if hh == 0 else (lane16 >= HEAD_DIM)
        km_hi, km_lo = _split(jnp.where(inhead16, kmean2, 0.0))
        sc = jnp.where(jrow < qblk, _nt(km_hi, q2) + _nt(km_lo, q2), -jnp.inf)
        flag = jnp.where(jrow == qblk, 0.0, 1.0)
        for _ in range(MOBA_TOPK):
            best = jnp.max(sc, axis=0, keepdims=True)
            first = jnp.min(jnp.where(sc == best, jrow, nblk), axis=0, keepdims=True)
            hit = jrow == first
            flag = jnp.where(hit, 0.0, flag)
            sc = jnp.where(hit, -jnp.inf, sc)
        flags.append(flag)
    pad = jnp.zeros((HEAD_DIM - nblk, TR), jnp.float32)
    placed = jnp.concatenate([flags[1], pad, flags[0], pad], axis=0).astype(jnp.bfloat16)
    maskpart = jnp.concatenate([_nt(eye_ref[...], placed[:, a * TQ:(a + 1) * TQ]) for a in range(TR // TQ)],
                               axis=0) * NEG
    q2f = q2.astype(jnp.float32)
    zero = jnp.zeros((TR, LANES), jnp.bfloat16)
    qq = jnp.concatenate(
        [jnp.concatenate([jnp.where(lo_half, q2f, maskpart).astype(jnp.bfloat16), zero], axis=1),
         jnp.concatenate([zero, jnp.where(lo_half, maskpart, q2f).astype(jnp.bfloat16)], axis=1)], axis=0)

    def step(c, first):
        start = pl.multiple_of(c * TK, TK)
        kk = jnp.concatenate([ke_ref[0, pl.ds(start, TK), :], ko_ref[0, pl.ds(start, TK), :]], axis=1)
        d = 2 * (i - c)
        b = jnp.concatenate(
            [jnp.concatenate([bias_ref[hh, _tile_kind(d + a - t)] for t in range(2)], axis=1)
             for hh in range(2) for a in range(2)], axis=0)
        _softmax_step(_nt(qq, kk) + b, ve_ref[0, pl.ds(start, TK), :], vo_ref[0, pl.ds(start, TK), :],
                      m_ref, acc_ref, first)

    step(i, True)

    def body(c, carry):
        step(c, False)
        return carry

    lax.fori_loop(0, i, body, 0)
    o_ref[0] = jnp.where(lo_half, _finish(acc_ref[0:TR], HEAD_DIM), _finish(acc_ref[TR:2 * TR], 0))


def _eye(n):
    return jnp.asarray(np.eye(n, dtype=np.float32), jnp.bfloat16)


def _moba(qa, kae, kao, vae, vao, kmean, bias):
    B, S, _ = qa.shape
    nt = S // TQ
    npair = MOBA_HEADS // 2
    kv_spec = pl.BlockSpec((1, S, LANES), lambda b, p, i: (b, 0, p))
    return pl.pallas_call(
        _moba_body,
        grid=(B, npair, S // TR),
        in_specs=[pl.BlockSpec((1, TR, LANES), lambda b, p, i: (b, i, p)),
                  kv_spec, kv_spec, kv_spec, kv_spec,
                  pl.BlockSpec((1, nt, LANES), lambda b, p, i: (b, 0, p)),
                  pl.BlockSpec((TQ, TQ), lambda b, p, i: (0, 0)),
                  pl.BlockSpec((2, _N_TILE_KINDS, TQ, TQ), lambda b, p, i: (p, 0, 0, 0))],
        out_specs=pl.BlockSpec((1, TR, LANES), lambda b, p, i: (b, i, p)),
        out_shape=jax.ShapeDtypeStruct((B, S, MOBA_HEADS * HEAD_DIM), jnp.float32),
        scratch_shapes=[pltpu.VMEM((2 * TR, LANES), jnp.float32)] * 2,
        compiler_params=pltpu.CompilerParams(dimension_semantics=("parallel", "parallel", "arbitrary"),
                                             vmem_limit_bytes=VMEM_LIMIT),
        name="moba",
    )(qa, kae, kao, vae, vao, kmean, _eye(TQ), bias)


def _nsa_body(q_ref, kcmp_ref, vcmp_ref, kse_ref, kso_ref, vsl_ref, kwn_ref, vwn_ref, gate_ref, ov_ref, eye_ref,
              gexp_ref, bias_ref, o_ref, m_ref, acc_ref):
    i = pl.program_id(2)

    def to_hi(o):
        return pltpu.roll(o, HEAD_DIM, 1)

    ncmp = kcmp_ref.shape[2]
    nsel = ov_ref.shape[0]
    R = NSA_HPG * TR
    bf = jnp.bfloat16
    lane = lax.broadcasted_iota(jnp.int32, (TR, LANES), 1)
    lo_half = lane < HEAD_DIM
    qf = [q_ref[0, :, 0:LANES].astype(jnp.float32), q_ref[0, :, LANES:2 * LANES].astype(jnp.float32)]
    q4 = jnp.concatenate([jnp.where(lo_half if hh % 2 == 0 else ~lo_half, qf[hh // 2], 0.0).astype(bf)
                          for hh in range(NSA_HPG)], axis=0)

    o_win = []
    for a in range(2):
        qt = 2 * i + a
        t0 = jnp.maximum(qt - (N_WIN_TILES - 1), 0)
        wstart = pl.multiple_of(t0 * TQ, TQ)
        qh = jnp.concatenate([q4[hh * TR + a * TQ:hh * TR + (a + 1) * TQ] for hh in range(NSA_HPG)], axis=0)
        kinds = []
        for r in range(N_WIN_TILES):
            dd = qt - (t0 + r)
            kinds.append(jnp.where(dd >= 0, _K_WIN + dd, _K_MASKED))
        bw = jnp.concatenate([jnp.concatenate([bias_ref[hh, k] for k in kinds], axis=1)
                              for hh in range(NSA_HPG)], axis=0)
        sw = _nt(qh, kwn_ref[0, 0, pl.ds(wstart, N_WIN_TILES * TQ), :]) + bw
        wm, wacc = m_ref.at[0:NSA_HPG * TQ], acc_ref.at[0:NSA_HPG * TQ]
        vw = vwn_ref[0, 0, pl.ds(wstart, N_WIN_TILES * TQ), :]
        _softmax_step(sw, vw, vw, wm, wacc, True)
        o_win.append(_finish(wacc[...], HEAD_DIM))

    tpos = i * TR + lax.broadcasted_iota(jnp.int32, (TR, ncmp), 0)
    cidx = lax.broadcasted_iota(jnp.int32, (TR, ncmp), 1)
    cbias1 = jnp.where((cidx * CMP_STRIDE + (CMP_LEN - 1) <= tpos) & (cidx < ncmp - 1), 0.0, NEG)
    lc = _nt(q4, kcmp_ref[0, 0]) + jnp.concatenate([cbias1] * NSA_HPG, axis=0)
    cblocks = [lc[:, c * LANES:(c + 1) * LANES] for c in range(ncmp // LANES)]
    cmax = cblocks[0]
    for blk in cblocks[1:]:
        cmax = jnp.maximum(cmax, blk)
    cm = jnp.maximum(jnp.broadcast_to(jnp.max(cmax, axis=-1, keepdims=True), (R, LANES)), 0.5 * NEG)
    ce = [jnp.exp2(blk - cm) for blk in cblocks]
    cden = ce[0]
    for eb in ce[1:]:
        cden = cden + eb
    cden = jnp.broadcast_to(jnp.sum(cden, axis=-1, keepdims=True), (R, LANES))
    crcp = 1.0 / jnp.where(cden > 0.0, cden, 1.0)
    pc = jnp.concatenate([eb * crcp for eb in ce], axis=1)
    pcb = pc.astype(bf)
    o_cmp = jnp.concatenate([_nn(pcb[0:R // 2], vcmp_ref[0, 0]), _nn(pcb[R // 2:], vcmp_ref[0, 0])], axis=0)
    psum = pc[0:TR] + pc[TR:2 * TR] + pc[2 * TR:3 * TR] + pc[3 * TR:4 * TR]
    p_hi, p_lo = _split(psum)
    ov = ov_ref[...]
    imp = _nt(ov, p_hi) + _nt(ov, p_lo)

    jrow = lax.broadcasted_iota(jnp.int32, (nsel, TR), 0)
    qblk = lax.shift_right_logical(i * TR + lax.broadcasted_iota(jnp.int32, (nsel, TR), 1),
                                   SEL_BLOCK.bit_length() - 1)
    cand = (jrow >= 1) & (jrow < qblk)
    keys = jnp.where(cand, imp, -1.0)
    flag = jnp.where((jrow == 0) | (jrow == qblk), 0.0, 1.0)
    for _ in range(SEL_TOPN - 2):
        best = jnp.max(keys, axis=0, keepdims=True)
        first = jnp.min(jnp.where(keys == best, jrow, nsel), axis=0, keepdims=True)
        hit = jrow == first
        flag = jnp.where(hit, 0.0, flag)
        keys = jnp.where(hit, -1.0, keys)
    if nsel < HEAD_DIM:
        flag = jnp.concatenate([flag, jnp.zeros((HEAD_DIM - nsel, TR), jnp.float32)], axis=0)
    flag2 = jnp.concatenate([flag, flag], axis=0).astype(bf)
    maskpart = jnp.concatenate([_nt(eye_ref[...], flag2[:, a * TQ:(a + 1) * TQ]) for a in range(TR // TQ)],
                               axis=0) * NEG

    zero = jnp.zeros((2 * TR, LANES), bf)
    qa2 = [jnp.concatenate([jnp.where(lo_half if par == 0 else ~lo_half, qf[s], maskpart).astype(bf)
                            for s in range(2)], axis=0) for par in range(2)]
    qq = jnp.concatenate([jnp.concatenate([qa2[0], zero], axis=1),
                          jnp.concatenate([zero, qa2[1]], axis=1)], axis=0)
    sel_heads = (0, 2, 1, 3)

    def sel_step(c, first):
        start = pl.multiple_of(c * TK, TK)
        kk = jnp.concatenate([kse_ref[0, 0, pl.ds(start, TK), :], kso_ref[0, 0, pl.ds(start, TK), :]], axis=1)
        d = 2 * (i - c)
        b = jnp.concatenate(
            [jnp.concatenate([bias_ref[h, _tile_kind(d + a - t)] for t in range(2)], axis=1)
             for h in sel_heads for a in range(2)], axis=0)
        v = vsl_ref[0, 0, pl.ds(start, TK), :]
        _softmax_step(_nt(qq, kk) + b, v, v, m_ref, acc_ref, first)

    sel_step(i, True)

    def sel_body(c, carry):
        sel_step(c, False)
        return carry

    lax.fori_loop(0, i, sel_body, 0)
    o_all = _finish(acc_ref[...], HEAD_DIM)
    o_sel = [o_all[0:2 * TR], to_hi(o_all[2 * TR:4 * TR])]

    g = jax.nn.sigmoid(gate_ref[0])
    g_hi, g_lo = _split(g)
    gx = _nn(g_hi, gexp_ref[0]) + _nn(g_lo, gexp_ref[0])
    for s in range(2):
        h_even, h_odd = 2 * s, 2 * s + 1
        cmp2 = jnp.where(lo_half, o_cmp[h_even * TR:(h_even + 1) * TR], to_hi(o_cmp[h_odd * TR:(h_odd + 1) * TR]))
        sel2 = jnp.where(lo_half, o_sel[0][s * TR:(s + 1) * TR], o_sel[1][s * TR:(s + 1) * TR])
        win2 = jnp.where(lo_half,
                         jnp.concatenate([o_win[a][h_even * TQ:(h_even + 1) * TQ] for a in range(2)], axis=0),
                         to_hi(jnp.concatenate([o_win[a][h_odd * TQ:(h_odd + 1) * TQ] for a in range(2)], axis=0)))
        base = s * 3 * LANES
        o_ref[0, :, s * LANES:(s + 1) * LANES] = (gx[:, base:base + LANES] * cmp2
                                                  + gx[:, base + LANES:base + 2 * LANES] * sel2
                                                  + gx[:, base + 2 * LANES:base + 3 * LANES] * win2)


def _gate_expand():
    e = np.zeros((NSA_GROUPS, LANES, 2 * 3 * LANES), np.float32)
    for g in range(NSA_GROUPS):
        for s in range(2):
            for c in range(3):
                for half in range(2):
                    h = g * NSA_HPG + 2 * s + half
                    col = (s * 3 + c) * LANES + half * HEAD_DIM
                    e[g, 3 * h + c, col:col + HEAD_DIM] = 1.0
    return e


def _nsa(qb, kcmp, vcmp, kse, kso, vsl, kwn, vwn, gate, bias):
    B, S, _ = qb.shape
    nt = S // TQ
    ncmp = kcmp.shape[2]
    nsel = S // SEL_BLOCK
    bf = jnp.bfloat16
    cs = np.arange(ncmp) * CMP_STRIDE
    bs = np.arange(nsel) * SEL_BLOCK
    ov = ((cs[None, :] < bs[:, None] + SEL_BLOCK) & (cs[None, :] + CMP_LEN > bs[:, None])
          & (np.arange(ncmp)[None, :] < ncmp - 1)).astype(np.float32)
    grp = lambda n: pl.BlockSpec((1, 1, n, LANES), lambda b, g, i: (b, g, 0, 0))
    return pl.pallas_call(
        _nsa_body,
        grid=(B, NSA_GROUPS, S // TR),
        in_specs=[pl.BlockSpec((1, TR, 2 * LANES), lambda b, g, i: (b, i, g)),
                  grp(ncmp), grp(ncmp), grp(S), grp(S), grp(S), grp(S), grp(S),
                  pl.BlockSpec((1, TR, LANES), lambda b, g, i: (b, i, 0)),
                  pl.BlockSpec((nsel, ncmp), lambda b, g, i: (0, 0)),
                  pl.BlockSpec((TQ, TQ), lambda b, g, i: (0, 0)),
                  pl.BlockSpec((1, LANES, 6 * LANES), lambda b, g, i: (g, 0, 0)),
                  pl.BlockSpec((NSA_HPG, _N_TILE_KINDS, TQ, TQ), lambda b, g, i: (2 + g, 0, 0, 0),
                               pipeline_mode=pl.Buffered(1))],
        out_specs=pl.BlockSpec((1, TR, 2 * LANES), lambda b, g, i: (b, i, g)),
        out_shape=jax.ShapeDtypeStruct((B, S, NSA_HEADS * HEAD_DIM), jnp.float32),
        scratch_shapes=[pltpu.VMEM((NSA_HPG * TR, LANES), jnp.float32)] * 2,
        compiler_params=pltpu.CompilerParams(dimension_semantics=("parallel", "parallel", "arbitrary"),
                                             vmem_limit_bytes=VMEM_LIMIT),
        name="nsa",
    )(qb, kcmp, vcmp, kse, kso, vsl, kwn, vwn, gate, jnp.asarray(ov, bf), _eye(TQ),
      jnp.asarray(_gate_expand(), bf), bias)


def _out_proj_body(x_ref, nw_ref, oa_ref, ob_ref, wz_ref, wa_ref, wb_ref, wo_ref, out_ref):
    x = x_ref[0]
    ms = jnp.mean(x * x, axis=-1, keepdims=True)
    h = (x * lax.rsqrt(ms + EPS) * nw_ref[...]).astype(jnp.bfloat16)
    z = _nn(h, wz_ref[:, 0:1024])
    ya = (oa_ref[0] * jax.nn.silu(z[:, 0:512])).astype(jnp.bfloat16)
    yb = (ob_ref[0] * jax.nn.silu(z[:, 512:1024])).astype(jnp.bfloat16)
    gm = jax.nn.sigmoid(_nn(h, wz_ref[:, 1024:3072]))
    merged = gm[:, 0:D_MODEL] * _nn(ya, wa_ref[...]) + gm[:, D_MODEL:] * _nn(yb, wb_ref[...])
    out_ref[0] = x + _nn(merged.astype(jnp.bfloat16), wo_ref[...])


def _out_proj(x, norm_w, oa, ob, wz, wa, wb, wo):
    B, S, _ = x.shape
    nw = norm_w.reshape(1, D_MODEL)
    tok = lambda w: pl.BlockSpec((1, TM, w), lambda b, i: (b, i, 0))
    full = lambda a: pl.BlockSpec(a.shape, lambda b, i: (0,) * a.ndim)
    return pl.pallas_call(
        _out_proj_body,
        grid=(B, S // TM),
        in_specs=[tok(D_MODEL), full(nw), tok(512), tok(512), full(wz), full(wa), full(wb), full(wo)],
        out_specs=tok(D_MODEL),
        out_shape=jax.ShapeDtypeStruct((B, S, D_MODEL), jnp.float32),
        compiler_params=pltpu.CompilerParams(dimension_semantics=("parallel", "parallel"),
                                             vmem_limit_bytes=VMEM_LIMIT),
        name="out_proj",
    )(x, nw, oa, ob, wz, wa, wb, wo)


def _pack_in_weights(w_in, q_norm_a, k_norm_a, q_norm_b, k_norm_sel, k_norm_win):
    bf = jnp.bfloat16
    w1 = jnp.concatenate([w_in[:, 0:1536], w_in[:, 2048:2560], w_in[:, 2560:3328],
                          w_in[:, 3328:3352], jnp.zeros((D_MODEL, LANES - 3 * NSA_HEADS), w_in.dtype)], axis=1)
    wz = jnp.concatenate([w_in[:, 1536:2048], w_in[:, 3352:3864], w_in[:, 3864:5912]], axis=1)
    ones = lambda n: jnp.ones((n,), jnp.float32)
    gain = jnp.concatenate([jnp.tile(q_norm_a, MOBA_HEADS), jnp.tile(k_norm_a, MOBA_HEADS), ones(512),
                            jnp.tile(q_norm_b, NSA_HEADS), ones(256), jnp.tile(k_norm_sel, NSA_GROUPS), ones(128),
                            jnp.tile(k_norm_win, NSA_GROUPS), ones(256)]).reshape(1, _N1)
    return w1.astype(bf), wz.astype(bf), gain


def kernel(x, norm_w, w_in, q_norm_a, k_norm_a, q_norm_b, k_norm_cmp, k_norm_sel, k_norm_win, cmp_pos_k, cmp_w1_k, cmp_w2_k, cmp_pos_v, cmp_w1_v, cmp_w2_v, rel_bias, w_branch_a, w_branch_b, w_out):
    bf = jnp.bfloat16
    w1, wz, gain = _pack_in_weights(w_in[0], q_norm_a[0], k_norm_a[0], q_norm_b[0], k_norm_sel[0], k_norm_win[0])
    bias = _bias_tiles(rel_bias)
    qa, kae, kao, vae, vao, qb, kmean, kc, vc, kse, kso, vsl, kwn, vwn, gate = _in_proj(x, norm_w[0], w1, gain)
    kcmp, vcmp = _compress(kc, vc, cmp_pos_k[0], cmp_w1_k[0], cmp_w2_k[0],
                           cmp_pos_v[0], cmp_w1_v[0], cmp_w2_v[0], k_norm_cmp[0])
    oa = _moba(qa, kae, kao, vae, vao, kmean[:, :, 0, :], bias)
    ob = _nsa(qb, kcmp, vcmp, kse, kso, vsl, kwn, vwn, gate, bias)
    return _out_proj(x, norm_w[0], oa, ob, wz, w_branch_a[0].astype(bf), w_branch_b[0].astype(bf),
                     w_out[0].astype(bf))
```

```python
import math

import jax
import jax.numpy as jnp
import numpy as np
from jax import lax
from jax.experimental import pallas as pl
from jax.experimental.pallas import tpu as pltpu

D_MODEL = 1024
HEAD_DIM = 64
MOBA_HEADS = 8
NSA_HEADS = 8
NSA_GROUPS = 2
NSA_HPG = NSA_HEADS // NSA_GROUPS
MOBA_BLOCK = 256
MOBA_TOPK = 3
CMP_LEN = 32
CMP_STRIDE = 16
CMP_HIDDEN = 256
SEL_BLOCK = 64
SEL_TOPN = 16
WINDOW = 512
NUM_BUCKETS = 32
MAX_DISTANCE = 1024
EPS = 1e-6
NEG = -1e30

LANES = 128
TQ = 256
TK = 2 * TQ
TR = 2 * TQ
TM = 2 * TQ
N_FAR = 5
N_WIN_TILES = WINDOW // TQ + 1
VMEM_LIMIT = 56 * 1024 * 1024
LOG2E = math.log2(math.e)
QSCALE = LOG2E / math.sqrt(HEAD_DIM)

_K_WIN = N_FAR + 1
_K_MASKED = _K_WIN + N_WIN_TILES
_N_TILE_KINDS = _K_MASKED + 1

_C_QA, _C_KA, _C_VA, _C_ZA, _C_QB = 0, 512, 1024, 1536, 2048
_C_KC, _C_VC, _C_KSL, _C_VSL, _C_KWN, _C_VWN, _C_GATE = 2560, 2688, 2816, 2944, 3072, 3200, 3328
_C_ZB, _C_GM, _N_ALL = 3456, 3968, 6016
_N1 = _C_ZB


def _nt(a, b):
    return lax.dot_general(a, b, (((1,), (1,)), ((), ())), preferred_element_type=jnp.float32)


def _nn(a, b):
    return jnp.dot(a, b, preferred_element_type=jnp.float32)


def _split(x):
    hi = x.astype(jnp.bfloat16)
    lo = (x - hi.astype(jnp.float32)).astype(jnp.bfloat16)
    return hi, lo


def _bucket_table(n):
    d = np.arange(n)
    nf = np.maximum(d, NUM_BUCKETS // 2).astype(np.float64)
    large = NUM_BUCKETS // 2 + (np.log(nf / (NUM_BUCKETS // 2)) / math.log(MAX_DISTANCE / (NUM_BUCKETS // 2))
                                * (NUM_BUCKETS - NUM_BUCKETS // 2)).astype(np.int64)
    return np.where(d < NUM_BUCKETS // 2, d, np.minimum(large, NUM_BUCKETS - 1))


def _bias_onehot():
    oh = np.zeros((40, _N_TILE_KINDS * 2 * TQ), np.float32)
    buckets = _bucket_table((N_FAR + 2) * TQ + 1)
    for t in range(_N_TILE_KINDS):
        window = _K_WIN <= t < _K_MASKED
        delta = t - _K_WIN if window else t
        for u in range(2 * TQ):
            d = TQ * delta + TQ - u
            masked = t == _K_MASKED or d < 0 or (window and d >= WINDOW)
            oh[NUM_BUCKETS if masked else buckets[d], t * 2 * TQ + u] = 1.0
    return oh


def _bias_body(relt_ref, oh_ref, out_ref):
    g = jnp.dot(jnp.broadcast_to(relt_ref[0], (8, 40)), oh_ref[...], preferred_element_type=jnp.float32,
                precision=lax.Precision.HIGHEST)
    for t in range(_N_TILE_KINDS):
        full = jnp.broadcast_to(g[0:1, t * 2 * TQ:(t + 1) * 2 * TQ], (TQ, 2 * TQ))
        rolled = pltpu.roll(full, TQ, 1, stride=1, stride_axis=0)
        out_ref[0, t] = rolled[:, :TQ]


def _bias_tiles(rel_bias):
    nh = MOBA_HEADS + NSA_HEADS
    relt = jnp.concatenate([rel_bias.T.astype(jnp.float32) * LOG2E, jnp.full((nh, 1), NEG, jnp.float32),
                            jnp.zeros((nh, 7), jnp.float32)], axis=1).reshape(nh, 1, 40)
    oh = jnp.asarray(_bias_onehot())
    return pl.pallas_call(
        _bias_body,
        grid=(nh,),
        in_specs=[pl.BlockSpec((1, 1, 40), lambda h: (h, 0, 0)), pl.BlockSpec(oh.shape, lambda h: (0, 0))],
        out_specs=pl.BlockSpec((1, _N_TILE_KINDS, TQ, TQ), lambda h: (h, 0, 0, 0)),
        out_shape=jax.ShapeDtypeStruct((nh, _N_TILE_KINDS, TQ, TQ), jnp.float32),
        compiler_params=pltpu.CompilerParams(dimension_semantics=("parallel",), vmem_limit_bytes=VMEM_LIMIT),
        name="bias_tiles",
    )(relt, oh)


def _seg_norm(x, lo_half, gain):
    x2 = x * x
    ss_lo = jnp.sum(jnp.where(lo_half, x2, 0.0), axis=-1, keepdims=True)
    ss_hi = jnp.sum(jnp.where(lo_half, 0.0, x2), axis=-1, keepdims=True)
    ss = jnp.where(lo_half, ss_lo, ss_hi)
    return x * lax.rsqrt(ss * (1.0 / HEAD_DIM) + EPS) * gain


def _in_proj_body(x_ref, nw_ref, w_ref, gain_ref,
                  qa_ref, kae_ref, kao_ref, vae_ref, vao_ref, qb_ref, kmean_ref, kc_ref, vc_ref,
                  kse_ref, kso_ref, vsl_ref, kwn_ref, vwn_ref, gate_ref, stage_k_ref, stage_v_ref):
    i = pl.program_id(1)
    x = x_ref[0]
    ms = jnp.mean(x * x, axis=-1, keepdims=True)
    h = (x * lax.rsqrt(ms + EPS) * nw_ref[...]).astype(jnp.bfloat16)
    bf = jnp.bfloat16
    lane = lax.broadcasted_iota(jnp.int32, (TM, LANES), 1)
    row = lax.broadcasted_iota(jnp.int32, (TM, LANES), 0)
    lo_half = lane < HEAD_DIM
    blk = i * (TM // MOBA_BLOCK) + lax.shift_right_logical(row, MOBA_BLOCK.bit_length() - 1)
    sblk = i * (TM // SEL_BLOCK) + lax.shift_right_logical(row, SEL_BLOCK.bit_length() - 1)

    def normed(col, width, scale=None):
        p = _nn(h, w_ref[:, col:col + width])
        outs = []
        for s in range(width // LANES):
            c = col + s * LANES
            y = _seg_norm(p[:, s * LANES:(s + 1) * LANES], lo_half, gain_ref[:, c:c + LANES])
            outs.append(y if scale is None else y * scale)
        return outs

    qa = normed(_C_QA, 512, QSCALE)
    for s in range(4):
        qa_ref[0, :, s * LANES:(s + 1) * LANES] = qa[s].astype(bf)
    ka = normed(_C_KA, 512)
    oh_hi = jnp.where(lane == HEAD_DIM + blk, 1.0, 0.0)
    oh_lo = jnp.where(lane == blk, 1.0, 0.0)
    for s in range(4):
        kae_ref[0, :, s * LANES:(s + 1) * LANES] = jnp.where(lo_half, ka[s], oh_hi).astype(bf)
        kao_ref[0, :, s * LANES:(s + 1) * LANES] = jnp.where(lo_half, oh_lo, ka[s]).astype(bf)
        for j in range(TM // MOBA_BLOCK):
            kmean_ref[0, j, :, s * LANES:(s + 1) * LANES] = jnp.broadcast_to(
                jnp.mean(ka[s][j * MOBA_BLOCK:(j + 1) * MOBA_BLOCK], axis=0, keepdims=True), (8, LANES))
    one_hi = jnp.where(lane == HEAD_DIM, 1.0, 0.0)
    one_lo = jnp.where(lane == 0, 1.0, 0.0)
    va = _nn(h, w_ref[:, _C_VA:_C_VA + 512])
    for s in range(4):
        vs = va[:, s * LANES:(s + 1) * LANES]
        vae_ref[0, :, s * LANES:(s + 1) * LANES] = jnp.where(lo_half, vs, one_hi).astype(bf)
        vao_ref[0, :, s * LANES:(s + 1) * LANES] = jnp.where(lo_half, one_lo, vs).astype(bf)
    qb = normed(_C_QB, 512, QSCALE)
    for s in range(4):
        qb_ref[0, :, s * LANES:(s + 1) * LANES] = qb[s].astype(bf)

    rest = _nn(h, w_ref[:, _C_KC:_C_ZB])

    def slab(c):
        return rest[:, c - _C_KC:c - _C_KC + LANES]

    def chunk_store(ref, stage_ref, y):
        stage_ref[...] = y
        lo = lax.broadcasted_iota(jnp.int32, (TM // CMP_STRIDE, LANES), 1) < HEAD_DIM
        for u in range(CMP_STRIDE // 2):
            t0 = stage_ref[pl.ds(2 * u, TM // CMP_STRIDE, stride=CMP_STRIDE), :]
            t1 = stage_ref[pl.ds(2 * u + 1, TM // CMP_STRIDE, stride=CMP_STRIDE), :]
            ref[0, 0, :, u * LANES:(u + 1) * LANES] = jnp.where(lo, t0, pltpu.roll(t1, HEAD_DIM, 1)).astype(bf)
            ref[0, 1, :, u * LANES:(u + 1) * LANES] = jnp.where(lo, pltpu.roll(t0, HEAD_DIM, 1), t1).astype(bf)

    chunk_store(kc_ref, stage_k_ref, slab(_C_KC))
    chunk_store(vc_ref, stage_v_ref, slab(_C_VC))

    def dup_store(ref, y):
        r = pltpu.roll(y, HEAD_DIM, 1)
        ref[0, 0] = jnp.where(lo_half, y, r).astype(bf)
        ref[0, 1] = jnp.where(lo_half, r, y).astype(bf)

    y = _seg_norm(slab(_C_KSL), lo_half, gain_ref[:, _C_KSL:_C_KSL + LANES])
    r = pltpu.roll(y, HEAD_DIM, 1)
    oh = jnp.where((lane & (HEAD_DIM - 1)) == sblk, 1.0, 0.0)
    kse_ref[0, 0] = jnp.where(lo_half, y, oh).astype(bf)
    kso_ref[0, 0] = jnp.where(lo_half, oh, r).astype(bf)
    kse_ref[0, 1] = jnp.where(lo_half, r, oh).astype(bf)
    kso_ref[0, 1] = jnp.where(lo_half, oh, y).astype(bf)
    def aug_store(ref, y):
        ref[0, 0] = jnp.where(lo_half, y, one_hi).astype(bf)
        ref[0, 1] = jnp.where(lo_half, pltpu.roll(y, HEAD_DIM, 1), one_hi).astype(bf)

    aug_store(vsl_ref, slab(_C_VSL))
    dup_store(kwn_ref, _seg_norm(slab(_C_KWN), lo_half, gain_ref[:, _C_KWN:_C_KWN + LANES]))
    aug_store(vwn_ref, slab(_C_VWN))
    gate_ref[0] = slab(_C_GATE)


def _in_proj(x, norm_w, w1, gain_row):
    B, S, _ = x.shape
    nt = S // TQ
    bf = jnp.bfloat16
    tok = lambda w, dt: jax.ShapeDtypeStruct((B, S, w), dt)
    grp = jax.ShapeDtypeStruct((B, NSA_GROUPS, S, LANES), bf)
    tok_spec = lambda w: pl.BlockSpec((1, TM, w), lambda b, i: (b, i, 0))
    grp_spec = pl.BlockSpec((1, NSA_GROUPS, TM, LANES), lambda b, i: (b, 0, i, 0))
    flat = CMP_STRIDE * HEAD_DIM
    chunk = jax.ShapeDtypeStruct((B, NSA_GROUPS, S // CMP_STRIDE, flat), bf)
    chunk_spec = pl.BlockSpec((1, NSA_GROUPS, TM // CMP_STRIDE, flat), lambda b, i: (b, 0, i, 0))
    full = lambda a: pl.BlockSpec(a.shape, lambda b, i: (0,) * a.ndim, pipeline_mode=pl.Buffered(1))
    nw = norm_w.reshape(1, D_MODEL)
    return pl.pallas_call(
        _in_proj_body,
        grid=(B, S // TM),
        in_specs=[tok_spec(D_MODEL), full(nw), full(w1), full(gain_row)],
        out_specs=[tok_spec(512), tok_spec(512), tok_spec(512), tok_spec(512), tok_spec(512), tok_spec(512),
                   pl.BlockSpec((1, TM // MOBA_BLOCK, 8, 512), lambda b, i: (b, i, 0, 0)),
                   chunk_spec, chunk_spec,
                   grp_spec, grp_spec, grp_spec, grp_spec, grp_spec, tok_spec(LANES)],
        out_shape=[tok(512, bf), tok(512, bf), tok(512, bf), tok(512, bf), tok(512, bf), tok(512, bf),
                   jax.ShapeDtypeStruct((B, nt, 8, 512), jnp.float32),
                   chunk, chunk, grp, grp, grp, grp, grp, tok(LANES, jnp.float32)],
        scratch_shapes=[pltpu.VMEM((TM, LANES), jnp.float32)] * 2,
        compiler_params=pltpu.CompilerParams(dimension_semantics=("parallel", "parallel"),
                                             vmem_limit_bytes=VMEM_LIMIT),
        name="in_proj",
    )(x, nw, w1, gain_row)


def _gelu_tanh(x):
    return 0.5 * x * (1.0 + jnp.tanh(math.sqrt(2.0 / math.pi) * (x + 0.044715 * (x * x * x))))


def _compress_body(xk_ref, xv_ref, w1k_ref, w1v_ref, pk_ref, pv_ref,
                   w2k_ref, w2v_ref, gk_ref, kcmp_ref, vcmp_ref):
    nchunk = xk_ref.shape[2]
    half = CMP_STRIDE * HEAD_DIM

    def branch(x_ref, w1_ref, pos_ref, w2_ref, gain):
        pos = jnp.broadcast_to(pos_ref[...], (8, CMP_LEN * HEAD_DIM))
        posw = (_nn(pos[:, 0:half], w1_ref[...])[0:1, 0:CMP_HIDDEN]
                + _nn(pos[:, half:], w1_ref[...])[0:1, CMP_HIDDEN:])
        outs = []
        for g in range(NSA_GROUPS):
            ab = _nn(x_ref[0, g], w1_ref[...])
            top, bot = ab[:, 0:CMP_HIDDEN], ab[:, CMP_HIDDEN:]
            hid = top + pltpu.roll(bot, nchunk - 1, 0) + posw
            y = _nn(_gelu_tanh(hid).astype(jnp.bfloat16), w2_ref[...])
            if gain is not None:
                ms = jnp.sum(y * y, axis=-1, keepdims=True) * (0.5 / HEAD_DIM)
                y = y * lax.rsqrt(ms + EPS) * gain
            else:
                lane = lax.broadcasted_iota(jnp.int32, y.shape, 1)
                y = jnp.where(lane < HEAD_DIM, y, jnp.where(lane == HEAD_DIM, 1.0, 0.0))
            outs.append(y.astype(jnp.bfloat16))
        return outs

    k0, k1 = branch(xk_ref, w1k_ref, pk_ref, w2k_ref, gk_ref[...])
    kcmp_ref[0, 0], kcmp_ref[0, 1] = k0, k1
    v0, v1 = branch(xv_ref, w1v_ref, pv_ref, w2v_ref, None)
    vcmp_ref[0, 0], vcmp_ref[0, 1] = v0, v1


def _compress(xk, xv, pos_k, w1_k, w2_k, pos_v, w1_v, w2_v, gain_cmp):
    B, _, nchunk, half = xk.shape
    bf = jnp.bfloat16
    halves = lambda w1: jnp.concatenate([w1[:half], w1[half:]], axis=1).astype(bf)
    args = [xk, xv, halves(w1_k), halves(w1_v),
            pos_k.reshape(1, -1).astype(bf), pos_v.reshape(1, -1).astype(bf),
            jnp.concatenate([w2_k, w2_k], axis=1).astype(bf), jnp.concatenate([w2_v, w2_v], axis=1).astype(bf),
            jnp.concatenate([gain_cmp, gain_cmp]).reshape(1, LANES)]
    x_spec = pl.BlockSpec((1, NSA_GROUPS, nchunk, half), lambda b: (b, 0, 0, 0))
    full = lambda a: pl.BlockSpec(a.shape, lambda b: (0,) * a.ndim)
    out = jax.ShapeDtypeStruct((B, NSA_GROUPS, nchunk, LANES), bf)
    o_spec = pl.BlockSpec((1, NSA_GROUPS, nchunk, LANES), lambda b: (b, 0, 0, 0))
    return pl.pallas_call(
        _compress_body,
        grid=(B,),
        in_specs=[x_spec, x_spec] + [full(a) for a in args[2:]],
        out_specs=[o_spec, o_spec],
        out_shape=[out, out],
        compiler_params=pltpu.CompilerParams(dimension_semantics=("parallel",), vmem_limit_bytes=VMEM_LIMIT),
        name="compress",
    )(*args)


def _softmax_step(s, v_top, v_bot, m_ref, acc_ref, first):
    blocks = [s[:, c * LANES:(c + 1) * LANES] for c in range(s.shape[1] // LANES)]
    bmax = blocks[0]
    for blk in blocks[1:]:
        bmax = jnp.maximum(bmax, blk)
    m_new = jnp.broadcast_to(jnp.max(bmax, axis=-1, keepdims=True), m_ref.shape)
    if not first:
        m_old = m_ref[...]
        m_new = jnp.maximum(m_old, m_new)
        alpha = jnp.exp2(m_old - m_new)
    pb = jnp.concatenate([jnp.exp2(blk - m_new) for blk in blocks], axis=1).astype(jnp.bfloat16)
    half = s.shape[0] // 2
    pv = jnp.concatenate([_nn(pb[0:half], v_top), _nn(pb[half:], v_bot)], axis=0)
    acc_ref[...] = pv if first else alpha * acc_ref[...] + pv
    m_ref[...] = m_new


def _finish(acc, ones_lane):
    lane = lax.broadcasted_iota(jnp.int32, acc.shape, 1)
    return acc / jnp.sum(jnp.where(lane == ones_lane, acc, 0.0), axis=-1, keepdims=True)


def _tile_kind(delta):
    return jnp.where(delta >= 0, jnp.minimum(delta, N_FAR), _K_MASKED)


def _chunk_kinds(qi, c):
    d0 = qi - 2 * c
    kind0 = jnp.minimum(d0, N_FAR)
    kind1 = jnp.where(d0 >= 1, jnp.minimum(d0 - 1, N_FAR), _K_MASKED)
    return kind0, kind1


def _moba_body(q_ref, ke_ref, ko_ref, ve_ref, vo_ref, kmean_ref, eye_ref, bias_ref, o_ref, m_ref, acc_ref):
    i = pl.program_id(2)
    nblk = kmean_ref.shape[1]
    lane = lax.broadcasted_iota(jnp.int32, (TR, LANES), 1)
    lo_half = lane < HEAD_DIM
    lane16 = lax.broadcasted_iota(jnp.int32, (nblk, LANES), 1)
    jrow = lax.broadcasted_iota(jnp.int32, (nblk, TR), 0)
    qblk = 2 * i + jnp.where(lax.broadcasted_iota(jnp.int32, (nblk, TR), 1) >= TQ, 1, 0)
    q2 = q_ref[0]
    kmean2 = kmean_ref[0]
    flags = []
    for hh in range(2):
        inhead16 = (lane16 < HEAD_DIM) if hh == 0 else (lane16 >= HEAD_DIM)
        km_hi, km_lo = _split(jnp.where(inhead16, kmean2, 0.0))
        sc = jnp.where(jrow < qblk, _nt(km_hi, q2) + _nt(km_lo, q2), -jnp.inf)
        flag = jnp.where(jrow == qblk, 0.0, 1.0)
        for _ in range(MOBA_TOPK):
            best = jnp.max(sc, axis=0, keepdims=True)
            first = jnp.min(jnp.where(sc == best, jrow, nblk), axis=0, keepdims=True)
            hit = jrow == first
            flag = jnp.where(hit, 0.0, flag)
            sc = jnp.where(hit, -jnp.inf, sc)
        flags.append(flag)
    pad = jnp.zeros((HEAD_DIM - nblk, TR), jnp.float32)
    placed = jnp.concatenate([flags[1], pad, flags[0], pad], axis=0).astype(jnp.bfloat16)
    maskpart = jnp.concatenate([_nt(eye_ref[...], placed[:, a * TQ:(a + 1) * TQ]) for a in range(TR // TQ)],
                               axis=0) * NEG
    q2f = q2.astype(jnp.float32)
    zero = jnp.zeros((TR, LANES), jnp.bfloat16)
    qq = jnp.concatenate(
        [jnp.concatenate([jnp.where(lo_half, q2f, maskpart).astype(jnp.bfloat16), zero], axis=1),
         jnp.concatenate([zero, jnp.where(lo_half, maskpart, q2f).astype(jnp.bfloat16)], axis=1)], axis=0)

    def step(c, first):
        start = pl.multiple_of(c * TK, TK)
        kk = jnp.concatenate([ke_ref[0, pl.ds(start, TK), :], ko_ref[0, pl.ds(start, TK), :]], axis=1)
        d = 2 * (i - c)
        b = jnp.concatenate(
            [jnp.concatenate([bias_ref[hh, _tile_kind(d + a - t)] for t in range(2)], axis=1)
             for hh in range(2) for a in range(2)], axis=0)
        _softmax_step(_nt(qq, kk) + b, ve_ref[0, pl.ds(start, TK), :], vo_ref[0, pl.ds(start, TK), :],
                      m_ref, acc_ref, first)

    step(i, True)

    def body(c, carry):
        step(c, False)
        return carry

    lax.fori_loop(0, i, body, 0)
    o_ref[0] = jnp.where(lo_half, _finish(acc_ref[0:TR], HEAD_DIM), _finish(acc_ref[TR:2 * TR], 0))


def _eye(n):
    return jnp.asarray(np.eye(n, dtype=np.float32), jnp.bfloat16)


def _moba(qa, kae, kao, vae, vao, kmean, bias):
    B, S, _ = qa.shape
    nt = S // TQ
    npair = MOBA_HEADS // 2
    kv_spec = pl.BlockSpec((1, S, LANES), lambda b, p, i: (b, 0, p))
    return pl.pallas_call(
        _moba_body,
        grid=(B, npair, S // TR),
        in_specs=[pl.BlockSpec((1, TR, LANES), lambda b, p, i: (b, i, p)),
                  kv_spec, kv_spec, kv_spec, kv_spec,
                  pl.BlockSpec((1, nt, LANES), lambda b, p, i: (b, 0, p)),
                  pl.BlockSpec((TQ, TQ), lambda b, p, i: (0, 0)),
                  pl.BlockSpec((2, _N_TILE_KINDS, TQ, TQ), lambda b, p, i: (p, 0, 0, 0))],
        out_specs=pl.BlockSpec((1, TR, LANES), lambda b, p, i: (b, i, p)),
        out_shape=jax.ShapeDtypeStruct((B, S, MOBA_HEADS * HEAD_DIM), jnp.float32),
        scratch_shapes=[pltpu.VMEM((2 * TR, LANES), jnp.float32)] * 2,
        compiler_params=pltpu.CompilerParams(dimension_semantics=("parallel", "parallel", "arbitrary"),
                                             vmem_limit_bytes=VMEM_LIMIT),
        name="moba",
    )(qa, kae, kao, vae, vao, kmean, _eye(TQ), bias)


def _nsa_body(q_ref, kcmp_ref, vcmp_ref, kse_ref, kso_ref, vsl_ref, kwn_ref, vwn_ref, gate_ref, ov_ref, eye_ref,
              gexp_ref, bias_ref, o_ref, m_ref, acc_ref):
    i = pl.program_id(2)

    def to_hi(o):
        return pltpu.roll(o, HEAD_DIM, 1)

    ncmp = kcmp_ref.shape[2]
    nsel = ov_ref.shape[0]
    R = NSA_HPG * TR
    bf = jnp.bfloat16
    lane = lax.broadcasted_iota(jnp.int32, (TR, LANES), 1)
    lo_half = lane < HEAD_DIM
    qf = [q_ref[0, :, 0:LANES].astype(jnp.float32), q_ref[0, :, LANES:2 * LANES].astype(jnp.float32)]
    q4 = jnp.concatenate([jnp.where(lo_half if hh % 2 == 0 else ~lo_half, qf[hh // 2], 0.0).astype(bf)
                          for hh in range(NSA_HPG)], axis=0)

    o_win = []
    for a in range(2):
        qt = 2 * i + a
        t0 = jnp.maximum(qt - (N_WIN_TILES - 1), 0)
        wstart = pl.multiple_of(t0 * TQ, TQ)
        qh = jnp.concatenate([q4[hh * TR + a * TQ:hh * TR + (a + 1) * TQ] for hh in range(NSA_HPG)], axis=0)
        kinds = []
        for r in range(N_WIN_TILES):
            dd = qt - (t0 + r)
            kinds.append(jnp.where(dd >= 0, _K_WIN + dd, _K_MASKED))
        bw = jnp.concatenate([jnp.concatenate([bias_ref[hh, k] for k in kinds], axis=1)
                              for hh in range(NSA_HPG)], axis=0)
        sw = _nt(qh, kwn_ref[0, 0, pl.ds(wstart, N_WIN_TILES * TQ), :]) + bw
        wm, wacc = m_ref.at[0:NSA_HPG * TQ], acc_ref.at[0:NSA_HPG * TQ]
        vw = vwn_ref[0, 0, pl.ds(wstart, N_WIN_TILES * TQ), :]
        _softmax_step(sw, vw, vw, wm, wacc, True)
        o_win.append(_finish(wacc[...], HEAD_DIM))

    tpos = i * TR + lax.broadcasted_iota(jnp.int32, (TR, ncmp), 0)
    cidx = lax.broadcasted_iota(jnp.int32, (TR, ncmp), 1)
    cbias1 = jnp.where((cidx * CMP_STRIDE + (CMP_LEN - 1) <= tpos) & (cidx < ncmp - 1), 0.0, NEG)
    lc = _nt(q4, kcmp_ref[0, 0]) + jnp.concatenate([cbias1] * NSA_HPG, axis=0)
    cblocks = [lc[:, c * LANES:(c + 1) * LANES] for c in range(ncmp // LANES)]
    cmax = cblocks[0]
    for blk in cblocks[1:]:
        cmax = jnp.maximum(cmax, blk)
    cm = jnp.maximum(jnp.broadcast_to(jnp.max(cmax, axis=-1, keepdims=True), (R, LANES)), 0.5 * NEG)
    ce = [jnp.exp2(blk - cm) for blk in cblocks]
    cden = ce[0]
    for eb in ce[1:]:
        cden = cden + eb
    cden = jnp.broadcast_to(jnp.sum(cden, axis=-1, keepdims=True), (R, LANES))
    crcp = 1.0 / jnp.where(cden > 0.0, cden, 1.0)
    pc = jnp.concatenate([eb * crcp for eb in ce], axis=1)
    pcb = pc.astype(bf)
    o_cmp = jnp.concatenate([_nn(pcb[0:R // 2], vcmp_ref[0, 0]), _nn(pcb[R // 2:], vcmp_ref[0, 0])], axis=0)
    psum = pc[0:TR] + pc[TR:2 * TR] + pc[2 * TR:3 * TR] + pc[3 * TR:4 * TR]
    p_hi, p_lo = _split(psum)
    ov = ov_ref[...]
    imp = _nt(ov, p_hi) + _nt(ov, p_lo)

    jrow = lax.broadcasted_iota(jnp.int32, (nsel, TR), 0)
    qblk = lax.shift_right_logical(i * TR + lax.broadcasted_iota(jnp.int32, (nsel, TR), 1),
                                   SEL_BLOCK.bit_length() - 1)
    cand = (jrow >= 1) & (jrow < qblk)
    keys = jnp.where(cand, imp, -1.0)
    flag = jnp.where((jrow == 0) | (jrow == qblk), 0.0, 1.0)
    for _ in range(SEL_TOPN - 2):
        best = jnp.max(keys, axis=0, keepdims=True)
        first = jnp.min(jnp.where(keys == best, jrow, nsel), axis=0, keepdims=True)
        hit = jrow == first
        flag = jnp.where(hit, 0.0, flag)
        keys = jnp.where(hit, -1.0, keys)
    if nsel < HEAD_DIM:
        flag = jnp.concatenate([flag, jnp.zeros((HEAD_DIM - nsel, TR), jnp.float32)], axis=0)
    flag2 = jnp.concatenate([flag, flag], axis=0).astype(bf)
    maskpart = jnp.concatenate([_nt(eye_ref[...], flag2[:, a * TQ:(a + 1) * TQ]) for a in range(TR // TQ)],
                               axis=0) * NEG

    zero = jnp.zeros((2 * TR, LANES), bf)
    qa2 = [jnp.concatenate([jnp.where(lo_half if par == 0 else ~lo_half, qf[s], maskpart).astype(bf)
                            for s in range(2)], axis=0) for par in range(2)]
    qq = jnp.concatenate([jnp.concatenate([qa2[0], zero], axis=1),
                          jnp.concatenate([zero, qa2[1]], axis=1)], axis=0)
    sel_heads = (0, 2, 1, 3)

    def sel_step(c, first):
        start = pl.multiple_of(c * TK, TK)
        kk = jnp.concatenate([kse_ref[0, 0, pl.ds(start, TK), :], kso_ref[0, 0, pl.ds(start, TK), :]], axis=1)
        d = 2 * (i - c)
        b = jnp.concatenate(
            [jnp.concatenate([bias_ref[h, _tile_kind(d + a - t)] for t in range(2)], axis=1)
             for h in sel_heads for a in range(2)], axis=0)
        v = vsl_ref[0, 0, pl.ds(start, TK), :]
        _softmax_step(_nt(qq, kk) + b, v, v, m_ref, acc_ref, first)

    sel_step(i, True)

    def sel_body(c, carry):
        sel_step(c, False)
        return carry

    lax.fori_loop(0, i, sel_body, 0)
    o_all = _finish(acc_ref[...], HEAD_DIM)
    o_sel = [o_all[0:2 * TR], to_hi(o_all[2 * TR:4 * TR])]

    g = jax.nn.sigmoid(gate_ref[0])
    g_hi, g_lo = _split(g)
    gx = _nn(g_hi, gexp_ref[0]) + _nn(g_lo, gexp_ref[0])
    for s in range(2):
        h_even, h_odd = 2 * s, 2 * s + 1
        cmp2 = jnp.where(lo_half, o_cmp[h_even * TR:(h_even + 1) * TR], to_hi(o_cmp[h_odd * TR:(h_odd + 1) * TR]))
        sel2 = jnp.where(lo_half, o_sel[0][s * TR:(s + 1) * TR], o_sel[1][s * TR:(s + 1) * TR])
        win2 = jnp.where(lo_half,
                         jnp.concatenate([o_win[a][h_even * TQ:(h_even + 1) * TQ] for a in range(2)], axis=0),
                         to_hi(jnp.concatenate([o_win[a][h_odd * TQ:(h_odd + 1) * TQ] for a in range(2)], axis=0)))
        base = s * 3 * LANES
        o_ref[0, :, s * LANES:(s + 1) * LANES] = (gx[:, base:base + LANES] * cmp2
                                                  + gx[:, base + LANES:base + 2 * LANES] * sel2
                                                  + gx[:, base + 2 * LANES:base + 3 * LANES] * win2)


def _gate_expand():
    e = np.zeros((NSA_GROUPS, LANES, 2 * 3 * LANES), np.float32)
    for g in range(NSA_GROUPS):
        for s in range(2):
            for c in range(3):
                for half in range(2):
                    h = g * NSA_HPG + 2 * s + half
                    col = (s * 3 + c) * LANES + half * HEAD_DIM
                    e[g, 3 * h + c, col:col + HEAD_DIM] = 1.0
    return e


def _nsa(qb, kcmp, vcmp, kse, kso, vsl, kwn, vwn, gate, bias):
    B, S, _ = qb.shape
    nt = S // TQ
    ncmp = kcmp.shape[2]
    nsel = S // SEL_BLOCK
    bf = jnp.bfloat16
    cs = np.arange(ncmp) * CMP_STRIDE
    bs = np.arange(nsel) * SEL_BLOCK
    ov = ((cs[None, :] < bs[:, None] + SEL_BLOCK) & (cs[None, :] + CMP_LEN > bs[:, None])
          & (np.arange(ncmp)[None, :] < ncmp - 1)).astype(np.float32)
    grp = lambda n: pl.BlockSpec((1, 1, n, LANES), lambda b, g, i: (b, g, 0, 0))
    return pl.pallas_call(
        _nsa_body,
        grid=(B, NSA_GROUPS, S // TR),
        in_specs=[pl.BlockSpec((1, TR, 2 * LANES), lambda b, g, i: (b, i, g)),
                  grp(ncmp), grp(ncmp), grp(S), grp(S), grp(S), grp(S), grp(S),
                  pl.BlockSpec((1, TR, LANES), lambda b, g, i: (b, i, 0)),
                  pl.BlockSpec((nsel, ncmp), lambda b, g, i: (0, 0)),
                  pl.BlockSpec((TQ, TQ), lambda b, g, i: (0, 0)),
                  pl.BlockSpec((1, LANES, 6 * LANES), lambda b, g, i: (g, 0, 0)),
                  pl.BlockSpec((NSA_HPG, _N_TILE_KINDS, TQ, TQ), lambda b, g, i: (2 + g, 0, 0, 0),
                               pipeline_mode=pl.Buffered(1))],
        out_specs=pl.BlockSpec((1, TR, 2 * LANES), lambda b, g, i: (b, i, g)),
        out_shape=jax.ShapeDtypeStruct((B, S, NSA_HEADS * HEAD_DIM), jnp.float32),
        scratch_shapes=[pltpu.VMEM((NSA_HPG * TR, LANES), jnp.float32)] * 2,
        compiler_params=pltpu.CompilerParams(dimension_semantics=("parallel", "parallel", "arbitrary"),
                                             vmem_limit_bytes=VMEM_LIMIT),
        name="nsa",
    )(qb, kcmp, vcmp, kse, kso, vsl, kwn, vwn, gate, jnp.asarray(ov, bf), _eye(TQ),
      jnp.asarray(_gate_expand(), bf), bias)


def _out_proj_body(x_ref, nw_ref, oa_ref, ob_ref, wz_ref, wa_ref, wb_ref, wo_ref, out_ref):
    x = x_ref[0]
    ms = jnp.mean(x * x, axis=-1, keepdims=True)
    h = (x * lax.rsqrt(ms + EPS) * nw_ref[...]).astype(jnp.bfloat16)
    ya = (oa_ref[0] * jax.nn.silu(_nn(h, wz_ref[:, _C_ZA:_C_ZA + 512]))).astype(jnp.bfloat16)
    yb = (ob_ref[0] * jax.nn.silu(_nn(h, wz_ref[:, _C_ZB:_C_ZB + 512]))).astype(jnp.bfloat16)
    gm = jax.nn.sigmoid(_nn(h, wz_ref[:, _C_GM:_C_GM + 2 * D_MODEL]))
    merged = gm[:, 0:D_MODEL] * _nn(ya, wa_ref[...]) + gm[:, D_MODEL:] * _nn(yb, wb_ref[...])
    out_ref[0] = x + _nn(merged.astype(jnp.bfloat16), wo_ref[...])


def _out_proj(x, norm_w, oa, ob, wz, wa, wb, wo):
    B, S, _ = x.shape
    nw = norm_w.reshape(1, D_MODEL)
    tok = lambda w: pl.BlockSpec((1, TM, w), lambda b, i: (b, i, 0))
    full = lambda a: pl.BlockSpec(a.shape, lambda b, i: (0,) * a.ndim, pipeline_mode=pl.Buffered(1))
    return pl.pallas_call(
        _out_proj_body,
        grid=(B, S // TM),
        in_specs=[tok(D_MODEL), full(nw), tok(512), tok(512), full(wz), full(wa), full(wb), full(wo)],
        out_specs=tok(D_MODEL),
        out_shape=jax.ShapeDtypeStruct((B, S, D_MODEL), jnp.float32),
        compiler_params=pltpu.CompilerParams(dimension_semantics=("parallel", "parallel"),
                                             vmem_limit_bytes=VMEM_LIMIT),
        name="out_proj",
    )(x, nw, oa, ob, wz, wa, wb, wo)


def _pack_in_weights(w_in, q_norm_a, k_norm_a, q_norm_b, k_norm_sel, k_norm_win):
    split = _C_GATE + 3 * NSA_HEADS
    w = jnp.concatenate([w_in[:, :split], jnp.zeros((D_MODEL, LANES - 3 * NSA_HEADS), w_in.dtype),
                         w_in[:, split:]], axis=1).astype(jnp.bfloat16)
    ones = lambda n: jnp.ones((n,), jnp.float32)
    gain = jnp.concatenate([jnp.tile(q_norm_a, MOBA_HEADS), jnp.tile(k_norm_a, MOBA_HEADS), ones(1024),
                            jnp.tile(q_norm_b, NSA_HEADS), ones(256), jnp.tile(k_norm_sel, NSA_GROUPS), ones(128),
                            jnp.tile(k_norm_win, NSA_GROUPS), ones(256)]).reshape(1, _N1)
    return w, gain


def kernel(x, norm_w, w_in, q_norm_a, k_norm_a, q_norm_b, k_norm_cmp, k_norm_sel, k_norm_win, cmp_pos_k, cmp_w1_k, cmp_w2_k, cmp_pos_v, cmp_w1_v, cmp_w2_v, rel_bias, w_branch_a, w_branch_b, w_out):
    bf = jnp.bfloat16
    w_all, gain = _pack_in_weights(w_in[0], q_norm_a[0], k_norm_a[0], q_norm_b[0], k_norm_sel[0], k_norm_win[0])
    bias = _bias_tiles(rel_bias)
    qa, kae, kao, vae, vao, qb, kmean, kc, vc, kse, kso, vsl, kwn, vwn, gate = _in_proj(x, norm_w[0], w_all, gain)
    kcmp, vcmp = _compress(kc, vc, cmp_pos_k[0], cmp_w1_k[0], cmp_w2_k[0],
                           cmp_pos_v[0], cmp_w1_v[0], cmp_w2_v[0], k_norm_cmp[0])
    oa = _moba(qa, kae, kao, vae, vao, kmean[:, :, 0, :], bias)
    ob = _nsa(qb, kcmp, vcmp, kse, kso, vsl, kwn, vwn, gate, bias)
    return _out_proj(x, norm_w[0], oa, ob, w_all, w_branch_a[0].astype(bf), w_branch_b[0].astype(bf),
                     w_out[0].astype(bf))
```

```python
import math

import jax
import jax.numpy as jnp
import numpy as np
from jax import lax
from jax.experimental import pallas as pl
from jax.experimental.pallas import tpu as pltpu

D_MODEL = 1024
HEAD_DIM = 64
MOBA_HEADS = 8
NSA_HEADS = 8
NSA_GROUPS = 2
NSA_HPG = NSA_HEADS // NSA_GROUPS
MOBA_BLOCK = 256
MOBA_TOPK = 3
CMP_LEN = 32
CMP_STRIDE = 16
CMP_HIDDEN = 256
SEL_BLOCK = 64
SEL_TOPN = 16
WINDOW = 512
NUM_BUCKETS = 32
MAX_DISTANCE = 1024
EPS = 1e-6
NEG = -1e30

LANES = 128
TQ = 256
TK = 2 * TQ
TR = 2 * TQ
TM = 2 * TQ
N_FAR = 5
N_WIN_TILES = WINDOW // TQ + 1
VMEM_LIMIT = 56 * 1024 * 1024
LOG2E = math.log2(math.e)
QSCALE = LOG2E / math.sqrt(HEAD_DIM)

_K_WIN = N_FAR + 1
_K_MASKED = _K_WIN + N_WIN_TILES
_N_TILE_KINDS = _K_MASKED + 1

_C_QA, _C_KA, _C_VA, _C_ZA, _C_QB = 0, 512, 1024, 1536, 2048
_C_KC, _C_VC, _C_KSL, _C_VSL, _C_KWN, _C_VWN, _C_GATE = 2560, 2688, 2816, 2944, 3072, 3200, 3328
_N1 = _C_GATE + LANES


def _nt(a, b):
    return lax.dot_general(a, b, (((1,), (1,)), ((), ())), preferred_element_type=jnp.float32)


def _nn(a, b):
    return jnp.dot(a, b, preferred_element_type=jnp.float32)


def _split(x):
    hi = x.astype(jnp.bfloat16)
    lo = (x - hi.astype(jnp.float32)).astype(jnp.bfloat16)
    return hi, lo


def _bucket_table(n):
    d = np.arange(n)
    nf = np.maximum(d, NUM_BUCKETS // 2).astype(np.float64)
    large = NUM_BUCKETS // 2 + (np.log(nf / (NUM_BUCKETS // 2)) / math.log(MAX_DISTANCE / (NUM_BUCKETS // 2))
                                * (NUM_BUCKETS - NUM_BUCKETS // 2)).astype(np.int64)
    return np.where(d < NUM_BUCKETS // 2, d, np.minimum(large, NUM_BUCKETS - 1))


def _bias_onehot():
    oh = np.zeros((40, _N_TILE_KINDS * 2 * TQ), np.float32)
    buckets = _bucket_table((N_FAR + 2) * TQ + 1)
    for t in range(_N_TILE_KINDS):
        window = _K_WIN <= t < _K_MASKED
        delta = t - _K_WIN if window else t
        for u in range(2 * TQ):
            d = TQ * delta + TQ - u
            masked = t == _K_MASKED or d < 0 or (window and d >= WINDOW)
            oh[NUM_BUCKETS if masked else buckets[d], t * 2 * TQ + u] = 1.0
    return oh


def _bias_body(relt_ref, oh_ref, out_ref):
    g = jnp.dot(jnp.broadcast_to(relt_ref[0], (8, 40)), oh_ref[...], preferred_element_type=jnp.float32,
                precision=lax.Precision.HIGHEST)
    for t in range(_N_TILE_KINDS):
        full = jnp.broadcast_to(g[0:1, t * 2 * TQ:(t + 1) * 2 * TQ], (TQ, 2 * TQ))
        rolled = pltpu.roll(full, TQ, 1, stride=1, stride_axis=0)
        out_ref[0, t] = rolled[:, :TQ]


def _bias_tiles(rel_bias):
    nh = MOBA_HEADS + NSA_HEADS
    relt = jnp.concatenate([rel_bias.T.astype(jnp.float32) * LOG2E, jnp.full((nh, 1), NEG, jnp.float32),
                            jnp.zeros((nh, 7), jnp.float32)], axis=1).reshape(nh, 1, 40)
    oh = jnp.asarray(_bias_onehot())
    return pl.pallas_call(
        _bias_body,
        grid=(nh,),
        in_specs=[pl.BlockSpec((1, 1, 40), lambda h: (h, 0, 0)), pl.BlockSpec(oh.shape, lambda h: (0, 0))],
        out_specs=pl.BlockSpec((1, _N_TILE_KINDS, TQ, TQ), lambda h: (h, 0, 0, 0)),
        out_shape=jax.ShapeDtypeStruct((nh, _N_TILE_KINDS, TQ, TQ), jnp.float32),
        compiler_params=pltpu.CompilerParams(dimension_semantics=("parallel",), vmem_limit_bytes=VMEM_LIMIT),
        name="bias_tiles",
    )(relt, oh)


def _seg_norm(x, lo_half, gain):
    x2 = x * x
    ss_lo = jnp.sum(jnp.where(lo_half, x2, 0.0), axis=-1, keepdims=True)
    ss_hi = jnp.sum(jnp.where(lo_half, 0.0, x2), axis=-1, keepdims=True)
    ss = jnp.where(lo_half, ss_lo, ss_hi)
    return x * lax.rsqrt(ss * (1.0 / HEAD_DIM) + EPS) * gain


def _in_proj_body(x_ref, nw_ref, w_ref, wg_ref, gain_ref,
                  qa_ref, kae_ref, kao_ref, vae_ref, vao_ref, qb_ref, kmean_ref, kc_ref, vc_ref,
                  kse_ref, kso_ref, vsl_ref, kwn_ref, vwn_ref, gate_ref, stage_k_ref, stage_v_ref):
    i = pl.program_id(1)
    x = x_ref[0]
    ms = jnp.mean(x * x, axis=-1, keepdims=True)
    h = (x * lax.rsqrt(ms + EPS) * nw_ref[...]).astype(jnp.bfloat16)
    bf = jnp.bfloat16
    lane = lax.broadcasted_iota(jnp.int32, (TM, LANES), 1)
    row = lax.broadcasted_iota(jnp.int32, (TM, LANES), 0)
    lo_half = lane < HEAD_DIM
    blk = i * (TM // MOBA_BLOCK) + lax.shift_right_logical(row, MOBA_BLOCK.bit_length() - 1)
    sblk = i * (TM // SEL_BLOCK) + lax.shift_right_logical(row, SEL_BLOCK.bit_length() - 1)

    def normed(col, width, scale=None):
        p = _nn(h, w_ref[:, col:col + width])
        outs = []
        for s in range(width // LANES):
            c = col + s * LANES
            y = _seg_norm(p[:, s * LANES:(s + 1) * LANES], lo_half, gain_ref[:, c:c + LANES])
            outs.append(y if scale is None else y * scale)
        return outs

    qa = normed(_C_QA, 512, QSCALE)
    for s in range(4):
        qa_ref[0, :, s * LANES:(s + 1) * LANES] = qa[s].astype(bf)
    ka = normed(_C_KA, 512)
    oh_hi = jnp.where(lane == HEAD_DIM + blk, 1.0, 0.0)
    oh_lo = jnp.where(lane == blk, 1.0, 0.0)
    for s in range(4):
        kae_ref[0, :, s * LANES:(s + 1) * LANES] = jnp.where(lo_half, ka[s], oh_hi).astype(bf)
        kao_ref[0, :, s * LANES:(s + 1) * LANES] = jnp.where(lo_half, oh_lo, ka[s]).astype(bf)
        for j in range(TM // MOBA_BLOCK):
            kmean_ref[0, j, :, s * LANES:(s + 1) * LANES] = jnp.broadcast_to(
                jnp.mean(ka[s][j * MOBA_BLOCK:(j + 1) * MOBA_BLOCK], axis=0, keepdims=True), (8, LANES))
    one_hi = jnp.where(lane == HEAD_DIM, 1.0, 0.0)
    one_lo = jnp.where(lane == 0, 1.0, 0.0)
    va = _nn(h, w_ref[:, _C_VA:_C_VA + 512])
    for s in range(4):
        vs = va[:, s * LANES:(s + 1) * LANES]
        vae_ref[0, :, s * LANES:(s + 1) * LANES] = jnp.where(lo_half, vs, one_hi).astype(bf)
        vao_ref[0, :, s * LANES:(s + 1) * LANES] = jnp.where(lo_half, one_lo, vs).astype(bf)
    qb = normed(_C_QB, 512, QSCALE)
    for s in range(4):
        qb_ref[0, :, s * LANES:(s + 1) * LANES] = qb[s].astype(bf)

    rest = _nn(h, w_ref[:, _C_KC:_C_GATE])
    gate_ref[0] = jnp.concatenate([_nn(h[0:TM // 2], wg_ref[...]), _nn(h[TM // 2:], wg_ref[...])], axis=0)

    def slab(c):
        return rest[:, c - _C_KC:c - _C_KC + LANES]

    def chunk_store(ref, stage_ref, y):
        stage_ref[...] = y
        lo = lax.broadcasted_iota(jnp.int32, (TM // CMP_STRIDE, LANES), 1) < HEAD_DIM
        for u in range(CMP_STRIDE // 2):
            t0 = stage_ref[pl.ds(2 * u, TM // CMP_STRIDE, stride=CMP_STRIDE), :]
            t1 = stage_ref[pl.ds(2 * u + 1, TM // CMP_STRIDE, stride=CMP_STRIDE), :]
            ref[0, 0, :, u * LANES:(u + 1) * LANES] = jnp.where(lo, t0, pltpu.roll(t1, HEAD_DIM, 1)).astype(bf)
            ref[0, 1, :, u * LANES:(u + 1) * LANES] = jnp.where(lo, pltpu.roll(t0, HEAD_DIM, 1), t1).astype(bf)

    chunk_store(kc_ref, stage_k_ref, slab(_C_KC))
    chunk_store(vc_ref, stage_v_ref, slab(_C_VC))

    def dup_store(ref, y):
        r = pltpu.roll(y, HEAD_DIM, 1)
        ref[0, 0] = jnp.where(lo_half, y, r).astype(bf)
        ref[0, 1] = jnp.where(lo_half, r, y).astype(bf)

    y = _seg_norm(slab(_C_KSL), lo_half, gain_ref[:, _C_KSL:_C_KSL + LANES])
    r = pltpu.roll(y, HEAD_DIM, 1)
    oh = jnp.where((lane & (HEAD_DIM - 1)) == sblk, 1.0, 0.0)
    kse_ref[0, 0] = jnp.where(lo_half, y, oh).astype(bf)
    kso_ref[0, 0] = jnp.where(lo_half, oh, r).astype(bf)
    kse_ref[0, 1] = jnp.where(lo_half, r, oh).astype(bf)
    kso_ref[0, 1] = jnp.where(lo_half, oh, y).astype(bf)
    def aug_store(ref, y):
        ref[0, 0] = jnp.where(lo_half, y, one_hi).astype(bf)
        ref[0, 1] = jnp.where(lo_half, pltpu.roll(y, HEAD_DIM, 1), one_hi).astype(bf)

    aug_store(vsl_ref, slab(_C_VSL))
    dup_store(kwn_ref, _seg_norm(slab(_C_KWN), lo_half, gain_ref[:, _C_KWN:_C_KWN + LANES]))
    aug_store(vwn_ref, slab(_C_VWN))


def _in_proj(x, norm_w, w1, wg, gain_row):
    B, S, _ = x.shape
    nt = S // TQ
    bf = jnp.bfloat16
    tok = lambda w, dt: jax.ShapeDtypeStruct((B, S, w), dt)
    grp = jax.ShapeDtypeStruct((B, NSA_GROUPS, S, LANES), bf)
    tok_spec = lambda w: pl.BlockSpec((1, TM, w), lambda b, i: (b, i, 0))
    grp_spec = pl.BlockSpec((1, NSA_GROUPS, TM, LANES), lambda b, i: (b, 0, i, 0))
    flat = CMP_STRIDE * HEAD_DIM
    chunk = jax.ShapeDtypeStruct((B, NSA_GROUPS, S // CMP_STRIDE, flat), bf)
    chunk_spec = pl.BlockSpec((1, NSA_GROUPS, TM // CMP_STRIDE, flat), lambda b, i: (b, 0, i, 0))
    full = lambda a: pl.BlockSpec(a.shape, lambda b, i: (0,) * a.ndim, pipeline_mode=pl.Buffered(1))
    nw = norm_w.reshape(1, D_MODEL)
    return pl.pallas_call(
        _in_proj_body,
        grid=(B, S // TM),
        in_specs=[tok_spec(D_MODEL), full(nw), full(w1), full(wg), full(gain_row)],
        out_specs=[tok_spec(512), tok_spec(512), tok_spec(512), tok_spec(512), tok_spec(512), tok_spec(512),
                   pl.BlockSpec((1, TM // MOBA_BLOCK, 8, 512), lambda b, i: (b, i, 0, 0)),
                   chunk_spec, chunk_spec,
                   grp_spec, grp_spec, grp_spec, grp_spec, grp_spec, tok_spec(LANES)],
        out_shape=[tok(512, bf), tok(512, bf), tok(512, bf), tok(512, bf), tok(512, bf), tok(512, bf),
                   jax.ShapeDtypeStruct((B, nt, 8, 512), jnp.float32),
                   chunk, chunk, grp, grp, grp, grp, grp, tok(LANES, jnp.float32)],
        scratch_shapes=[pltpu.VMEM((TM, LANES), jnp.float32)] * 2,
        compiler_params=pltpu.CompilerParams(dimension_semantics=("parallel", "parallel"),
                                             vmem_limit_bytes=VMEM_LIMIT),
        name="in_proj",
    )(x, nw, w1, wg, gain_row)


def _gelu_tanh(x):
    return 0.5 * x * (1.0 + jnp.tanh(math.sqrt(2.0 / math.pi) * (x + 0.044715 * (x * x * x))))


def _compress_body(xk_ref, xv_ref, w1k_ref, w1v_ref, pk_ref, pv_ref,
                   w2k_ref, w2v_ref, gk_ref, kcmp_ref, vcmp_ref):
    nchunk = xk_ref.shape[2]
    half = CMP_STRIDE * HEAD_DIM

    def branch(x_ref, w1_ref, pos_ref, w2_ref, gain):
        pos = jnp.broadcast_to(pos_ref[...], (8, CMP_LEN * HEAD_DIM))
        posw = (_nn(pos[:, 0:half], w1_ref[...])[0:1, 0:CMP_HIDDEN]
                + _nn(pos[:, half:], w1_ref[...])[0:1, CMP_HIDDEN:])
        outs = []
        for g in range(NSA_GROUPS):
            ab = _nn(x_ref[0, g], w1_ref[...])
            top, bot = ab[:, 0:CMP_HIDDEN], ab[:, CMP_HIDDEN:]
            hid = top + pltpu.roll(bot, nchunk - 1, 0) + posw
            y = _nn(_gelu_tanh(hid).astype(jnp.bfloat16), w2_ref[...])
            if gain is not None:
                ms = jnp.sum(y * y, axis=-1, keepdims=True) * (0.5 / HEAD_DIM)
                y = y * lax.rsqrt(ms + EPS) * gain
            else:
                lane = lax.broadcasted_iota(jnp.int32, y.shape, 1)
                y = jnp.where(lane < HEAD_DIM, y, jnp.where(lane == HEAD_DIM, 1.0, 0.0))
            outs.append(y.astype(jnp.bfloat16))
        return outs

    k0, k1 = branch(xk_ref, w1k_ref, pk_ref, w2k_ref, gk_ref[...])
    kcmp_ref[0, 0], kcmp_ref[0, 1] = k0, k1
    v0, v1 = branch(xv_ref, w1v_ref, pv_ref, w2v_ref, None)
    vcmp_ref[0, 0], vcmp_ref[0, 1] = v0, v1


def _compress(xk, xv, pos_k, w1_k, w2_k, pos_v, w1_v, w2_v, gain_cmp):
    B, _, nchunk, half = xk.shape
    bf = jnp.bfloat16
    halves = lambda w1: jnp.concatenate([w1[:half], w1[half:]], axis=1).astype(bf)
    args = [xk, xv, halves(w1_k), halves(w1_v),
            pos_k.reshape(1, -1).astype(bf), pos_v.reshape(1, -1).astype(bf),
            jnp.concatenate([w2_k, w2_k], axis=1).astype(bf), jnp.concatenate([w2_v, w2_v], axis=1).astype(bf),
            jnp.concatenate([gain_cmp, gain_cmp]).reshape(1, LANES)]
    x_spec = pl.BlockSpec((1, NSA_GROUPS, nchunk, half), lambda b: (b, 0, 0, 0))
    full = lambda a: pl.BlockSpec(a.shape, lambda b: (0,) * a.ndim)
    out = jax.ShapeDtypeStruct((B, NSA_GROUPS, nchunk, LANES), bf)
    o_spec = pl.BlockSpec((1, NSA_GROUPS, nchunk, LANES), lambda b: (b, 0, 0, 0))
    return pl.pallas_call(
        _compress_body,
        grid=(B,),
        in_specs=[x_spec, x_spec] + [full(a) for a in args[2:]],
        out_specs=[o_spec, o_spec],
        out_shape=[out, out],
        compiler_params=pltpu.CompilerParams(dimension_semantics=("parallel",), vmem_limit_bytes=VMEM_LIMIT),
        name="compress",
    )(*args)


def _softmax_step(s, v_top, v_bot, m_ref, acc_ref, first):
    blocks = [s[:, c * LANES:(c + 1) * LANES] for c in range(s.shape[1] // LANES)]
    bmax = blocks[0]
    for blk in blocks[1:]:
        bmax = jnp.maximum(bmax, blk)
    m_new = jnp.broadcast_to(jnp.max(bmax, axis=-1, keepdims=True), m_ref.shape)
    if not first:
        m_old = m_ref[...]
        m_new = jnp.maximum(m_old, m_new)
        alpha = jnp.exp2(m_old - m_new)
    pb = jnp.concatenate([jnp.exp2(blk - m_new) for blk in blocks], axis=1).astype(jnp.bfloat16)
    half = s.shape[0] // 2
    pv = jnp.concatenate([_nn(pb[0:half], v_top), _nn(pb[half:], v_bot)], axis=0)
    acc_ref[...] = pv if first else alpha * acc_ref[...] + pv
    m_ref[...] = m_new


def _finish(acc, ones_lane):
    lane = lax.broadcasted_iota(jnp.int32, acc.shape, 1)
    return acc / jnp.sum(jnp.where(lane == ones_lane, acc, 0.0), axis=-1, keepdims=True)


def _tile_kind(delta):
    return jnp.where(delta >= 0, jnp.minimum(delta, N_FAR), _K_MASKED)


def _chunk_kinds(qi, c):
    d0 = qi - 2 * c
    kind0 = jnp.minimum(d0, N_FAR)
    kind1 = jnp.where(d0 >= 1, jnp.minimum(d0 - 1, N_FAR), _K_MASKED)
    return kind0, kind1


def _moba_body(q_ref, ke_ref, ko_ref, ve_ref, vo_ref, kmean_ref, eye_ref, bias_ref, o_ref, m_ref, acc_ref):
    i = pl.program_id(2)
    nblk = kmean_ref.shape[1]
    lane = lax.broadcasted_iota(jnp.int32, (TR, LANES), 1)
    lo_half = lane < HEAD_DIM
    lane16 = lax.broadcasted_iota(jnp.int32, (nblk, LANES), 1)
    jrow = lax.broadcasted_iota(jnp.int32, (nblk, TR), 0)
    qblk = 2 * i + jnp.where(lax.broadcasted_iota(jnp.int32, (nblk, TR), 1) >= TQ, 1, 0)
    q2 = q_ref[0]
    kmean2 = kmean_ref[0]
    flags = []
    for hh in range(2):
        inhead16 = (lane16 < HEAD_DIM) if hh == 0 else (lane16 >= HEAD_DIM)
        km_hi, km_lo = _split(jnp.where(inhead16, kmean2, 0.0))
        sc = jnp.where(jrow < qblk, _nt(km_hi, q2) + _nt(km_lo, q2), -jnp.inf)
        flag = jnp.where(jrow == qblk, 0.0, 1.0)
        for _ in range(MOBA_TOPK):
            best = jnp.max(sc, axis=0, keepdims=True)
            first = jnp.min(jnp.where(sc == best, jrow, nblk), axis=0, keepdims=True)
            hit = jrow == first
            flag = jnp.where(hit, 0.0, flag)
            sc = jnp.where(hit, -jnp.inf, sc)
        flags.append(flag)
    pad = jnp.zeros((HEAD_DIM - nblk, TR), jnp.float32)
    placed = jnp.concatenate([flags[1], pad, flags[0], pad], axis=0).astype(jnp.bfloat16)
    maskpart = jnp.concatenate([_nt(eye_ref[...], placed[:, a * TQ:(a + 1) * TQ]) for a in range(TR // TQ)],
                               axis=0) * NEG
    q2f = q2.astype(jnp.float32)
    zero = jnp.zeros((TR, LANES), jnp.bfloat16)
    qq = jnp.concatenate(
        [jnp.concatenate([jnp.where(lo_half, q2f, maskpart).astype(jnp.bfloat16), zero], axis=1),
         jnp.concatenate([zero, jnp.where(lo_half, maskpart, q2f).astype(jnp.bfloat16)], axis=1)], axis=0)

    def step(c, first):
        start = pl.multiple_of(c * TK, TK)
        kk = jnp.concatenate([ke_ref[0, pl.ds(start, TK), :], ko_ref[0, pl.ds(start, TK), :]], axis=1)
        d = 2 * (i - c)
        b = jnp.concatenate(
            [jnp.concatenate([bias_ref[hh, _tile_kind(d + a - t)] for t in range(2)], axis=1)
             for hh in range(2) for a in range(2)], axis=0)
        _softmax_step(_nt(qq, kk) + b, ve_ref[0, pl.ds(start, TK), :], vo_ref[0, pl.ds(start, TK), :],
                      m_ref, acc_ref, first)

    step(i, True)

    def body(c, carry):
        step(c, False)
        return carry

    lax.fori_loop(0, i, body, 0)
    o_ref[0] = jnp.where(lo_half, _finish(acc_ref[0:TR], HEAD_DIM), _finish(acc_ref[TR:2 * TR], 0))


def _eye(n):
    return jnp.asarray(np.eye(n, dtype=np.float32), jnp.bfloat16)


def _moba(qa, kae, kao, vae, vao, kmean, bias):
    B, S, _ = qa.shape
    nt = S // TQ
    npair = MOBA_HEADS // 2
    kv_spec = pl.BlockSpec((1, S, LANES), lambda b, p, i: (b, 0, p))
    return pl.pallas_call(
        _moba_body,
        grid=(B, npair, S // TR),
        in_specs=[pl.BlockSpec((1, TR, LANES), lambda b, p, i: (b, i, p)),
                  kv_spec, kv_spec, kv_spec, kv_spec,
                  pl.BlockSpec((1, nt, LANES), lambda b, p, i: (b, 0, p)),
                  pl.BlockSpec((TQ, TQ), lambda b, p, i: (0, 0)),
                  pl.BlockSpec((2, _N_TILE_KINDS, TQ, TQ), lambda b, p, i: (p, 0, 0, 0))],
        out_specs=pl.BlockSpec((1, TR, LANES), lambda b, p, i: (b, i, p)),
        out_shape=jax.ShapeDtypeStruct((B, S, MOBA_HEADS * HEAD_DIM), jnp.float32),
        scratch_shapes=[pltpu.VMEM((2 * TR, LANES), jnp.float32)] * 2,
        compiler_params=pltpu.CompilerParams(dimension_semantics=("parallel", "parallel", "arbitrary"),
                                             vmem_limit_bytes=VMEM_LIMIT),
        name="moba",
    )(qa, kae, kao, vae, vao, kmean, _eye(TQ), bias)


def _nsa_body(q_ref, kcmp_ref, vcmp_ref, kse_ref, kso_ref, vsl_ref, kwn_ref, vwn_ref, gate_ref, ov_ref, eye_ref,
              gexp_ref, bias_ref, o_ref, m_ref, acc_ref):
    i = pl.program_id(2)

    def to_hi(o):
        return pltpu.roll(o, HEAD_DIM, 1)

    ncmp = kcmp_ref.shape[2]
    nsel = ov_ref.shape[0]
    R = NSA_HPG * TR
    bf = jnp.bfloat16
    lane = lax.broadcasted_iota(jnp.int32, (TR, LANES), 1)
    lo_half = lane < HEAD_DIM
    qf = [q_ref[0, :, 0:LANES].astype(jnp.float32), q_ref[0, :, LANES:2 * LANES].astype(jnp.float32)]
    q4 = jnp.concatenate([jnp.where(lo_half if hh % 2 == 0 else ~lo_half, qf[hh // 2], 0.0).astype(bf)
                          for hh in range(NSA_HPG)], axis=0)

    o_win = []
    for a in range(2):
        qt = 2 * i + a
        t0 = jnp.maximum(qt - (N_WIN_TILES - 1), 0)
        wstart = pl.multiple_of(t0 * TQ, TQ)
        qh = jnp.concatenate([q4[hh * TR + a * TQ:hh * TR + (a + 1) * TQ] for hh in range(NSA_HPG)], axis=0)
        kinds = []
        for r in range(N_WIN_TILES):
            dd = qt - (t0 + r)
            kinds.append(jnp.where(dd >= 0, _K_WIN + dd, _K_MASKED))
        bw = jnp.concatenate([jnp.concatenate([bias_ref[hh, k] for k in kinds], axis=1)
                              for hh in range(NSA_HPG)], axis=0)
        sw = _nt(qh, kwn_ref[0, 0, pl.ds(wstart, N_WIN_TILES * TQ), :]) + bw
        wm, wacc = m_ref.at[0:NSA_HPG * TQ], acc_ref.at[0:NSA_HPG * TQ]
        vw = vwn_ref[0, 0, pl.ds(wstart, N_WIN_TILES * TQ), :]
        _softmax_step(sw, vw, vw, wm, wacc, True)
        o_win.append(_finish(wacc[...], HEAD_DIM))

    tpos = i * TR + lax.broadcasted_iota(jnp.int32, (TR, ncmp), 0)
    cidx = lax.broadcasted_iota(jnp.int32, (TR, ncmp), 1)
    cbias1 = jnp.where((cidx * CMP_STRIDE + (CMP_LEN - 1) <= tpos) & (cidx < ncmp - 1), 0.0, NEG)
    lc = _nt(q4, kcmp_ref[0, 0]) + jnp.concatenate([cbias1] * NSA_HPG, axis=0)
    cblocks = [lc[:, c * LANES:(c + 1) * LANES] for c in range(ncmp // LANES)]
    cmax = cblocks[0]
    for blk in cblocks[1:]:
        cmax = jnp.maximum(cmax, blk)
    cm = jnp.maximum(jnp.broadcast_to(jnp.max(cmax, axis=-1, keepdims=True), (R, LANES)), 0.5 * NEG)
    ce = [jnp.exp2(blk - cm) for blk in cblocks]
    cden = ce[0]
    for eb in ce[1:]:
        cden = cden + eb
    cden = jnp.broadcast_to(jnp.sum(cden, axis=-1, keepdims=True), (R, LANES))
    crcp = 1.0 / jnp.where(cden > 0.0, cden, 1.0)
    pc = jnp.concatenate([eb * crcp for eb in ce], axis=1)
    pcb = pc.astype(bf)
    o_cmp = jnp.concatenate([_nn(pcb[0:R // 2], vcmp_ref[0, 0]), _nn(pcb[R // 2:], vcmp_ref[0, 0])], axis=0)
    psum = pc[0:TR] + pc[TR:2 * TR] + pc[2 * TR:3 * TR] + pc[3 * TR:4 * TR]
    p_hi, p_lo = _split(psum)
    ov = ov_ref[...]
    imp = _nt(ov, p_hi) + _nt(ov, p_lo)

    jrow = lax.broadcasted_iota(jnp.int32, (nsel, TR), 0)
    qblk = lax.shift_right_logical(i * TR + lax.broadcasted_iota(jnp.int32, (nsel, TR), 1),
                                   SEL_BLOCK.bit_length() - 1)
    cand = (jrow >= 1) & (jrow < qblk)
    keys = jnp.where(cand, imp, -1.0)
    flag = jnp.where((jrow == 0) | (jrow == qblk), 0.0, 1.0)
    for _ in range(SEL_TOPN - 2):
        best = jnp.max(keys, axis=0, keepdims=True)
        first = jnp.min(jnp.where(keys == best, jrow, nsel), axis=0, keepdims=True)
        hit = jrow == first
        flag = jnp.where(hit, 0.0, flag)
        keys = jnp.where(hit, -1.0, keys)
    if nsel < HEAD_DIM:
        flag = jnp.concatenate([flag, jnp.zeros((HEAD_DIM - nsel, TR), jnp.float32)], axis=0)
    flag2 = jnp.concatenate([flag, flag], axis=0).astype(bf)
    maskpart = jnp.concatenate([_nt(eye_ref[...], flag2[:, a * TQ:(a + 1) * TQ]) for a in range(TR // TQ)],
                               axis=0) * NEG

    zero = jnp.zeros((2 * TR, LANES), bf)
    qa2 = [jnp.concatenate([jnp.where(lo_half if par == 0 else ~lo_half, qf[s], maskpart).astype(bf)
                            for s in range(2)], axis=0) for par in range(2)]
    qq = jnp.concatenate([jnp.concatenate([qa2[0], zero], axis=1),
                          jnp.concatenate([zero, qa2[1]], axis=1)], axis=0)
    sel_heads = (0, 2, 1, 3)

    def sel_step(c, first):
        start = pl.multiple_of(c * TK, TK)
        kk = jnp.concatenate([kse_ref[0, 0, pl.ds(start, TK), :], kso_ref[0, 0, pl.ds(start, TK), :]], axis=1)
        d = 2 * (i - c)
        b = jnp.concatenate(
            [jnp.concatenate([bias_ref[h, _tile_kind(d + a - t)] for t in range(2)], axis=1)
             for h in sel_heads for a in range(2)], axis=0)
        v = vsl_ref[0, 0, pl.ds(start, TK), :]
        _softmax_step(_nt(qq, kk) + b, v, v, m_ref, acc_ref, first)

    sel_step(i, True)

    def sel_body(c, carry):
        sel_step(c, False)
        return carry

    lax.fori_loop(0, i, sel_body, 0)
    o_all = _finish(acc_ref[...], HEAD_DIM)
    o_sel = [o_all[0:2 * TR], to_hi(o_all[2 * TR:4 * TR])]

    g = jax.nn.sigmoid(gate_ref[0])
    g_hi, g_lo = _split(g)
    gx = _nn(g_hi, gexp_ref[0]) + _nn(g_lo, gexp_ref[0])
    for s in range(2):
        h_even, h_odd = 2 * s, 2 * s + 1
        cmp2 = jnp.where(lo_half, o_cmp[h_even * TR:(h_even + 1) * TR], to_hi(o_cmp[h_odd * TR:(h_odd + 1) * TR]))
        sel2 = jnp.where(lo_half, o_sel[0][s * TR:(s + 1) * TR], o_sel[1][s * TR:(s + 1) * TR])
        win2 = jnp.where(lo_half,
                         jnp.concatenate([o_win[a][h_even * TQ:(h_even + 1) * TQ] for a in range(2)], axis=0),
                         to_hi(jnp.concatenate([o_win[a][h_odd * TQ:(h_odd + 1) * TQ] for a in range(2)], axis=0)))
        base = s * 3 * LANES
        o_ref[0, :, s * LANES:(s + 1) * LANES] = (gx[:, base:base + LANES] * cmp2
                                                  + gx[:, base + LANES:base + 2 * LANES] * sel2
                                                  + gx[:, base + 2 * LANES:base + 3 * LANES] * win2)


def _gate_expand():
    e = np.zeros((NSA_GROUPS, LANES, 2 * 3 * LANES), np.float32)
    for g in range(NSA_GROUPS):
        for s in range(2):
            for c in range(3):
                for half in range(2):
                    h = g * NSA_HPG + 2 * s + half
                    col = (s * 3 + c) * LANES + half * HEAD_DIM
                    e[g, 3 * h + c, col:col + HEAD_DIM] = 1.0
    return e


def _nsa(qb, kcmp, vcmp, kse, kso, vsl, kwn, vwn, gate, bias):
    B, S, _ = qb.shape
    nt = S // TQ
    ncmp = kcmp.shape[2]
    nsel = S // SEL_BLOCK
    bf = jnp.bfloat16
    cs = np.arange(ncmp) * CMP_STRIDE
    bs = np.arange(nsel) * SEL_BLOCK
    ov = ((cs[None, :] < bs[:, None] + SEL_BLOCK) & (cs[None, :] + CMP_LEN > bs[:, None])
          & (np.arange(ncmp)[None, :] < ncmp - 1)).astype(np.float32)
    grp = lambda n: pl.BlockSpec((1, 1, n, LANES), lambda b, g, i: (b, g, 0, 0))
    return pl.pallas_call(
        _nsa_body,
        grid=(B, NSA_GROUPS, S // TR),
        in_specs=[pl.BlockSpec((1, TR, 2 * LANES), lambda b, g, i: (b, i, g)),
                  grp(ncmp), grp(ncmp), grp(S), grp(S), grp(S), grp(S), grp(S),
                  pl.BlockSpec((1, TR, LANES), lambda b, g, i: (b, i, 0)),
                  pl.BlockSpec((nsel, ncmp), lambda b, g, i: (0, 0)),
                  pl.BlockSpec((TQ, TQ), lambda b, g, i: (0, 0)),
                  pl.BlockSpec((1, LANES, 6 * LANES), lambda b, g, i: (g, 0, 0)),
                  pl.BlockSpec((NSA_HPG, _N_TILE_KINDS, TQ, TQ), lambda b, g, i: (2 + g, 0, 0, 0),
                               pipeline_mode=pl.Buffered(1))],
        out_specs=pl.BlockSpec((1, TR, 2 * LANES), lambda b, g, i: (b, i, g)),
        out_shape=jax.ShapeDtypeStruct((B, S, NSA_HEADS * HEAD_DIM), jnp.float32),
        scratch_shapes=[pltpu.VMEM((NSA_HPG * TR, LANES), jnp.float32)] * 2,
        compiler_params=pltpu.CompilerParams(dimension_semantics=("parallel", "parallel", "arbitrary"),
                                             vmem_limit_bytes=VMEM_LIMIT),
        name="nsa",
    )(qb, kcmp, vcmp, kse, kso, vsl, kwn, vwn, gate, jnp.asarray(ov, bf), _eye(TQ),
      jnp.asarray(_gate_expand(), bf), bias)


def _out_proj_body(x_ref, nw_ref, oa_ref, ob_ref, wz_ref, wzb_ref, wgm_ref, wa_ref, wb_ref, wo_ref, out_ref):
    x = x_ref[0]
    ms = jnp.mean(x * x, axis=-1, keepdims=True)
    h = (x * lax.rsqrt(ms + EPS) * nw_ref[...]).astype(jnp.bfloat16)
    ya = (oa_ref[0] * jax.nn.silu(_nn(h, wz_ref[:, _C_ZA:_C_ZA + 512]))).astype(jnp.bfloat16)
    yb = (ob_ref[0] * jax.nn.silu(_nn(h, wzb_ref[...]))).astype(jnp.bfloat16)
    gm = jax.nn.sigmoid(_nn(h, wgm_ref[...]))
    merged = gm[:, 0:D_MODEL] * _nn(ya, wa_ref[...]) + gm[:, D_MODEL:] * _nn(yb, wb_ref[...])
    out_ref[0] = x + _nn(merged.astype(jnp.bfloat16), wo_ref[...])


def _out_proj(x, norm_w, oa, ob, wz, wzb, wgm, wa, wb, wo):
    B, S, _ = x.shape
    nw = norm_w.reshape(1, D_MODEL)
    tok = lambda w: pl.BlockSpec((1, TM, w), lambda b, i: (b, i, 0))
    full = lambda a: pl.BlockSpec(a.shape, lambda b, i: (0,) * a.ndim, pipeline_mode=pl.Buffered(1))
    return pl.pallas_call(
        _out_proj_body,
        grid=(B, S // TM),
        in_specs=[tok(D_MODEL), full(nw), tok(512), tok(512), full(wz), full(wzb), full(wgm),
                  full(wa), full(wb), full(wo)],
        out_specs=tok(D_MODEL),
        out_shape=jax.ShapeDtypeStruct((B, S, D_MODEL), jnp.float32),
        compiler_params=pltpu.CompilerParams(dimension_semantics=("parallel", "parallel"),
                                             vmem_limit_bytes=VMEM_LIMIT),
        name="out_proj",
    )(x, nw, oa, ob, wz, wzb, wgm, wa, wb, wo)


def _pack_in_weights(w_in, q_norm_a, k_norm_a, q_norm_b, k_norm_sel, k_norm_win):
    bf = jnp.bfloat16
    ng = 3 * NSA_HEADS
    w = (w_in[:, :_C_GATE].astype(bf),
         jnp.pad(w_in[:, _C_GATE:_C_GATE + ng], ((0, 0), (0, LANES - ng))).astype(bf),
         w_in[:, _C_GATE + ng:_C_GATE + ng + NSA_HEADS * HEAD_DIM].astype(bf),
         w_in[:, _C_GATE + ng + NSA_HEADS * HEAD_DIM:].astype(bf))
    ones = lambda n: jnp.ones((n,), jnp.float32)
    gain = jnp.concatenate([jnp.tile(q_norm_a, MOBA_HEADS), jnp.tile(k_norm_a, MOBA_HEADS), ones(1024),
                            jnp.tile(q_norm_b, NSA_HEADS), ones(256), jnp.tile(k_norm_sel, NSA_GROUPS), ones(128),
                            jnp.tile(k_norm_win, NSA_GROUPS), ones(256)]).reshape(1, _N1)
    return w, gain


def kernel(x, norm_w, w_in, q_norm_a, k_norm_a, q_norm_b, k_norm_cmp, k_norm_sel, k_norm_win, cmp_pos_k, cmp_w1_k, cmp_w2_k, cmp_pos_v, cmp_w1_v, cmp_w2_v, rel_bias, w_branch_a, w_branch_b, w_out):
    bf = jnp.bfloat16
    (w_main, w_gate, w_zb, w_gm), gain = _pack_in_weights(w_in[0], q_norm_a[0], k_norm_a[0], q_norm_b[0],
                                                           k_norm_sel[0], k_norm_win[0])
    bias = _bias_tiles(rel_bias)
    qa, kae, kao, vae, vao, qb, kmean, kc, vc, kse, kso, vsl, kwn, vwn, gate = _in_proj(x, norm_w[0], w_main, w_gate,
                                                                                        gain)
    kcmp, vcmp = _compress(kc, vc, cmp_pos_k[0], cmp_w1_k[0], cmp_w2_k[0],
                           cmp_pos_v[0], cmp_w1_v[0], cmp_w2_v[0], k_norm_cmp[0])
    oa = _moba(qa, kae, kao, vae, vao, kmean[:, :, 0, :], bias)
    ob = _nsa(qb, kcmp, vcmp, kse, kso, vsl, kwn, vwn, gate, bias)
    return _out_proj(x, norm_w[0], oa, ob, w_main, w_zb, w_gm, w_branch_a[0].astype(bf), w_branch_b[0].astype(bf),
                     w_out[0].astype(bf))
```

```python
import math

import jax
import jax.numpy as jnp
import numpy as np
from jax import lax
from jax.experimental import pallas as pl
from jax.experimental.pallas import tpu as pltpu

D_MODEL = 1024
HEAD_DIM = 64
MOBA_HEADS = 8
NSA_HEADS = 8
NSA_GROUPS = 2
NSA_HPG = NSA_HEADS // NSA_GROUPS
MOBA_BLOCK = 256
MOBA_TOPK = 3
CMP_LEN = 32
CMP_STRIDE = 16
CMP_HIDDEN = 256
SEL_BLOCK = 64
SEL_TOPN = 16
WINDOW = 512
NUM_BUCKETS = 32
MAX_DISTANCE = 1024
EPS = 1e-6
NEG = -1e30

LANES = 128
TQ = 256
TK = 2 * TQ
TR = 2 * TQ
TM = 2 * TQ
N_FAR = 5
N_WIN_TILES = WINDOW // TQ + 1
VMEM_LIMIT = 56 * 1024 * 1024
LOG2E = math.log2(math.e)
QSCALE = LOG2E / math.sqrt(HEAD_DIM)

_K_WIN = N_FAR + 1
_K_MASKED = _K_WIN + N_WIN_TILES
_N_TILE_KINDS = _K_MASKED + 1

_C_QA, _C_KA, _C_VA, _C_ZA, _C_QB = 0, 512, 1024, 1536, 2048
_C_KC, _C_VC, _C_KSL, _C_VSL, _C_KWN, _C_VWN, _C_GATE = 2560, 2688, 2816, 2944, 3072, 3200, 3328
_N1 = _C_GATE + LANES


def _nt(a, b):
    return lax.dot_general(a, b, (((1,), (1,)), ((), ())), preferred_element_type=jnp.float32)


def _nn(a, b):
    return jnp.dot(a, b, preferred_element_type=jnp.float32)


def _split(x):
    hi = x.astype(jnp.bfloat16)
    lo = (x - hi.astype(jnp.float32)).astype(jnp.bfloat16)
    return hi, lo


def _bucket_table(n):
    d = np.arange(n)
    nf = np.maximum(d, NUM_BUCKETS // 2).astype(np.float64)
    large = NUM_BUCKETS // 2 + (np.log(nf / (NUM_BUCKETS // 2)) / math.log(MAX_DISTANCE / (NUM_BUCKETS // 2))
                                * (NUM_BUCKETS - NUM_BUCKETS // 2)).astype(np.int64)
    return np.where(d < NUM_BUCKETS // 2, d, np.minimum(large, NUM_BUCKETS - 1))


def _bias_onehot():
    oh = np.zeros((40, _N_TILE_KINDS * 2 * TQ), np.float32)
    buckets = _bucket_table((N_FAR + 2) * TQ + 1)
    for t in range(_N_TILE_KINDS):
        window = _K_WIN <= t < _K_MASKED
        delta = t - _K_WIN if window else t
        for u in range(2 * TQ):
            d = TQ * delta + TQ - u
            masked = t == _K_MASKED or d < 0 or (window and d >= WINDOW)
            oh[NUM_BUCKETS if masked else buckets[d], t * 2 * TQ + u] = 1.0
    return oh


def _bias_body(relt_ref, oh_ref, out_ref):
    g = jnp.dot(jnp.broadcast_to(relt_ref[0], (8, 40)), oh_ref[...], preferred_element_type=jnp.float32,
                precision=lax.Precision.HIGHEST)
    for t in range(_N_TILE_KINDS):
        full = jnp.broadcast_to(g[0:1, t * 2 * TQ:(t + 1) * 2 * TQ], (TQ, 2 * TQ))
        rolled = pltpu.roll(full, TQ, 1, stride=1, stride_axis=0)
        out_ref[0, t] = rolled[:, :TQ]


def _bias_tiles(rel_bias):
    nh = MOBA_HEADS + NSA_HEADS
    relt = jnp.concatenate([rel_bias.T.astype(jnp.float32) * LOG2E, jnp.full((nh, 1), NEG, jnp.float32),
                            jnp.zeros((nh, 7), jnp.float32)], axis=1).reshape(nh, 1, 40)
    oh = jnp.asarray(_bias_onehot())
    return pl.pallas_call(
        _bias_body,
        grid=(nh,),
        in_specs=[pl.BlockSpec((1, 1, 40), lambda h: (h, 0, 0)), pl.BlockSpec(oh.shape, lambda h: (0, 0))],
        out_specs=pl.BlockSpec((1, _N_TILE_KINDS, TQ, TQ), lambda h: (h, 0, 0, 0)),
        out_shape=jax.ShapeDtypeStruct((nh, _N_TILE_KINDS, TQ, TQ), jnp.float32),
        compiler_params=pltpu.CompilerParams(dimension_semantics=("parallel",), vmem_limit_bytes=VMEM_LIMIT),
        name="bias_tiles",
    )(relt, oh)


def _seg_norm(x, lo_half, gain):
    x2 = x * x
    ss_lo = jnp.sum(jnp.where(lo_half, x2, 0.0), axis=-1, keepdims=True)
    ss_hi = jnp.sum(jnp.where(lo_half, 0.0, x2), axis=-1, keepdims=True)
    ss = jnp.where(lo_half, ss_lo, ss_hi)
    return x * lax.rsqrt(ss * (1.0 / HEAD_DIM) + EPS) * gain


def _in_proj_body(x_ref, nw_ref, w_ref, wg_ref, gain_ref,
                  qa_ref, kae_ref, kao_ref, vae_ref, vao_ref, qb_ref, kmean_ref, kc_ref, vc_ref,
                  kse_ref, kso_ref, vsl_ref, kwn_ref, vwn_ref, gate_ref, stage_k_ref, stage_v_ref):
    i = pl.program_id(1)
    x = x_ref[0]
    ms = jnp.mean(x * x, axis=-1, keepdims=True)
    h = (x * lax.rsqrt(ms + EPS) * nw_ref[...]).astype(jnp.bfloat16)
    bf = jnp.bfloat16
    lane = lax.broadcasted_iota(jnp.int32, (TM, LANES), 1)
    row = lax.broadcasted_iota(jnp.int32, (TM, LANES), 0)
    lo_half = lane < HEAD_DIM
    blk = i * (TM // MOBA_BLOCK) + lax.shift_right_logical(row, MOBA_BLOCK.bit_length() - 1)
    sblk = i * (TM // SEL_BLOCK) + lax.shift_right_logical(row, SEL_BLOCK.bit_length() - 1)

    def normed(col, width, scale=None):
        p = _nn(h, w_ref[:, col:col + width])
        outs = []
        for s in range(width // LANES):
            c = col + s * LANES
            y = _seg_norm(p[:, s * LANES:(s + 1) * LANES], lo_half, gain_ref[:, c:c + LANES])
            outs.append(y if scale is None else y * scale)
        return outs

    qa = normed(_C_QA, 512, QSCALE)
    for s in range(4):
        qa_ref[0, :, s * LANES:(s + 1) * LANES] = qa[s].astype(bf)
    ka = normed(_C_KA, 512)
    oh_hi = jnp.where(lane == HEAD_DIM + blk, 1.0, 0.0)
    oh_lo = jnp.where(lane == blk, 1.0, 0.0)
    for s in range(4):
        kae_ref[0, :, s * LANES:(s + 1) * LANES] = jnp.where(lo_half, ka[s], oh_hi).astype(bf)
        kao_ref[0, :, s * LANES:(s + 1) * LANES] = jnp.where(lo_half, oh_lo, ka[s]).astype(bf)
        for j in range(TM // MOBA_BLOCK):
            kmean_ref[0, j, :, s * LANES:(s + 1) * LANES] = jnp.broadcast_to(
                jnp.mean(ka[s][j * MOBA_BLOCK:(j + 1) * MOBA_BLOCK], axis=0, keepdims=True), (8, LANES))
    one_hi = jnp.where(lane == HEAD_DIM, 1.0, 0.0)
    one_lo = jnp.where(lane == 0, 1.0, 0.0)
    va = _nn(h, w_ref[:, _C_VA:_C_VA + 512])
    for s in range(4):
        vs = va[:, s * LANES:(s + 1) * LANES]
        vae_ref[0, :, s * LANES:(s + 1) * LANES] = jnp.where(lo_half, vs, one_hi).astype(bf)
        vao_ref[0, :, s * LANES:(s + 1) * LANES] = jnp.where(lo_half, one_lo, vs).astype(bf)
    qb = normed(_C_QB, 512, QSCALE)
    for s in range(4):
        qb_ref[0, :, s * LANES:(s + 1) * LANES] = qb[s].astype(bf)

    rest = _nn(h, w_ref[:, _C_KC:_C_GATE])
    gate_ref[0] = jnp.concatenate([_nn(h[0:TM // 2], wg_ref[...]), _nn(h[TM // 2:], wg_ref[...])], axis=0)

    def slab(c):
        return rest[:, c - _C_KC:c - _C_KC + LANES]

    def chunk_store(ref, stage_ref, y):
        stage_ref[...] = y
        lo = lax.broadcasted_iota(jnp.int32, (TM // CMP_STRIDE, LANES), 1) < HEAD_DIM
        for u in range(CMP_STRIDE // 2):
            t0 = stage_ref[pl.ds(2 * u, TM // CMP_STRIDE, stride=CMP_STRIDE), :]
            t1 = stage_ref[pl.ds(2 * u + 1, TM // CMP_STRIDE, stride=CMP_STRIDE), :]
            ref[0, 0, :, u * LANES:(u + 1) * LANES] = jnp.where(lo, t0, pltpu.roll(t1, HEAD_DIM, 1)).astype(bf)
            ref[0, 1, :, u * LANES:(u + 1) * LANES] = jnp.where(lo, pltpu.roll(t0, HEAD_DIM, 1), t1).astype(bf)

    chunk_store(kc_ref, stage_k_ref, slab(_C_KC))
    chunk_store(vc_ref, stage_v_ref, slab(_C_VC))

    def dup_store(ref, y):
        r = pltpu.roll(y, HEAD_DIM, 1)
        ref[0, 0] = jnp.where(lo_half, y, r).astype(bf)
        ref[0, 1] = jnp.where(lo_half, r, y).astype(bf)

    y = _seg_norm(slab(_C_KSL), lo_half, gain_ref[:, _C_KSL:_C_KSL + LANES])
    r = pltpu.roll(y, HEAD_DIM, 1)
    oh = jnp.where((lane & (HEAD_DIM - 1)) == sblk, 1.0, 0.0)
    kse_ref[0, 0] = jnp.where(lo_half, y, oh).astype(bf)
    kso_ref[0, 0] = jnp.where(lo_half, oh, r).astype(bf)
    kse_ref[0, 1] = jnp.where(lo_half, r, oh).astype(bf)
    kso_ref[0, 1] = jnp.where(lo_half, oh, y).astype(bf)
    def aug_store(ref, y):
        ref[0, 0] = jnp.where(lo_half, y, one_hi).astype(bf)
        ref[0, 1] = jnp.where(lo_half, pltpu.roll(y, HEAD_DIM, 1), one_hi).astype(bf)

    aug_store(vsl_ref, slab(_C_VSL))
    dup_store(kwn_ref, _seg_norm(slab(_C_KWN), lo_half, gain_ref[:, _C_KWN:_C_KWN + LANES]))
    aug_store(vwn_ref, slab(_C_VWN))


def _in_proj(x, norm_w, w1, wg, gain_row):
    B, S, _ = x.shape
    nt = S // TQ
    bf = jnp.bfloat16
    tok = lambda w, dt: jax.ShapeDtypeStruct((B, S, w), dt)
    grp = jax.ShapeDtypeStruct((B, NSA_GROUPS, S, LANES), bf)
    tok_spec = lambda w: pl.BlockSpec((1, TM, w), lambda b, i: (b, i, 0))
    grp_spec = pl.BlockSpec((1, NSA_GROUPS, TM, LANES), lambda b, i: (b, 0, i, 0))
    flat = CMP_STRIDE * HEAD_DIM
    chunk = jax.ShapeDtypeStruct((B, NSA_GROUPS, S // CMP_STRIDE, flat), bf)
    chunk_spec = pl.BlockSpec((1, NSA_GROUPS, TM // CMP_STRIDE, flat), lambda b, i: (b, 0, i, 0))
    full = lambda a: pl.BlockSpec(a.shape, lambda b, i: (0,) * a.ndim, pipeline_mode=pl.Buffered(1))
    nw = norm_w.reshape(1, D_MODEL)
    return pl.pallas_call(
        _in_proj_body,
        grid=(B, S // TM),
        in_specs=[tok_spec(D_MODEL), full(nw), full(w1), full(wg), full(gain_row)],
        out_specs=[tok_spec(512), tok_spec(512), tok_spec(512), tok_spec(512), tok_spec(512), tok_spec(512),
                   pl.BlockSpec((1, TM // MOBA_BLOCK, 8, 512), lambda b, i: (b, i, 0, 0)),
                   chunk_spec, chunk_spec,
                   grp_spec, grp_spec, grp_spec, grp_spec, grp_spec, tok_spec(LANES)],
        out_shape=[tok(512, bf), tok(512, bf), tok(512, bf), tok(512, bf), tok(512, bf), tok(512, bf),
                   jax.ShapeDtypeStruct((B, nt, 8, 512), jnp.float32),
                   chunk, chunk, grp, grp, grp, grp, grp, tok(LANES, jnp.float32)],
        scratch_shapes=[pltpu.VMEM((TM, LANES), jnp.float32)] * 2,
        compiler_params=pltpu.CompilerParams(dimension_semantics=("parallel", "parallel"),
                                             vmem_limit_bytes=VMEM_LIMIT),
        name="in_proj",
    )(x, nw, w1, wg, gain_row)


def _gelu_tanh(x):
    return 0.5 * x * (1.0 + jnp.tanh(math.sqrt(2.0 / math.pi) * (x + 0.044715 * (x * x * x))))


def _compress_body(xk_ref, xv_ref, w1k_ref, w1v_ref, pk_ref, pv_ref,
                   w2k_ref, w2v_ref, gk_ref, kcmp_ref, vcmp_ref):
    nchunk = xk_ref.shape[2]
    half = CMP_STRIDE * HEAD_DIM

    def branch(x_ref, w1_ref, pos_ref, w2_ref, gain):
        pos = jnp.broadcast_to(pos_ref[...], (8, CMP_LEN * HEAD_DIM))
        posw = (_nn(pos[:, 0:half], w1_ref[...])[0:1, 0:CMP_HIDDEN]
                + _nn(pos[:, half:], w1_ref[...])[0:1, CMP_HIDDEN:])
        outs = []
        for g in range(NSA_GROUPS):
            ab = _nn(x_ref[0, g], w1_ref[...])
            top, bot = ab[:, 0:CMP_HIDDEN], ab[:, CMP_HIDDEN:]
            hid = top + pltpu.roll(bot, nchunk - 1, 0) + posw
            y = _nn(_gelu_tanh(hid).astype(jnp.bfloat16), w2_ref[...])
            if gain is not None:
                ms = jnp.sum(y * y, axis=-1, keepdims=True) * (0.5 / HEAD_DIM)
                y = y * lax.rsqrt(ms + EPS) * gain
            else:
                lane = lax.broadcasted_iota(jnp.int32, y.shape, 1)
                y = jnp.where(lane < HEAD_DIM, y, jnp.where(lane == HEAD_DIM, 1.0, 0.0))
            outs.append(y.astype(jnp.bfloat16))
        return outs

    k0, k1 = branch(xk_ref, w1k_ref, pk_ref, w2k_ref, gk_ref[...])
    kcmp_ref[0, 0], kcmp_ref[0, 1] = k0, k1
    v0, v1 = branch(xv_ref, w1v_ref, pv_ref, w2v_ref, None)
    vcmp_ref[0, 0], vcmp_ref[0, 1] = v0, v1


def _compress(xk, xv, pos_k, w1_k, w2_k, pos_v, w1_v, w2_v, gain_cmp):
    B, _, nchunk, half = xk.shape
    bf = jnp.bfloat16
    halves = lambda w1: jnp.concatenate([w1[:half], w1[half:]], axis=1).astype(bf)
    args = [xk, xv, halves(w1_k), halves(w1_v),
            pos_k.reshape(1, -1).astype(bf), pos_v.reshape(1, -1).astype(bf),
            jnp.concatenate([w2_k, w2_k], axis=1).astype(bf), jnp.concatenate([w2_v, w2_v], axis=1).astype(bf),
            jnp.concatenate([gain_cmp, gain_cmp]).reshape(1, LANES)]
    x_spec = pl.BlockSpec((1, NSA_GROUPS, nchunk, half), lambda b: (b, 0, 0, 0))
    full = lambda a: pl.BlockSpec(a.shape, lambda b: (0,) * a.ndim)
    out = jax.ShapeDtypeStruct((B, NSA_GROUPS, nchunk, LANES), bf)
    o_spec = pl.BlockSpec((1, NSA_GROUPS, nchunk, LANES), lambda b: (b, 0, 0, 0))
    return pl.pallas_call(
        _compress_body,
        grid=(B,),
        in_specs=[x_spec, x_spec] + [full(a) for a in args[2:]],
        out_specs=[o_spec, o_spec],
        out_shape=[out, out],
        compiler_params=pltpu.CompilerParams(dimension_semantics=("parallel",), vmem_limit_bytes=VMEM_LIMIT),
        name="compress",
    )(*args)


def _softmax_step(s, v_top, v_bot, m_ref, acc_ref, first):
    blocks = [s[:, c * LANES:(c + 1) * LANES] for c in range(s.shape[1] // LANES)]
    bmax = blocks[0]
    for blk in blocks[1:]:
        bmax = jnp.maximum(bmax, blk)
    m_new = jnp.broadcast_to(jnp.max(bmax, axis=-1, keepdims=True), m_ref.shape)
    if not first:
        m_old = m_ref[...]
        m_new = jnp.maximum(m_old, m_new)
        alpha = jnp.exp2(m_old - m_new)
    pb = jnp.concatenate([jnp.exp2(blk - m_new) for blk in blocks], axis=1).astype(jnp.bfloat16)
    half = s.shape[0] // 2
    pv = jnp.concatenate([_nn(pb[0:half], v_top), _nn(pb[half:], v_bot)], axis=0)
    acc_ref[...] = pv if first else alpha * acc_ref[...] + pv
    m_ref[...] = m_new


def _finish(acc, ones_lane):
    lane = lax.broadcasted_iota(jnp.int32, acc.shape, 1)
    return acc / jnp.sum(jnp.where(lane == ones_lane, acc, 0.0), axis=-1, keepdims=True)


def _tile_kind(delta):
    return jnp.where(delta >= 0, jnp.minimum(delta, N_FAR), _K_MASKED)


def _chunk_kinds(qi, c):
    d0 = qi - 2 * c
    kind0 = jnp.minimum(d0, N_FAR)
    kind1 = jnp.where(d0 >= 1, jnp.minimum(d0 - 1, N_FAR), _K_MASKED)
    return kind0, kind1


def _moba_body(q_ref, ke_ref, ko_ref, ve_ref, vo_ref, kmean_ref, eye_ref, bias_ref, o_ref, m_ref, acc_ref):
    i = pl.program_id(2)
    nblk = kmean_ref.shape[1]
    lane = lax.broadcasted_iota(jnp.int32, (TR, LANES), 1)
    lo_half = lane < HEAD_DIM
    lane16 = lax.broadcasted_iota(jnp.int32, (nblk, LANES), 1)
    jrow = lax.broadcasted_iota(jnp.int32, (nblk, TR), 0)
    qblk = 2 * i + jnp.where(lax.broadcasted_iota(jnp.int32, (nblk, TR), 1) >= TQ, 1, 0)
    q2 = q_ref[0]
    kmean2 = kmean_ref[0]
    flags = []
    for hh in range(2):
        inhead16 = (lane16 < HEAD_DIM) if hh == 0 else (lane16 >= HEAD_DIM)
        km_hi, km_lo = _split(jnp.where(inhead16, kmean2, 0.0))
        sc = jnp.where(jrow < qblk, _nt(km_hi, q2) + _nt(km_lo, q2), -jnp.inf)
        flag = jnp.where(jrow == qblk, 0.0, 1.0)
        for _ in range(MOBA_TOPK):
            best = jnp.max(sc, axis=0, keepdims=True)
            first = jnp.min(jnp.where(sc == best, jrow, nblk), axis=0, keepdims=True)
            hit = jrow == first
            flag = jnp.where(hit, 0.0, flag)
            sc = jnp.where(hit, -jnp.inf, sc)
        flags.append(flag)
    pad = jnp.zeros((HEAD_DIM - nblk, TR), jnp.float32)
    placed = jnp.concatenate([flags[1], pad, flags[0], pad], axis=0).astype(jnp.bfloat16)
    maskpart = jnp.concatenate([_nt(eye_ref[...], placed[:, a * TQ:(a + 1) * TQ]) for a in range(TR // TQ)],
                               axis=0) * NEG
    q2f = q2.astype(jnp.float32)
    zero = jnp.zeros((TR, LANES), jnp.bfloat16)
    qq = jnp.concatenate(
        [jnp.concatenate([jnp.where(lo_half, q2f, maskpart).astype(jnp.bfloat16), zero], axis=1),
         jnp.concatenate([zero, jnp.where(lo_half, maskpart, q2f).astype(jnp.bfloat16)], axis=1)], axis=0)

    def step(c, first):
        start = pl.multiple_of(c * TK, TK)
        kk = jnp.concatenate([ke_ref[0, pl.ds(start, TK), :], ko_ref[0, pl.ds(start, TK), :]], axis=1)
        d = 2 * (i - c)
        b = jnp.concatenate(
            [jnp.concatenate([bias_ref[hh, _tile_kind(d + a - t)] for t in range(2)], axis=1)
             for hh in range(2) for a in range(2)], axis=0)
        _softmax_step(_nt(qq, kk) + b, ve_ref[0, pl.ds(start, TK), :], vo_ref[0, pl.ds(start, TK), :],
                      m_ref, acc_ref, first)

    step(i, True)

    def body(c, carry):
        step(c, False)
        return carry

    lax.fori_loop(0, i, body, 0)
    o_ref[0] = jnp.where(lo_half, _finish(acc_ref[0:TR], HEAD_DIM), _finish(acc_ref[TR:2 * TR], 0))


def _eye(n):
    return jnp.asarray(np.eye(n, dtype=np.float32), jnp.bfloat16)


def _moba(qa, kae, kao, vae, vao, kmean, bias):
    B, S, _ = qa.shape
    nt = S // TQ
    npair = MOBA_HEADS // 2
    kv_spec = pl.BlockSpec((1, S, LANES), lambda b, p, i: (b, 0, p))
    return pl.pallas_call(
        _moba_body,
        grid=(B, npair, S // TR),
        in_specs=[pl.BlockSpec((1, TR, LANES), lambda b, p, i: (b, i, p)),
                  kv_spec, kv_spec, kv_spec, kv_spec,
                  pl.BlockSpec((1, nt, LANES), lambda b, p, i: (b, 0, p)),
                  pl.BlockSpec((TQ, TQ), lambda b, p, i: (0, 0)),
                  pl.BlockSpec((2, _N_TILE_KINDS, TQ, TQ), lambda b, p, i: (p, 0, 0, 0))],
        out_specs=pl.BlockSpec((1, TR, LANES), lambda b, p, i: (b, i, p)),
        out_shape=jax.ShapeDtypeStruct((B, S, MOBA_HEADS * HEAD_DIM), jnp.float32),
        scratch_shapes=[pltpu.VMEM((2 * TR, LANES), jnp.float32)] * 2,
        compiler_params=pltpu.CompilerParams(dimension_semantics=("parallel", "parallel", "arbitrary"),
                                             vmem_limit_bytes=VMEM_LIMIT),
        name="moba",
    )(qa, kae, kao, vae, vao, kmean, _eye(TQ), bias)


def _nsa_body(q_ref, kcmp_ref, vcmp_ref, kse_ref, kso_ref, vsl_ref, kwn_ref, vwn_ref, gate_ref, ov_ref, eye_ref,
              gexp_ref, bias_ref, o_ref, m_ref, acc_ref):
    i = pl.program_id(2)

    def to_hi(o):
        return pltpu.roll(o, HEAD_DIM, 1)

    ncmp = kcmp_ref.shape[2]
    nsel = ov_ref.shape[0]
    R = NSA_HPG * TR
    bf = jnp.bfloat16
    lane = lax.broadcasted_iota(jnp.int32, (TR, LANES), 1)
    lo_half = lane < HEAD_DIM
    qf = [q_ref[0, :, 0:LANES].astype(jnp.float32), q_ref[0, :, LANES:2 * LANES].astype(jnp.float32)]
    q4 = jnp.concatenate([jnp.where(lo_half if hh % 2 == 0 else ~lo_half, qf[hh // 2], 0.0).astype(bf)
                          for hh in range(NSA_HPG)], axis=0)

    tpos = i * TR + lax.broadcasted_iota(jnp.int32, (TR, ncmp), 0)
    cidx = lax.broadcasted_iota(jnp.int32, (TR, ncmp), 1)
    cbias1 = jnp.where((cidx * CMP_STRIDE + (CMP_LEN - 1) <= tpos) & (cidx < ncmp - 1), 0.0, NEG)
    lc = _nt(q4, kcmp_ref[0, 0]) + jnp.concatenate([cbias1] * NSA_HPG, axis=0)
    cblocks = [lc[:, c * LANES:(c + 1) * LANES] for c in range(ncmp // LANES)]
    cmax = cblocks[0]
    for blk in cblocks[1:]:
        cmax = jnp.maximum(cmax, blk)
    cm = jnp.maximum(jnp.broadcast_to(jnp.max(cmax, axis=-1, keepdims=True), (R, LANES)), 0.5 * NEG)
    ce = [jnp.exp2(blk - cm) for blk in cblocks]
    cden = ce[0]
    for eb in ce[1:]:
        cden = cden + eb
    cden = jnp.broadcast_to(jnp.sum(cden, axis=-1, keepdims=True), (R, LANES))
    crcp = 1.0 / jnp.where(cden > 0.0, cden, 1.0)
    pc = jnp.concatenate([eb * crcp for eb in ce], axis=1)
    pcb = pc.astype(bf)
    o_cmp = jnp.concatenate([_nn(pcb[0:R // 2], vcmp_ref[0, 0]), _nn(pcb[R // 2:], vcmp_ref[0, 0])], axis=0)
    psum = pc[0:TR] + pc[TR:2 * TR] + pc[2 * TR:3 * TR] + pc[3 * TR:4 * TR]
    p_hi, p_lo = _split(psum)
    ov = ov_ref[...]
    imp = _nt(ov, p_hi) + _nt(ov, p_lo)

    o_win = []
    for a in range(2):
        qt = 2 * i + a
        t0 = jnp.maximum(qt - (N_WIN_TILES - 1), 0)
        wstart = pl.multiple_of(t0 * TQ, TQ)
        qh = jnp.concatenate([q4[hh * TR + a * TQ:hh * TR + (a + 1) * TQ] for hh in range(NSA_HPG)], axis=0)
        kinds = []
        for r in range(N_WIN_TILES):
            dd = qt - (t0 + r)
            kinds.append(jnp.where(dd >= 0, _K_WIN + dd, _K_MASKED))
        bw = jnp.concatenate([jnp.concatenate([bias_ref[hh, k] for k in kinds], axis=1)
                              for hh in range(NSA_HPG)], axis=0)
        sw = _nt(qh, kwn_ref[0, 0, pl.ds(wstart, N_WIN_TILES * TQ), :]) + bw
        wm, wacc = m_ref.at[0:NSA_HPG * TQ], acc_ref.at[0:NSA_HPG * TQ]
        vw = vwn_ref[0, 0, pl.ds(wstart, N_WIN_TILES * TQ), :]
        _softmax_step(sw, vw, vw, wm, wacc, True)
        o_win.append(_finish(wacc[...], HEAD_DIM))

    jrow = lax.broadcasted_iota(jnp.int32, (nsel, TR), 0)
    qblk = lax.shift_right_logical(i * TR + lax.broadcasted_iota(jnp.int32, (nsel, TR), 1),
                                   SEL_BLOCK.bit_length() - 1)
    cand = (jrow >= 1) & (jrow < qblk)
    keys = jnp.where(cand, imp, -1.0)
    flag = jnp.where((jrow == 0) | (jrow == qblk), 0.0, 1.0)
    for _ in range(SEL_TOPN - 2):
        best = jnp.max(keys, axis=0, keepdims=True)
        first = jnp.min(jnp.where(keys == best, jrow, nsel), axis=0, keepdims=True)
        hit = jrow == first
        flag = jnp.where(hit, 0.0, flag)
        keys = jnp.where(hit, -1.0, keys)
    if nsel < HEAD_DIM:
        flag = jnp.concatenate([flag, jnp.zeros((HEAD_DIM - nsel, TR), jnp.float32)], axis=0)
    flag2 = jnp.concatenate([flag, flag], axis=0).astype(bf)
    maskpart = jnp.concatenate([_nt(eye_ref[...], flag2[:, a * TQ:(a + 1) * TQ]) for a in range(TR // TQ)],
                               axis=0) * NEG

    zero = jnp.zeros((2 * TR, LANES), bf)
    qa2 = [jnp.concatenate([jnp.where(lo_half if par == 0 else ~lo_half, qf[s], maskpart).astype(bf)
                            for s in range(2)], axis=0) for par in range(2)]
    qq = jnp.concatenate([jnp.concatenate([qa2[0], zero], axis=1),
                          jnp.concatenate([zero, qa2[1]], axis=1)], axis=0)
    sel_heads = (0, 2, 1, 3)

    def sel_step(c, first):
        start = pl.multiple_of(c * TK, TK)
        kk = jnp.concatenate([kse_ref[0, 0, pl.ds(start, TK), :], kso_ref[0, 0, pl.ds(start, TK), :]], axis=1)
        d = 2 * (i - c)
        b = jnp.concatenate(
            [jnp.concatenate([bias_ref[h, _tile_kind(d + a - t)] for t in range(2)], axis=1)
             for h in sel_heads for a in range(2)], axis=0)
        v = vsl_ref[0, 0, pl.ds(start, TK), :]
        _softmax_step(_nt(qq, kk) + b, v, v, m_ref, acc_ref, first)

    sel_step(i, True)

    def sel_body(c, carry):
        sel_step(c, False)
        return carry

    lax.fori_loop(0, i, sel_body, 0)
    o_all = _finish(acc_ref[...], HEAD_DIM)
    o_sel = [o_all[0:2 * TR], to_hi(o_all[2 * TR:4 * TR])]

    g = jax.nn.sigmoid(gate_ref[0])
    g_hi, g_lo = _split(g)
    gx = _nn(g_hi, gexp_ref[0]) + _nn(g_lo, gexp_ref[0])
    for s in range(2):
        h_even, h_odd = 2 * s, 2 * s + 1
        cmp2 = jnp.where(lo_half, o_cmp[h_even * TR:(h_even + 1) * TR], to_hi(o_cmp[h_odd * TR:(h_odd + 1) * TR]))
        sel2 = jnp.where(lo_half, o_sel[0][s * TR:(s + 1) * TR], o_sel[1][s * TR:(s + 1) * TR])
        win2 = jnp.where(lo_half,
                         jnp.concatenate([o_win[a][h_even * TQ:(h_even + 1) * TQ] for a in range(2)], axis=0),
                         to_hi(jnp.concatenate([o_win[a][h_odd * TQ:(h_odd + 1) * TQ] for a in range(2)], axis=0)))
        base = s * 3 * LANES
        o_ref[0, :, s * LANES:(s + 1) * LANES] = (gx[:, base:base + LANES] * cmp2
                                                  + gx[:, base + LANES:base + 2 * LANES] * sel2
                                                  + gx[:, base + 2 * LANES:base + 3 * LANES] * win2)


def _gate_expand():
    e = np.zeros((NSA_GROUPS, LANES, 2 * 3 * LANES), np.float32)
    for g in range(NSA_GROUPS):
        for s in range(2):
            for c in range(3):
                for half in range(2):
                    h = g * NSA_HPG + 2 * s + half
                    col = (s * 3 + c) * LANES + half * HEAD_DIM
                    e[g, 3 * h + c, col:col + HEAD_DIM] = 1.0
    return e


def _nsa(qb, kcmp, vcmp, kse, kso, vsl, kwn, vwn, gate, bias):
    B, S, _ = qb.shape
    nt = S // TQ
    ncmp = kcmp.shape[2]
    nsel = S // SEL_BLOCK
    bf = jnp.bfloat16
    cs = np.arange(ncmp) * CMP_STRIDE
    bs = np.arange(nsel) * SEL_BLOCK
    ov = ((cs[None, :] < bs[:, None] + SEL_BLOCK) & (cs[None, :] + CMP_LEN > bs[:, None])
          & (np.arange(ncmp)[None, :] < ncmp - 1)).astype(np.float32)
    grp = lambda n: pl.BlockSpec((1, 1, n, LANES), lambda b, g, i: (b, g, 0, 0))
    return pl.pallas_call(
        _nsa_body,
        grid=(B, NSA_GROUPS, S // TR),
        in_specs=[pl.BlockSpec((1, TR, 2 * LANES), lambda b, g, i: (b, i, g)),
                  grp(ncmp), grp(ncmp), grp(S), grp(S), grp(S), grp(S), grp(S),
                  pl.BlockSpec((1, TR, LANES), lambda b, g, i: (b, i, 0)),
                  pl.BlockSpec((nsel, ncmp), lambda b, g, i: (0, 0)),
                  pl.BlockSpec((TQ, TQ), lambda b, g, i: (0, 0)),
                  pl.BlockSpec((1, LANES, 6 * LANES), lambda b, g, i: (g, 0, 0)),
                  pl.BlockSpec((NSA_HPG, _N_TILE_KINDS, TQ, TQ), lambda b, g, i: (2 + g, 0, 0, 0),
                               pipeline_mode=pl.Buffered(1))],
        out_specs=pl.BlockSpec((1, TR, 2 * LANES), lambda b, g, i: (b, i, g)),
        out_shape=jax.ShapeDtypeStruct((B, S, NSA_HEADS * HEAD_DIM), jnp.float32),
        scratch_shapes=[pltpu.VMEM((NSA_HPG * TR, LANES), jnp.float32)] * 2,
        compiler_params=pltpu.CompilerParams(dimension_semantics=("parallel", "parallel", "arbitrary"),
                                             vmem_limit_bytes=VMEM_LIMIT),
        name="nsa",
    )(qb, kcmp, vcmp, kse, kso, vsl, kwn, vwn, gate, jnp.asarray(ov, bf), _eye(TQ),
      jnp.asarray(_gate_expand(), bf), bias)


def _out_proj_body(x_ref, nw_ref, oa_ref, ob_ref, wz_ref, wzb_ref, wgm_ref, wa_ref, wb_ref, wo_ref, out_ref):
    x = x_ref[0]
    ms = jnp.mean(x * x, axis=-1, keepdims=True)
    h = (x * lax.rsqrt(ms + EPS) * nw_ref[...]).astype(jnp.bfloat16)
    ya = (oa_ref[0] * jax.nn.silu(_nn(h, wz_ref[:, _C_ZA:_C_ZA + 512]))).astype(jnp.bfloat16)
    yb = (ob_ref[0] * jax.nn.silu(_nn(h, wzb_ref[...]))).astype(jnp.bfloat16)
    gm = jax.nn.sigmoid(_nn(h, wgm_ref[...]))
    merged = gm[:, 0:D_MODEL] * _nn(ya, wa_ref[...]) + gm[:, D_MODEL:] * _nn(yb, wb_ref[...])
    out_ref[0] = x + _nn(merged.astype(jnp.bfloat16), wo_ref[...])


def _out_proj(x, norm_w, oa, ob, wz, wzb, wgm, wa, wb, wo):
    B, S, _ = x.shape
    nw = norm_w.reshape(1, D_MODEL)
    tok = lambda w: pl.BlockSpec((1, TM, w), lambda b, i: (b, i, 0))
    full = lambda a: pl.BlockSpec(a.shape, lambda b, i: (0,) * a.ndim, pipeline_mode=pl.Buffered(1))
    return pl.pallas_call(
        _out_proj_body,
        grid=(B, S // TM),
        in_specs=[tok(D_MODEL), full(nw), tok(512), tok(512), full(wz), full(wzb), full(wgm),
                  full(wa), full(wb), full(wo)],
        out_specs=tok(D_MODEL),
        out_shape=jax.ShapeDtypeStruct((B, S, D_MODEL), jnp.float32),
        compiler_params=pltpu.CompilerParams(dimension_semantics=("parallel", "parallel"),
                                             vmem_limit_bytes=VMEM_LIMIT),
        name="out_proj",
    )(x, nw, oa, ob, wz, wzb, wgm, wa, wb, wo)


def _pack_in_weights(w_in, q_norm_a, k_norm_a, q_norm_b, k_norm_sel, k_norm_win):
    bf = jnp.bfloat16
    ng = 3 * NSA_HEADS
    cols = lambda a, b: lax.slice_in_dim(w_in, a, b, axis=2).astype(bf).reshape(D_MODEL, b - a)
    z_b = _C_GATE + ng
    w = (cols(0, _C_GATE),
         jnp.pad(cols(_C_GATE, z_b), ((0, 0), (0, LANES - ng))),
         cols(z_b, z_b + NSA_HEADS * HEAD_DIM),
         cols(z_b + NSA_HEADS * HEAD_DIM, w_in.shape[2]))
    ones = lambda n: jnp.ones((n,), jnp.float32)
    gain = jnp.concatenate([jnp.tile(q_norm_a, MOBA_HEADS), jnp.tile(k_norm_a, MOBA_HEADS), ones(1024),
                            jnp.tile(q_norm_b, NSA_HEADS), ones(256), jnp.tile(k_norm_sel, NSA_GROUPS), ones(128),
                            jnp.tile(k_norm_win, NSA_GROUPS), ones(256)]).reshape(1, _N1)
    return w, gain


def kernel(x, norm_w, w_in, q_norm_a, k_norm_a, q_norm_b, k_norm_cmp, k_norm_sel, k_norm_win, cmp_pos_k, cmp_w1_k, cmp_w2_k, cmp_pos_v, cmp_w1_v, cmp_w2_v, rel_bias, w_branch_a, w_branch_b, w_out):
    bf = jnp.bfloat16
    (w_main, w_gate, w_zb, w_gm), gain = _pack_in_weights(w_in, q_norm_a[0], k_norm_a[0], q_norm_b[0],
                                                           k_norm_sel[0], k_norm_win[0])
    bias = _bias_tiles(rel_bias)
    qa, kae, kao, vae, vao, qb, kmean, kc, vc, kse, kso, vsl, kwn, vwn, gate = _in_proj(x, norm_w[0], w_main, w_gate,
                                                                                        gain)
    kcmp, vcmp = _compress(kc, vc, cmp_pos_k[0], cmp_w1_k[0], cmp_w2_k[0],
                           cmp_pos_v[0], cmp_w1_v[0], cmp_w2_v[0], k_norm_cmp[0])
    oa = _moba(qa, kae, kao, vae, vao, kmean[:, :, 0, :], bias)
    ob = _nsa(qb, kcmp, vcmp, kse, kso, vsl, kwn, vwn, gate, bias)
    return _out_proj(x, norm_w[0], oa, ob, w_main, w_zb, w_gm, w_branch_a[0].astype(bf), w_branch_b[0].astype(bf),
                     w_out[0].astype(bf))
```

```python
import math

import jax
import jax.numpy as jnp
import numpy as np
from jax import lax
from jax.experimental import pallas as pl
from jax.experimental.pallas import tpu as pltpu

D_MODEL = 1024
HEAD_DIM = 64
MOBA_HEADS = 8
NSA_HEADS = 8
NSA_GROUPS = 2
NSA_HPG = NSA_HEADS // NSA_GROUPS
MOBA_BLOCK = 256
MOBA_TOPK = 3
CMP_LEN = 32
CMP_STRIDE = 16
CMP_HIDDEN = 256
SEL_BLOCK = 64
SEL_TOPN = 16
WINDOW = 512
NUM_BUCKETS = 32
MAX_DISTANCE = 1024
EPS = 1e-6
NEG = -1e30

LANES = 128
TQ = 256
TK = 2 * TQ
TR = 2 * TQ
TM = 2 * TQ
MOBA_PAIRS = 2
N_FAR = 5
N_WIN_TILES = WINDOW // TQ + 1
VMEM_LIMIT = 56 * 1024 * 1024
LOG2E = math.log2(math.e)
QSCALE = LOG2E / math.sqrt(HEAD_DIM)

_K_WIN = N_FAR + 1
_K_MASKED = _K_WIN + N_WIN_TILES
_N_TILE_KINDS = _K_MASKED + 1

_C_QA, _C_KA, _C_VA, _C_ZA, _C_QB = 0, 512, 1024, 1536, 2048
_C_KC, _C_VC, _C_KSL, _C_VSL, _C_KWN, _C_VWN, _C_GATE = 2560, 2688, 2816, 2944, 3072, 3200, 3328
_N1 = _C_GATE + LANES


def _nt(a, b):
    return lax.dot_general(a, b, (((1,), (1,)), ((), ())), preferred_element_type=jnp.float32)


def _nn(a, b):
    return jnp.dot(a, b, preferred_element_type=jnp.float32)


def _split(x):
    hi = x.astype(jnp.bfloat16)
    lo = (x - hi.astype(jnp.float32)).astype(jnp.bfloat16)
    return hi, lo


def _bucket_table(n):
    d = np.arange(n)
    nf = np.maximum(d, NUM_BUCKETS // 2).astype(np.float64)
    large = NUM_BUCKETS // 2 + (np.log(nf / (NUM_BUCKETS // 2)) / math.log(MAX_DISTANCE / (NUM_BUCKETS // 2))
                                * (NUM_BUCKETS - NUM_BUCKETS // 2)).astype(np.int64)
    return np.where(d < NUM_BUCKETS // 2, d, np.minimum(large, NUM_BUCKETS - 1))


def _bias_onehot():
    oh = np.zeros((40, _N_TILE_KINDS * 2 * TQ), np.float32)
    buckets = _bucket_table((N_FAR + 2) * TQ + 1)
    for t in range(_N_TILE_KINDS):
        window = _K_WIN <= t < _K_MASKED
        delta = t - _K_WIN if window else t
        for u in range(2 * TQ):
            d = TQ * delta + TQ - u
            masked = t == _K_MASKED or d < 0 or (window and d >= WINDOW)
            oh[NUM_BUCKETS if masked else buckets[d], t * 2 * TQ + u] = 1.0
    return oh


def _bias_body(relt_ref, oh_ref, out_ref):
    g = jnp.dot(jnp.broadcast_to(relt_ref[0], (8, 40)), oh_ref[...], preferred_element_type=jnp.float32,
                precision=lax.Precision.HIGHEST)
    for t in range(_N_TILE_KINDS):
        full = jnp.broadcast_to(g[0:1, t * 2 * TQ:(t + 1) * 2 * TQ], (TQ, 2 * TQ))
        rolled = pltpu.roll(full, TQ, 1, stride=1, stride_axis=0)
        out_ref[0, t] = rolled[:, :TQ]


def _bias_tiles(rel_bias):
    nh = MOBA_HEADS + NSA_HEADS
    relt = jnp.concatenate([rel_bias.T.astype(jnp.float32) * LOG2E, jnp.full((nh, 1), NEG, jnp.float32),
                            jnp.zeros((nh, 7), jnp.float32)], axis=1).reshape(nh, 1, 40)
    oh = jnp.asarray(_bias_onehot())
    return pl.pallas_call(
        _bias_body,
        grid=(nh,),
        in_specs=[pl.BlockSpec((1, 1, 40), lambda h: (h, 0, 0)), pl.BlockSpec(oh.shape, lambda h: (0, 0))],
        out_specs=pl.BlockSpec((1, _N_TILE_KINDS, TQ, TQ), lambda h: (h, 0, 0, 0)),
        out_shape=jax.ShapeDtypeStruct((nh, _N_TILE_KINDS, TQ, TQ), jnp.float32),
        compiler_params=pltpu.CompilerParams(dimension_semantics=("parallel",), vmem_limit_bytes=VMEM_LIMIT),
        name="bias_tiles",
    )(relt, oh)


def _seg_norm(x, lo_half, gain):
    x2 = x * x
    ss_lo = jnp.sum(jnp.where(lo_half, x2, 0.0), axis=-1, keepdims=True)
    ss_hi = jnp.sum(jnp.where(lo_half, 0.0, x2), axis=-1, keepdims=True)
    ss = jnp.where(lo_half, ss_lo, ss_hi)
    return x * lax.rsqrt(ss * (1.0 / HEAD_DIM) + EPS) * gain


def _in_proj_body(x_ref, nw_ref, w_ref, wg_ref, gain_ref,
                  qa_ref, kae_ref, kao_ref, vae_ref, vao_ref, qb_ref, kmean_ref, kc_ref, vc_ref,
                  kse_ref, kso_ref, vsl_ref, kwn_ref, vwn_ref, gate_ref, stage_k_ref, stage_v_ref):
    i = pl.program_id(1)
    x = x_ref[0]
    ms = jnp.mean(x * x, axis=-1, keepdims=True)
    h = (x * lax.rsqrt(ms + EPS) * nw_ref[...]).astype(jnp.bfloat16)
    bf = jnp.bfloat16
    lane = lax.broadcasted_iota(jnp.int32, (TM, LANES), 1)
    row = lax.broadcasted_iota(jnp.int32, (TM, LANES), 0)
    lo_half = lane < HEAD_DIM
    blk = i * (TM // MOBA_BLOCK) + lax.shift_right_logical(row, MOBA_BLOCK.bit_length() - 1)
    sblk = i * (TM // SEL_BLOCK) + lax.shift_right_logical(row, SEL_BLOCK.bit_length() - 1)

    def normed(col, width, scale=None):
        p = _nn(h, w_ref[:, col:col + width])
        outs = []
        for s in range(width // LANES):
            c = col + s * LANES
            y = _seg_norm(p[:, s * LANES:(s + 1) * LANES], lo_half, gain_ref[:, c:c + LANES])
            outs.append(y if scale is None else y * scale)
        return outs

    qa = normed(_C_QA, 512, QSCALE)
    for s in range(4):
        qa_ref[0, :, s * LANES:(s + 1) * LANES] = qa[s].astype(bf)
    ka = normed(_C_KA, 512)
    oh_hi = jnp.where(lane == HEAD_DIM + blk, 1.0, 0.0)
    oh_lo = jnp.where(lane == blk, 1.0, 0.0)
    for s in range(4):
        kae_ref[0, :, s * LANES:(s + 1) * LANES] = jnp.where(lo_half, ka[s], oh_hi).astype(bf)
        kao_ref[0, :, s * LANES:(s + 1) * LANES] = jnp.where(lo_half, oh_lo, ka[s]).astype(bf)
        for j in range(TM // MOBA_BLOCK):
            kmean_ref[0, j, :, s * LANES:(s + 1) * LANES] = jnp.broadcast_to(
                jnp.mean(ka[s][j * MOBA_BLOCK:(j + 1) * MOBA_BLOCK], axis=0, keepdims=True), (8, LANES))
    one_hi = jnp.where(lane == HEAD_DIM, 1.0, 0.0)
    one_lo = jnp.where(lane == 0, 1.0, 0.0)
    va = _nn(h, w_ref[:, _C_VA:_C_VA + 512])
    for s in range(4):
        vs = va[:, s * LANES:(s + 1) * LANES]
        vae_ref[0, :, s * LANES:(s + 1) * LANES] = jnp.where(lo_half, vs, one_hi).astype(bf)
        vao_ref[0, :, s * LANES:(s + 1) * LANES] = jnp.where(lo_half, one_lo, vs).astype(bf)
    qb = normed(_C_QB, 512, QSCALE)
    for s in range(4):
        qb_ref[0, :, s * LANES:(s + 1) * LANES] = qb[s].astype(bf)

    rest = _nn(h, w_ref[:, _C_KC:_C_GATE])
    gate_ref[0] = jnp.concatenate([_nn(h[0:TM // 2], wg_ref[...]), _nn(h[TM // 2:], wg_ref[...])], axis=0)

    def slab(c):
        return rest[:, c - _C_KC:c - _C_KC + LANES]

    def chunk_store(ref, stage_ref, y):
        stage_ref[...] = y
        lo = lax.broadcasted_iota(jnp.int32, (TM // CMP_STRIDE, LANES), 1) < HEAD_DIM
        for u in range(CMP_STRIDE // 2):
            t0 = stage_ref[pl.ds(2 * u, TM // CMP_STRIDE, stride=CMP_STRIDE), :]
            t1 = stage_ref[pl.ds(2 * u + 1, TM // CMP_STRIDE, stride=CMP_STRIDE), :]
            ref[0, 0, :, u * LANES:(u + 1) * LANES] = jnp.where(lo, t0, pltpu.roll(t1, HEAD_DIM, 1)).astype(bf)
            ref[0, 1, :, u * LANES:(u + 1) * LANES] = jnp.where(lo, pltpu.roll(t0, HEAD_DIM, 1), t1).astype(bf)

    chunk_store(kc_ref, stage_k_ref, slab(_C_KC))
    chunk_store(vc_ref, stage_v_ref, slab(_C_VC))

    def dup_store(ref, y):
        r = pltpu.roll(y, HEAD_DIM, 1)
        ref[0, 0] = jnp.where(lo_half, y, r).astype(bf)
        ref[0, 1] = jnp.where(lo_half, r, y).astype(bf)

    y = _seg_norm(slab(_C_KSL), lo_half, gain_ref[:, _C_KSL:_C_KSL + LANES])
    r = pltpu.roll(y, HEAD_DIM, 1)
    oh = jnp.where((lane & (HEAD_DIM - 1)) == sblk, 1.0, 0.0)
    kse_ref[0, 0] = jnp.where(lo_half, y, oh).astype(bf)
    kso_ref[0, 0] = jnp.where(lo_half, oh, r).astype(bf)
    kse_ref[0, 1] = jnp.where(lo_half, r, oh).astype(bf)
    kso_ref[0, 1] = jnp.where(lo_half, oh, y).astype(bf)
    def aug_store(ref, y):
        ref[0, 0] = jnp.where(lo_half, y, one_hi).astype(bf)
        ref[0, 1] = jnp.where(lo_half, pltpu.roll(y, HEAD_DIM, 1), one_hi).astype(bf)

    aug_store(vsl_ref, slab(_C_VSL))
    dup_store(kwn_ref, _seg_norm(slab(_C_KWN), lo_half, gain_ref[:, _C_KWN:_C_KWN + LANES]))
    aug_store(vwn_ref, slab(_C_VWN))


def _in_proj(x, norm_w, w1, wg, gain_row):
    B, S, _ = x.shape
    nt = S // TQ
    bf = jnp.bfloat16
    tok = lambda w, dt: jax.ShapeDtypeStruct((B, S, w), dt)
    grp = jax.ShapeDtypeStruct((B, NSA_GROUPS, S, LANES), bf)
    tok_spec = lambda w: pl.BlockSpec((1, TM, w), lambda b, i: (b, i, 0))
    grp_spec = pl.BlockSpec((1, NSA_GROUPS, TM, LANES), lambda b, i: (b, 0, i, 0))
    flat = CMP_STRIDE * HEAD_DIM
    chunk = jax.ShapeDtypeStruct((B, NSA_GROUPS, S // CMP_STRIDE, flat), bf)
    chunk_spec = pl.BlockSpec((1, NSA_GROUPS, TM // CMP_STRIDE, flat), lambda b, i: (b, 0, i, 0))
    full = lambda a: pl.BlockSpec(a.shape, lambda b, i: (0,) * a.ndim, pipeline_mode=pl.Buffered(1))
    nw = norm_w.reshape(1, D_MODEL)
    return pl.pallas_call(
        _in_proj_body,
        grid=(B, S // TM),
        in_specs=[tok_spec(D_MODEL), full(nw), full(w1), full(wg), full(gain_row)],
        out_specs=[tok_spec(512), tok_spec(512), tok_spec(512), tok_spec(512), tok_spec(512), tok_spec(512),
                   pl.BlockSpec((1, TM // MOBA_BLOCK, 8, 512), lambda b, i: (b, i, 0, 0)),
                   chunk_spec, chunk_spec,
                   grp_spec, grp_spec, grp_spec, grp_spec, grp_spec, tok_spec(LANES)],
        out_shape=[tok(512, bf), tok(512, bf), tok(512, bf), tok(512, bf), tok(512, bf), tok(512, bf),
                   jax.ShapeDtypeStruct((B, nt, 8, 512), jnp.float32),
                   chunk, chunk, grp, grp, grp, grp, grp, tok(LANES, jnp.float32)],
        scratch_shapes=[pltpu.VMEM((TM, LANES), jnp.float32)] * 2,
        compiler_params=pltpu.CompilerParams(dimension_semantics=("parallel", "parallel"),
                                             vmem_limit_bytes=VMEM_LIMIT),
        name="in_proj",
    )(x, nw, w1, wg, gain_row)


def _gelu_tanh(x):
    return 0.5 * x * (1.0 + jnp.tanh(math.sqrt(2.0 / math.pi) * (x + 0.044715 * (x * x * x))))


def _compress_body(xk_ref, xv_ref, w1k_ref, w1v_ref, pk_ref, pv_ref,
                   w2k_ref, w2v_ref, gk_ref, kcmp_ref, vcmp_ref):
    nchunk = xk_ref.shape[2]
    half = CMP_STRIDE * HEAD_DIM

    def branch(x_ref, w1_ref, pos_ref, w2_ref, gain):
        pos = jnp.broadcast_to(pos_ref[...], (8, CMP_LEN * HEAD_DIM))
        posw = (_nn(pos[:, 0:half], w1_ref[...])[0:1, 0:CMP_HIDDEN]
                + _nn(pos[:, half:], w1_ref[...])[0:1, CMP_HIDDEN:])
        outs = []
        for g in range(NSA_GROUPS):
            ab = _nn(x_ref[0, g], w1_ref[...])
            top, bot = ab[:, 0:CMP_HIDDEN], ab[:, CMP_HIDDEN:]
            hid = top + pltpu.roll(bot, nchunk - 1, 0) + posw
            y = _nn(_gelu_tanh(hid).astype(jnp.bfloat16), w2_ref[...])
            if gain is not None:
                ms = jnp.sum(y * y, axis=-1, keepdims=True) * (0.5 / HEAD_DIM)
                y = y * lax.rsqrt(ms + EPS) * gain
            else:
                lane = lax.broadcasted_iota(jnp.int32, y.shape, 1)
                y = jnp.where(lane < HEAD_DIM, y, jnp.where(lane == HEAD_DIM, 1.0, 0.0))
            outs.append(y.astype(jnp.bfloat16))
        return outs

    k0, k1 = branch(xk_ref, w1k_ref, pk_ref, w2k_ref, gk_ref[...])
    kcmp_ref[0, 0], kcmp_ref[0, 1] = k0, k1
    v0, v1 = branch(xv_ref, w1v_ref, pv_ref, w2v_ref, None)
    vcmp_ref[0, 0], vcmp_ref[0, 1] = v0, v1


def _compress(xk, xv, pos_k, w1_k, w2_k, pos_v, w1_v, w2_v, gain_cmp):
    B, _, nchunk, half = xk.shape
    bf = jnp.bfloat16
    halves = lambda w1: jnp.concatenate([w1[:half], w1[half:]], axis=1).astype(bf)
    args = [xk, xv, halves(w1_k), halves(w1_v),
            pos_k.reshape(1, -1).astype(bf), pos_v.reshape(1, -1).astype(bf),
            jnp.concatenate([w2_k, w2_k], axis=1).astype(bf), jnp.concatenate([w2_v, w2_v], axis=1).astype(bf),
            jnp.concatenate([gain_cmp, gain_cmp]).reshape(1, LANES)]
    x_spec = pl.BlockSpec((1, NSA_GROUPS, nchunk, half), lambda b: (b, 0, 0, 0))
    full = lambda a: pl.BlockSpec(a.shape, lambda b: (0,) * a.ndim)
    out = jax.ShapeDtypeStruct((B, NSA_GROUPS, nchunk, LANES), bf)
    o_spec = pl.BlockSpec((1, NSA_GROUPS, nchunk, LANES), lambda b: (b, 0, 0, 0))
    return pl.pallas_call(
        _compress_body,
        grid=(B,),
        in_specs=[x_spec, x_spec] + [full(a) for a in args[2:]],
        out_specs=[o_spec, o_spec],
        out_shape=[out, out],
        compiler_params=pltpu.CompilerParams(dimension_semantics=("parallel",), vmem_limit_bytes=VMEM_LIMIT),
        name="compress",
    )(*args)


def _softmax_step(s, v_top, v_bot, m_ref, acc_ref, first):
    blocks = [s[:, c * LANES:(c + 1) * LANES] for c in range(s.shape[1] // LANES)]
    bmax = blocks[0]
    for blk in blocks[1:]:
        bmax = jnp.maximum(bmax, blk)
    m_new = jnp.broadcast_to(jnp.max(bmax, axis=-1, keepdims=True), m_ref.shape)
    if not first:
        m_old = m_ref[...]
        m_new = jnp.maximum(m_old, m_new)
        alpha = jnp.exp2(m_old - m_new)
    pb = jnp.concatenate([jnp.exp2(blk - m_new) for blk in blocks], axis=1).astype(jnp.bfloat16)
    half = s.shape[0] // 2
    pv = jnp.concatenate([_nn(pb[0:half], v_top), _nn(pb[half:], v_bot)], axis=0)
    acc_ref[...] = pv if first else alpha * acc_ref[...] + pv
    m_ref[...] = m_new


def _finish(acc, ones_lane):
    lane = lax.broadcasted_iota(jnp.int32, acc.shape, 1)
    return acc / jnp.sum(jnp.where(lane == ones_lane, acc, 0.0), axis=-1, keepdims=True)


def _tile_kind(delta):
    return jnp.where(delta >= 0, jnp.minimum(delta, N_FAR), _K_MASKED)


def _chunk_kinds(qi, c):
    d0 = qi - 2 * c
    kind0 = jnp.minimum(d0, N_FAR)
    kind1 = jnp.where(d0 >= 1, jnp.minimum(d0 - 1, N_FAR), _K_MASKED)
    return kind0, kind1


def _moba_body(q_ref, ke_ref, ko_ref, ve_ref, vo_ref, kmean_ref, eye_ref, bias_ref, o_ref, m_ref, acc_ref):
    i = pl.program_id(2)
    nblk = kmean_ref.shape[1]
    lane = lax.broadcasted_iota(jnp.int32, (TR, LANES), 1)
    lo_half = lane < HEAD_DIM
    lane16 = lax.broadcasted_iota(jnp.int32, (nblk, LANES), 1)
    jrow = lax.broadcasted_iota(jnp.int32, (nblk, TR), 0)
    qblk = 2 * i + jnp.where(lax.broadcasted_iota(jnp.int32, (nblk, TR), 1) >= TQ, 1, 0)
    pad = jnp.zeros((HEAD_DIM - nblk, TR), jnp.float32)
    zero = jnp.zeros((TR, LANES), jnp.bfloat16)

    def augmented_q(pr):
        q2 = q_ref[0, :, pr * LANES:(pr + 1) * LANES]
        kmean2 = kmean_ref[0, :, pr * LANES:(pr + 1) * LANES]
        flags = []
        for hh in range(2):
            inhead16 = (lane16 < HEAD_DIM) if hh == 0 else (lane16 >= HEAD_DIM)
            km_hi, km_lo = _split(jnp.where(inhead16, kmean2, 0.0))
            sc = jnp.where(jrow < qblk, _nt(km_hi, q2) + _nt(km_lo, q2), -jnp.inf)
            flag = jnp.where(jrow == qblk, 0.0, 1.0)
            for _ in range(MOBA_TOPK):
                best = jnp.max(sc, axis=0, keepdims=True)
                first = jnp.min(jnp.where(sc == best, jrow, nblk), axis=0, keepdims=True)
                hit = jrow == first
                flag = jnp.where(hit, 0.0, flag)
                sc = jnp.where(hit, -jnp.inf, sc)
            flags.append(flag)
        placed = jnp.concatenate([flags[1], pad, flags[0], pad], axis=0).astype(jnp.bfloat16)
        maskpart = jnp.concatenate([_nt(eye_ref[...], placed[:, a * TQ:(a + 1) * TQ]) for a in range(TR // TQ)],
                                   axis=0) * NEG
        q2f = q2.astype(jnp.float32)
        return jnp.concatenate(
            [jnp.concatenate([jnp.where(lo_half, q2f, maskpart).astype(jnp.bfloat16), zero], axis=1),
             jnp.concatenate([zero, jnp.where(lo_half, maskpart, q2f).astype(jnp.bfloat16)], axis=1)], axis=0)

    qqs = [augmented_q(pr) for pr in range(MOBA_PAIRS)]

    def step(c, first):
        rows = pl.ds(pl.multiple_of(c * TK, TK), TK)
        d = 2 * (i - c)
        for pr in range(MOBA_PAIRS):
            cols = slice(pr * LANES, (pr + 1) * LANES)
            kk = jnp.concatenate([ke_ref[0, rows, cols], ko_ref[0, rows, cols]], axis=1)
            b = jnp.concatenate(
                [jnp.concatenate([bias_ref[2 * pr + hh, _tile_kind(d + a - t)] for t in range(2)], axis=1)
                 for hh in range(2) for a in range(2)], axis=0)
            _softmax_step(_nt(qqs[pr], kk) + b, ve_ref[0, rows, cols], vo_ref[0, rows, cols],
                          m_ref.at[pr], acc_ref.at[pr], first)

    step(i, True)

    def body(c, carry):
        step(c, False)
        return carry

    lax.fori_loop(0, i, body, 0)
    for pr in range(MOBA_PAIRS):
        o_ref[0, :, pr * LANES:(pr + 1) * LANES] = jnp.where(
            lo_half, _finish(acc_ref[pr, 0:TR], HEAD_DIM), _finish(acc_ref[pr, TR:2 * TR], 0))


def _eye(n):
    return jnp.asarray(np.eye(n, dtype=np.float32), jnp.bfloat16)


def _moba(qa, kae, kao, vae, vao, kmean, bias):
    B, S, _ = qa.shape
    nt = S // TQ
    width = MOBA_PAIRS * LANES
    kv_spec = pl.BlockSpec((1, S, width), lambda b, p, i: (b, 0, p))
    return pl.pallas_call(
        _moba_body,
        grid=(B, MOBA_HEADS // (2 * MOBA_PAIRS), S // TR),
        in_specs=[pl.BlockSpec((1, TR, width), lambda b, p, i: (b, i, p)),
                  kv_spec, kv_spec, kv_spec, kv_spec,
                  pl.BlockSpec((1, nt, width), lambda b, p, i: (b, 0, p)),
                  pl.BlockSpec((TQ, TQ), lambda b, p, i: (0, 0)),
                  pl.BlockSpec((2 * MOBA_PAIRS, _N_TILE_KINDS, TQ, TQ), lambda b, p, i: (p, 0, 0, 0),
                               pipeline_mode=pl.Buffered(1))],
        out_specs=pl.BlockSpec((1, TR, width), lambda b, p, i: (b, i, p)),
        out_shape=jax.ShapeDtypeStruct((B, S, MOBA_HEADS * HEAD_DIM), jnp.float32),
        scratch_shapes=[pltpu.VMEM((MOBA_PAIRS, 2 * TR, LANES), jnp.float32)] * 2,
        compiler_params=pltpu.CompilerParams(dimension_semantics=("parallel", "parallel", "arbitrary"),
                                             vmem_limit_bytes=VMEM_LIMIT),
        name="moba",
    )(qa, kae, kao, vae, vao, kmean, _eye(TQ), bias)


def _nsa_body(q_ref, kcmp_ref, vcmp_ref, kse_ref, kso_ref, vsl_ref, kwn_ref, vwn_ref, gate_ref, ov_ref, eye_ref,
              gexp_ref, bias_ref, o_ref, m_ref, acc_ref):
    i = pl.program_id(2)

    def to_hi(o):
        return pltpu.roll(o, HEAD_DIM, 1)

    ncmp = kcmp_ref.shape[2]
    nsel = ov_ref.shape[0]
    R = NSA_HPG * TR
    bf = jnp.bfloat16
    lane = lax.broadcasted_iota(jnp.int32, (TR, LANES), 1)
    lo_half = lane < HEAD_DIM
    qf = [q_ref[0, :, 0:LANES].astype(jnp.float32), q_ref[0, :, LANES:2 * LANES].astype(jnp.float32)]
    q4 = jnp.concatenate([jnp.where(lo_half if hh % 2 == 0 else ~lo_half, qf[hh // 2], 0.0).astype(bf)
                          for hh in range(NSA_HPG)], axis=0)

    tpos = i * TR + lax.broadcasted_iota(jnp.int32, (TR, ncmp), 0)
    cidx = lax.broadcasted_iota(jnp.int32, (TR, ncmp), 1)
    cbias1 = jnp.where((cidx * CMP_STRIDE + (CMP_LEN - 1) <= tpos) & (cidx < ncmp - 1), 0.0, NEG)
    lc = _nt(q4, kcmp_ref[0, 0]) + jnp.concatenate([cbias1] * NSA_HPG, axis=0)
    cblocks = [lc[:, c * LANES:(c + 1) * LANES] for c in range(ncmp // LANES)]
    cmax = cblocks[0]
    for blk in cblocks[1:]:
        cmax = jnp.maximum(cmax, blk)
    cm = jnp.maximum(jnp.broadcast_to(jnp.max(cmax, axis=-1, keepdims=True), (R, LANES)), 0.5 * NEG)
    ce = [jnp.exp2(blk - cm) for blk in cblocks]
    cden = ce[0]
    for eb in ce[1:]:
        cden = cden + eb
    cden = jnp.broadcast_to(jnp.sum(cden, axis=-1, keepdims=True), (R, LANES))
    crcp = 1.0 / jnp.where(cden > 0.0, cden, 1.0)
    pc = jnp.concatenate([eb * crcp for eb in ce], axis=1)
    pcb = pc.astype(bf)
    o_cmp = jnp.concatenate([_nn(pcb[0:R // 2], vcmp_ref[0, 0]), _nn(pcb[R // 2:], vcmp_ref[0, 0])], axis=0)
    psum = pc[0:TR] + pc[TR:2 * TR] + pc[2 * TR:3 * TR] + pc[3 * TR:4 * TR]
    p_hi, p_lo = _split(psum)
    ov = ov_ref[...]
    imp = _nt(ov, p_hi) + _nt(ov, p_lo)

    o_win = []
    for a in range(2):
        qt = 2 * i + a
        t0 = jnp.maximum(qt - (N_WIN_TILES - 1), 0)
        wstart = pl.multiple_of(t0 * TQ, TQ)
        qh = jnp.concatenate([q4[hh * TR + a * TQ:hh * TR + (a + 1) * TQ] for hh in range(NSA_HPG)], axis=0)
        kinds = []
        for r in range(N_WIN_TILES):
            dd = qt - (t0 + r)
            kinds.append(jnp.where(dd >= 0, _K_WIN + dd, _K_MASKED))
        bw = jnp.concatenate([jnp.concatenate([bias_ref[hh, k] for k in kinds], axis=1)
                              for hh in range(NSA_HPG)], axis=0)
        sw = _nt(qh, kwn_ref[0, 0, pl.ds(wstart, N_WIN_TILES * TQ), :]) + bw
        wm, wacc = m_ref.at[0:NSA_HPG * TQ], acc_ref.at[0:NSA_HPG * TQ]
        vw = vwn_ref[0, 0, pl.ds(wstart, N_WIN_TILES * TQ), :]
        _softmax_step(sw, vw, vw, wm, wacc, True)
        o_win.append(_finish(wacc[...], HEAD_DIM))

    jrow = lax.broadcasted_iota(jnp.int32, (nsel, TR), 0)
    qblk = lax.shift_right_logical(i * TR + lax.broadcasted_iota(jnp.int32, (nsel, TR), 1),
                                   SEL_BLOCK.bit_length() - 1)
    cand = (jrow >= 1) & (jrow < qblk)
    keys = jnp.where(cand, imp, -1.0)
    flag = jnp.where((jrow == 0) | (jrow == qblk), 0.0, 1.0)
    for _ in range(SEL_TOPN - 2):
        best = jnp.max(keys, axis=0, keepdims=True)
        first = jnp.min(jnp.where(keys == best, jrow, nsel), axis=0, keepdims=True)
        hit = jrow == first
        flag = jnp.where(hit, 0.0, flag)
        keys = jnp.where(hit, -1.0, keys)
    if nsel < HEAD_DIM:
        flag = jnp.concatenate([flag, jnp.zeros((HEAD_DIM - nsel, TR), jnp.float32)], axis=0)
    flag2 = jnp.concatenate([flag, flag], axis=0).astype(bf)
    maskpart = jnp.concatenate([_nt(eye_ref[...], flag2[:, a * TQ:(a + 1) * TQ]) for a in range(TR // TQ)],
                               axis=0) * NEG

    zero = jnp.zeros((2 * TR, LANES), bf)
    qa2 = [jnp.concatenate([jnp.where(lo_half if par == 0 else ~lo_half, qf[s], maskpart).astype(bf)
                            for s in range(2)], axis=0) for par in range(2)]
    qq = jnp.concatenate([jnp.concatenate([qa2[0], zero], axis=1),
                          jnp.concatenate([zero, qa2[1]], axis=1)], axis=0)
    sel_heads = (0, 2, 1, 3)

    def sel_step(c, first):
        start = pl.multiple_of(c * TK, TK)
        kk = jnp.concatenate([kse_ref[0, 0, pl.ds(start, TK), :], kso_ref[0, 0, pl.ds(start, TK), :]], axis=1)
        d = 2 * (i - c)
        b = jnp.concatenate(
            [jnp.concatenate([bias_ref[h, _tile_kind(d + a - t)] for t in range(2)], axis=1)
             for h in sel_heads for a in range(2)], axis=0)
        v = vsl_ref[0, 0, pl.ds(start, TK), :]
        _softmax_step(_nt(qq, kk) + b, v, v, m_ref, acc_ref, first)

    sel_step(i, True)

    def sel_body(c, carry):
        sel_step(c, False)
        return carry

    lax.fori_loop(0, i, sel_body, 0)
    o_all = _finish(acc_ref[...], HEAD_DIM)
    o_sel = [o_all[0:2 * TR], to_hi(o_all[2 * TR:4 * TR])]

    g = jax.nn.sigmoid(gate_ref[0])
    g_hi, g_lo = _split(g)
    gx = _nn(g_hi, gexp_ref[0]) + _nn(g_lo, gexp_ref[0])
    for s in range(2):
        h_even, h_odd = 2 * s, 2 * s + 1
        cmp2 = jnp.where(lo_half, o_cmp[h_even * TR:(h_even + 1) * TR], to_hi(o_cmp[h_odd * TR:(h_odd + 1) * TR]))
        sel2 = jnp.where(lo_half, o_sel[0][s * TR:(s + 1) * TR], o_sel[1][s * TR:(s + 1) * TR])
        win2 = jnp.where(lo_half,
                         jnp.concatenate([o_win[a][h_even * TQ:(h_even + 1) * TQ] for a in range(2)], axis=0),
                         to_hi(jnp.concatenate([o_win[a][h_odd * TQ:(h_odd + 1) * TQ] for a in range(2)], axis=0)))
        base = s * 3 * LANES
        o_ref[0, :, s * LANES:(s + 1) * LANES] = (gx[:, base:base + LANES] * cmp2
                                                  + gx[:, base + LANES:base + 2 * LANES] * sel2
                                                  + gx[:, base + 2 * LANES:base + 3 * LANES] * win2)


def _gate_expand():
    e = np.zeros((NSA_GROUPS, LANES, 2 * 3 * LANES), np.float32)
    for g in range(NSA_GROUPS):
        for s in range(2):
            for c in range(3):
                for half in range(2):
                    h = g * NSA_HPG + 2 * s + half
                    col = (s * 3 + c) * LANES + half * HEAD_DIM
                    e[g, 3 * h + c, col:col + HEAD_DIM] = 1.0
    return e


def _nsa(qb, kcmp, vcmp, kse, kso, vsl, kwn, vwn, gate, bias):
    B, S, _ = qb.shape
    nt = S // TQ
    ncmp = kcmp.shape[2]
    nsel = S // SEL_BLOCK
    bf = jnp.bfloat16
    cs = np.arange(ncmp) * CMP_STRIDE
    bs = np.arange(nsel) * SEL_BLOCK
    ov = ((cs[None, :] < bs[:, None] + SEL_BLOCK) & (cs[None, :] + CMP_LEN > bs[:, None])
          & (np.arange(ncmp)[None, :] < ncmp - 1)).astype(np.float32)
    grp = lambda n: pl.BlockSpec((1, 1, n, LANES), lambda b, g, i: (b, g, 0, 0))
    return pl.pallas_call(
        _nsa_body,
        grid=(B, NSA_GROUPS, S // TR),
        in_specs=[pl.BlockSpec((1, TR, 2 * LANES), lambda b, g, i: (b, i, g)),
                  grp(ncmp), grp(ncmp), grp(S), grp(S), grp(S), grp(S), grp(S),
                  pl.BlockSpec((1, TR, LANES), lambda b, g, i: (b, i, 0)),
                  pl.BlockSpec((nsel, ncmp), lambda b, g, i: (0, 0)),
                  pl.BlockSpec((TQ, TQ), lambda b, g, i: (0, 0)),
                  pl.BlockSpec((1, LANES, 6 * LANES), lambda b, g, i: (g, 0, 0)),
                  pl.BlockSpec((NSA_HPG, _N_TILE_KINDS, TQ, TQ), lambda b, g, i: (2 + g, 0, 0, 0),
                               pipeline_mode=pl.Buffered(1))],
        out_specs=pl.BlockSpec((1, TR, 2 * LANES), lambda b, g, i: (b, i, g)),
        out_shape=jax.ShapeDtypeStruct((B, S, NSA_HEADS * HEAD_DIM), jnp.float32),
        scratch_shapes=[pltpu.VMEM((NSA_HPG * TR, LANES), jnp.float32)] * 2,
        compiler_params=pltpu.CompilerParams(dimension_semantics=("parallel", "parallel", "arbitrary"),
                                             vmem_limit_bytes=VMEM_LIMIT),
        name="nsa",
    )(qb, kcmp, vcmp, kse, kso, vsl, kwn, vwn, gate, jnp.asarray(ov, bf), _eye(TQ),
      jnp.asarray(_gate_expand(), bf), bias)


def _out_proj_body(x_ref, nw_ref, oa_ref, ob_ref, wz_ref, wzb_ref, wgm_ref, wa_ref, wb_ref, wo_ref, out_ref):
    x = x_ref[0]
    ms = jnp.mean(x * x, axis=-1, keepdims=True)
    h = (x * lax.rsqrt(ms + EPS) * nw_ref[...]).astype(jnp.bfloat16)
    ya = (oa_ref[0] * jax.nn.silu(_nn(h, wz_ref[:, _C_ZA:_C_ZA + 512]))).astype(jnp.bfloat16)
    yb = (ob_ref[0] * jax.nn.silu(_nn(h, wzb_ref[...]))).astype(jnp.bfloat16)
    gm = jax.nn.sigmoid(_nn(h, wgm_ref[...]))
    merged = gm[:, 0:D_MODEL] * _nn(ya, wa_ref[...]) + gm[:, D_MODEL:] * _nn(yb, wb_ref[...])
    out_ref[0] = x + _nn(merged.astype(jnp.bfloat16), wo_ref[...])


def _out_proj(x, norm_w, oa, ob, wz, wzb, wgm, wa, wb, wo):
    B, S, _ = x.shape
    nw = norm_w.reshape(1, D_MODEL)
    tok = lambda w: pl.BlockSpec((1, TM, w), lambda b, i: (b, i, 0))
    full = lambda a: pl.BlockSpec(a.shape, lambda b, i: (0,) * a.ndim, pipeline_mode=pl.Buffered(1))
    return pl.pallas_call(
        _out_proj_body,
        grid=(B, S // TM),
        in_specs=[tok(D_MODEL), full(nw), tok(512), tok(512), full(wz), full(wzb), full(wgm),
                  full(wa), full(wb), full(wo)],
        out_specs=tok(D_MODEL),
        out_shape=jax.ShapeDtypeStruct((B, S, D_MODEL), jnp.float32),
        compiler_params=pltpu.CompilerParams(dimension_semantics=("parallel", "parallel"),
                                             vmem_limit_bytes=VMEM_LIMIT),
        name="out_proj",
    )(x, nw, oa, ob, wz, wzb, wgm, wa, wb, wo)


def _pack_in_weights(w_in, q_norm_a, k_norm_a, q_norm_b, k_norm_sel, k_norm_win):
    bf = jnp.bfloat16
    ng = 3 * NSA_HEADS
    cols = lambda a, b: lax.slice_in_dim(w_in, a, b, axis=2).astype(bf).reshape(D_MODEL, b - a)
    z_b = _C_GATE + ng
    w = (cols(0, _C_GATE),
         jnp.pad(cols(_C_GATE, z_b), ((0, 0), (0, LANES - ng))),
         cols(z_b, z_b + NSA_HEADS * HEAD_DIM),
         cols(z_b + NSA_HEADS * HEAD_DIM, w_in.shape[2]))
    ones = lambda n: jnp.ones((n,), jnp.float32)
    gain = jnp.concatenate([jnp.tile(q_norm_a, MOBA_HEADS), jnp.tile(k_norm_a, MOBA_HEADS), ones(1024),
                            jnp.tile(q_norm_b, NSA_HEADS), ones(256), jnp.tile(k_norm_sel, NSA_GROUPS), ones(128),
                            jnp.tile(k_norm_win, NSA_GROUPS), ones(256)]).reshape(1, _N1)
    return w, gain


def kernel(x, norm_w, w_in, q_norm_a, k_norm_a, q_norm_b, k_norm_cmp, k_norm_sel, k_norm_win, cmp_pos_k, cmp_w1_k, cmp_w2_k, cmp_pos_v, cmp_w1_v, cmp_w2_v, rel_bias, w_branch_a, w_branch_b, w_out):
    bf = jnp.bfloat16
    (w_main, w_gate, w_zb, w_gm), gain = _pack_in_weights(w_in, q_norm_a[0], k_norm_a[0], q_norm_b[0],
                                                           k_norm_sel[0], k_norm_win[0])
    bias = _bias_tiles(rel_bias)
    qa, kae, kao, vae, vao, qb, kmean, kc, vc, kse, kso, vsl, kwn, vwn, gate = _in_proj(x, norm_w[0], w_main, w_gate,
                                                                                        gain)
    kcmp, vcmp = _compress(kc, vc, cmp_pos_k[0], cmp_w1_k[0], cmp_w2_k[0],
                           cmp_pos_v[0], cmp_w1_v[0], cmp_w2_v[0], k_norm_cmp[0])
    oa = _moba(qa, kae, kao, vae, vao, kmean[:, :, 0, :], bias)
    ob = _nsa(qb, kcmp, vcmp, kse, kso, vsl, kwn, vwn, gate, bias)
    return _out_proj(x, norm_w[0], oa, ob, w_main, w_zb, w_gm, w_branch_a[0].astype(bf), w_branch_b[0].astype(bf),
                     w_out[0].astype(bf))
```

```python
import math

import jax
import jax.numpy as jnp
import numpy as np
from jax import lax
from jax.experimental import pallas as pl
from jax.experimental.pallas import tpu as pltpu

D_MODEL = 1024
HEAD_DIM = 64
MOBA_HEADS = 8
NSA_HEADS = 8
NSA_GROUPS = 2
NSA_HPG = NSA_HEADS // NSA_GROUPS
MOBA_BLOCK = 256
MOBA_TOPK = 3
CMP_LEN = 32
CMP_STRIDE = 16
CMP_HIDDEN = 256
SEL_BLOCK = 64
SEL_TOPN = 16
WINDOW = 512
NUM_BUCKETS = 32
MAX_DISTANCE = 1024
EPS = 1e-6
NEG = -1e30

LANES = 128
TQ = 256
TK = 2 * TQ
TR = 2 * TQ
TM = 2 * TQ
MOBA_PAIRS = 2
N_FAR = 5
N_WIN_TILES = WINDOW // TQ + 1
VMEM_LIMIT = 56 * 1024 * 1024
LOG2E = math.log2(math.e)
QSCALE = LOG2E / math.sqrt(HEAD_DIM)

_K_WIN = N_FAR + 1
_K_MASKED = _K_WIN + N_WIN_TILES
_N_TILE_KINDS = _K_MASKED + 1

_C_QA, _C_KA, _C_VA, _C_ZA, _C_QB = 0, 512, 1024, 1536, 2048
_C_KC, _C_VC, _C_KSL, _C_VSL, _C_KWN, _C_VWN, _C_GATE = 2560, 2688, 2816, 2944, 3072, 3200, 3328
_N1 = _C_GATE + LANES


def _nt(a, b):
    return lax.dot_general(a, b, (((1,), (1,)), ((), ())), preferred_element_type=jnp.float32)


def _nn(a, b):
    return jnp.dot(a, b, preferred_element_type=jnp.float32)


def _split(x):
    hi = x.astype(jnp.bfloat16)
    lo = (x - hi.astype(jnp.float32)).astype(jnp.bfloat16)
    return hi, lo


def _bucket_table(n):
    d = np.arange(n)
    nf = np.maximum(d, NUM_BUCKETS // 2).astype(np.float64)
    large = NUM_BUCKETS // 2 + (np.log(nf / (NUM_BUCKETS // 2)) / math.log(MAX_DISTANCE / (NUM_BUCKETS // 2))
                                * (NUM_BUCKETS - NUM_BUCKETS // 2)).astype(np.int64)
    return np.where(d < NUM_BUCKETS // 2, d, np.minimum(large, NUM_BUCKETS - 1))


def _bias_onehot():
    oh = np.zeros((40, _N_TILE_KINDS * 2 * TQ), np.float32)
    buckets = _bucket_table((N_FAR + 2) * TQ + 1)
    for t in range(_N_TILE_KINDS):
        window = _K_WIN <= t < _K_MASKED
        delta = t - _K_WIN if window else t
        for u in range(2 * TQ):
            d = TQ * delta + TQ - u
            masked = t == _K_MASKED or d < 0 or (window and d >= WINDOW)
            oh[NUM_BUCKETS if masked else buckets[d], t * 2 * TQ + u] = 1.0
    return oh


def _bias_body(relt_ref, oh_ref, out_ref):
    g = jnp.dot(jnp.broadcast_to(relt_ref[0], (8, 40)), oh_ref[...], preferred_element_type=jnp.float32,
                precision=lax.Precision.HIGHEST)
    for t in range(_N_TILE_KINDS):
        full = jnp.broadcast_to(g[0:1, t * 2 * TQ:(t + 1) * 2 * TQ], (TQ, 2 * TQ))
        rolled = pltpu.roll(full, TQ, 1, stride=1, stride_axis=0)
        out_ref[0, t] = rolled[:, :TQ]


def _bias_tiles(rel_bias):
    nh = MOBA_HEADS + NSA_HEADS
    relt = jnp.concatenate([rel_bias.T.astype(jnp.float32) * LOG2E, jnp.full((nh, 1), NEG, jnp.float32),
                            jnp.zeros((nh, 7), jnp.float32)], axis=1).reshape(nh, 1, 40)
    oh = jnp.asarray(_bias_onehot())
    return pl.pallas_call(
        _bias_body,
        grid=(nh,),
        in_specs=[pl.BlockSpec((1, 1, 40), lambda h: (h, 0, 0)), pl.BlockSpec(oh.shape, lambda h: (0, 0))],
        out_specs=pl.BlockSpec((1, _N_TILE_KINDS, TQ, TQ), lambda h: (h, 0, 0, 0)),
        out_shape=jax.ShapeDtypeStruct((nh, _N_TILE_KINDS, TQ, TQ), jnp.float32),
        compiler_params=pltpu.CompilerParams(dimension_semantics=("parallel",), vmem_limit_bytes=VMEM_LIMIT),
        name="bias_tiles",
    )(relt, oh)


def _seg_norm(x, lo_half, gain):
    x2 = x * x
    ss_lo = jnp.sum(jnp.where(lo_half, x2, 0.0), axis=-1, keepdims=True)
    ss_hi = jnp.sum(jnp.where(lo_half, 0.0, x2), axis=-1, keepdims=True)
    ss = jnp.where(lo_half, ss_lo, ss_hi)
    return x * lax.rsqrt(ss * (1.0 / HEAD_DIM) + EPS) * gain


def _in_proj_body(x_ref, nw_ref, w_ref, wg_ref, gain_ref,
                  qa_ref, kae_ref, kao_ref, vae_ref, vao_ref, qb_ref, kmean_ref, kc_ref, vc_ref,
                  kse_ref, kso_ref, vsl_ref, kwn_ref, vwn_ref, gate_ref, stage_k_ref, stage_v_ref):
    i = pl.program_id(1)
    x = x_ref[0]
    ms = jnp.mean(x * x, axis=-1, keepdims=True)
    h = (x * lax.rsqrt(ms + EPS) * nw_ref[...]).astype(jnp.bfloat16)
    bf = jnp.bfloat16
    lane = lax.broadcasted_iota(jnp.int32, (TM, LANES), 1)
    row = lax.broadcasted_iota(jnp.int32, (TM, LANES), 0)
    lo_half = lane < HEAD_DIM
    blk = i * (TM // MOBA_BLOCK) + lax.shift_right_logical(row, MOBA_BLOCK.bit_length() - 1)
    sblk = i * (TM // SEL_BLOCK) + lax.shift_right_logical(row, SEL_BLOCK.bit_length() - 1)

    def normed(col, width, scale=None):
        p = _nn(h, w_ref[:, col:col + width])
        outs = []
        for s in range(width // LANES):
            c = col + s * LANES
            y = _seg_norm(p[:, s * LANES:(s + 1) * LANES], lo_half, gain_ref[:, c:c + LANES])
            outs.append(y if scale is None else y * scale)
        return outs

    qa = normed(_C_QA, 512, QSCALE)
    for s in range(4):
        qa_ref[0, :, s * LANES:(s + 1) * LANES] = qa[s].astype(bf)
    ka = normed(_C_KA, 512)
    oh_hi = jnp.where(lane == HEAD_DIM + blk, 1.0, 0.0)
    oh_lo = jnp.where(lane == blk, 1.0, 0.0)
    for s in range(4):
        kae_ref[0, :, s * LANES:(s + 1) * LANES] = jnp.where(lo_half, ka[s], oh_hi).astype(bf)
        kao_ref[0, :, s * LANES:(s + 1) * LANES] = jnp.where(lo_half, oh_lo, ka[s]).astype(bf)
        for j in range(TM // MOBA_BLOCK):
            kmean_ref[0, j, :, s * LANES:(s + 1) * LANES] = jnp.broadcast_to(
                jnp.mean(ka[s][j * MOBA_BLOCK:(j + 1) * MOBA_BLOCK], axis=0, keepdims=True), (8, LANES))
    one_hi = jnp.where(lane == HEAD_DIM, 1.0, 0.0)
    one_lo = jnp.where(lane == 0, 1.0, 0.0)
    va = _nn(h, w_ref[:, _C_VA:_C_VA + 512])
    for s in range(4):
        vs = va[:, s * LANES:(s + 1) * LANES]
        vae_ref[0, :, s * LANES:(s + 1) * LANES] = jnp.where(lo_half, vs, one_hi).astype(bf)
        vao_ref[0, :, s * LANES:(s + 1) * LANES] = jnp.where(lo_half, one_lo, vs).astype(bf)
    qb = normed(_C_QB, 512, QSCALE)
    for s in range(4):
        qb_ref[0, :, s * LANES:(s + 1) * LANES] = qb[s].astype(bf)

    rest = _nn(h, w_ref[:, _C_KC:_C_GATE])
    gate_ref[0] = jnp.concatenate([_nn(h[0:TM // 2], wg_ref[...]), _nn(h[TM // 2:], wg_ref[...])], axis=0)

    def slab(c):
        return rest[:, c - _C_KC:c - _C_KC + LANES]

    def chunk_store(ref, stage_ref, y):
        stage_ref[...] = y
        lo = lax.broadcasted_iota(jnp.int32, (TM // CMP_STRIDE, LANES), 1) < HEAD_DIM
        for u in range(CMP_STRIDE // 2):
            t0 = stage_ref[pl.ds(2 * u, TM // CMP_STRIDE, stride=CMP_STRIDE), :]
            t1 = stage_ref[pl.ds(2 * u + 1, TM // CMP_STRIDE, stride=CMP_STRIDE), :]
            ref[0, 0, :, u * LANES:(u + 1) * LANES] = jnp.where(lo, t0, pltpu.roll(t1, HEAD_DIM, 1)).astype(bf)
            ref[0, 1, :, u * LANES:(u + 1) * LANES] = jnp.where(lo, pltpu.roll(t0, HEAD_DIM, 1), t1).astype(bf)

    chunk_store(kc_ref, stage_k_ref, slab(_C_KC))
    chunk_store(vc_ref, stage_v_ref, slab(_C_VC))

    def dup_store(ref, y):
        r = pltpu.roll(y, HEAD_DIM, 1)
        ref[0, 0] = jnp.where(lo_half, y, r).astype(bf)
        ref[0, 1] = jnp.where(lo_half, r, y).astype(bf)

    y = _seg_norm(slab(_C_KSL), lo_half, gain_ref[:, _C_KSL:_C_KSL + LANES])
    r = pltpu.roll(y, HEAD_DIM, 1)
    oh = jnp.where((lane & (HEAD_DIM - 1)) == sblk, 1.0, 0.0)
    kse_ref[0, 0] = jnp.where(lo_half, y, oh).astype(bf)
    kso_ref[0, 0] = jnp.where(lo_half, oh, r).astype(bf)
    kse_ref[0, 1] = jnp.where(lo_half, r, oh).astype(bf)
    kso_ref[0, 1] = jnp.where(lo_half, oh, y).astype(bf)
    def aug_store(ref, y):
        ref[0, 0] = jnp.where(lo_half, y, one_hi).astype(bf)
        ref[0, 1] = jnp.where(lo_half, pltpu.roll(y, HEAD_DIM, 1), one_hi).astype(bf)

    aug_store(vsl_ref, slab(_C_VSL))
    dup_store(kwn_ref, _seg_norm(slab(_C_KWN), lo_half, gain_ref[:, _C_KWN:_C_KWN + LANES]))
    aug_store(vwn_ref, slab(_C_VWN))


def _in_proj(x, norm_w, w1, wg, gain_row):
    B, S, _ = x.shape
    nt = S // TQ
    bf = jnp.bfloat16
    tok = lambda w, dt: jax.ShapeDtypeStruct((B, S, w), dt)
    grp = jax.ShapeDtypeStruct((B, NSA_GROUPS, S, LANES), bf)
    tok_spec = lambda w: pl.BlockSpec((1, TM, w), lambda b, i: (b, i, 0))
    grp_spec = pl.BlockSpec((1, NSA_GROUPS, TM, LANES), lambda b, i: (b, 0, i, 0))
    flat = CMP_STRIDE * HEAD_DIM
    chunk = jax.ShapeDtypeStruct((B, NSA_GROUPS, S // CMP_STRIDE, flat), bf)
    chunk_spec = pl.BlockSpec((1, NSA_GROUPS, TM // CMP_STRIDE, flat), lambda b, i: (b, 0, i, 0))
    full = lambda a: pl.BlockSpec(a.shape, lambda b, i: (0,) * a.ndim, pipeline_mode=pl.Buffered(1))
    nw = norm_w.reshape(1, D_MODEL)
    return pl.pallas_call(
        _in_proj_body,
        grid=(B, S // TM),
        in_specs=[tok_spec(D_MODEL), full(nw), full(w1), full(wg), full(gain_row)],
        out_specs=[tok_spec(512), tok_spec(512), tok_spec(512), tok_spec(512), tok_spec(512), tok_spec(512),
                   pl.BlockSpec((1, TM // MOBA_BLOCK, 8, 512), lambda b, i: (b, i, 0, 0)),
                   chunk_spec, chunk_spec,
                   grp_spec, grp_spec, grp_spec, grp_spec, grp_spec, tok_spec(LANES)],
        out_shape=[tok(512, bf), tok(512, bf), tok(512, bf), tok(512, bf), tok(512, bf), tok(512, bf),
                   jax.ShapeDtypeStruct((B, nt, 8, 512), jnp.float32),
                   chunk, chunk, grp, grp, grp, grp, grp, tok(LANES, jnp.float32)],
        scratch_shapes=[pltpu.VMEM((TM, LANES), jnp.float32)] * 2,
        compiler_params=pltpu.CompilerParams(dimension_semantics=("parallel", "parallel"),
                                             vmem_limit_bytes=VMEM_LIMIT),
        name="in_proj",
    )(x, nw, w1, wg, gain_row)


def _gelu_tanh(x):
    return 0.5 * x * (1.0 + jnp.tanh(math.sqrt(2.0 / math.pi) * (x + 0.044715 * (x * x * x))))


def _compress_body(xk_ref, xv_ref, w1k_ref, w1v_ref, pk_ref, pv_ref,
                   w2k_ref, w2v_ref, gk_ref, kcmp_ref, vcmp_ref):
    nchunk = xk_ref.shape[2]
    half = CMP_STRIDE * HEAD_DIM

    def branch(x_ref, w1_ref, pos_ref, w2_ref, gain):
        pos = jnp.broadcast_to(pos_ref[...], (8, CMP_LEN * HEAD_DIM))
        posw = (_nn(pos[:, 0:half], w1_ref[...])[0:1, 0:CMP_HIDDEN]
                + _nn(pos[:, half:], w1_ref[...])[0:1, CMP_HIDDEN:])
        outs = []
        for g in range(NSA_GROUPS):
            ab = _nn(x_ref[0, g], w1_ref[...])
            top, bot = ab[:, 0:CMP_HIDDEN], ab[:, CMP_HIDDEN:]
            hid = top + pltpu.roll(bot, nchunk - 1, 0) + posw
            y = _nn(_gelu_tanh(hid).astype(jnp.bfloat16), w2_ref[...])
            if gain is not None:
                ms = jnp.sum(y * y, axis=-1, keepdims=True) * (0.5 / HEAD_DIM)
                y = y * lax.rsqrt(ms + EPS) * gain
            else:
                lane = lax.broadcasted_iota(jnp.int32, y.shape, 1)
                y = jnp.where(lane < HEAD_DIM, y, jnp.where(lane == HEAD_DIM, 1.0, 0.0))
            outs.append(y.astype(jnp.bfloat16))
        return outs

    k0, k1 = branch(xk_ref, w1k_ref, pk_ref, w2k_ref, gk_ref[...])
    kcmp_ref[0, 0], kcmp_ref[0, 1] = k0, k1
    v0, v1 = branch(xv_ref, w1v_ref, pv_ref, w2v_ref, None)
    vcmp_ref[0, 0], vcmp_ref[0, 1] = v0, v1


def _compress(xk, xv, pos_k, w1_k, w2_k, pos_v, w1_v, w2_v, gain_cmp):
    B, _, nchunk, half = xk.shape
    bf = jnp.bfloat16
    halves = lambda w1: jnp.concatenate([w1[:half], w1[half:]], axis=1).astype(bf)
    args = [xk, xv, halves(w1_k), halves(w1_v),
            pos_k.reshape(1, -1).astype(bf), pos_v.reshape(1, -1).astype(bf),
            jnp.concatenate([w2_k, w2_k], axis=1).astype(bf), jnp.concatenate([w2_v, w2_v], axis=1).astype(bf),
            jnp.concatenate([gain_cmp, gain_cmp]).reshape(1, LANES)]
    x_spec = pl.BlockSpec((1, NSA_GROUPS, nchunk, half), lambda b: (b, 0, 0, 0))
    full = lambda a: pl.BlockSpec(a.shape, lambda b: (0,) * a.ndim)
    out = jax.ShapeDtypeStruct((B, NSA_GROUPS, nchunk, LANES), bf)
    o_spec = pl.BlockSpec((1, NSA_GROUPS, nchunk, LANES), lambda b: (b, 0, 0, 0))
    return pl.pallas_call(
        _compress_body,
        grid=(B,),
        in_specs=[x_spec, x_spec] + [full(a) for a in args[2:]],
        out_specs=[o_spec, o_spec],
        out_shape=[out, out],
        compiler_params=pltpu.CompilerParams(dimension_semantics=("parallel",), vmem_limit_bytes=VMEM_LIMIT),
        name="compress",
    )(*args)


def _softmax_step(s, v_top, v_bot, m_ref, acc_ref, first):
    blocks = [s[:, c * LANES:(c + 1) * LANES] for c in range(s.shape[1] // LANES)]
    bmax = blocks[0]
    for blk in blocks[1:]:
        bmax = jnp.maximum(bmax, blk)
    m_new = jnp.broadcast_to(jnp.max(bmax, axis=-1, keepdims=True), m_ref.shape)
    if not first:
        m_old = m_ref[...]
        m_new = jnp.maximum(m_old, m_new)
        alpha = jnp.exp2(m_old - m_new)
    pb = jnp.concatenate([jnp.exp2(blk - m_new) for blk in blocks], axis=1).astype(jnp.bfloat16)
    half = s.shape[0] // 2
    pv = jnp.concatenate([_nn(pb[0:half], v_top), _nn(pb[half:], v_bot)], axis=0)
    acc_ref[...] = pv if first else alpha * acc_ref[...] + pv
    m_ref[...] = m_new


def _finish(acc, ones_lane):
    lane = lax.broadcasted_iota(jnp.int32, acc.shape, 1)
    return acc / jnp.sum(jnp.where(lane == ones_lane, acc, 0.0), axis=-1, keepdims=True)


def _tile_kind(delta):
    return jnp.where(delta >= 0, jnp.minimum(delta, N_FAR), _K_MASKED)


def _chunk_kinds(qi, c):
    d0 = qi - 2 * c
    kind0 = jnp.minimum(d0, N_FAR)
    kind1 = jnp.where(d0 >= 1, jnp.minimum(d0 - 1, N_FAR), _K_MASKED)
    return kind0, kind1


def _moba_body(q_ref, ke_ref, ko_ref, ve_ref, vo_ref, kmean_ref, eye_ref, bias_ref, o_ref, m_ref, acc_ref):
    i = pl.program_id(2)
    nblk = kmean_ref.shape[1]
    lane = lax.broadcasted_iota(jnp.int32, (TR, LANES), 1)
    lo_half = lane < HEAD_DIM
    lane16 = lax.broadcasted_iota(jnp.int32, (nblk, LANES), 1)
    jrow = lax.broadcasted_iota(jnp.int32, (nblk, TR), 0)
    qblk = 2 * i + jnp.where(lax.broadcasted_iota(jnp.int32, (nblk, TR), 1) >= TQ, 1, 0)
    pad = jnp.zeros((HEAD_DIM - nblk, TR), jnp.float32)
    zero = jnp.zeros((TR, LANES), jnp.bfloat16)

    def augmented_q(pr):
        q2 = q_ref[0, :, pr * LANES:(pr + 1) * LANES]
        kmean2 = kmean_ref[0, :, pr * LANES:(pr + 1) * LANES]
        flags = []
        for hh in range(2):
            inhead16 = (lane16 < HEAD_DIM) if hh == 0 else (lane16 >= HEAD_DIM)
            km_hi, km_lo = _split(jnp.where(inhead16, kmean2, 0.0))
            sc = jnp.where(jrow < qblk, _nt(km_hi, q2) + _nt(km_lo, q2), -jnp.inf)
            flag = jnp.where(jrow == qblk, 0.0, 1.0)
            for _ in range(MOBA_TOPK):
                best = jnp.max(sc, axis=0, keepdims=True)
                first = jnp.min(jnp.where(sc == best, jrow, nblk), axis=0, keepdims=True)
                hit = jrow == first
                flag = jnp.where(hit, 0.0, flag)
                sc = jnp.where(hit, -jnp.inf, sc)
            flags.append(flag)
        placed = jnp.concatenate([flags[1], pad, flags[0], pad], axis=0).astype(jnp.bfloat16)
        maskpart = jnp.concatenate([_nt(eye_ref[...], placed[:, a * TQ:(a + 1) * TQ]) for a in range(TR // TQ)],
                                   axis=0) * NEG
        q2f = q2.astype(jnp.float32)
        return jnp.concatenate(
            [jnp.concatenate([jnp.where(lo_half, q2f, maskpart).astype(jnp.bfloat16), zero], axis=1),
             jnp.concatenate([zero, jnp.where(lo_half, maskpart, q2f).astype(jnp.bfloat16)], axis=1)], axis=0)

    qqs = [augmented_q(pr) for pr in range(MOBA_PAIRS)]

    def step(c, first):
        rows = pl.ds(pl.multiple_of(c * TK, TK), TK)
        d = 2 * (i - c)
        for pr in range(MOBA_PAIRS):
            cols = slice(pr * LANES, (pr + 1) * LANES)
            kk = jnp.concatenate([ke_ref[0, rows, cols], ko_ref[0, rows, cols]], axis=1)
            b = jnp.concatenate(
                [jnp.concatenate([bias_ref[2 * pr + hh, _tile_kind(d + a - t)] for t in range(2)], axis=1)
                 for hh in range(2) for a in range(2)], axis=0)
            _softmax_step(_nt(qqs[pr], kk) + b, ve_ref[0, rows, cols], vo_ref[0, rows, cols],
                          m_ref.at[pr], acc_ref.at[pr], first)

    step(i, True)

    def body(c, carry):
        step(c, False)
        return carry

    lax.fori_loop(0, i, body, 0)
    for pr in range(MOBA_PAIRS):
        o_ref[0, :, pr * LANES:(pr + 1) * LANES] = jnp.where(
            lo_half, _finish(acc_ref[pr, 0:TR], HEAD_DIM), _finish(acc_ref[pr, TR:2 * TR], 0))


def _eye(n):
    return jnp.asarray(np.eye(n, dtype=np.float32), jnp.bfloat16)


def _moba(qa, kae, kao, vae, vao, kmean, bias):
    B, S, _ = qa.shape
    nt = S // TQ
    width = MOBA_PAIRS * LANES
    kv_spec = pl.BlockSpec((1, S, width), lambda b, p, i: (b, 0, p))
    return pl.pallas_call(
        _moba_body,
        grid=(B, MOBA_HEADS // (2 * MOBA_PAIRS), S // TR),
        in_specs=[pl.BlockSpec((1, TR, width), lambda b, p, i: (b, i, p)),
                  kv_spec, kv_spec, kv_spec, kv_spec,
                  pl.BlockSpec((1, nt, width), lambda b, p, i: (b, 0, p)),
                  pl.BlockSpec((TQ, TQ), lambda b, p, i: (0, 0)),
                  pl.BlockSpec((2 * MOBA_PAIRS, _N_TILE_KINDS, TQ, TQ), lambda b, p, i: (p, 0, 0, 0),
                               pipeline_mode=pl.Buffered(1))],
        out_specs=pl.BlockSpec((1, TR, width), lambda b, p, i: (b, i, p)),
        out_shape=jax.ShapeDtypeStruct((B, S, MOBA_HEADS * HEAD_DIM), jnp.float32),
        scratch_shapes=[pltpu.VMEM((MOBA_PAIRS, 2 * TR, LANES), jnp.float32)] * 2,
        compiler_params=pltpu.CompilerParams(dimension_semantics=("parallel", "parallel", "arbitrary"),
                                             vmem_limit_bytes=VMEM_LIMIT),
        name="moba",
    )(qa, kae, kao, vae, vao, kmean, _eye(TQ), bias)


def _nsa_body(q_ref, kcmp_ref, vcmp_ref, kse_ref, kso_ref, vsl_ref, kwn_ref, vwn_ref, gate_ref, ov_ref, eye_ref,
              gexp_ref, bias_ref, o_ref, m_ref, acc_ref):
    i = pl.program_id(1)

    def to_hi(o):
        return pltpu.roll(o, HEAD_DIM, 1)

    ncmp = kcmp_ref.shape[2]
    nsel = ov_ref.shape[0]
    R = NSA_HPG * TR
    bf = jnp.bfloat16
    lane = lax.broadcasted_iota(jnp.int32, (TR, LANES), 1)
    lo_half = lane < HEAD_DIM
    sel_heads = (0, 2, 1, 3)

    def prepare(g):
        qf = [q_ref[0, :, (2 * g + s) * LANES:(2 * g + s + 1) * LANES].astype(jnp.float32) for s in range(2)]
        q4 = jnp.concatenate([jnp.where(lo_half if hh % 2 == 0 else ~lo_half, qf[hh // 2], 0.0).astype(bf)
                              for hh in range(NSA_HPG)], axis=0)

        tpos = i * TR + lax.broadcasted_iota(jnp.int32, (TR, ncmp), 0)
        cidx = lax.broadcasted_iota(jnp.int32, (TR, ncmp), 1)
        cbias1 = jnp.where((cidx * CMP_STRIDE + (CMP_LEN - 1) <= tpos) & (cidx < ncmp - 1), 0.0, NEG)
        lc = _nt(q4, kcmp_ref[0, g]) + jnp.concatenate([cbias1] * NSA_HPG, axis=0)
        cblocks = [lc[:, c * LANES:(c + 1) * LANES] for c in range(ncmp // LANES)]
        cmax = cblocks[0]
        for blk in cblocks[1:]:
            cmax = jnp.maximum(cmax, blk)
        cm = jnp.maximum(jnp.broadcast_to(jnp.max(cmax, axis=-1, keepdims=True), (R, LANES)), 0.5 * NEG)
        ce = [jnp.exp2(blk - cm) for blk in cblocks]
        cden = ce[0]
        for eb in ce[1:]:
            cden = cden + eb
        cden = jnp.broadcast_to(jnp.sum(cden, axis=-1, keepdims=True), (R, LANES))
        crcp = 1.0 / jnp.where(cden > 0.0, cden, 1.0)
        pc = jnp.concatenate([eb * crcp for eb in ce], axis=1)
        pcb = pc.astype(bf)
        o_cmp = jnp.concatenate([_nn(pcb[0:R // 2], vcmp_ref[0, g]), _nn(pcb[R // 2:], vcmp_ref[0, g])], axis=0)
        psum = pc[0:TR] + pc[TR:2 * TR] + pc[2 * TR:3 * TR] + pc[3 * TR:4 * TR]
        p_hi, p_lo = _split(psum)
        ov = ov_ref[...]
        imp = _nt(ov, p_hi) + _nt(ov, p_lo)

        o_win = []
        for a in range(2):
            qt = 2 * i + a
            t0 = jnp.maximum(qt - (N_WIN_TILES - 1), 0)
            wstart = pl.multiple_of(t0 * TQ, TQ)
            qh = jnp.concatenate([q4[hh * TR + a * TQ:hh * TR + (a + 1) * TQ] for hh in range(NSA_HPG)], axis=0)
            kinds = []
            for r in range(N_WIN_TILES):
                dd = qt - (t0 + r)
                kinds.append(jnp.where(dd >= 0, _K_WIN + dd, _K_MASKED))
            bw = jnp.concatenate([jnp.concatenate([bias_ref[NSA_HPG * g + hh, k] for k in kinds], axis=1)
                                  for hh in range(NSA_HPG)], axis=0)
            sw = _nt(qh, kwn_ref[0, g, pl.ds(wstart, N_WIN_TILES * TQ), :]) + bw
            wm, wacc = m_ref.at[g, 0:NSA_HPG * TQ], acc_ref.at[g, 0:NSA_HPG * TQ]
            vw = vwn_ref[0, g, pl.ds(wstart, N_WIN_TILES * TQ), :]
            _softmax_step(sw, vw, vw, wm, wacc, True)
            o_win.append(_finish(wacc[...], HEAD_DIM))

        jrow = lax.broadcasted_iota(jnp.int32, (nsel, TR), 0)
        qblk = lax.shift_right_logical(i * TR + lax.broadcasted_iota(jnp.int32, (nsel, TR), 1),
                                       SEL_BLOCK.bit_length() - 1)
        cand = (jrow >= 1) & (jrow < qblk)
        keys = jnp.where(cand, imp, -1.0)
        flag = jnp.where((jrow == 0) | (jrow == qblk), 0.0, 1.0)
        for _ in range(SEL_TOPN - 2):
            best = jnp.max(keys, axis=0, keepdims=True)
            first = jnp.min(jnp.where(keys == best, jrow, nsel), axis=0, keepdims=True)
            hit = jrow == first
            flag = jnp.where(hit, 0.0, flag)
            keys = jnp.where(hit, -1.0, keys)
        if nsel < HEAD_DIM:
            flag = jnp.concatenate([flag, jnp.zeros((HEAD_DIM - nsel, TR), jnp.float32)], axis=0)
        flag2 = jnp.concatenate([flag, flag], axis=0).astype(bf)
        maskpart = jnp.concatenate([_nt(eye_ref[...], flag2[:, a * TQ:(a + 1) * TQ]) for a in range(TR // TQ)],
                                   axis=0) * NEG

        zero = jnp.zeros((2 * TR, LANES), bf)
        qa2 = [jnp.concatenate([jnp.where(lo_half if par == 0 else ~lo_half, qf[s], maskpart).astype(bf)
                                for s in range(2)], axis=0) for par in range(2)]
        qq = jnp.concatenate([jnp.concatenate([qa2[0], zero], axis=1),
                              jnp.concatenate([zero, qa2[1]], axis=1)], axis=0)
        return qq, o_cmp, o_win

    state = [prepare(g) for g in range(NSA_GROUPS)]

    def sel_step(c, first):
        start = pl.multiple_of(c * TK, TK)
        d = 2 * (i - c)
        for g in range(NSA_GROUPS):
            kk = jnp.concatenate([kse_ref[0, g, pl.ds(start, TK), :], kso_ref[0, g, pl.ds(start, TK), :]], axis=1)
            b = jnp.concatenate(
                [jnp.concatenate([bias_ref[NSA_HPG * g + h, _tile_kind(d + a - t)] for t in range(2)], axis=1)
                 for h in sel_heads for a in range(2)], axis=0)
            v = vsl_ref[0, g, pl.ds(start, TK), :]
            _softmax_step(_nt(state[g][0], kk) + b, v, v, m_ref.at[g], acc_ref.at[g], first)

    sel_step(i, True)

    def sel_body(c, carry):
        sel_step(c, False)
        return carry

    lax.fori_loop(0, i, sel_body, 0)

    gate = jax.nn.sigmoid(gate_ref[0])
    g_hi, g_lo = _split(gate)
    for g in range(NSA_GROUPS):
        _, o_cmp, o_win = state[g]
        o_all = _finish(acc_ref[g], HEAD_DIM)
        o_sel = [o_all[0:2 * TR], to_hi(o_all[2 * TR:4 * TR])]
        gx = _nn(g_hi, gexp_ref[g]) + _nn(g_lo, gexp_ref[g])
        for s in range(2):
            h_even, h_odd = 2 * s, 2 * s + 1
            cmp2 = jnp.where(lo_half, o_cmp[h_even * TR:(h_even + 1) * TR],
                             to_hi(o_cmp[h_odd * TR:(h_odd + 1) * TR]))
            sel2 = jnp.where(lo_half, o_sel[0][s * TR:(s + 1) * TR], o_sel[1][s * TR:(s + 1) * TR])
            win2 = jnp.where(lo_half,
                             jnp.concatenate([o_win[a][h_even * TQ:(h_even + 1) * TQ] for a in range(2)], axis=0),
                             to_hi(jnp.concatenate([o_win[a][h_odd * TQ:(h_odd + 1) * TQ] for a in range(2)], axis=0)))
            base = s * 3 * LANES
            o_ref[0, :, (2 * g + s) * LANES:(2 * g + s + 1) * LANES] = (
                gx[:, base:base + LANES] * cmp2 + gx[:, base + LANES:base + 2 * LANES] * sel2
                + gx[:, base + 2 * LANES:base + 3 * LANES] * win2)


def _gate_expand():
    e = np.zeros((NSA_GROUPS, LANES, 2 * 3 * LANES), np.float32)
    for g in range(NSA_GROUPS):
        for s in range(2):
            for c in range(3):
                for half in range(2):
                    h = g * NSA_HPG + 2 * s + half
                    col = (s * 3 + c) * LANES + half * HEAD_DIM
                    e[g, 3 * h + c, col:col + HEAD_DIM] = 1.0
    return e


def _nsa(qb, kcmp, vcmp, kse, kso, vsl, kwn, vwn, gate, bias):
    B, S, _ = qb.shape
    nt = S // TQ
    ncmp = kcmp.shape[2]
    nsel = S // SEL_BLOCK
    bf = jnp.bfloat16
    cs = np.arange(ncmp) * CMP_STRIDE
    bs = np.arange(nsel) * SEL_BLOCK
    ov = ((cs[None, :] < bs[:, None] + SEL_BLOCK) & (cs[None, :] + CMP_LEN > bs[:, None])
          & (np.arange(ncmp)[None, :] < ncmp - 1)).astype(np.float32)
    once = pl.Buffered(1)
    grp = lambda n: pl.BlockSpec((1, NSA_GROUPS, n, LANES), lambda b, i: (b, 0, 0, 0), pipeline_mode=once)
    width = NSA_HEADS * HEAD_DIM
    return pl.pallas_call(
        _nsa_body,
        grid=(B, S // TR),
        in_specs=[pl.BlockSpec((1, TR, width), lambda b, i: (b, i, 0)),
                  grp(ncmp), grp(ncmp), grp(S), grp(S), grp(S), grp(S), grp(S),
                  pl.BlockSpec((1, TR, LANES), lambda b, i: (b, i, 0)),
                  pl.BlockSpec((nsel, ncmp), lambda b, i: (0, 0), pipeline_mode=once),
                  pl.BlockSpec((TQ, TQ), lambda b, i: (0, 0), pipeline_mode=once),
                  pl.BlockSpec((NSA_GROUPS, LANES, 6 * LANES), lambda b, i: (0, 0, 0), pipeline_mode=once),
                  pl.BlockSpec((NSA_HEADS, _N_TILE_KINDS, TQ, TQ), lambda b, i: (1, 0, 0, 0), pipeline_mode=once)],
        out_specs=pl.BlockSpec((1, TR, width), lambda b, i: (b, i, 0)),
        out_shape=jax.ShapeDtypeStruct((B, S, width), jnp.float32),
        scratch_shapes=[pltpu.VMEM((NSA_GROUPS, NSA_HPG * TR, LANES), jnp.float32)] * 2,
        compiler_params=pltpu.CompilerParams(dimension_semantics=("parallel", "arbitrary"),
                                             vmem_limit_bytes=VMEM_LIMIT),
        name="nsa",
    )(qb, kcmp, vcmp, kse, kso, vsl, kwn, vwn, gate, jnp.asarray(ov, bf), _eye(TQ),
      jnp.asarray(_gate_expand(), bf), bias)


def _out_proj_body(x_ref, nw_ref, oa_ref, ob_ref, wz_ref, wzb_ref, wgm_ref, wa_ref, wb_ref, wo_ref, out_ref):
    x = x_ref[0]
    ms = jnp.mean(x * x, axis=-1, keepdims=True)
    h = (x * lax.rsqrt(ms + EPS) * nw_ref[...]).astype(jnp.bfloat16)
    ya = (oa_ref[0] * jax.nn.silu(_nn(h, wz_ref[:, _C_ZA:_C_ZA + 512]))).astype(jnp.bfloat16)
    yb = (ob_ref[0] * jax.nn.silu(_nn(h, wzb_ref[...]))).astype(jnp.bfloat16)
    gm = jax.nn.sigmoid(_nn(h, wgm_ref[...]))
    merged = gm[:, 0:D_MODEL] * _nn(ya, wa_ref[...]) + gm[:, D_MODEL:] * _nn(yb, wb_ref[...])
    out_ref[0] = x + _nn(merged.astype(jnp.bfloat16), wo_ref[...])


def _out_proj(x, norm_w, oa, ob, wz, wzb, wgm, wa, wb, wo):
    B, S, _ = x.shape
    nw = norm_w.reshape(1, D_MODEL)
    tok = lambda w: pl.BlockSpec((1, TM, w), lambda b, i: (b, i, 0))
    full = lambda a: pl.BlockSpec(a.shape, lambda b, i: (0,) * a.ndim, pipeline_mode=pl.Buffered(1))
    return pl.pallas_call(
        _out_proj_body,
        grid=(B, S // TM),
        in_specs=[tok(D_MODEL), full(nw), tok(512), tok(512), full(wz), full(wzb), full(wgm),
                  full(wa), full(wb), full(wo)],
        out_specs=tok(D_MODEL),
        out_shape=jax.ShapeDtypeStruct((B, S, D_MODEL), jnp.float32),
        compiler_params=pltpu.CompilerParams(dimension_semantics=("parallel", "parallel"),
                                             vmem_limit_bytes=VMEM_LIMIT),
        name="out_proj",
    )(x, nw, oa, ob, wz, wzb, wgm, wa, wb, wo)


def _pack_in_weights(w_in, q_norm_a, k_norm_a, q_norm_b, k_norm_sel, k_norm_win):
    bf = jnp.bfloat16
    ng = 3 * NSA_HEADS
    cols = lambda a, b: lax.slice_in_dim(w_in, a, b, axis=2).astype(bf).reshape(D_MODEL, b - a)
    z_b = _C_GATE + ng
    w = (cols(0, _C_GATE),
         jnp.pad(cols(_C_GATE, z_b), ((0, 0), (0, LANES - ng))),
         cols(z_b, z_b + NSA_HEADS * HEAD_DIM),
         cols(z_b + NSA_HEADS * HEAD_DIM, w_in.shape[2]))
    ones = lambda n: jnp.ones((n,), jnp.float32)
    gain = jnp.concatenate([jnp.tile(q_norm_a, MOBA_HEADS), jnp.tile(k_norm_a, MOBA_HEADS), ones(1024),
                            jnp.tile(q_norm_b, NSA_HEADS), ones(256), jnp.tile(k_norm_sel, NSA_GROUPS), ones(128),
                            jnp.tile(k_norm_win, NSA_GROUPS), ones(256)]).reshape(1, _N1)
    return w, gain


def kernel(x, norm_w, w_in, q_norm_a, k_norm_a, q_norm_b, k_norm_cmp, k_norm_sel, k_norm_win, cmp_pos_k, cmp_w1_k, cmp_w2_k, cmp_pos_v, cmp_w1_v, cmp_w2_v, rel_bias, w_branch_a, w_branch_b, w_out):
    bf = jnp.bfloat16
    (w_main, w_gate, w_zb, w_gm), gain = _pack_in_weights(w_in, q_norm_a[0], k_norm_a[0], q_norm_b[0],
                                                           k_norm_sel[0], k_norm_win[0])
    bias = _bias_tiles(rel_bias)
    qa, kae, kao, vae, vao, qb, kmean, kc, vc, kse, kso, vsl, kwn, vwn, gate = _in_proj(x, norm_w[0], w_main, w_gate,
                                                                                        gain)
    kcmp, vcmp = _compress(kc, vc, cmp_pos_k[0], cmp_w1_k[0], cmp_w2_k[0],
                           cmp_pos_v[0], cmp_w1_v[0], cmp_w2_v[0], k_norm_cmp[0])
    oa = _moba(qa, kae, kao, vae, vao, kmean[:, :, 0, :], bias)
    ob = _nsa(qb, kcmp, vcmp, kse, kso, vsl, kwn, vwn, gate, bias)
    return _out_proj(x, norm_w[0], oa, ob, w_main, w_zb, w_gm, w_branch_a[0].astype(bf), w_branch_b[0].astype(bf),
                     w_out[0].astype(bf))
```

```python
import math

import jax
import jax.numpy as jnp
import numpy as np
from jax import lax
from jax.experimental import pallas as pl
from jax.experimental.pallas import tpu as pltpu

D_MODEL = 1024
HEAD_DIM = 64
MOBA_HEADS = 8
NSA_HEADS = 8
NSA_GROUPS = 2
NSA_HPG = NSA_HEADS // NSA_GROUPS
MOBA_BLOCK = 256
MOBA_TOPK = 3
CMP_LEN = 32
CMP_STRIDE = 16
CMP_HIDDEN = 256
SEL_BLOCK = 64
SEL_TOPN = 16
WINDOW = 512
NUM_BUCKETS = 32
MAX_DISTANCE = 1024
EPS = 1e-6
NEG = -1e30

LANES = 128
TQ = 256
TK = 2 * TQ
TR = 2 * TQ
TM = 2 * TQ
MOBA_PAIRS = 2
PROJ_CHAINS = 2
N_FAR = 5
N_WIN_TILES = WINDOW // TQ + 1
VMEM_LIMIT = 56 * 1024 * 1024
LOG2E = math.log2(math.e)
QSCALE = LOG2E / math.sqrt(HEAD_DIM)

_K_WIN = N_FAR + 1
_K_MASKED = _K_WIN + N_WIN_TILES
_N_TILE_KINDS = _K_MASKED + 1

_C_QA, _C_KA, _C_VA, _C_ZA, _C_QB = 0, 512, 1024, 1536, 2048
_C_KC, _C_VC, _C_KSL, _C_VSL, _C_KWN, _C_VWN, _C_GATE = 2560, 2688, 2816, 2944, 3072, 3200, 3328
_N1 = _C_GATE + LANES


def _nt(a, b):
    return lax.dot_general(a, b, (((1,), (1,)), ((), ())), preferred_element_type=jnp.float32)


def _nn(a, b):
    return jnp.dot(a, b, preferred_element_type=jnp.float32)


def _split(x):
    hi = x.astype(jnp.bfloat16)
    lo = (x - hi.astype(jnp.float32)).astype(jnp.bfloat16)
    return hi, lo


def _bucket_table(n):
    d = np.arange(n)
    nf = np.maximum(d, NUM_BUCKETS // 2).astype(np.float64)
    large = NUM_BUCKETS // 2 + (np.log(nf / (NUM_BUCKETS // 2)) / math.log(MAX_DISTANCE / (NUM_BUCKETS // 2))
                                * (NUM_BUCKETS - NUM_BUCKETS // 2)).astype(np.int64)
    return np.where(d < NUM_BUCKETS // 2, d, np.minimum(large, NUM_BUCKETS - 1))


def _bias_onehot():
    oh = np.zeros((40, _N_TILE_KINDS * 2 * TQ), np.float32)
    buckets = _bucket_table((N_FAR + 2) * TQ + 1)
    for t in range(_N_TILE_KINDS):
        window = _K_WIN <= t < _K_MASKED
        delta = t - _K_WIN if window else t
        for u in range(2 * TQ):
            d = TQ * delta + TQ - u
            masked = t == _K_MASKED or d < 0 or (window and d >= WINDOW)
            oh[NUM_BUCKETS if masked else buckets[d], t * 2 * TQ + u] = 1.0
    return oh


def _bias_body(relt_ref, oh_ref, out_ref):
    g = jnp.dot(jnp.broadcast_to(relt_ref[0], (8, 40)), oh_ref[...], preferred_element_type=jnp.float32,
                precision=lax.Precision.HIGHEST)
    for t in range(_N_TILE_KINDS):
        full = jnp.broadcast_to(g[0:1, t * 2 * TQ:(t + 1) * 2 * TQ], (TQ, 2 * TQ))
        rolled = pltpu.roll(full, TQ, 1, stride=1, stride_axis=0)
        out_ref[0, t] = rolled[:, :TQ]


def _bias_tiles(rel_bias):
    nh = MOBA_HEADS + NSA_HEADS
    relt = jnp.concatenate([rel_bias.T.astype(jnp.float32) * LOG2E, jnp.full((nh, 1), NEG, jnp.float32),
                            jnp.zeros((nh, 7), jnp.float32)], axis=1).reshape(nh, 1, 40)
    oh = jnp.asarray(_bias_onehot())
    return pl.pallas_call(
        _bias_body,
        grid=(nh,),
        in_specs=[pl.BlockSpec((1, 1, 40), lambda h: (h, 0, 0)), pl.BlockSpec(oh.shape, lambda h: (0, 0))],
        out_specs=pl.BlockSpec((1, _N_TILE_KINDS, TQ, TQ), lambda h: (h, 0, 0, 0)),
        out_shape=jax.ShapeDtypeStruct((nh, _N_TILE_KINDS, TQ, TQ), jnp.float32),
        compiler_params=pltpu.CompilerParams(dimension_semantics=("parallel",), vmem_limit_bytes=VMEM_LIMIT),
        name="bias_tiles",
    )(relt, oh)


def _seg_norm(x, lo_half, gain):
    x2 = x * x
    ss_lo = jnp.sum(jnp.where(lo_half, x2, 0.0), axis=-1, keepdims=True)
    ss_hi = jnp.sum(jnp.where(lo_half, 0.0, x2), axis=-1, keepdims=True)
    ss = jnp.where(lo_half, ss_lo, ss_hi)
    return x * lax.rsqrt(ss * (1.0 / HEAD_DIM) + EPS) * gain


def _in_proj_body(x_ref, nw_ref, w_ref, wg_ref, gain_ref, *refs):
    outs, stages = refs[:-2], refs[-2:]
    for c in range(PROJ_CHAINS):
        rows = pl.ds(c * TM, TM)
        tok = lambda r: r.at[:, rows, :]
        grp = lambda r: r.at[:, :, rows, :]
        chunk = lambda r: r.at[:, :, pl.ds(c * (TM // CMP_STRIDE), TM // CMP_STRIDE), :]
        (qa, kae, kao, vae, vao, qb, kmean, kc, vc, kse, kso, vsl, kwn, vwn, gate) = outs
        _in_proj_tile(pl.program_id(1) * PROJ_CHAINS + c, tok(x_ref), nw_ref, w_ref, wg_ref, gain_ref,
                      tok(qa), tok(kae), tok(kao), tok(vae), tok(vao), tok(qb),
                      kmean.at[:, pl.ds(c * (TM // MOBA_BLOCK), TM // MOBA_BLOCK)], chunk(kc), chunk(vc),
                      grp(kse), grp(kso), grp(vsl), grp(kwn), grp(vwn), tok(gate),
                      stages[0].at[c], stages[1].at[c])


def _in_proj_tile(i, x_ref, nw_ref, w_ref, wg_ref, gain_ref,
                  qa_ref, kae_ref, kao_ref, vae_ref, vao_ref, qb_ref, kmean_ref, kc_ref, vc_ref,
                  kse_ref, kso_ref, vsl_ref, kwn_ref, vwn_ref, gate_ref, stage_k_ref, stage_v_ref):
    x = x_ref[0]
    ms = jnp.mean(x * x, axis=-1, keepdims=True)
    h = (x * lax.rsqrt(ms + EPS) * nw_ref[...]).astype(jnp.bfloat16)
    bf = jnp.bfloat16
    lane = lax.broadcasted_iota(jnp.int32, (TM, LANES), 1)
    row = lax.broadcasted_iota(jnp.int32, (TM, LANES), 0)
    lo_half = lane < HEAD_DIM
    blk = i * (TM // MOBA_BLOCK) + lax.shift_right_logical(row, MOBA_BLOCK.bit_length() - 1)
    sblk = i * (TM // SEL_BLOCK) + lax.shift_right_logical(row, SEL_BLOCK.bit_length() - 1)

    def normed(col, width, scale=None):
        p = _nn(h, w_ref[:, col:col + width])
        outs = []
        for s in range(width // LANES):
            c = col + s * LANES
            y = _seg_norm(p[:, s * LANES:(s + 1) * LANES], lo_half, gain_ref[:, c:c + LANES])
            outs.append(y if scale is None else y * scale)
        return outs

    qa = normed(_C_QA, 512, QSCALE)
    for s in range(4):
        qa_ref[0, :, s * LANES:(s + 1) * LANES] = qa[s].astype(bf)
    ka = normed(_C_KA, 512)
    oh_hi = jnp.where(lane == HEAD_DIM + blk, 1.0, 0.0)
    oh_lo = jnp.where(lane == blk, 1.0, 0.0)
    for s in range(4):
        kae_ref[0, :, s * LANES:(s + 1) * LANES] = jnp.where(lo_half, ka[s], oh_hi).astype(bf)
        kao_ref[0, :, s * LANES:(s + 1) * LANES] = jnp.where(lo_half, oh_lo, ka[s]).astype(bf)
        for j in range(TM // MOBA_BLOCK):
            kmean_ref[0, j, :, s * LANES:(s + 1) * LANES] = jnp.broadcast_to(
                jnp.mean(ka[s][j * MOBA_BLOCK:(j + 1) * MOBA_BLOCK], axis=0, keepdims=True), (8, LANES))
    one_hi = jnp.where(lane == HEAD_DIM, 1.0, 0.0)
    one_lo = jnp.where(lane == 0, 1.0, 0.0)
    va = _nn(h, w_ref[:, _C_VA:_C_VA + 512])
    for s in range(4):
        vs = va[:, s * LANES:(s + 1) * LANES]
        vae_ref[0, :, s * LANES:(s + 1) * LANES] = jnp.where(lo_half, vs, one_hi).astype(bf)
        vao_ref[0, :, s * LANES:(s + 1) * LANES] = jnp.where(lo_half, one_lo, vs).astype(bf)
    qb = normed(_C_QB, 512, QSCALE)
    for s in range(4):
        qb_ref[0, :, s * LANES:(s + 1) * LANES] = qb[s].astype(bf)

    rest = _nn(h, w_ref[:, _C_KC:_C_GATE])
    gate_ref[0] = jnp.concatenate([_nn(h[0:TM // 2], wg_ref[...]), _nn(h[TM // 2:], wg_ref[...])], axis=0)

    def slab(c):
        return rest[:, c - _C_KC:c - _C_KC + LANES]

    def chunk_store(ref, stage_ref, y):
        stage_ref[...] = y
        lo = lax.broadcasted_iota(jnp.int32, (TM // CMP_STRIDE, LANES), 1) < HEAD_DIM
        for u in range(CMP_STRIDE // 2):
            t0 = stage_ref[pl.ds(2 * u, TM // CMP_STRIDE, stride=CMP_STRIDE), :]
            t1 = stage_ref[pl.ds(2 * u + 1, TM // CMP_STRIDE, stride=CMP_STRIDE), :]
            ref[0, 0, :, u * LANES:(u + 1) * LANES] = jnp.where(lo, t0, pltpu.roll(t1, HEAD_DIM, 1)).astype(bf)
            ref[0, 1, :, u * LANES:(u + 1) * LANES] = jnp.where(lo, pltpu.roll(t0, HEAD_DIM, 1), t1).astype(bf)

    chunk_store(kc_ref, stage_k_ref, slab(_C_KC))
    chunk_store(vc_ref, stage_v_ref, slab(_C_VC))

    def dup_store(ref, y):
        r = pltpu.roll(y, HEAD_DIM, 1)
        ref[0, 0] = jnp.where(lo_half, y, r).astype(bf)
        ref[0, 1] = jnp.where(lo_half, r, y).astype(bf)

    y = _seg_norm(slab(_C_KSL), lo_half, gain_ref[:, _C_KSL:_C_KSL + LANES])
    r = pltpu.roll(y, HEAD_DIM, 1)
    oh = jnp.where((lane & (HEAD_DIM - 1)) == sblk, 1.0, 0.0)
    kse_ref[0, 0] = jnp.where(lo_half, y, oh).astype(bf)
    kso_ref[0, 0] = jnp.where(lo_half, oh, r).astype(bf)
    kse_ref[0, 1] = jnp.where(lo_half, r, oh).astype(bf)
    kso_ref[0, 1] = jnp.where(lo_half, oh, y).astype(bf)
    def aug_store(ref, y):
        ref[0, 0] = jnp.where(lo_half, y, one_hi).astype(bf)
        ref[0, 1] = jnp.where(lo_half, pltpu.roll(y, HEAD_DIM, 1), one_hi).astype(bf)

    aug_store(vsl_ref, slab(_C_VSL))
    dup_store(kwn_ref, _seg_norm(slab(_C_KWN), lo_half, gain_ref[:, _C_KWN:_C_KWN + LANES]))
    aug_store(vwn_ref, slab(_C_VWN))


def _in_proj(x, norm_w, w1, wg, gain_row):
    B, S, _ = x.shape
    nt = S // TQ
    bf = jnp.bfloat16
    tok = lambda w, dt: jax.ShapeDtypeStruct((B, S, w), dt)
    grp = jax.ShapeDtypeStruct((B, NSA_GROUPS, S, LANES), bf)
    rows = PROJ_CHAINS * TM
    tok_spec = lambda w: pl.BlockSpec((1, rows, w), lambda b, i: (b, i, 0))
    grp_spec = pl.BlockSpec((1, NSA_GROUPS, rows, LANES), lambda b, i: (b, 0, i, 0))
    flat = CMP_STRIDE * HEAD_DIM
    chunk = jax.ShapeDtypeStruct((B, NSA_GROUPS, S // CMP_STRIDE, flat), bf)
    chunk_spec = pl.BlockSpec((1, NSA_GROUPS, rows // CMP_STRIDE, flat), lambda b, i: (b, 0, i, 0))
    full = lambda a: pl.BlockSpec(a.shape, lambda b, i: (0,) * a.ndim, pipeline_mode=pl.Buffered(1))
    nw = norm_w.reshape(1, D_MODEL)
    return pl.pallas_call(
        _in_proj_body,
        grid=(B, S // rows),
        in_specs=[tok_spec(D_MODEL), full(nw), full(w1), full(wg), full(gain_row)],
        out_specs=[tok_spec(512), tok_spec(512), tok_spec(512), tok_spec(512), tok_spec(512), tok_spec(512),
                   pl.BlockSpec((1, rows // MOBA_BLOCK, 8, 512), lambda b, i: (b, i, 0, 0)),
                   chunk_spec, chunk_spec,
                   grp_spec, grp_spec, grp_spec, grp_spec, grp_spec, tok_spec(LANES)],
        out_shape=[tok(512, bf), tok(512, bf), tok(512, bf), tok(512, bf), tok(512, bf), tok(512, bf),
                   jax.ShapeDtypeStruct((B, nt, 8, 512), jnp.float32),
                   chunk, chunk, grp, grp, grp, grp, grp, tok(LANES, jnp.float32)],
        scratch_shapes=[pltpu.VMEM((PROJ_CHAINS, TM, LANES), jnp.float32)] * 2,
        compiler_params=pltpu.CompilerParams(dimension_semantics=("parallel", "parallel"),
                                             vmem_limit_bytes=VMEM_LIMIT),
        name="in_proj",
    )(x, nw, w1, wg, gain_row)


def _gelu_tanh(x):
    return 0.5 * x * (1.0 + jnp.tanh(math.sqrt(2.0 / math.pi) * (x + 0.044715 * (x * x * x))))


def _compress_body(xk_ref, xv_ref, w1k_ref, w1v_ref, pk_ref, pv_ref,
                   w2k_ref, w2v_ref, gk_ref, kcmp_ref, vcmp_ref):
    nchunk = xk_ref.shape[2]
    half = CMP_STRIDE * HEAD_DIM

    def branch(x_ref, w1_ref, pos_ref, w2_ref, gain):
        pos = jnp.broadcast_to(pos_ref[...], (8, CMP_LEN * HEAD_DIM))
        posw = (_nn(pos[:, 0:half], w1_ref[...])[0:1, 0:CMP_HIDDEN]
                + _nn(pos[:, half:], w1_ref[...])[0:1, CMP_HIDDEN:])
        outs = []
        for g in range(NSA_GROUPS):
            ab = _nn(x_ref[0, g], w1_ref[...])
            top, bot = ab[:, 0:CMP_HIDDEN], ab[:, CMP_HIDDEN:]
            hid = top + pltpu.roll(bot, nchunk - 1, 0) + posw
            y = _nn(_gelu_tanh(hid).astype(jnp.bfloat16), w2_ref[...])
            if gain is not None:
                ms = jnp.sum(y * y, axis=-1, keepdims=True) * (0.5 / HEAD_DIM)
                y = y * lax.rsqrt(ms + EPS) * gain
            else:
                lane = lax.broadcasted_iota(jnp.int32, y.shape, 1)
                y = jnp.where(lane < HEAD_DIM, y, jnp.where(lane == HEAD_DIM, 1.0, 0.0))
            outs.append(y.astype(jnp.bfloat16))
        return outs

    k0, k1 = branch(xk_ref, w1k_ref, pk_ref, w2k_ref, gk_ref[...])
    kcmp_ref[0, 0], kcmp_ref[0, 1] = k0, k1
    v0, v1 = branch(xv_ref, w1v_ref, pv_ref, w2v_ref, None)
    vcmp_ref[0, 0], vcmp_ref[0, 1] = v0, v1


def _compress(xk, xv, pos_k, w1_k, w2_k, pos_v, w1_v, w2_v, gain_cmp):
    B, _, nchunk, half = xk.shape
    bf = jnp.bfloat16
    halves = lambda w1: jnp.concatenate([w1[:half], w1[half:]], axis=1).astype(bf)
    args = [xk, xv, halves(w1_k), halves(w1_v),
            pos_k.reshape(1, -1).astype(bf), pos_v.reshape(1, -1).astype(bf),
            jnp.concatenate([w2_k, w2_k], axis=1).astype(bf), jnp.concatenate([w2_v, w2_v], axis=1).astype(bf),
            jnp.concatenate([gain_cmp, gain_cmp]).reshape(1, LANES)]
    x_spec = pl.BlockSpec((1, NSA_GROUPS, nchunk, half), lambda b: (b, 0, 0, 0))
    full = lambda a: pl.BlockSpec(a.shape, lambda b: (0,) * a.ndim)
    out = jax.ShapeDtypeStruct((B, NSA_GROUPS, nchunk, LANES), bf)
    o_spec = pl.BlockSpec((1, NSA_GROUPS, nchunk, LANES), lambda b: (b, 0, 0, 0))
    return pl.pallas_call(
        _compress_body,
        grid=(B,),
        in_specs=[x_spec, x_spec] + [full(a) for a in args[2:]],
        out_specs=[o_spec, o_spec],
        out_shape=[out, out],
        compiler_params=pltpu.CompilerParams(dimension_semantics=("parallel",), vmem_limit_bytes=VMEM_LIMIT),
        name="compress",
    )(*args)


def _softmax_step(s, v_top, v_bot, m_ref, acc_ref, first):
    blocks = [s[:, c * LANES:(c + 1) * LANES] for c in range(s.shape[1] // LANES)]
    bmax = blocks[0]
    for blk in blocks[1:]:
        bmax = jnp.maximum(bmax, blk)
    m_new = jnp.broadcast_to(jnp.max(bmax, axis=-1, keepdims=True), m_ref.shape)
    if not first:
        m_old = m_ref[...]
        m_new = jnp.maximum(m_old, m_new)
        alpha = jnp.exp2(m_old - m_new)
    pb = jnp.concatenate([jnp.exp2(blk - m_new) for blk in blocks], axis=1).astype(jnp.bfloat16)
    half = s.shape[0] // 2
    pv = jnp.concatenate([_nn(pb[0:half], v_top), _nn(pb[half:], v_bot)], axis=0)
    acc_ref[...] = pv if first else alpha * acc_ref[...] + pv
    m_ref[...] = m_new


def _finish(acc, ones_lane):
    lane = lax.broadcasted_iota(jnp.int32, acc.shape, 1)
    return acc / jnp.sum(jnp.where(lane == ones_lane, acc, 0.0), axis=-1, keepdims=True)


def _tile_kind(delta):
    return jnp.where(delta >= 0, jnp.minimum(delta, N_FAR), _K_MASKED)


def _chunk_kinds(qi, c):
    d0 = qi - 2 * c
    kind0 = jnp.minimum(d0, N_FAR)
    kind1 = jnp.where(d0 >= 1, jnp.minimum(d0 - 1, N_FAR), _K_MASKED)
    return kind0, kind1


def _moba_body(q_ref, ke_ref, ko_ref, ve_ref, vo_ref, kmean_ref, eye_ref, bias_ref, o_ref, m_ref, acc_ref):
    i = pl.program_id(2)
    nblk = kmean_ref.shape[1]
    lane = lax.broadcasted_iota(jnp.int32, (TR, LANES), 1)
    lo_half = lane < HEAD_DIM
    lane16 = lax.broadcasted_iota(jnp.int32, (nblk, LANES), 1)
    jrow = lax.broadcasted_iota(jnp.int32, (nblk, TR), 0)
    qblk = 2 * i + jnp.where(lax.broadcasted_iota(jnp.int32, (nblk, TR), 1) >= TQ, 1, 0)
    pad = jnp.zeros((HEAD_DIM - nblk, TR), jnp.float32)
    zero = jnp.zeros((TR, LANES), jnp.bfloat16)

    def augmented_q(pr):
        q2 = q_ref[0, :, pr * LANES:(pr + 1) * LANES]
        kmean2 = kmean_ref[0, :, pr * LANES:(pr + 1) * LANES]
        flags = []
        for hh in range(2):
            inhead16 = (lane16 < HEAD_DIM) if hh == 0 else (lane16 >= HEAD_DIM)
            km_hi, km_lo = _split(jnp.where(inhead16, kmean2, 0.0))
            sc = jnp.where(jrow < qblk, _nt(km_hi, q2) + _nt(km_lo, q2), -jnp.inf)
            flag = jnp.where(jrow == qblk, 0.0, 1.0)
            for _ in range(MOBA_TOPK):
                best = jnp.max(sc, axis=0, keepdims=True)
                first = jnp.min(jnp.where(sc == best, jrow, nblk), axis=0, keepdims=True)
                hit = jrow == first
                flag = jnp.where(hit, 0.0, flag)
                sc = jnp.where(hit, -jnp.inf, sc)
            flags.append(flag)
        placed = jnp.concatenate([flags[1], pad, flags[0], pad], axis=0).astype(jnp.bfloat16)
        maskpart = jnp.concatenate([_nt(eye_ref[...], placed[:, a * TQ:(a + 1) * TQ]) for a in range(TR // TQ)],
                                   axis=0) * NEG
        q2f = q2.astype(jnp.float32)
        return jnp.concatenate(
            [jnp.concatenate([jnp.where(lo_half, q2f, maskpart).astype(jnp.bfloat16), zero], axis=1),
             jnp.concatenate([zero, jnp.where(lo_half, maskpart, q2f).astype(jnp.bfloat16)], axis=1)], axis=0)

    qqs = [augmented_q(pr) for pr in range(MOBA_PAIRS)]

    def step(c, first):
        rows = pl.ds(pl.multiple_of(c * TK, TK), TK)
        d = 2 * (i - c)
        for pr in range(MOBA_PAIRS):
            cols = slice(pr * LANES, (pr + 1) * LANES)
            kk = jnp.concatenate([ke_ref[0, rows, cols], ko_ref[0, rows, cols]], axis=1)
            b = jnp.concatenate(
                [jnp.concatenate([bias_ref[2 * pr + hh, _tile_kind(d + a - t)] for t in range(2)], axis=1)
                 for hh in range(2) for a in range(2)], axis=0)
            _softmax_step(_nt(qqs[pr], kk) + b, ve_ref[0, rows, cols], vo_ref[0, rows, cols],
                          m_ref.at[pr], acc_ref.at[pr], first)

    step(i, True)

    def body(c, carry):
        step(c, False)
        return carry

    lax.fori_loop(0, i, body, 0)
    for pr in range(MOBA_PAIRS):
        o_ref[0, :, pr * LANES:(pr + 1) * LANES] = jnp.where(
            lo_half, _finish(acc_ref[pr, 0:TR], HEAD_DIM), _finish(acc_ref[pr, TR:2 * TR], 0))


def _eye(n):
    return jnp.asarray(np.eye(n, dtype=np.float32), jnp.bfloat16)


def _moba(qa, kae, kao, vae, vao, kmean, bias):
    B, S, _ = qa.shape
    nt = S // TQ
    width = MOBA_PAIRS * LANES
    kv_spec = pl.BlockSpec((1, S, width), lambda b, p, i: (b, 0, p))
    return pl.pallas_call(
        _moba_body,
        grid=(B, MOBA_HEADS // (2 * MOBA_PAIRS), S // TR),
        in_specs=[pl.BlockSpec((1, TR, width), lambda b, p, i: (b, i, p)),
                  kv_spec, kv_spec, kv_spec, kv_spec,
                  pl.BlockSpec((1, nt, width), lambda b, p, i: (b, 0, p)),
                  pl.BlockSpec((TQ, TQ), lambda b, p, i: (0, 0)),
                  pl.BlockSpec((2 * MOBA_PAIRS, _N_TILE_KINDS, TQ, TQ), lambda b, p, i: (p, 0, 0, 0),
                               pipeline_mode=pl.Buffered(1))],
        out_specs=pl.BlockSpec((1, TR, width), lambda b, p, i: (b, i, p)),
        out_shape=jax.ShapeDtypeStruct((B, S, MOBA_HEADS * HEAD_DIM), jnp.float32),
        scratch_shapes=[pltpu.VMEM((MOBA_PAIRS, 2 * TR, LANES), jnp.float32)] * 2,
        compiler_params=pltpu.CompilerParams(dimension_semantics=("parallel", "parallel", "arbitrary"),
                                             vmem_limit_bytes=VMEM_LIMIT),
        name="moba",
    )(qa, kae, kao, vae, vao, kmean, _eye(TQ), bias)


def _nsa_body(q_ref, kcmp_ref, vcmp_ref, kse_ref, kso_ref, vsl_ref, kwn_ref, vwn_ref, gate_ref, ov_ref, eye_ref,
              gexp_ref, bias_ref, o_ref, m_ref, acc_ref):
    i = pl.program_id(1)

    def to_hi(o):
        return pltpu.roll(o, HEAD_DIM, 1)

    ncmp = kcmp_ref.shape[2]
    nsel = ov_ref.shape[0]
    R = NSA_HPG * TR
    bf = jnp.bfloat16
    lane = lax.broadcasted_iota(jnp.int32, (TR, LANES), 1)
    lo_half = lane < HEAD_DIM
    sel_heads = (0, 2, 1, 3)

    def prepare(g):
        qf = [q_ref[0, :, (2 * g + s) * LANES:(2 * g + s + 1) * LANES].astype(jnp.float32) for s in range(2)]
        q4 = jnp.concatenate([jnp.where(lo_half if hh % 2 == 0 else ~lo_half, qf[hh // 2], 0.0).astype(bf)
                              for hh in range(NSA_HPG)], axis=0)

        tpos = i * TR + lax.broadcasted_iota(jnp.int32, (TR, ncmp), 0)
        cidx = lax.broadcasted_iota(jnp.int32, (TR, ncmp), 1)
        cbias1 = jnp.where((cidx * CMP_STRIDE + (CMP_LEN - 1) <= tpos) & (cidx < ncmp - 1), 0.0, NEG)
        lc = _nt(q4, kcmp_ref[0, g]) + jnp.concatenate([cbias1] * NSA_HPG, axis=0)
        cblocks = [lc[:, c * LANES:(c + 1) * LANES] for c in range(ncmp // LANES)]
        cmax = cblocks[0]
        for blk in cblocks[1:]:
            cmax = jnp.maximum(cmax, blk)
        cm = jnp.maximum(jnp.broadcast_to(jnp.max(cmax, axis=-1, keepdims=True), (R, LANES)), 0.5 * NEG)
        ce = [jnp.exp2(blk - cm) for blk in cblocks]
        cden = ce[0]
        for eb in ce[1:]:
            cden = cden + eb
        cden = jnp.broadcast_to(jnp.sum(cden, axis=-1, keepdims=True), (R, LANES))
        crcp = 1.0 / jnp.where(cden > 0.0, cden, 1.0)
        pc = jnp.concatenate([eb * crcp for eb in ce], axis=1)
        pcb = pc.astype(bf)
        o_cmp = jnp.concatenate([_nn(pcb[0:R // 2], vcmp_ref[0, g]), _nn(pcb[R // 2:], vcmp_ref[0, g])], axis=0)
        psum = pc[0:TR] + pc[TR:2 * TR] + pc[2 * TR:3 * TR] + pc[3 * TR:4 * TR]
        p_hi, p_lo = _split(psum)
        ov = ov_ref[...]
        imp = _nt(ov, p_hi) + _nt(ov, p_lo)

        o_win = []
        for a in range(2):
            qt = 2 * i + a
            t0 = jnp.maximum(qt - (N_WIN_TILES - 1), 0)
            wstart = pl.multiple_of(t0 * TQ, TQ)
            qh = jnp.concatenate([q4[hh * TR + a * TQ:hh * TR + (a + 1) * TQ] for hh in range(NSA_HPG)], axis=0)
            kinds = []
            for r in range(N_WIN_TILES):
                dd = qt - (t0 + r)
                kinds.append(jnp.where(dd >= 0, _K_WIN + dd, _K_MASKED))
            bw = jnp.concatenate([jnp.concatenate([bias_ref[NSA_HPG * g + hh, k] for k in kinds], axis=1)
                                  for hh in range(NSA_HPG)], axis=0)
            sw = _nt(qh, kwn_ref[0, g, pl.ds(wstart, N_WIN_TILES * TQ), :]) + bw
            wm, wacc = m_ref.at[g, 0:NSA_HPG * TQ], acc_ref.at[g, 0:NSA_HPG * TQ]
            vw = vwn_ref[0, g, pl.ds(wstart, N_WIN_TILES * TQ), :]
            _softmax_step(sw, vw, vw, wm, wacc, True)
            o_win.append(_finish(wacc[...], HEAD_DIM))

        jrow = lax.broadcasted_iota(jnp.int32, (nsel, TR), 0)
        qblk = lax.shift_right_logical(i * TR + lax.broadcasted_iota(jnp.int32, (nsel, TR), 1),
                                       SEL_BLOCK.bit_length() - 1)
        cand = (jrow >= 1) & (jrow < qblk)
        keys = jnp.where(cand, imp, -1.0)
        flag = jnp.where((jrow == 0) | (jrow == qblk), 0.0, 1.0)
        for _ in range(SEL_TOPN - 2):
            best = jnp.max(keys, axis=0, keepdims=True)
            first = jnp.min(jnp.where(keys == best, jrow, nsel), axis=0, keepdims=True)
            hit = jrow == first
            flag = jnp.where(hit, 0.0, flag)
            keys = jnp.where(hit, -1.0, keys)
        if nsel < HEAD_DIM:
            flag = jnp.concatenate([flag, jnp.zeros((HEAD_DIM - nsel, TR), jnp.float32)], axis=0)
        flag2 = jnp.concatenate([flag, flag], axis=0).astype(bf)
        maskpart = jnp.concatenate([_nt(eye_ref[...], flag2[:, a * TQ:(a + 1) * TQ]) for a in range(TR // TQ)],
                                   axis=0) * NEG

        zero = jnp.zeros((2 * TR, LANES), bf)
        qa2 = [jnp.concatenate([jnp.where(lo_half if par == 0 else ~lo_half, qf[s], maskpart).astype(bf)
                                for s in range(2)], axis=0) for par in range(2)]
        qq = jnp.concatenate([jnp.concatenate([qa2[0], zero], axis=1),
                              jnp.concatenate([zero, qa2[1]], axis=1)], axis=0)
        return qq, o_cmp, o_win

    state = [prepare(g) for g in range(NSA_GROUPS)]

    def sel_step(c, first):
        start = pl.multiple_of(c * TK, TK)
        d = 2 * (i - c)
        for g in range(NSA_GROUPS):
            kk = jnp.concatenate([kse_ref[0, g, pl.ds(start, TK), :], kso_ref[0, g, pl.ds(start, TK), :]], axis=1)
            b = jnp.concatenate(
                [jnp.concatenate([bias_ref[NSA_HPG * g + h, _tile_kind(d + a - t)] for t in range(2)], axis=1)
                 for h in sel_heads for a in range(2)], axis=0)
            v = vsl_ref[0, g, pl.ds(start, TK), :]
            _softmax_step(_nt(state[g][0], kk) + b, v, v, m_ref.at[g], acc_ref.at[g], first)

    sel_step(i, True)

    def sel_body(c, carry):
        sel_step(c, False)
        return carry

    lax.fori_loop(0, i, sel_body, 0)

    gate = jax.nn.sigmoid(gate_ref[0])
    g_hi, g_lo = _split(gate)
    for g in range(NSA_GROUPS):
        _, o_cmp, o_win = state[g]
        o_all = _finish(acc_ref[g], HEAD_DIM)
        o_sel = [o_all[0:2 * TR], to_hi(o_all[2 * TR:4 * TR])]
        gx = _nn(g_hi, gexp_ref[g]) + _nn(g_lo, gexp_ref[g])
        for s in range(2):
            h_even, h_odd = 2 * s, 2 * s + 1
            cmp2 = jnp.where(lo_half, o_cmp[h_even * TR:(h_even + 1) * TR],
                             to_hi(o_cmp[h_odd * TR:(h_odd + 1) * TR]))
            sel2 = jnp.where(lo_half, o_sel[0][s * TR:(s + 1) * TR], o_sel[1][s * TR:(s + 1) * TR])
            win2 = jnp.where(lo_half,
                             jnp.concatenate([o_win[a][h_even * TQ:(h_even + 1) * TQ] for a in range(2)], axis=0),
                             to_hi(jnp.concatenate([o_win[a][h_odd * TQ:(h_odd + 1) * TQ] for a in range(2)], axis=0)))
            base = s * 3 * LANES
            o_ref[0, :, (2 * g + s) * LANES:(2 * g + s + 1) * LANES] = (
                gx[:, base:base + LANES] * cmp2 + gx[:, base + LANES:base + 2 * LANES] * sel2
                + gx[:, base + 2 * LANES:base + 3 * LANES] * win2)


def _gate_expand():
    e = np.zeros((NSA_GROUPS, LANES, 2 * 3 * LANES), np.float32)
    for g in range(NSA_GROUPS):
        for s in range(2):
            for c in range(3):
                for half in range(2):
                    h = g * NSA_HPG + 2 * s + half
                    col = (s * 3 + c) * LANES + half * HEAD_DIM
                    e[g, 3 * h + c, col:col + HEAD_DIM] = 1.0
    return e


def _nsa(qb, kcmp, vcmp, kse, kso, vsl, kwn, vwn, gate, bias):
    B, S, _ = qb.shape
    nt = S // TQ
    ncmp = kcmp.shape[2]
    nsel = S // SEL_BLOCK
    bf = jnp.bfloat16
    cs = np.arange(ncmp) * CMP_STRIDE
    bs = np.arange(nsel) * SEL_BLOCK
    ov = ((cs[None, :] < bs[:, None] + SEL_BLOCK) & (cs[None, :] + CMP_LEN > bs[:, None])
          & (np.arange(ncmp)[None, :] < ncmp - 1)).astype(np.float32)
    once = pl.Buffered(1)
    grp = lambda n: pl.BlockSpec((1, NSA_GROUPS, n, LANES), lambda b, i: (b, 0, 0, 0), pipeline_mode=once)
    width = NSA_HEADS * HEAD_DIM
    return pl.pallas_call(
        _nsa_body,
        grid=(B, S // TR),
        in_specs=[pl.BlockSpec((1, TR, width), lambda b, i: (b, i, 0)),
                  grp(ncmp), grp(ncmp), grp(S), grp(S), grp(S), grp(S), grp(S),
                  pl.BlockSpec((1, TR, LANES), lambda b, i: (b, i, 0)),
                  pl.BlockSpec((nsel, ncmp), lambda b, i: (0, 0), pipeline_mode=once),
                  pl.BlockSpec((TQ, TQ), lambda b, i: (0, 0), pipeline_mode=once),
                  pl.BlockSpec((NSA_GROUPS, LANES, 6 * LANES), lambda b, i: (0, 0, 0), pipeline_mode=once),
                  pl.BlockSpec((NSA_HEADS, _N_TILE_KINDS, TQ, TQ), lambda b, i: (1, 0, 0, 0), pipeline_mode=once)],
        out_specs=pl.BlockSpec((1, TR, width), lambda b, i: (b, i, 0)),
        out_shape=jax.ShapeDtypeStruct((B, S, width), jnp.float32),
        scratch_shapes=[pltpu.VMEM((NSA_GROUPS, NSA_HPG * TR, LANES), jnp.float32)] * 2,
        compiler_params=pltpu.CompilerParams(dimension_semantics=("parallel", "arbitrary"),
                                             vmem_limit_bytes=VMEM_LIMIT),
        name="nsa",
    )(qb, kcmp, vcmp, kse, kso, vsl, kwn, vwn, gate, jnp.asarray(ov, bf), _eye(TQ),
      jnp.asarray(_gate_expand(), bf), bias)


def _out_proj_body(x_ref, nw_ref, oa_ref, ob_ref, wz_ref, wzb_ref, wgm_ref, wa_ref, wb_ref, wo_ref, out_ref):
    x = x_ref[0]
    ms = jnp.mean(x * x, axis=-1, keepdims=True)
    h = (x * lax.rsqrt(ms + EPS) * nw_ref[...]).astype(jnp.bfloat16)
    ya = (oa_ref[0] * jax.nn.silu(_nn(h, wz_ref[:, _C_ZA:_C_ZA + 512]))).astype(jnp.bfloat16)
    yb = (ob_ref[0] * jax.nn.silu(_nn(h, wzb_ref[...]))).astype(jnp.bfloat16)
    gm = jax.nn.sigmoid(_nn(h, wgm_ref[...]))
    merged = gm[:, 0:D_MODEL] * _nn(ya, wa_ref[...]) + gm[:, D_MODEL:] * _nn(yb, wb_ref[...])
    out_ref[0] = x + _nn(merged.astype(jnp.bfloat16), wo_ref[...])


def _out_proj(x, norm_w, oa, ob, wz, wzb, wgm, wa, wb, wo):
    B, S, _ = x.shape
    nw = norm_w.reshape(1, D_MODEL)
    tok = lambda w: pl.BlockSpec((1, TM, w), lambda b, i: (b, i, 0))
    full = lambda a: pl.BlockSpec(a.shape, lambda b, i: (0,) * a.ndim, pipeline_mode=pl.Buffered(1))
    return pl.pallas_call(
        _out_proj_body,
        grid=(B, S // TM),
        in_specs=[tok(D_MODEL), full(nw), tok(512), tok(512), full(wz), full(wzb), full(wgm),
                  full(wa), full(wb), full(wo)],
        out_specs=tok(D_MODEL),
        out_shape=jax.ShapeDtypeStruct((B, S, D_MODEL), jnp.float32),
        compiler_params=pltpu.CompilerParams(dimension_semantics=("parallel", "parallel"),
                                             vmem_limit_bytes=VMEM_LIMIT),
        name="out_proj",
    )(x, nw, oa, ob, wz, wzb, wgm, wa, wb, wo)


def _pack_in_weights(w_in, q_norm_a, k_norm_a, q_norm_b, k_norm_sel, k_norm_win):
    bf = jnp.bfloat16
    ng = 3 * NSA_HEADS
    cols = lambda a, b: lax.slice_in_dim(w_in, a, b, axis=2).astype(bf).reshape(D_MODEL, b - a)
    z_b = _C_GATE + ng
    w = (cols(0, _C_GATE),
         jnp.pad(cols(_C_GATE, z_b), ((0, 0), (0, LANES - ng))),
         cols(z_b, z_b + NSA_HEADS * HEAD_DIM),
         cols(z_b + NSA_HEADS * HEAD_DIM, w_in.shape[2]))
    ones = lambda n: jnp.ones((n,), jnp.float32)
    gain = jnp.concatenate([jnp.tile(q_norm_a, MOBA_HEADS), jnp.tile(k_norm_a, MOBA_HEADS), ones(1024),
                            jnp.tile(q_norm_b, NSA_HEADS), ones(256), jnp.tile(k_norm_sel, NSA_GROUPS), ones(128),
                            jnp.tile(k_norm_win, NSA_GROUPS), ones(256)]).reshape(1, _N1)
    return w, gain


def kernel(x, norm_w, w_in, q_norm_a, k_norm_a, q_norm_b, k_norm_cmp, k_norm_sel, k_norm_win, cmp_pos_k, cmp_w1_k, cmp_w2_k, cmp_pos_v, cmp_w1_v, cmp_w2_v, rel_bias, w_branch_a, w_branch_b, w_out):
    bf = jnp.bfloat16
    (w_main, w_gate, w_zb, w_gm), gain = _pack_in_weights(w_in, q_norm_a[0], k_norm_a[0], q_norm_b[0],
                                                           k_norm_sel[0], k_norm_win[0])
    bias = _bias_tiles(rel_bias)
    qa, kae, kao, vae, vao, qb, kmean, kc, vc, kse, kso, vsl, kwn, vwn, gate = _in_proj(x, norm_w[0], w_main, w_gate,
                                                                                        gain)
    kcmp, vcmp = _compress(kc, vc, cmp_pos_k[0], cmp_w1_k[0], cmp_w2_k[0],
                           cmp_pos_v[0], cmp_w1_v[0], cmp_w2_v[0], k_norm_cmp[0])
    oa = _moba(qa, kae, kao, vae, vao, kmean[:, :, 0, :], bias)
    ob = _nsa(qb, kcmp, vcmp, kse, kso, vsl, kwn, vwn, gate, bias)
    return _out_proj(x, norm_w[0], oa, ob, w_main, w_zb, w_gm, w_branch_a[0].astype(bf), w_branch_b[0].astype(bf),
                     w_out[0].astype(bf))
```

```python
import math

import jax
import jax.numpy as jnp
import numpy as np
from jax import lax
from jax.experimental import pallas as pl
from jax.experimental.pallas import tpu as pltpu

D_MODEL = 1024
HEAD_DIM = 64
MOBA_HEADS = 8
NSA_HEADS = 8
NSA_GROUPS = 2
NSA_HPG = NSA_HEADS // NSA_GROUPS
MOBA_BLOCK = 256
MOBA_TOPK = 3
CMP_LEN = 32
CMP_STRIDE = 16
CMP_HIDDEN = 256
SEL_BLOCK = 64
SEL_TOPN = 16
WINDOW = 512
NUM_BUCKETS = 32
MAX_DISTANCE = 1024
EPS = 1e-6
NEG = -1e30

LANES = 128
TQ = 256
TK = 2 * TQ
TR = 2 * TQ
TM = 2 * TQ
MOBA_PAIRS = 4
PROJ_CHAINS = 2
N_FAR = 5
N_WIN_TILES = WINDOW // TQ + 1
VMEM_LIMIT = 56 * 1024 * 1024
LOG2E = math.log2(math.e)
QSCALE = LOG2E / math.sqrt(HEAD_DIM)

_K_WIN = N_FAR + 1
_K_MASKED = _K_WIN + N_WIN_TILES
_N_TILE_KINDS = _K_MASKED + 1

_C_QA, _C_KA, _C_VA, _C_ZA, _C_QB = 0, 512, 1024, 1536, 2048
_C_KC, _C_VC, _C_KSL, _C_VSL, _C_KWN, _C_VWN, _C_GATE = 2560, 2688, 2816, 2944, 3072, 3200, 3328
_N1 = _C_GATE + LANES


def _nt(a, b):
    return lax.dot_general(a, b, (((1,), (1,)), ((), ())), preferred_element_type=jnp.float32)


def _nn(a, b):
    return jnp.dot(a, b, preferred_element_type=jnp.float32)


def _split(x):
    hi = x.astype(jnp.bfloat16)
    lo = (x - hi.astype(jnp.float32)).astype(jnp.bfloat16)
    return hi, lo


def _bucket_table(n):
    d = np.arange(n)
    nf = np.maximum(d, NUM_BUCKETS // 2).astype(np.float64)
    large = NUM_BUCKETS // 2 + (np.log(nf / (NUM_BUCKETS // 2)) / math.log(MAX_DISTANCE / (NUM_BUCKETS // 2))
                                * (NUM_BUCKETS - NUM_BUCKETS // 2)).astype(np.int64)
    return np.where(d < NUM_BUCKETS // 2, d, np.minimum(large, NUM_BUCKETS - 1))


def _bias_onehot():
    oh = np.zeros((40, _N_TILE_KINDS * 2 * TQ), np.float32)
    buckets = _bucket_table((N_FAR + 2) * TQ + 1)
    for t in range(_N_TILE_KINDS):
        window = _K_WIN <= t < _K_MASKED
        delta = t - _K_WIN if window else t
        for u in range(2 * TQ):
            d = TQ * delta + TQ - u
            masked = t == _K_MASKED or d < 0 or (window and d >= WINDOW)
            oh[NUM_BUCKETS if masked else buckets[d], t * 2 * TQ + u] = 1.0
    return oh


def _bias_body(relt_ref, oh_ref, out_ref):
    g = jnp.dot(jnp.broadcast_to(relt_ref[0], (8, 40)), oh_ref[...], preferred_element_type=jnp.float32,
                precision=lax.Precision.HIGHEST)
    for t in range(_N_TILE_KINDS):
        full = jnp.broadcast_to(g[0:1, t * 2 * TQ:(t + 1) * 2 * TQ], (TQ, 2 * TQ))
        rolled = pltpu.roll(full, TQ, 1, stride=1, stride_axis=0)
        out_ref[0, t] = rolled[:, :TQ]


def _bias_tiles(rel_bias):
    nh = MOBA_HEADS + NSA_HEADS
    relt = jnp.concatenate([rel_bias.T.astype(jnp.float32) * LOG2E, jnp.full((nh, 1), NEG, jnp.float32),
                            jnp.zeros((nh, 7), jnp.float32)], axis=1).reshape(nh, 1, 40)
    oh = jnp.asarray(_bias_onehot())
    return pl.pallas_call(
        _bias_body,
        grid=(nh,),
        in_specs=[pl.BlockSpec((1, 1, 40), lambda h: (h, 0, 0)), pl.BlockSpec(oh.shape, lambda h: (0, 0))],
        out_specs=pl.BlockSpec((1, _N_TILE_KINDS, TQ, TQ), lambda h: (h, 0, 0, 0)),
        out_shape=jax.ShapeDtypeStruct((nh, _N_TILE_KINDS, TQ, TQ), jnp.float32),
        compiler_params=pltpu.CompilerParams(dimension_semantics=("parallel",), vmem_limit_bytes=VMEM_LIMIT),
        name="bias_tiles",
    )(relt, oh)


def _seg_norm(x, lo_half, gain):
    x2 = x * x
    ss_lo = jnp.sum(jnp.where(lo_half, x2, 0.0), axis=-1, keepdims=True)
    ss_hi = jnp.sum(jnp.where(lo_half, 0.0, x2), axis=-1, keepdims=True)
    ss = jnp.where(lo_half, ss_lo, ss_hi)
    return x * lax.rsqrt(ss * (1.0 / HEAD_DIM) + EPS) * gain


def _in_proj_body(x_ref, nw_ref, w_ref, wg_ref, gain_ref, *refs):
    outs, stages = refs[:-2], refs[-2:]
    for c in range(PROJ_CHAINS):
        rows = pl.ds(c * TM, TM)
        tok = lambda r: r.at[:, rows, :]
        grp = lambda r: r.at[:, :, rows, :]
        chunk = lambda r: r.at[:, :, pl.ds(c * (TM // CMP_STRIDE), TM // CMP_STRIDE), :]
        (qa, kae, kao, vae, vao, qb, kmean, kc, vc, kse, kso, vsl, kwn, vwn, gate) = outs
        _in_proj_tile(pl.program_id(1) * PROJ_CHAINS + c, tok(x_ref), nw_ref, w_ref, wg_ref, gain_ref,
                      tok(qa), tok(kae), tok(kao), tok(vae), tok(vao), tok(qb),
                      kmean.at[:, pl.ds(c * (TM // MOBA_BLOCK), TM // MOBA_BLOCK)], chunk(kc), chunk(vc),
                      grp(kse), grp(kso), grp(vsl), grp(kwn), grp(vwn), tok(gate),
                      stages[0].at[c], stages[1].at[c])


def _in_proj_tile(i, x_ref, nw_ref, w_ref, wg_ref, gain_ref,
                  qa_ref, kae_ref, kao_ref, vae_ref, vao_ref, qb_ref, kmean_ref, kc_ref, vc_ref,
                  kse_ref, kso_ref, vsl_ref, kwn_ref, vwn_ref, gate_ref, stage_k_ref, stage_v_ref):
    x = x_ref[0]
    ms = jnp.mean(x * x, axis=-1, keepdims=True)
    h = (x * lax.rsqrt(ms + EPS) * nw_ref[...]).astype(jnp.bfloat16)
    bf = jnp.bfloat16
    lane = lax.broadcasted_iota(jnp.int32, (TM, LANES), 1)
    row = lax.broadcasted_iota(jnp.int32, (TM, LANES), 0)
    lo_half = lane < HEAD_DIM
    blk = i * (TM // MOBA_BLOCK) + lax.shift_right_logical(row, MOBA_BLOCK.bit_length() - 1)
    sblk = i * (TM // SEL_BLOCK) + lax.shift_right_logical(row, SEL_BLOCK.bit_length() - 1)

    def normed(col, width, scale=None):
        p = _nn(h, w_ref[:, col:col + width])
        outs = []
        for s in range(width // LANES):
            c = col + s * LANES
            y = _seg_norm(p[:, s * LANES:(s + 1) * LANES], lo_half, gain_ref[:, c:c + LANES])
            outs.append(y if scale is None else y * scale)
        return outs

    qa = normed(_C_QA, 512, QSCALE)
    for s in range(4):
        qa_ref[0, :, s * LANES:(s + 1) * LANES] = qa[s].astype(bf)
    ka = normed(_C_KA, 512)
    oh_hi = jnp.where(lane == HEAD_DIM + blk, 1.0, 0.0)
    oh_lo = jnp.where(lane == blk, 1.0, 0.0)
    for s in range(4):
        kae_ref[0, :, s * LANES:(s + 1) * LANES] = jnp.where(lo_half, ka[s], oh_hi).astype(bf)
        kao_ref[0, :, s * LANES:(s + 1) * LANES] = jnp.where(lo_half, oh_lo, ka[s]).astype(bf)
        for j in range(TM // MOBA_BLOCK):
            kmean_ref[0, j, :, s * LANES:(s + 1) * LANES] = jnp.broadcast_to(
                jnp.mean(ka[s][j * MOBA_BLOCK:(j + 1) * MOBA_BLOCK], axis=0, keepdims=True), (8, LANES))
    one_hi = jnp.where(lane == HEAD_DIM, 1.0, 0.0)
    one_lo = jnp.where(lane == 0, 1.0, 0.0)
    va = _nn(h, w_ref[:, _C_VA:_C_VA + 512])
    for s in range(4):
        vs = va[:, s * LANES:(s + 1) * LANES]
        vae_ref[0, :, s * LANES:(s + 1) * LANES] = jnp.where(lo_half, vs, one_hi).astype(bf)
        vao_ref[0, :, s * LANES:(s + 1) * LANES] = jnp.where(lo_half, one_lo, vs).astype(bf)
    qb = normed(_C_QB, 512, QSCALE)
    for s in range(4):
        qb_ref[0, :, s * LANES:(s + 1) * LANES] = qb[s].astype(bf)

    rest = _nn(h, w_ref[:, _C_KC:_C_GATE])
    gate_ref[0] = jnp.concatenate([_nn(h[0:TM // 2], wg_ref[...]), _nn(h[TM // 2:], wg_ref[...])], axis=0)

    def slab(c):
        return rest[:, c - _C_KC:c - _C_KC + LANES]

    def chunk_store(ref, stage_ref, y):
        stage_ref[...] = y
        lo = lax.broadcasted_iota(jnp.int32, (TM // CMP_STRIDE, LANES), 1) < HEAD_DIM
        for u in range(CMP_STRIDE // 2):
            t0 = stage_ref[pl.ds(2 * u, TM // CMP_STRIDE, stride=CMP_STRIDE), :]
            t1 = stage_ref[pl.ds(2 * u + 1, TM // CMP_STRIDE, stride=CMP_STRIDE), :]
            ref[0, 0, :, u * LANES:(u + 1) * LANES] = jnp.where(lo, t0, pltpu.roll(t1, HEAD_DIM, 1)).astype(bf)
            ref[0, 1, :, u * LANES:(u + 1) * LANES] = jnp.where(lo, pltpu.roll(t0, HEAD_DIM, 1), t1).astype(bf)

    chunk_store(kc_ref, stage_k_ref, slab(_C_KC))
    chunk_store(vc_ref, stage_v_ref, slab(_C_VC))

    def dup_store(ref, y):
        r = pltpu.roll(y, HEAD_DIM, 1)
        ref[0, 0] = jnp.where(lo_half, y, r).astype(bf)
        ref[0, 1] = jnp.where(lo_half, r, y).astype(bf)

    y = _seg_norm(slab(_C_KSL), lo_half, gain_ref[:, _C_KSL:_C_KSL + LANES])
    r = pltpu.roll(y, HEAD_DIM, 1)
    oh = jnp.where((lane & (HEAD_DIM - 1)) == sblk, 1.0, 0.0)
    kse_ref[0, 0] = jnp.where(lo_half, y, oh).astype(bf)
    kso_ref[0, 0] = jnp.where(lo_half, oh, r).astype(bf)
    kse_ref[0, 1] = jnp.where(lo_half, r, oh).astype(bf)
    kso_ref[0, 1] = jnp.where(lo_half, oh, y).astype(bf)
    def aug_store(ref, y):
        ref[0, 0] = jnp.where(lo_half, y, one_hi).astype(bf)
        ref[0, 1] = jnp.where(lo_half, pltpu.roll(y, HEAD_DIM, 1), one_hi).astype(bf)

    aug_store(vsl_ref, slab(_C_VSL))
    dup_store(kwn_ref, _seg_norm(slab(_C_KWN), lo_half, gain_ref[:, _C_KWN:_C_KWN + LANES]))
    aug_store(vwn_ref, slab(_C_VWN))


def _in_proj(x, norm_w, w1, wg, gain_row):
    B, S, _ = x.shape
    nt = S // TQ
    bf = jnp.bfloat16
    tok = lambda w, dt: jax.ShapeDtypeStruct((B, S, w), dt)
    grp = jax.ShapeDtypeStruct((B, NSA_GROUPS, S, LANES), bf)
    rows = PROJ_CHAINS * TM
    tok_spec = lambda w: pl.BlockSpec((1, rows, w), lambda b, i: (b, i, 0))
    grp_spec = pl.BlockSpec((1, NSA_GROUPS, rows, LANES), lambda b, i: (b, 0, i, 0))
    flat = CMP_STRIDE * HEAD_DIM
    chunk = jax.ShapeDtypeStruct((B, NSA_GROUPS, S // CMP_STRIDE, flat), bf)
    chunk_spec = pl.BlockSpec((1, NSA_GROUPS, rows // CMP_STRIDE, flat), lambda b, i: (b, 0, i, 0))
    full = lambda a: pl.BlockSpec(a.shape, lambda b, i: (0,) * a.ndim, pipeline_mode=pl.Buffered(1))
    nw = norm_w.reshape(1, D_MODEL)
    return pl.pallas_call(
        _in_proj_body,
        grid=(B, S // rows),
        in_specs=[tok_spec(D_MODEL), full(nw), full(w1), full(wg), full(gain_row)],
        out_specs=[tok_spec(512), tok_spec(512), tok_spec(512), tok_spec(512), tok_spec(512), tok_spec(512),
                   pl.BlockSpec((1, rows // MOBA_BLOCK, 8, 512), lambda b, i: (b, i, 0, 0)),
                   chunk_spec, chunk_spec,
                   grp_spec, grp_spec, grp_spec, grp_spec, grp_spec, tok_spec(LANES)],
        out_shape=[tok(512, bf), tok(512, bf), tok(512, bf), tok(512, bf), tok(512, bf), tok(512, bf),
                   jax.ShapeDtypeStruct((B, nt, 8, 512), jnp.float32),
                   chunk, chunk, grp, grp, grp, grp, grp, tok(LANES, jnp.float32)],
        scratch_shapes=[pltpu.VMEM((PROJ_CHAINS, TM, LANES), jnp.float32)] * 2,
        compiler_params=pltpu.CompilerParams(dimension_semantics=("parallel", "parallel"),
                                             vmem_limit_bytes=VMEM_LIMIT),
        name="in_proj",
    )(x, nw, w1, wg, gain_row)


def _gelu_tanh(x):
    return 0.5 * x * (1.0 + jnp.tanh(math.sqrt(2.0 / math.pi) * (x + 0.044715 * (x * x * x))))


def _compress_body(xk_ref, xv_ref, w1k_ref, w1v_ref, pk_ref, pv_ref,
                   w2k_ref, w2v_ref, gk_ref, kcmp_ref, vcmp_ref):
    nchunk = xk_ref.shape[2]
    half = CMP_STRIDE * HEAD_DIM

    def branch(x_ref, w1_ref, pos_ref, w2_ref, gain):
        pos = jnp.broadcast_to(pos_ref[...], (8, CMP_LEN * HEAD_DIM))
        posw = (_nn(pos[:, 0:half], w1_ref[...])[0:1, 0:CMP_HIDDEN]
                + _nn(pos[:, half:], w1_ref[...])[0:1, CMP_HIDDEN:])
        outs = []
        for g in range(NSA_GROUPS):
            ab = _nn(x_ref[0, g], w1_ref[...])
            top, bot = ab[:, 0:CMP_HIDDEN], ab[:, CMP_HIDDEN:]
            hid = top + pltpu.roll(bot, nchunk - 1, 0) + posw
            y = _nn(_gelu_tanh(hid).astype(jnp.bfloat16), w2_ref[...])
            if gain is not None:
                ms = jnp.sum(y * y, axis=-1, keepdims=True) * (0.5 / HEAD_DIM)
                y = y * lax.rsqrt(ms + EPS) * gain
            else:
                lane = lax.broadcasted_iota(jnp.int32, y.shape, 1)
                y = jnp.where(lane < HEAD_DIM, y, jnp.where(lane == HEAD_DIM, 1.0, 0.0))
            outs.append(y.astype(jnp.bfloat16))
        return outs

    k0, k1 = branch(xk_ref, w1k_ref, pk_ref, w2k_ref, gk_ref[...])
    kcmp_ref[0, 0], kcmp_ref[0, 1] = k0, k1
    v0, v1 = branch(xv_ref, w1v_ref, pv_ref, w2v_ref, None)
    vcmp_ref[0, 0], vcmp_ref[0, 1] = v0, v1


def _compress(xk, xv, pos_k, w1_k, w2_k, pos_v, w1_v, w2_v, gain_cmp):
    B, _, nchunk, half = xk.shape
    bf = jnp.bfloat16
    halves = lambda w1: jnp.concatenate([w1[:half], w1[half:]], axis=1).astype(bf)
    args = [xk, xv, halves(w1_k), halves(w1_v),
            pos_k.reshape(1, -1).astype(bf), pos_v.reshape(1, -1).astype(bf),
            jnp.concatenate([w2_k, w2_k], axis=1).astype(bf), jnp.concatenate([w2_v, w2_v], axis=1).astype(bf),
            jnp.concatenate([gain_cmp, gain_cmp]).reshape(1, LANES)]
    x_spec = pl.BlockSpec((1, NSA_GROUPS, nchunk, half), lambda b: (b, 0, 0, 0))
    full = lambda a: pl.BlockSpec(a.shape, lambda b: (0,) * a.ndim)
    out = jax.ShapeDtypeStruct((B, NSA_GROUPS, nchunk, LANES), bf)
    o_spec = pl.BlockSpec((1, NSA_GROUPS, nchunk, LANES), lambda b: (b, 0, 0, 0))
    return pl.pallas_call(
        _compress_body,
        grid=(B,),
        in_specs=[x_spec, x_spec] + [full(a) for a in args[2:]],
        out_specs=[o_spec, o_spec],
        out_shape=[out, out],
        compiler_params=pltpu.CompilerParams(dimension_semantics=("parallel",), vmem_limit_bytes=VMEM_LIMIT),
        name="compress",
    )(*args)


def _softmax_step(s, v_top, v_bot, m_ref, acc_ref, first):
    blocks = [s[:, c * LANES:(c + 1) * LANES] for c in range(s.shape[1] // LANES)]
    bmax = blocks[0]
    for blk in blocks[1:]:
        bmax = jnp.maximum(bmax, blk)
    m_new = jnp.broadcast_to(jnp.max(bmax, axis=-1, keepdims=True), m_ref.shape)
    if not first:
        m_old = m_ref[...]
        m_new = jnp.maximum(m_old, m_new)
        alpha = jnp.exp2(m_old - m_new)
    pb = jnp.concatenate([jnp.exp2(blk - m_new) for blk in blocks], axis=1).astype(jnp.bfloat16)
    half = s.shape[0] // 2
    pv = jnp.concatenate([_nn(pb[0:half], v_top), _nn(pb[half:], v_bot)], axis=0)
    acc_ref[...] = pv if first else alpha * acc_ref[...] + pv
    m_ref[...] = m_new


def _finish(acc, ones_lane):
    lane = lax.broadcasted_iota(jnp.int32, acc.shape, 1)
    return acc / jnp.sum(jnp.where(lane == ones_lane, acc, 0.0), axis=-1, keepdims=True)


def _tile_kind(delta):
    return jnp.where(delta >= 0, jnp.minimum(delta, N_FAR), _K_MASKED)


def _moba_body(q_ref, ke_ref, ko_ref, ve_ref, vo_ref, kmean_ref, eye_ref, bias_ref, o_ref, m_ref, acc_ref):
    i = pl.program_id(2)
    nblk = kmean_ref.shape[1]
    lane = lax.broadcasted_iota(jnp.int32, (TR, LANES), 1)
    lo_half = lane < HEAD_DIM
    lane16 = lax.broadcasted_iota(jnp.int32, (nblk, LANES), 1)
    jrow = lax.broadcasted_iota(jnp.int32, (nblk, TR), 0)
    qblk = 2 * i + jnp.where(lax.broadcasted_iota(jnp.int32, (nblk, TR), 1) >= TQ, 1, 0)
    pad = jnp.zeros((HEAD_DIM - nblk, TR), jnp.float32)
    zero = jnp.zeros((TR, LANES), jnp.bfloat16)

    def augmented_q(pr):
        q2 = q_ref[0, :, pr * LANES:(pr + 1) * LANES]
        kmean2 = kmean_ref[0, :, pr * LANES:(pr + 1) * LANES]
        flags = []
        for hh in range(2):
            inhead16 = (lane16 < HEAD_DIM) if hh == 0 else (lane16 >= HEAD_DIM)
            km_hi, km_lo = _split(jnp.where(inhead16, kmean2, 0.0))
            sc = jnp.where(jrow < qblk, _nt(km_hi, q2) + _nt(km_lo, q2), -jnp.inf)
            flag = jnp.where(jrow == qblk, 0.0, 1.0)
            for _ in range(MOBA_TOPK):
                best = jnp.max(sc, axis=0, keepdims=True)
                first = jnp.min(jnp.where(sc == best, jrow, nblk), axis=0, keepdims=True)
                hit = jrow == first
                flag = jnp.where(hit, 0.0, flag)
                sc = jnp.where(hit, -jnp.inf, sc)
            flags.append(flag)
        placed = jnp.concatenate([flags[1], pad, flags[0], pad], axis=0).astype(jnp.bfloat16)
        maskpart = jnp.concatenate([_nt(eye_ref[...], placed[:, a * TQ:(a + 1) * TQ]) for a in range(TR // TQ)],
                                   axis=0) * NEG
        q2f = q2.astype(jnp.float32)
        return jnp.concatenate(
            [jnp.concatenate([jnp.where(lo_half, q2f, maskpart).astype(jnp.bfloat16), zero], axis=1),
             jnp.concatenate([zero, jnp.where(lo_half, maskpart, q2f).astype(jnp.bfloat16)], axis=1)], axis=0)

    qqs = [augmented_q(pr) for pr in range(MOBA_PAIRS)]

    def step(c, first):
        rows = pl.ds(pl.multiple_of(c * TK, TK), TK)
        d = 2 * (i - c)
        for pr in range(MOBA_PAIRS):
            cols = slice(pr * LANES, (pr + 1) * LANES)
            kk = jnp.concatenate([ke_ref[0, rows, cols], ko_ref[0, rows, cols]], axis=1)
            b = jnp.concatenate(
                [jnp.concatenate([bias_ref[2 * pr + hh, _tile_kind(d + a - t)] for t in range(2)], axis=1)
                 for hh in range(2) for a in range(2)], axis=0)
            _softmax_step(_nt(qqs[pr], kk) + b, ve_ref[0, rows, cols], vo_ref[0, rows, cols],
                          m_ref.at[pr], acc_ref.at[pr], first)

    step(i, True)

    def body(c, carry):
        step(c, False)
        return carry

    lax.fori_loop(0, i, body, 0)
    for pr in range(MOBA_PAIRS):
        o_ref[0, :, pr * LANES:(pr + 1) * LANES] = jnp.where(
            lo_half, _finish(acc_ref[pr, 0:TR], HEAD_DIM), _finish(acc_ref[pr, TR:2 * TR], 0))


def _eye(n):
    return jnp.asarray(np.eye(n, dtype=np.float32), jnp.bfloat16)


def _moba(qa, kae, kao, vae, vao, kmean, bias):
    B, S, _ = qa.shape
    nt = S // TQ
    width = MOBA_PAIRS * LANES
    kv_spec = pl.BlockSpec((1, S, width), lambda b, p, i: (b, 0, p), pipeline_mode=pl.Buffered(1))
    return pl.pallas_call(
        _moba_body,
        grid=(B, MOBA_HEADS // (2 * MOBA_PAIRS), S // TR),
        in_specs=[pl.BlockSpec((1, TR, width), lambda b, p, i: (b, i, p)),
                  kv_spec, kv_spec, kv_spec, kv_spec,
                  pl.BlockSpec((1, nt, width), lambda b, p, i: (b, 0, p)),
                  pl.BlockSpec((TQ, TQ), lambda b, p, i: (0, 0)),
                  pl.BlockSpec((2 * MOBA_PAIRS, _N_TILE_KINDS, TQ, TQ), lambda b, p, i: (p, 0, 0, 0),
                               pipeline_mode=pl.Buffered(1))],
        out_specs=pl.BlockSpec((1, TR, width), lambda b, p, i: (b, i, p)),
        out_shape=jax.ShapeDtypeStruct((B, S, MOBA_HEADS * HEAD_DIM), jnp.float32),
        scratch_shapes=[pltpu.VMEM((MOBA_PAIRS, 2 * TR, LANES), jnp.float32)] * 2,
        compiler_params=pltpu.CompilerParams(dimension_semantics=("parallel", "parallel", "arbitrary"),
                                             vmem_limit_bytes=VMEM_LIMIT),
        name="moba",
    )(qa, kae, kao, vae, vao, kmean, _eye(TQ), bias)


def _nsa_body(q_ref, kcmp_ref, vcmp_ref, kse_ref, kso_ref, vsl_ref, kwn_ref, vwn_ref, gate_ref, ov_ref, eye_ref,
              gexp_ref, bias_ref, o_ref, m_ref, acc_ref):
    i = pl.program_id(1)

    def to_hi(o):
        return pltpu.roll(o, HEAD_DIM, 1)

    ncmp = kcmp_ref.shape[2]
    nsel = ov_ref.shape[0]
    R = NSA_HPG * TR
    bf = jnp.bfloat16
    lane = lax.broadcasted_iota(jnp.int32, (TR, LANES), 1)
    lo_half = lane < HEAD_DIM
    sel_heads = (0, 2, 1, 3)

    def prepare(g):
        qf = [q_ref[0, :, (2 * g + s) * LANES:(2 * g + s + 1) * LANES].astype(jnp.float32) for s in range(2)]
        q4 = jnp.concatenate([jnp.where(lo_half if hh % 2 == 0 else ~lo_half, qf[hh // 2], 0.0).astype(bf)
                              for hh in range(NSA_HPG)], axis=0)

        tpos = i * TR + lax.broadcasted_iota(jnp.int32, (TR, ncmp), 0)
        cidx = lax.broadcasted_iota(jnp.int32, (TR, ncmp), 1)
        cbias1 = jnp.where((cidx * CMP_STRIDE + (CMP_LEN - 1) <= tpos) & (cidx < ncmp - 1), 0.0, NEG)
        lc = _nt(q4, kcmp_ref[0, g]) + jnp.concatenate([cbias1] * NSA_HPG, axis=0)
        cblocks = [lc[:, c * LANES:(c + 1) * LANES] for c in range(ncmp // LANES)]
        cmax = cblocks[0]
        for blk in cblocks[1:]:
            cmax = jnp.maximum(cmax, blk)
        cm = jnp.maximum(jnp.broadcast_to(jnp.max(cmax, axis=-1, keepdims=True), (R, LANES)), 0.5 * NEG)
        ce = [jnp.exp2(blk - cm) for blk in cblocks]
        cden = ce[0]
        for eb in ce[1:]:
            cden = cden + eb
        cden = jnp.broadcast_to(jnp.sum(cden, axis=-1, keepdims=True), (R, LANES))
        crcp = 1.0 / jnp.where(cden > 0.0, cden, 1.0)
        pc = jnp.concatenate([eb * crcp for eb in ce], axis=1)
        pcb = pc.astype(bf)
        o_cmp = jnp.concatenate([_nn(pcb[0:R // 2], vcmp_ref[0, g]), _nn(pcb[R // 2:], vcmp_ref[0, g])], axis=0)
        psum = pc[0:TR] + pc[TR:2 * TR] + pc[2 * TR:3 * TR] + pc[3 * TR:4 * TR]
        p_hi, p_lo = _split(psum)
        ov = ov_ref[...]
        imp = _nt(ov, p_hi) + _nt(ov, p_lo)

        o_win = []
        for a in range(2):
            qt = 2 * i + a
            t0 = jnp.maximum(qt - (N_WIN_TILES - 1), 0)
            wstart = pl.multiple_of(t0 * TQ, TQ)
            qh = jnp.concatenate([q4[hh * TR + a * TQ:hh * TR + (a + 1) * TQ] for hh in range(NSA_HPG)], axis=0)
            kinds = []
            for r in range(N_WIN_TILES):
                dd = qt - (t0 + r)
                kinds.append(jnp.where(dd >= 0, _K_WIN + dd, _K_MASKED))
            bw = jnp.concatenate([jnp.concatenate([bias_ref[NSA_HPG * g + hh, k] for k in kinds], axis=1)
                                  for hh in range(NSA_HPG)], axis=0)
            sw = _nt(qh, kwn_ref[0, g, pl.ds(wstart, N_WIN_TILES * TQ), :]) + bw
            wm, wacc = m_ref.at[g, 0:NSA_HPG * TQ], acc_ref.at[g, 0:NSA_HPG * TQ]
            vw = vwn_ref[0, g, pl.ds(wstart, N_WIN_TILES * TQ), :]
            _softmax_step(sw, vw, vw, wm, wacc, True)
            o_win.append(_finish(wacc[...], HEAD_DIM))

        jrow = lax.broadcasted_iota(jnp.int32, (nsel, TR), 0)
        qblk = lax.shift_right_logical(i * TR + lax.broadcasted_iota(jnp.int32, (nsel, TR), 1),
                                       SEL_BLOCK.bit_length() - 1)
        cand = (jrow >= 1) & (jrow < qblk)
        keys = jnp.where(cand, imp, -1.0)
        flag = jnp.where((jrow == 0) | (jrow == qblk), 0.0, 1.0)
        for _ in range(SEL_TOPN - 2):
            best = jnp.max(keys, axis=0, keepdims=True)
            first = jnp.min(jnp.where(keys == best, jrow, nsel), axis=0, keepdims=True)
            hit = jrow == first
            flag = jnp.where(hit, 0.0, flag)
            keys = jnp.where(hit, -1.0, keys)
        if nsel < HEAD_DIM:
            flag = jnp.concatenate([flag, jnp.zeros((HEAD_DIM - nsel, TR), jnp.float32)], axis=0)
        flag2 = jnp.concatenate([flag, flag], axis=0).astype(bf)
        maskpart = jnp.concatenate([_nt(eye_ref[...], flag2[:, a * TQ:(a + 1) * TQ]) for a in range(TR // TQ)],
                                   axis=0) * NEG

        zero = jnp.zeros((2 * TR, LANES), bf)
        qa2 = [jnp.concatenate([jnp.where(lo_half if par == 0 else ~lo_half, qf[s], maskpart).astype(bf)
                                for s in range(2)], axis=0) for par in range(2)]
        qq = jnp.concatenate([jnp.concatenate([qa2[0], zero], axis=1),
                              jnp.concatenate([zero, qa2[1]], axis=1)], axis=0)
        return qq, o_cmp, o_win

    state = [prepare(g) for g in range(NSA_GROUPS)]

    def sel_step(c, first):
        start = pl.multiple_of(c * TK, TK)
        d = 2 * (i - c)
        for g in range(NSA_GROUPS):
            kk = jnp.concatenate([kse_ref[0, g, pl.ds(start, TK), :], kso_ref[0, g, pl.ds(start, TK), :]], axis=1)
            b = jnp.concatenate(
                [jnp.concatenate([bias_ref[NSA_HPG * g + h, _tile_kind(d + a - t)] for t in range(2)], axis=1)
                 for h in sel_heads for a in range(2)], axis=0)
            v = vsl_ref[0, g, pl.ds(start, TK), :]
            _softmax_step(_nt(state[g][0], kk) + b, v, v, m_ref.at[g], acc_ref.at[g], first)

    sel_step(i, True)

    def sel_body(c, carry):
        sel_step(c, False)
        return carry

    lax.fori_loop(0, i, sel_body, 0)

    gate = jax.nn.sigmoid(gate_ref[0])
    g_hi, g_lo = _split(gate)
    for g in range(NSA_GROUPS):
        _, o_cmp, o_win = state[g]
        o_all = _finish(acc_ref[g], HEAD_DIM)
        o_sel = [o_all[0:2 * TR], to_hi(o_all[2 * TR:4 * TR])]
        gx = _nn(g_hi, gexp_ref[g]) + _nn(g_lo, gexp_ref[g])
        for s in range(2):
            h_even, h_odd = 2 * s, 2 * s + 1
            cmp2 = jnp.where(lo_half, o_cmp[h_even * TR:(h_even + 1) * TR],
                             to_hi(o_cmp[h_odd * TR:(h_odd + 1) * TR]))
            sel2 = jnp.where(lo_half, o_sel[0][s * TR:(s + 1) * TR], o_sel[1][s * TR:(s + 1) * TR])
            win2 = jnp.where(lo_half,
                             jnp.concatenate([o_win[a][h_even * TQ:(h_even + 1) * TQ] for a in range(2)], axis=0),
                             to_hi(jnp.concatenate([o_win[a][h_odd * TQ:(h_odd + 1) * TQ] for a in range(2)], axis=0)))
            base = s * 3 * LANES
            o_ref[0, :, (2 * g + s) * LANES:(2 * g + s + 1) * LANES] = (
                gx[:, base:base + LANES] * cmp2 + gx[:, base + LANES:base + 2 * LANES] * sel2
                + gx[:, base + 2 * LANES:base + 3 * LANES] * win2)


def _gate_expand():
    e = np.zeros((NSA_GROUPS, LANES, 2 * 3 * LANES), np.float32)
    for g in range(NSA_GROUPS):
        for s in range(2):
            for c in range(3):
                for half in range(2):
                    h = g * NSA_HPG + 2 * s + half
                    col = (s * 3 + c) * LANES + half * HEAD_DIM
                    e[g, 3 * h + c, col:col + HEAD_DIM] = 1.0
    return e


def _nsa(qb, kcmp, vcmp, kse, kso, vsl, kwn, vwn, gate, bias):
    B, S, _ = qb.shape
    nt = S // TQ
    ncmp = kcmp.shape[2]
    nsel = S // SEL_BLOCK
    bf = jnp.bfloat16
    cs = np.arange(ncmp) * CMP_STRIDE
    bs = np.arange(nsel) * SEL_BLOCK
    ov = ((cs[None, :] < bs[:, None] + SEL_BLOCK) & (cs[None, :] + CMP_LEN > bs[:, None])
          & (np.arange(ncmp)[None, :] < ncmp - 1)).astype(np.float32)
    once = pl.Buffered(1)
    grp = lambda n: pl.BlockSpec((1, NSA_GROUPS, n, LANES), lambda b, i: (b, 0, 0, 0), pipeline_mode=once)
    width = NSA_HEADS * HEAD_DIM
    return pl.pallas_call(
        _nsa_body,
        grid=(B, S // TR),
        in_specs=[pl.BlockSpec((1, TR, width), lambda b, i: (b, i, 0)),
                  grp(ncmp), grp(ncmp), grp(S), grp(S), grp(S), grp(S), grp(S),
                  pl.BlockSpec((1, TR, LANES), lambda b, i: (b, i, 0)),
                  pl.BlockSpec((nsel, ncmp), lambda b, i: (0, 0), pipeline_mode=once),
                  pl.BlockSpec((TQ, TQ), lambda b, i: (0, 0), pipeline_mode=once),
                  pl.BlockSpec((NSA_GROUPS, LANES, 6 * LANES), lambda b, i: (0, 0, 0), pipeline_mode=once),
                  pl.BlockSpec((NSA_HEADS, _N_TILE_KINDS, TQ, TQ), lambda b, i: (1, 0, 0, 0), pipeline_mode=once)],
        out_specs=pl.BlockSpec((1, TR, width), lambda b, i: (b, i, 0)),
        out_shape=jax.ShapeDtypeStruct((B, S, width), jnp.float32),
        scratch_shapes=[pltpu.VMEM((NSA_GROUPS, NSA_HPG * TR, LANES), jnp.float32)] * 2,
        compiler_params=pltpu.CompilerParams(dimension_semantics=("parallel", "arbitrary"),
                                             vmem_limit_bytes=VMEM_LIMIT),
        name="nsa",
    )(qb, kcmp, vcmp, kse, kso, vsl, kwn, vwn, gate, jnp.asarray(ov, bf), _eye(TQ),
      jnp.asarray(_gate_expand(), bf), bias)


def _out_proj_body(x_ref, nw_ref, oa_ref, ob_ref, wz_ref, wzb_ref, wgm_ref, wa_ref, wb_ref, wo_ref, out_ref):
    for c in range(PROJ_CHAINS):
        rows = pl.ds(c * TM, TM)
        x = x_ref[0, rows, :]
        ms = jnp.mean(x * x, axis=-1, keepdims=True)
        h = (x * lax.rsqrt(ms + EPS) * nw_ref[...]).astype(jnp.bfloat16)
        ya = (oa_ref[0, rows, :] * jax.nn.silu(_nn(h, wz_ref[:, _C_ZA:_C_ZA + 512]))).astype(jnp.bfloat16)
        yb = (ob_ref[0, rows, :] * jax.nn.silu(_nn(h, wzb_ref[...]))).astype(jnp.bfloat16)
        gm = jax.nn.sigmoid(_nn(h, wgm_ref[...]))
        merged = gm[:, 0:D_MODEL] * _nn(ya, wa_ref[...]) + gm[:, D_MODEL:] * _nn(yb, wb_ref[...])
        out_ref[0, rows, :] = x + _nn(merged.astype(jnp.bfloat16), wo_ref[...])


def _out_proj(x, norm_w, oa, ob, wz, wzb, wgm, wa, wb, wo):
    B, S, _ = x.shape
    nw = norm_w.reshape(1, D_MODEL)
    tok = lambda w: pl.BlockSpec((1, PROJ_CHAINS * TM, w), lambda b, i: (b, i, 0))
    full = lambda a: pl.BlockSpec(a.shape, lambda b, i: (0,) * a.ndim, pipeline_mode=pl.Buffered(1))
    return pl.pallas_call(
        _out_proj_body,
        grid=(B, S // (PROJ_CHAINS * TM)),
        in_specs=[tok(D_MODEL), full(nw), tok(512), tok(512), full(wz), full(wzb), full(wgm),
                  full(wa), full(wb), full(wo)],
        out_specs=tok(D_MODEL),
        out_shape=jax.ShapeDtypeStruct((B, S, D_MODEL), jnp.float32),
        compiler_params=pltpu.CompilerParams(dimension_semantics=("parallel", "parallel"),
                                             vmem_limit_bytes=VMEM_LIMIT),
        name="out_proj",
    )(x, nw, oa, ob, wz, wzb, wgm, wa, wb, wo)


def _pack_in_weights(w_in, q_norm_a, k_norm_a, q_norm_b, k_norm_sel, k_norm_win):
    bf = jnp.bfloat16
    ng = 3 * NSA_HEADS
    cols = lambda a, b: lax.slice_in_dim(w_in, a, b, axis=2).astype(bf).reshape(D_MODEL, b - a)
    z_b = _C_GATE + ng
    w = (cols(0, _C_GATE),
         jnp.pad(cols(_C_GATE, z_b), ((0, 0), (0, LANES - ng))),
         cols(z_b, z_b + NSA_HEADS * HEAD_DIM),
         cols(z_b + NSA_HEADS * HEAD_DIM, w_in.shape[2]))
    ones = lambda n: jnp.ones((n,), jnp.float32)
    gain = jnp.concatenate([jnp.tile(q_norm_a, MOBA_HEADS), jnp.tile(k_norm_a, MOBA_HEADS), ones(1024),
                            jnp.tile(q_norm_b, NSA_HEADS), ones(256), jnp.tile(k_norm_sel, NSA_GROUPS), ones(128),
                            jnp.tile(k_norm_win, NSA_GROUPS), ones(256)]).reshape(1, _N1)
    return w, gain


def kernel(x, norm_w, w_in, q_norm_a, k_norm_a, q_norm_b, k_norm_cmp, k_norm_sel, k_norm_win, cmp_pos_k, cmp_w1_k, cmp_w2_k, cmp_pos_v, cmp_w1_v, cmp_w2_v, rel_bias, w_branch_a, w_branch_b, w_out):
    bf = jnp.bfloat16
    (w_main, w_gate, w_zb, w_gm), gain = _pack_in_weights(w_in, q_norm_a[0], k_norm_a[0], q_norm_b[0],
                                                           k_norm_sel[0], k_norm_win[0])
    bias = _bias_tiles(rel_bias)
    qa, kae, kao, vae, vao, qb, kmean, kc, vc, kse, kso, vsl, kwn, vwn, gate = _in_proj(x, norm_w[0], w_main, w_gate,
                                                                                        gain)
    kcmp, vcmp = _compress(kc, vc, cmp_pos_k[0], cmp_w1_k[0], cmp_w2_k[0],
                           cmp_pos_v[0], cmp_w1_v[0], cmp_w2_v[0], k_norm_cmp[0])
    oa = _moba(qa, kae, kao, vae, vao, kmean[:, :, 0, :], bias)
    ob = _nsa(qb, kcmp, vcmp, kse, kso, vsl, kwn, vwn, gate, bias)
    return _out_proj(x, norm_w[0], oa, ob, w_main, w_zb, w_gm, w_branch_a[0].astype(bf), w_branch_b[0].astype(bf),
                     w_out[0].astype(bf))
```

```python
import math

import jax
import jax.numpy as jnp
import numpy as np
from jax import lax
from jax.experimental import pallas as pl
from jax.experimental.pallas import tpu as pltpu

D_MODEL = 1024
HEAD_DIM = 64
MOBA_HEADS = 8
NSA_HEADS = 8
NSA_GROUPS = 2
NSA_HPG = NSA_HEADS // NSA_GROUPS
MOBA_BLOCK = 256
MOBA_TOPK = 3
CMP_LEN = 32
CMP_STRIDE = 16
CMP_HIDDEN = 256
SEL_BLOCK = 64
SEL_TOPN = 16
WINDOW = 512
NUM_BUCKETS = 32
MAX_DISTANCE = 1024
EPS = 1e-6
NEG = -1e30

LANES = 128
TQ = 256
TK = 2 * TQ
TR = 2 * TQ
TM = 2 * TQ
MOBA_PAIRS = 4
PROJ_CHAINS = 2
N_FAR = 5
N_WIN_TILES = WINDOW // TQ + 1
VMEM_LIMIT = 56 * 1024 * 1024
LOG2E = math.log2(math.e)
QSCALE = LOG2E / math.sqrt(HEAD_DIM)

_K_WIN = N_FAR + 1
_K_MASKED = _K_WIN + N_WIN_TILES
_N_TILE_KINDS = _K_MASKED + 1

_C_QA, _C_KA, _C_VA, _C_ZA, _C_QB = 0, 512, 1024, 1536, 2048
_C_KC, _C_VC, _C_KSL, _C_VSL, _C_KWN, _C_VWN, _C_GATE = 2560, 2688, 2816, 2944, 3072, 3200, 3328
_N1 = _C_GATE + LANES


def _nt(a, b):
    return lax.dot_general(a, b, (((1,), (1,)), ((), ())), preferred_element_type=jnp.float32)


def _nn(a, b):
    return jnp.dot(a, b, preferred_element_type=jnp.float32)


def _split(x):
    hi = x.astype(jnp.bfloat16)
    lo = (x - hi.astype(jnp.float32)).astype(jnp.bfloat16)
    return hi, lo


def _bucket_table(n):
    d = np.arange(n)
    nf = np.maximum(d, NUM_BUCKETS // 2).astype(np.float64)
    large = NUM_BUCKETS // 2 + (np.log(nf / (NUM_BUCKETS // 2)) / math.log(MAX_DISTANCE / (NUM_BUCKETS // 2))
                                * (NUM_BUCKETS - NUM_BUCKETS // 2)).astype(np.int64)
    return np.where(d < NUM_BUCKETS // 2, d, np.minimum(large, NUM_BUCKETS - 1))


def _bias_onehot():
    oh = np.zeros((40, _N_TILE_KINDS * 2 * TQ), np.float32)
    buckets = _bucket_table((N_FAR + 2) * TQ + 1)
    for t in range(_N_TILE_KINDS):
        window = _K_WIN <= t < _K_MASKED
        delta = t - _K_WIN if window else t
        for u in range(2 * TQ):
            d = TQ * delta + TQ - u
            masked = t == _K_MASKED or d < 0 or (window and d >= WINDOW)
            oh[NUM_BUCKETS if masked else buckets[d], t * 2 * TQ + u] = 1.0
    return oh


def _bias_body(relt_ref, oh_ref, out_ref):
    g = jnp.dot(jnp.broadcast_to(relt_ref[0], (8, 40)), oh_ref[...], preferred_element_type=jnp.float32,
                precision=lax.Precision.HIGHEST)
    for t in range(_N_TILE_KINDS):
        full = jnp.broadcast_to(g[0:1, t * 2 * TQ:(t + 1) * 2 * TQ], (TQ, 2 * TQ))
        rolled = pltpu.roll(full, TQ, 1, stride=1, stride_axis=0)
        out_ref[0, t] = rolled[:, :TQ]


def _bias_tiles(rel_bias):
    nh = MOBA_HEADS + NSA_HEADS
    relt = jnp.concatenate([rel_bias.T.astype(jnp.float32) * LOG2E, jnp.full((nh, 1), NEG, jnp.float32),
                            jnp.zeros((nh, 7), jnp.float32)], axis=1).reshape(nh, 1, 40)
    oh = jnp.asarray(_bias_onehot())
    return pl.pallas_call(
        _bias_body,
        grid=(nh,),
        in_specs=[pl.BlockSpec((1, 1, 40), lambda h: (h, 0, 0)), pl.BlockSpec(oh.shape, lambda h: (0, 0))],
        out_specs=pl.BlockSpec((1, _N_TILE_KINDS, TQ, TQ), lambda h: (h, 0, 0, 0)),
        out_shape=jax.ShapeDtypeStruct((nh, _N_TILE_KINDS, TQ, TQ), jnp.float32),
        compiler_params=pltpu.CompilerParams(dimension_semantics=("parallel",), vmem_limit_bytes=VMEM_LIMIT),
        name="bias_tiles",
    )(relt, oh)


def _seg_norm(x, lo_half, gain):
    x2 = x * x
    ss_lo = jnp.sum(jnp.where(lo_half, x2, 0.0), axis=-1, keepdims=True)
    ss_hi = jnp.sum(jnp.where(lo_half, 0.0, x2), axis=-1, keepdims=True)
    ss = jnp.where(lo_half, ss_lo, ss_hi)
    return x * lax.rsqrt(ss * (1.0 / HEAD_DIM) + EPS) * gain


def _in_proj_body(x_ref, nw_ref, w_ref, wg_ref, gain_ref, *refs):
    outs, stages = refs[:-2], refs[-2:]
    for c in range(PROJ_CHAINS):
        rows = pl.ds(c * TM, TM)
        tok = lambda r: r.at[:, rows, :]
        grp = lambda r: r.at[:, :, rows, :]
        chunk = lambda r: r.at[:, :, pl.ds(c * (TM // CMP_STRIDE), TM // CMP_STRIDE), :]
        (qa, kae, kao, vae, vao, qb, kmean, kc, vc, kse, kso, vsl, kwn, vwn, gate) = outs
        _in_proj_tile(pl.program_id(1) * PROJ_CHAINS + c, tok(x_ref), nw_ref, w_ref, wg_ref, gain_ref,
                      tok(qa), tok(kae), tok(kao), tok(vae), tok(vao), tok(qb),
                      kmean.at[:, pl.ds(c * (TM // MOBA_BLOCK), TM // MOBA_BLOCK)], chunk(kc), chunk(vc),
                      grp(kse), grp(kso), grp(vsl), grp(kwn), grp(vwn), tok(gate),
                      stages[0].at[c], stages[1].at[c])


def _in_proj_tile(i, x_ref, nw_ref, w_ref, wg_ref, gain_ref,
                  qa_ref, kae_ref, kao_ref, vae_ref, vao_ref, qb_ref, kmean_ref, kc_ref, vc_ref,
                  kse_ref, kso_ref, vsl_ref, kwn_ref, vwn_ref, gate_ref, stage_k_ref, stage_v_ref):
    x = x_ref[0]
    ms = jnp.mean(x * x, axis=-1, keepdims=True)
    h = (x * lax.rsqrt(ms + EPS) * nw_ref[...]).astype(jnp.bfloat16)
    bf = jnp.bfloat16
    lane = lax.broadcasted_iota(jnp.int32, (TM, LANES), 1)
    row = lax.broadcasted_iota(jnp.int32, (TM, LANES), 0)
    lo_half = lane < HEAD_DIM
    blk = i * (TM // MOBA_BLOCK) + lax.shift_right_logical(row, MOBA_BLOCK.bit_length() - 1)
    sblk = i * (TM // SEL_BLOCK) + lax.shift_right_logical(row, SEL_BLOCK.bit_length() - 1)

    def normed(col, width, scale=None):
        p = _nn(h, w_ref[:, col:col + width])
        outs = []
        for s in range(width // LANES):
            c = col + s * LANES
            y = _seg_norm(p[:, s * LANES:(s + 1) * LANES], lo_half, gain_ref[:, c:c + LANES])
            outs.append(y if scale is None else y * scale)
        return outs

    qa = normed(_C_QA, 512, QSCALE)
    for s in range(4):
        qa_ref[0, :, s * LANES:(s + 1) * LANES] = qa[s].astype(bf)
    ka = normed(_C_KA, 512)
    oh_hi = jnp.where(lane == HEAD_DIM + blk, 1.0, 0.0)
    oh_lo = jnp.where(lane == blk, 1.0, 0.0)
    for s in range(4):
        kae_ref[0, :, s * LANES:(s + 1) * LANES] = jnp.where(lo_half, ka[s], oh_hi).astype(bf)
        kao_ref[0, :, s * LANES:(s + 1) * LANES] = jnp.where(lo_half, oh_lo, ka[s]).astype(bf)
        for j in range(TM // MOBA_BLOCK):
            kmean_ref[0, j, :, s * LANES:(s + 1) * LANES] = jnp.broadcast_to(
                jnp.mean(ka[s][j * MOBA_BLOCK:(j + 1) * MOBA_BLOCK], axis=0, keepdims=True), (8, LANES))
    one_hi = jnp.where(lane == HEAD_DIM, 1.0, 0.0)
    one_lo = jnp.where(lane == 0, 1.0, 0.0)
    va = _nn(h, w_ref[:, _C_VA:_C_VA + 512])
    for s in range(4):
        vs = va[:, s * LANES:(s + 1) * LANES]
        vae_ref[0, :, s * LANES:(s + 1) * LANES] = jnp.where(lo_half, vs, one_hi).astype(bf)
        vao_ref[0, :, s * LANES:(s + 1) * LANES] = jnp.where(lo_half, one_lo, vs).astype(bf)
    qb = normed(_C_QB, 512, QSCALE)
    for s in range(4):
        qb_ref[0, :, s * LANES:(s + 1) * LANES] = qb[s].astype(bf)

    rest = _nn(h, w_ref[:, _C_KC:_C_GATE])
    gate_ref[0] = jnp.concatenate([_nn(h[0:TM // 2], wg_ref[...]), _nn(h[TM // 2:], wg_ref[...])], axis=0)

    def slab(c):
        return rest[:, c - _C_KC:c - _C_KC + LANES]

    def chunk_store(ref, stage_ref, y):
        stage_ref[...] = y
        lo = lax.broadcasted_iota(jnp.int32, (TM // CMP_STRIDE, LANES), 1) < HEAD_DIM
        for u in range(CMP_STRIDE // 2):
            t0 = stage_ref[pl.ds(2 * u, TM // CMP_STRIDE, stride=CMP_STRIDE), :]
            t1 = stage_ref[pl.ds(2 * u + 1, TM // CMP_STRIDE, stride=CMP_STRIDE), :]
            ref[0, 0, :, u * LANES:(u + 1) * LANES] = jnp.where(lo, t0, pltpu.roll(t1, HEAD_DIM, 1)).astype(bf)
            ref[0, 1, :, u * LANES:(u + 1) * LANES] = jnp.where(lo, pltpu.roll(t0, HEAD_DIM, 1), t1).astype(bf)

    chunk_store(kc_ref, stage_k_ref, slab(_C_KC))
    chunk_store(vc_ref, stage_v_ref, slab(_C_VC))

    def dup_store(ref, y):
        r = pltpu.roll(y, HEAD_DIM, 1)
        ref[0, 0] = jnp.where(lo_half, y, r).astype(bf)
        ref[0, 1] = jnp.where(lo_half, r, y).astype(bf)

    y = _seg_norm(slab(_C_KSL), lo_half, gain_ref[:, _C_KSL:_C_KSL + LANES])
    r = pltpu.roll(y, HEAD_DIM, 1)
    oh = jnp.where((lane & (HEAD_DIM - 1)) == sblk, 1.0, 0.0)
    kse_ref[0, 0] = jnp.where(lo_half, y, oh).astype(bf)
    kso_ref[0, 0] = jnp.where(lo_half, oh, r).astype(bf)
    kse_ref[0, 1] = jnp.where(lo_half, r, oh).astype(bf)
    kso_ref[0, 1] = jnp.where(lo_half, oh, y).astype(bf)
    def aug_store(ref, y):
        ref[0, 0] = jnp.where(lo_half, y, one_hi).astype(bf)
        ref[0, 1] = jnp.where(lo_half, pltpu.roll(y, HEAD_DIM, 1), one_hi).astype(bf)

    aug_store(vsl_ref, slab(_C_VSL))
    dup_store(kwn_ref, _seg_norm(slab(_C_KWN), lo_half, gain_ref[:, _C_KWN:_C_KWN + LANES]))
    aug_store(vwn_ref, slab(_C_VWN))


def _in_proj(x, norm_w, w1, wg, gain_row):
    B, S, _ = x.shape
    nt = S // TQ
    bf = jnp.bfloat16
    tok = lambda w, dt: jax.ShapeDtypeStruct((B, S, w), dt)
    grp = jax.ShapeDtypeStruct((B, NSA_GROUPS, S, LANES), bf)
    rows = PROJ_CHAINS * TM
    tok_spec = lambda w: pl.BlockSpec((1, rows, w), lambda b, i: (b, i, 0))
    grp_spec = pl.BlockSpec((1, NSA_GROUPS, rows, LANES), lambda b, i: (b, 0, i, 0))
    flat = CMP_STRIDE * HEAD_DIM
    chunk = jax.ShapeDtypeStruct((B, NSA_GROUPS, S // CMP_STRIDE, flat), bf)
    chunk_spec = pl.BlockSpec((1, NSA_GROUPS, rows // CMP_STRIDE, flat), lambda b, i: (b, 0, i, 0))
    full = lambda a: pl.BlockSpec(a.shape, lambda b, i: (0,) * a.ndim, pipeline_mode=pl.Buffered(1))
    nw = norm_w.reshape(1, D_MODEL)
    return pl.pallas_call(
        _in_proj_body,
        grid=(B, S // rows),
        in_specs=[tok_spec(D_MODEL), full(nw), full(w1), full(wg), full(gain_row)],
        out_specs=[tok_spec(512), tok_spec(512), tok_spec(512), tok_spec(512), tok_spec(512), tok_spec(512),
                   pl.BlockSpec((1, rows // MOBA_BLOCK, 8, 512), lambda b, i: (b, i, 0, 0)),
                   chunk_spec, chunk_spec,
                   grp_spec, grp_spec, grp_spec, grp_spec, grp_spec, tok_spec(LANES)],
        out_shape=[tok(512, bf), tok(512, bf), tok(512, bf), tok(512, bf), tok(512, bf), tok(512, bf),
                   jax.ShapeDtypeStruct((B, nt, 8, 512), jnp.float32),
                   chunk, chunk, grp, grp, grp, grp, grp, tok(LANES, jnp.float32)],
        scratch_shapes=[pltpu.VMEM((PROJ_CHAINS, TM, LANES), jnp.float32)] * 2,
        compiler_params=pltpu.CompilerParams(dimension_semantics=("parallel", "parallel"),
                                             vmem_limit_bytes=VMEM_LIMIT),
        name="in_proj",
    )(x, nw, w1, wg, gain_row)


def _gelu_tanh(x):
    return 0.5 * x * (1.0 + jnp.tanh(math.sqrt(2.0 / math.pi) * (x + 0.044715 * (x * x * x))))


def _compress_body(xk_ref, xv_ref, w1k_ref, w1v_ref, pk_ref, pv_ref,
                   w2k_ref, w2v_ref, gk_ref, kcmp_ref, vcmp_ref):
    nchunk = xk_ref.shape[2]
    half = CMP_STRIDE * HEAD_DIM

    def branch(x_ref, w1_ref, pos_ref, w2_ref, gain):
        pos = jnp.broadcast_to(pos_ref[...], (8, CMP_LEN * HEAD_DIM))
        posw = (_nn(pos[:, 0:half], w1_ref[...])[0:1, 0:CMP_HIDDEN]
                + _nn(pos[:, half:], w1_ref[...])[0:1, CMP_HIDDEN:])
        outs = []
        for g in range(NSA_GROUPS):
            ab = _nn(x_ref[0, g], w1_ref[...])
            top, bot = ab[:, 0:CMP_HIDDEN], ab[:, CMP_HIDDEN:]
            hid = top + pltpu.roll(bot, nchunk - 1, 0) + posw
            y = _nn(_gelu_tanh(hid).astype(jnp.bfloat16), w2_ref[...])
            if gain is not None:
                ms = jnp.sum(y * y, axis=-1, keepdims=True) * (0.5 / HEAD_DIM)
                y = y * lax.rsqrt(ms + EPS) * gain
            else:
                lane = lax.broadcasted_iota(jnp.int32, y.shape, 1)
                y = jnp.where(lane < HEAD_DIM, y, jnp.where(lane == HEAD_DIM, 1.0, 0.0))
            outs.append(y.astype(jnp.bfloat16))
        return outs

    k0, k1 = branch(xk_ref, w1k_ref, pk_ref, w2k_ref, gk_ref[...])
    kcmp_ref[0, 0], kcmp_ref[0, 1] = k0, k1
    v0, v1 = branch(xv_ref, w1v_ref, pv_ref, w2v_ref, None)
    vcmp_ref[0, 0], vcmp_ref[0, 1] = v0, v1


def _compress(xk, xv, pos_k, w1_k, w2_k, pos_v, w1_v, w2_v, gain_cmp):
    B, _, nchunk, half = xk.shape
    bf = jnp.bfloat16
    halves = lambda w1: jnp.concatenate([w1[:half], w1[half:]], axis=1).astype(bf)
    args = [xk, xv, halves(w1_k), halves(w1_v),
            pos_k.reshape(1, -1).astype(bf), pos_v.reshape(1, -1).astype(bf),
            jnp.concatenate([w2_k, w2_k], axis=1).astype(bf), jnp.concatenate([w2_v, w2_v], axis=1).astype(bf),
            jnp.concatenate([gain_cmp, gain_cmp]).reshape(1, LANES)]
    x_spec = pl.BlockSpec((1, NSA_GROUPS, nchunk, half), lambda b: (b, 0, 0, 0))
    full = lambda a: pl.BlockSpec(a.shape, lambda b: (0,) * a.ndim)
    out = jax.ShapeDtypeStruct((B, NSA_GROUPS, nchunk, LANES), bf)
    o_spec = pl.BlockSpec((1, NSA_GROUPS, nchunk, LANES), lambda b: (b, 0, 0, 0))
    return pl.pallas_call(
        _compress_body,
        grid=(B,),
        in_specs=[x_spec, x_spec] + [full(a) for a in args[2:]],
        out_specs=[o_spec, o_spec],
        out_shape=[out, out],
        compiler_params=pltpu.CompilerParams(dimension_semantics=("parallel",), vmem_limit_bytes=VMEM_LIMIT),
        name="compress",
    )(*args)


def _softmax_step(s, v_top, v_bot, m_ref, acc_ref, first):
    blocks = [s[:, c * LANES:(c + 1) * LANES] for c in range(s.shape[1] // LANES)]
    bmax = blocks[0]
    for blk in blocks[1:]:
        bmax = jnp.maximum(bmax, blk)
    m_new = jnp.broadcast_to(jnp.max(bmax, axis=-1, keepdims=True), m_ref.shape)
    if not first:
        m_old = m_ref[...]
        m_new = jnp.maximum(m_old, m_new)
        alpha = jnp.exp2(m_old - m_new)
    pb = jnp.concatenate([jnp.exp2(blk - m_new) for blk in blocks], axis=1).astype(jnp.bfloat16)
    half = s.shape[0] // 2
    pv = jnp.concatenate([_nn(pb[0:half], v_top), _nn(pb[half:], v_bot)], axis=0)
    acc_ref[...] = pv if first else alpha * acc_ref[...] + pv
    m_ref[...] = m_new


def _diag_chunk(qq, kk, v_top, v_bot, bias_of, m_ref, acc_ref):
    nh = qq.shape[0] // TR
    q_lo = jnp.concatenate([qq[h * TR:h * TR + TQ] for h in range(nh)], axis=0)
    q_hi = jnp.concatenate([qq[h * TR + TQ:(h + 1) * TR] for h in range(nh)], axis=0)
    s_lo = _nt(q_lo, kk[0:TQ]) + jnp.concatenate([bias_of(h, 0) for h in range(nh)], axis=0)
    s_hi = _nt(q_hi, kk) + jnp.concatenate(
        [jnp.concatenate([bias_of(h, 1), bias_of(h, 0)], axis=1) for h in range(nh)], axis=0)
    for s, vt, vb, off in ((s_lo, v_top[0:TQ], v_bot[0:TQ], 0), (s_hi, v_top, v_bot, TQ)):
        blocks = [s[:, c * LANES:(c + 1) * LANES] for c in range(s.shape[1] // LANES)]
        bmax = blocks[0]
        for blk in blocks[1:]:
            bmax = jnp.maximum(bmax, blk)
        m = jnp.broadcast_to(jnp.max(bmax, axis=-1, keepdims=True), (nh * TQ, LANES))
        pb = jnp.concatenate([jnp.exp2(blk - m) for blk in blocks], axis=1).astype(jnp.bfloat16)
        half = nh * TQ // 2
        pv = jnp.concatenate([_nn(pb[0:half], vt), _nn(pb[half:], vb)], axis=0)
        for h in range(nh):
            m_ref[h * TR + off:h * TR + off + TQ] = m[h * TQ:(h + 1) * TQ]
            acc_ref[h * TR + off:h * TR + off + TQ] = pv[h * TQ:(h + 1) * TQ]


def _finish(acc, ones_lane):
    lane = lax.broadcasted_iota(jnp.int32, acc.shape, 1)
    return acc / jnp.sum(jnp.where(lane == ones_lane, acc, 0.0), axis=-1, keepdims=True)


def _tile_kind(delta):
    return jnp.where(delta >= 0, jnp.minimum(delta, N_FAR), _K_MASKED)


def _moba_body(q_ref, ke_ref, ko_ref, ve_ref, vo_ref, kmean_ref, eye_ref, bias_ref, o_ref, m_ref, acc_ref):
    i = pl.program_id(2)
    nblk = kmean_ref.shape[1]
    lane = lax.broadcasted_iota(jnp.int32, (TR, LANES), 1)
    lo_half = lane < HEAD_DIM
    lane16 = lax.broadcasted_iota(jnp.int32, (nblk, LANES), 1)
    jrow = lax.broadcasted_iota(jnp.int32, (nblk, TR), 0)
    qblk = 2 * i + jnp.where(lax.broadcasted_iota(jnp.int32, (nblk, TR), 1) >= TQ, 1, 0)
    pad = jnp.zeros((HEAD_DIM - nblk, TR), jnp.float32)
    zero = jnp.zeros((TR, LANES), jnp.bfloat16)

    def augmented_q(pr):
        q2 = q_ref[0, :, pr * LANES:(pr + 1) * LANES]
        kmean2 = kmean_ref[0, :, pr * LANES:(pr + 1) * LANES]
        flags = []
        for hh in range(2):
            inhead16 = (lane16 < HEAD_DIM) if hh == 0 else (lane16 >= HEAD_DIM)
            km_hi, km_lo = _split(jnp.where(inhead16, kmean2, 0.0))
            sc = jnp.where(jrow < qblk, _nt(km_hi, q2) + _nt(km_lo, q2), -jnp.inf)
            flag = jnp.where(jrow == qblk, 0.0, 1.0)
            for _ in range(MOBA_TOPK):
                best = jnp.max(sc, axis=0, keepdims=True)
                first = jnp.min(jnp.where(sc == best, jrow, nblk), axis=0, keepdims=True)
                hit = jrow == first
                flag = jnp.where(hit, 0.0, flag)
                sc = jnp.where(hit, -jnp.inf, sc)
            flags.append(flag)
        placed = jnp.concatenate([flags[1], pad, flags[0], pad], axis=0).astype(jnp.bfloat16)
        maskpart = jnp.concatenate([_nt(eye_ref[...], placed[:, a * TQ:(a + 1) * TQ]) for a in range(TR // TQ)],
                                   axis=0) * NEG
        q2f = q2.astype(jnp.float32)
        return jnp.concatenate(
            [jnp.concatenate([jnp.where(lo_half, q2f, maskpart).astype(jnp.bfloat16), zero], axis=1),
             jnp.concatenate([zero, jnp.where(lo_half, maskpart, q2f).astype(jnp.bfloat16)], axis=1)], axis=0)

    qqs = [augmented_q(pr) for pr in range(MOBA_PAIRS)]

    def operands(c, pr):
        rows = pl.ds(pl.multiple_of(c * TK, TK), TK)
        cols = slice(pr * LANES, (pr + 1) * LANES)
        kk = jnp.concatenate([ke_ref[0, rows, cols], ko_ref[0, rows, cols]], axis=1)
        return kk, ve_ref[0, rows, cols], vo_ref[0, rows, cols]

    for pr in range(MOBA_PAIRS):
        kk, ve, vo = operands(i, pr)
        _diag_chunk(qqs[pr], kk, ve, vo, lambda hh, delta, pr=pr: bias_ref[2 * pr + hh, delta],
                    m_ref.at[pr], acc_ref.at[pr])

    def body(c, carry):
        d = 2 * (i - c)
        for pr in range(MOBA_PAIRS):
            kk, ve, vo = operands(c, pr)
            b = jnp.concatenate(
                [jnp.concatenate([bias_ref[2 * pr + hh, _tile_kind(d + a - t)] for t in range(2)], axis=1)
                 for hh in range(2) for a in range(2)], axis=0)
            _softmax_step(_nt(qqs[pr], kk) + b, ve, vo, m_ref.at[pr], acc_ref.at[pr], False)
        return carry

    lax.fori_loop(0, i, body, 0)
    for pr in range(MOBA_PAIRS):
        o_ref[0, :, pr * LANES:(pr + 1) * LANES] = jnp.where(
            lo_half, _finish(acc_ref[pr, 0:TR], HEAD_DIM), _finish(acc_ref[pr, TR:2 * TR], 0))


def _eye(n):
    return jnp.asarray(np.eye(n, dtype=np.float32), jnp.bfloat16)


def _moba(qa, kae, kao, vae, vao, kmean, bias):
    B, S, _ = qa.shape
    nt = S // TQ
    width = MOBA_PAIRS * LANES
    kv_spec = pl.BlockSpec((1, S, width), lambda b, p, i: (b, 0, p), pipeline_mode=pl.Buffered(1))
    return pl.pallas_call(
        _moba_body,
        grid=(B, MOBA_HEADS // (2 * MOBA_PAIRS), S // TR),
        in_specs=[pl.BlockSpec((1, TR, width), lambda b, p, i: (b, i, p)),
                  kv_spec, kv_spec, kv_spec, kv_spec,
                  pl.BlockSpec((1, nt, width), lambda b, p, i: (b, 0, p)),
                  pl.BlockSpec((TQ, TQ), lambda b, p, i: (0, 0)),
                  pl.BlockSpec((2 * MOBA_PAIRS, _N_TILE_KINDS, TQ, TQ), lambda b, p, i: (p, 0, 0, 0),
                               pipeline_mode=pl.Buffered(1))],
        out_specs=pl.BlockSpec((1, TR, width), lambda b, p, i: (b, i, p)),
        out_shape=jax.ShapeDtypeStruct((B, S, MOBA_HEADS * HEAD_DIM), jnp.float32),
        scratch_shapes=[pltpu.VMEM((MOBA_PAIRS, 2 * TR, LANES), jnp.float32)] * 2,
        compiler_params=pltpu.CompilerParams(dimension_semantics=("parallel", "parallel", "arbitrary"),
                                             vmem_limit_bytes=VMEM_LIMIT),
        name="moba",
    )(qa, kae, kao, vae, vao, kmean, _eye(TQ), bias)


def _nsa_body(q_ref, kcmp_ref, vcmp_ref, kse_ref, kso_ref, vsl_ref, kwn_ref, vwn_ref, gate_ref, ov_ref, eye_ref,
              gexp_ref, bias_ref, o_ref, m_ref, acc_ref):
    i = pl.program_id(1)

    def to_hi(o):
        return pltpu.roll(o, HEAD_DIM, 1)

    ncmp = kcmp_ref.shape[2]
    nsel = ov_ref.shape[0]
    R = NSA_HPG * TR
    bf = jnp.bfloat16
    lane = lax.broadcasted_iota(jnp.int32, (TR, LANES), 1)
    lo_half = lane < HEAD_DIM
    sel_heads = (0, 2, 1, 3)

    def prepare(g):
        qf = [q_ref[0, :, (2 * g + s) * LANES:(2 * g + s + 1) * LANES].astype(jnp.float32) for s in range(2)]
        q4 = jnp.concatenate([jnp.where(lo_half if hh % 2 == 0 else ~lo_half, qf[hh // 2], 0.0).astype(bf)
                              for hh in range(NSA_HPG)], axis=0)

        tpos = i * TR + lax.broadcasted_iota(jnp.int32, (TR, ncmp), 0)
        cidx = lax.broadcasted_iota(jnp.int32, (TR, ncmp), 1)
        cbias1 = jnp.where((cidx * CMP_STRIDE + (CMP_LEN - 1) <= tpos) & (cidx < ncmp - 1), 0.0, NEG)
        lc = _nt(q4, kcmp_ref[0, g]) + jnp.concatenate([cbias1] * NSA_HPG, axis=0)
        cblocks = [lc[:, c * LANES:(c + 1) * LANES] for c in range(ncmp // LANES)]
        cmax = cblocks[0]
        for blk in cblocks[1:]:
            cmax = jnp.maximum(cmax, blk)
        cm = jnp.maximum(jnp.broadcast_to(jnp.max(cmax, axis=-1, keepdims=True), (R, LANES)), 0.5 * NEG)
        ce = [jnp.exp2(blk - cm) for blk in cblocks]
        cden = ce[0]
        for eb in ce[1:]:
            cden = cden + eb
        cden = jnp.broadcast_to(jnp.sum(cden, axis=-1, keepdims=True), (R, LANES))
        crcp = 1.0 / jnp.where(cden > 0.0, cden, 1.0)
        pc = jnp.concatenate([eb * crcp for eb in ce], axis=1)
        pcb = pc.astype(bf)
        o_cmp = jnp.concatenate([_nn(pcb[0:R // 2], vcmp_ref[0, g]), _nn(pcb[R // 2:], vcmp_ref[0, g])], axis=0)
        psum = pc[0:TR] + pc[TR:2 * TR] + pc[2 * TR:3 * TR] + pc[3 * TR:4 * TR]
        p_hi, p_lo = _split(psum)
        ov = ov_ref[...]
        imp = _nt(ov, p_hi) + _nt(ov, p_lo)

        o_win = []
        for a in range(2):
            qt = 2 * i + a
            t0 = jnp.maximum(qt - (N_WIN_TILES - 1), 0)
            wstart = pl.multiple_of(t0 * TQ, TQ)
            qh = jnp.concatenate([q4[hh * TR + a * TQ:hh * TR + (a + 1) * TQ] for hh in range(NSA_HPG)], axis=0)
            kinds = []
            for r in range(N_WIN_TILES):
                dd = qt - (t0 + r)
                kinds.append(jnp.where(dd >= 0, _K_WIN + dd, _K_MASKED))
            bw = jnp.concatenate([jnp.concatenate([bias_ref[NSA_HPG * g + hh, k] for k in kinds], axis=1)
                                  for hh in range(NSA_HPG)], axis=0)
            sw = _nt(qh, kwn_ref[0, g, pl.ds(wstart, N_WIN_TILES * TQ), :]) + bw
            wm, wacc = m_ref.at[g, 0:NSA_HPG * TQ], acc_ref.at[g, 0:NSA_HPG * TQ]
            vw = vwn_ref[0, g, pl.ds(wstart, N_WIN_TILES * TQ), :]
            _softmax_step(sw, vw, vw, wm, wacc, True)
            o_win.append(_finish(wacc[...], HEAD_DIM))

        jrow = lax.broadcasted_iota(jnp.int32, (nsel, TR), 0)
        qblk = lax.shift_right_logical(i * TR + lax.broadcasted_iota(jnp.int32, (nsel, TR), 1),
                                       SEL_BLOCK.bit_length() - 1)
        cand = (jrow >= 1) & (jrow < qblk)
        keys = jnp.where(cand, imp, -1.0)
        flag = jnp.where((jrow == 0) | (jrow == qblk), 0.0, 1.0)
        for _ in range(SEL_TOPN - 2):
            best = jnp.max(keys, axis=0, keepdims=True)
            first = jnp.min(jnp.where(keys == best, jrow, nsel), axis=0, keepdims=True)
            hit = jrow == first
            flag = jnp.where(hit, 0.0, flag)
            keys = jnp.where(hit, -1.0, keys)
        if nsel < HEAD_DIM:
            flag = jnp.concatenate([flag, jnp.zeros((HEAD_DIM - nsel, TR), jnp.float32)], axis=0)
        flag2 = jnp.concatenate([flag, flag], axis=0).astype(bf)
        maskpart = jnp.concatenate([_nt(eye_ref[...], flag2[:, a * TQ:(a + 1) * TQ]) for a in range(TR // TQ)],
                                   axis=0) * NEG

        zero = jnp.zeros((2 * TR, LANES), bf)
        qa2 = [jnp.concatenate([jnp.where(lo_half if par == 0 else ~lo_half, qf[s], maskpart).astype(bf)
                                for s in range(2)], axis=0) for par in range(2)]
        qq = jnp.concatenate([jnp.concatenate([qa2[0], zero], axis=1),
                              jnp.concatenate([zero, qa2[1]], axis=1)], axis=0)
        return qq, o_cmp, o_win

    state = [prepare(g) for g in range(NSA_GROUPS)]

    def sel_operands(c, g):
        rows = pl.ds(pl.multiple_of(c * TK, TK), TK)
        kk = jnp.concatenate([kse_ref[0, g, rows, :], kso_ref[0, g, rows, :]], axis=1)
        return kk, vsl_ref[0, g, rows, :]

    for g in range(NSA_GROUPS):
        kk, v = sel_operands(i, g)
        _diag_chunk(state[g][0], kk, v, v, lambda hh, delta, g=g: bias_ref[NSA_HPG * g + sel_heads[hh], delta],
                    m_ref.at[g], acc_ref.at[g])

    def sel_body(c, carry):
        d = 2 * (i - c)
        for g in range(NSA_GROUPS):
            kk, v = sel_operands(c, g)
            b = jnp.concatenate(
                [jnp.concatenate([bias_ref[NSA_HPG * g + h, _tile_kind(d + a - t)] for t in range(2)], axis=1)
                 for h in sel_heads for a in range(2)], axis=0)
            _softmax_step(_nt(state[g][0], kk) + b, v, v, m_ref.at[g], acc_ref.at[g], False)
        return carry

    lax.fori_loop(0, i, sel_body, 0)

    gate = jax.nn.sigmoid(gate_ref[0])
    g_hi, g_lo = _split(gate)
    for g in range(NSA_GROUPS):
        _, o_cmp, o_win = state[g]
        o_all = _finish(acc_ref[g], HEAD_DIM)
        o_sel = [o_all[0:2 * TR], to_hi(o_all[2 * TR:4 * TR])]
        gx = _nn(g_hi, gexp_ref[g]) + _nn(g_lo, gexp_ref[g])
        for s in range(2):
            h_even, h_odd = 2 * s, 2 * s + 1
            cmp2 = jnp.where(lo_half, o_cmp[h_even * TR:(h_even + 1) * TR],
                             to_hi(o_cmp[h_odd * TR:(h_odd + 1) * TR]))
            sel2 = jnp.where(lo_half, o_sel[0][s * TR:(s + 1) * TR], o_sel[1][s * TR:(s + 1) * TR])
            win2 = jnp.where(lo_half,
                             jnp.concatenate([o_win[a][h_even * TQ:(h_even + 1) * TQ] for a in range(2)], axis=0),
                             to_hi(jnp.concatenate([o_win[a][h_odd * TQ:(h_odd + 1) * TQ] for a in range(2)], axis=0)))
            base = s * 3 * LANES
            o_ref[0, :, (2 * g + s) * LANES:(2 * g + s + 1) * LANES] = (
                gx[:, base:base + LANES] * cmp2 + gx[:, base + LANES:base + 2 * LANES] * sel2
                + gx[:, base + 2 * LANES:base + 3 * LANES] * win2)


def _gate_expand():
    e = np.zeros((NSA_GROUPS, LANES, 2 * 3 * LANES), np.float32)
    for g in range(NSA_GROUPS):
        for s in range(2):
            for c in range(3):
                for half in range(2):
                    h = g * NSA_HPG + 2 * s + half
                    col = (s * 3 + c) * LANES + half * HEAD_DIM
                    e[g, 3 * h + c, col:col + HEAD_DIM] = 1.0
    return e


def _nsa(qb, kcmp, vcmp, kse, kso, vsl, kwn, vwn, gate, bias):
    B, S, _ = qb.shape
    nt = S // TQ
    ncmp = kcmp.shape[2]
    nsel = S // SEL_BLOCK
    bf = jnp.bfloat16
    cs = np.arange(ncmp) * CMP_STRIDE
    bs = np.arange(nsel) * SEL_BLOCK
    ov = ((cs[None, :] < bs[:, None] + SEL_BLOCK) & (cs[None, :] + CMP_LEN > bs[:, None])
          & (np.arange(ncmp)[None, :] < ncmp - 1)).astype(np.float32)
    once = pl.Buffered(1)
    grp = lambda n: pl.BlockSpec((1, NSA_GROUPS, n, LANES), lambda b, i: (b, 0, 0, 0), pipeline_mode=once)
    width = NSA_HEADS * HEAD_DIM
    return pl.pallas_call(
        _nsa_body,
        grid=(B, S // TR),
        in_specs=[pl.BlockSpec((1, TR, width), lambda b, i: (b, i, 0)),
                  grp(ncmp), grp(ncmp), grp(S), grp(S), grp(S), grp(S), grp(S),
                  pl.BlockSpec((1, TR, LANES), lambda b, i: (b, i, 0)),
                  pl.BlockSpec((nsel, ncmp), lambda b, i: (0, 0), pipeline_mode=once),
                  pl.BlockSpec((TQ, TQ), lambda b, i: (0, 0), pipeline_mode=once),
                  pl.BlockSpec((NSA_GROUPS, LANES, 6 * LANES), lambda b, i: (0, 0, 0), pipeline_mode=once),
                  pl.BlockSpec((NSA_HEADS, _N_TILE_KINDS, TQ, TQ), lambda b, i: (1, 0, 0, 0), pipeline_mode=once)],
        out_specs=pl.BlockSpec((1, TR, width), lambda b, i: (b, i, 0)),
        out_shape=jax.ShapeDtypeStruct((B, S, width), jnp.float32),
        scratch_shapes=[pltpu.VMEM((NSA_GROUPS, NSA_HPG * TR, LANES), jnp.float32)] * 2,
        compiler_params=pltpu.CompilerParams(dimension_semantics=("parallel", "arbitrary"),
                                             vmem_limit_bytes=VMEM_LIMIT),
        name="nsa",
    )(qb, kcmp, vcmp, kse, kso, vsl, kwn, vwn, gate, jnp.asarray(ov, bf), _eye(TQ),
      jnp.asarray(_gate_expand(), bf), bias)


def _out_proj_body(x_ref, nw_ref, oa_ref, ob_ref, wz_ref, wzb_ref, wgm_ref, wa_ref, wb_ref, wo_ref, out_ref):
    for c in range(PROJ_CHAINS):
        rows = pl.ds(c * TM, TM)
        x = x_ref[0, rows, :]
        ms = jnp.mean(x * x, axis=-1, keepdims=True)
        h = (x * lax.rsqrt(ms + EPS) * nw_ref[...]).astype(jnp.bfloat16)
        ya = (oa_ref[0, rows, :] * jax.nn.silu(_nn(h, wz_ref[:, _C_ZA:_C_ZA + 512]))).astype(jnp.bfloat16)
        yb = (ob_ref[0, rows, :] * jax.nn.silu(_nn(h, wzb_ref[...]))).astype(jnp.bfloat16)
        gm = jax.nn.sigmoid(_nn(h, wgm_ref[...]))
        merged = gm[:, 0:D_MODEL] * _nn(ya, wa_ref[...]) + gm[:, D_MODEL:] * _nn(yb, wb_ref[...])
        out_ref[0, rows, :] = x + _nn(merged.astype(jnp.bfloat16), wo_ref[...])


def _out_proj(x, norm_w, oa, ob, wz, wzb, wgm, wa, wb, wo):
    B, S, _ = x.shape
    nw = norm_w.reshape(1, D_MODEL)
    tok = lambda w: pl.BlockSpec((1, PROJ_CHAINS * TM, w), lambda b, i: (b, i, 0))
    full = lambda a: pl.BlockSpec(a.shape, lambda b, i: (0,) * a.ndim, pipeline_mode=pl.Buffered(1))
    return pl.pallas_call(
        _out_proj_body,
        grid=(B, S // (PROJ_CHAINS * TM)),
        in_specs=[tok(D_MODEL), full(nw), tok(512), tok(512), full(wz), full(wzb), full(wgm),
                  full(wa), full(wb), full(wo)],
        out_specs=tok(D_MODEL),
        out_shape=jax.ShapeDtypeStruct((B, S, D_MODEL), jnp.float32),
        compiler_params=pltpu.CompilerParams(dimension_semantics=("parallel", "parallel"),
                                             vmem_limit_bytes=VMEM_LIMIT),
        name="out_proj",
    )(x, nw, oa, ob, wz, wzb, wgm, wa, wb, wo)


def _pack_in_weights(w_in, q_norm_a, k_norm_a, q_norm_b, k_norm_sel, k_norm_win):
    bf = jnp.bfloat16
    ng = 3 * NSA_HEADS
    cols = lambda a, b: lax.slice_in_dim(w_in, a, b, axis=2).astype(bf).reshape(D_MODEL, b - a)
    z_b = _C_GATE + ng
    w = (cols(0, _C_GATE),
         jnp.pad(cols(_C_GATE, z_b), ((0, 0), (0, LANES - ng))),
         cols(z_b, z_b + NSA_HEADS * HEAD_DIM),
         cols(z_b + NSA_HEADS * HEAD_DIM, w_in.shape[2]))
    ones = lambda n: jnp.ones((n,), jnp.float32)
    gain = jnp.concatenate([jnp.tile(q_norm_a, MOBA_HEADS), jnp.tile(k_norm_a, MOBA_HEADS), ones(1024),
                            jnp.tile(q_norm_b, NSA_HEADS), ones(256), jnp.tile(k_norm_sel, NSA_GROUPS), ones(128),
                            jnp.tile(k_norm_win, NSA_GROUPS), ones(256)]).reshape(1, _N1)
    return w, gain


def kernel(x, norm_w, w_in, q_norm_a, k_norm_a, q_norm_b, k_norm_cmp, k_norm_sel, k_norm_win, cmp_pos_k, cmp_w1_k, cmp_w2_k, cmp_pos_v, cmp_w1_v, cmp_w2_v, rel_bias, w_branch_a, w_branch_b, w_out):
    bf = jnp.bfloat16
    (w_main, w_gate, w_zb, w_gm), gain = _pack_in_weights(w_in, q_norm_a[0], k_norm_a[0], q_norm_b[0],
                                                           k_norm_sel[0], k_norm_win[0])
    bias = _bias_tiles(rel_bias)
    qa, kae, kao, vae, vao, qb, kmean, kc, vc, kse, kso, vsl, kwn, vwn, gate = _in_proj(x, norm_w[0], w_main, w_gate,
                                                                                        gain)
    kcmp, vcmp = _compress(kc, vc, cmp_pos_k[0], cmp_w1_k[0], cmp_w2_k[0],
                           cmp_pos_v[0], cmp_w1_v[0], cmp_w2_v[0], k_norm_cmp[0])
    oa = _moba(qa, kae, kao, vae, vao, kmean[:, :, 0, :], bias)
    ob = _nsa(qb, kcmp, vcmp, kse, kso, vsl, kwn, vwn, gate, bias)
    return _out_proj(x, norm_w[0], oa, ob, w_main, w_zb, w_gm, w_branch_a[0].astype(bf), w_branch_b[0].astype(bf),
                     w_out[0].astype(bf))
```

```python
import math

import jax
import jax.numpy as jnp
import numpy as np
from jax import lax
from jax.experimental import pallas as pl
from jax.experimental.pallas import tpu as pltpu

D_MODEL = 1024
HEAD_DIM = 64
MOBA_HEADS = 8
NSA_HEADS = 8
NSA_GROUPS = 2
NSA_HPG = NSA_HEADS // NSA_GROUPS
MOBA_BLOCK = 256
MOBA_TOPK = 3
CMP_LEN = 32
CMP_STRIDE = 16
CMP_HIDDEN = 256
SEL_BLOCK = 64
SEL_TOPN = 16
WINDOW = 512
NUM_BUCKETS = 32
MAX_DISTANCE = 1024
EPS = 1e-6
NEG = -1e30

LANES = 128
TQ = 256
TK = 2 * TQ
TR = 2 * TQ
TM = 2 * TQ
MOBA_PAIRS = 4
PROJ_CHAINS = 2
N_FAR = 5
N_WIN_TILES = WINDOW // TQ + 1
VMEM_LIMIT = 56 * 1024 * 1024
LOG2E = math.log2(math.e)
QSCALE = LOG2E / math.sqrt(HEAD_DIM)

_K_WIN = N_FAR + 1
_K_MASKED = _K_WIN + N_WIN_TILES
_N_TILE_KINDS = _K_MASKED + 1

_C_QA, _C_KA, _C_VA, _C_ZA, _C_QB = 0, 512, 1024, 1536, 2048
_C_KC, _C_VC, _C_KSL, _C_VSL, _C_KWN, _C_VWN, _C_GATE = 2560, 2688, 2816, 2944, 3072, 3200, 3328
_N1 = _C_GATE + LANES


def _nt(a, b):
    return lax.dot_general(a, b, (((1,), (1,)), ((), ())), preferred_element_type=jnp.float32)


def _nn(a, b):
    return jnp.dot(a, b, preferred_element_type=jnp.float32)


def _split(x):
    hi = x.astype(jnp.bfloat16)
    lo = (x - hi.astype(jnp.float32)).astype(jnp.bfloat16)
    return hi, lo


def _bucket_table(n):
    d = np.arange(n)
    nf = np.maximum(d, NUM_BUCKETS // 2).astype(np.float64)
    large = NUM_BUCKETS // 2 + (np.log(nf / (NUM_BUCKETS // 2)) / math.log(MAX_DISTANCE / (NUM_BUCKETS // 2))
                                * (NUM_BUCKETS - NUM_BUCKETS // 2)).astype(np.int64)
    return np.where(d < NUM_BUCKETS // 2, d, np.minimum(large, NUM_BUCKETS - 1))


def _bias_onehot():
    oh = np.zeros((40, _N_TILE_KINDS * 2 * TQ), np.float32)
    buckets = _bucket_table((N_FAR + 2) * TQ + 1)
    for t in range(_N_TILE_KINDS):
        window = _K_WIN <= t < _K_MASKED
        delta = t - _K_WIN if window else t
        for u in range(2 * TQ):
            d = TQ * delta + TQ - u
            masked = t == _K_MASKED or d < 0 or (window and d >= WINDOW)
            oh[NUM_BUCKETS if masked else buckets[d], t * 2 * TQ + u] = 1.0
    return oh


def _bias_body(relt_ref, oh_ref, out_ref):
    g = jnp.dot(jnp.broadcast_to(relt_ref[0], (8, 40)), oh_ref[...], preferred_element_type=jnp.float32,
                precision=lax.Precision.HIGHEST)
    for t in range(_N_TILE_KINDS):
        full = jnp.broadcast_to(g[0:1, t * 2 * TQ:(t + 1) * 2 * TQ], (TQ, 2 * TQ))
        rolled = pltpu.roll(full, TQ, 1, stride=1, stride_axis=0)
        out_ref[0, t] = rolled[:, :TQ]


def _bias_tiles(rel_bias):
    nh = MOBA_HEADS + NSA_HEADS
    relt = jnp.concatenate([rel_bias.T.astype(jnp.float32) * LOG2E, jnp.full((nh, 1), NEG, jnp.float32),
                            jnp.zeros((nh, 7), jnp.float32)], axis=1).reshape(nh, 1, 40)
    oh = jnp.asarray(_bias_onehot())
    return pl.pallas_call(
        _bias_body,
        grid=(nh,),
        in_specs=[pl.BlockSpec((1, 1, 40), lambda h: (h, 0, 0)), pl.BlockSpec(oh.shape, lambda h: (0, 0))],
        out_specs=pl.BlockSpec((1, _N_TILE_KINDS, TQ, TQ), lambda h: (h, 0, 0, 0)),
        out_shape=jax.ShapeDtypeStruct((nh, _N_TILE_KINDS, TQ, TQ), jnp.float32),
        compiler_params=pltpu.CompilerParams(dimension_semantics=("parallel",), vmem_limit_bytes=VMEM_LIMIT),
        name="bias_tiles",
    )(relt, oh)


def _seg_norm(x, lo_half, gain):
    x2 = x * x
    ss_lo = jnp.sum(jnp.where(lo_half, x2, 0.0), axis=-1, keepdims=True)
    ss_hi = jnp.sum(jnp.where(lo_half, 0.0, x2), axis=-1, keepdims=True)
    ss = jnp.where(lo_half, ss_lo, ss_hi)
    return x * lax.rsqrt(ss * (1.0 / HEAD_DIM) + EPS) * gain


def _in_proj_body(x_ref, nw_ref, w_ref, wg_ref, gain_ref, *refs):
    outs, stages = refs[:-2], refs[-2:]
    for c in range(PROJ_CHAINS):
        rows = pl.ds(c * TM, TM)
        tok = lambda r: r.at[:, rows, :]
        grp = lambda r: r.at[:, :, rows, :]
        chunk = lambda r: r.at[:, :, pl.ds(c * (TM // CMP_STRIDE), TM // CMP_STRIDE), :]
        (qa, kae, kao, vae, vao, qb, kmean, kc, vc, kse, kso, vsl, kwn, vwn, gate) = outs
        _in_proj_tile(pl.program_id(1) * PROJ_CHAINS + c, tok(x_ref), nw_ref, w_ref, wg_ref, gain_ref,
                      tok(qa), tok(kae), tok(kao), tok(vae), tok(vao), tok(qb),
                      kmean.at[:, pl.ds(c * (TM // MOBA_BLOCK), TM // MOBA_BLOCK)], chunk(kc), chunk(vc),
                      grp(kse), grp(kso), grp(vsl), grp(kwn), grp(vwn), tok(gate),
                      stages[0].at[c], stages[1].at[c])


def _in_proj_tile(i, x_ref, nw_ref, w_ref, wg_ref, gain_ref,
                  qa_ref, kae_ref, kao_ref, vae_ref, vao_ref, qb_ref, kmean_ref, kc_ref, vc_ref,
                  kse_ref, kso_ref, vsl_ref, kwn_ref, vwn_ref, gate_ref, stage_k_ref, stage_v_ref):
    x = x_ref[0]
    ms = jnp.mean(x * x, axis=-1, keepdims=True)
    h = (x * lax.rsqrt(ms + EPS) * nw_ref[...]).astype(jnp.bfloat16)
    bf = jnp.bfloat16
    lane = lax.broadcasted_iota(jnp.int32, (TM, LANES), 1)
    row = lax.broadcasted_iota(jnp.int32, (TM, LANES), 0)
    lo_half = lane < HEAD_DIM
    blk = i * (TM // MOBA_BLOCK) + lax.shift_right_logical(row, MOBA_BLOCK.bit_length() - 1)
    sblk = i * (TM // SEL_BLOCK) + lax.shift_right_logical(row, SEL_BLOCK.bit_length() - 1)

    def normed(col, width, scale=None):
        p = _nn(h, w_ref[:, col:col + width])
        outs = []
        for s in range(width // LANES):
            c = col + s * LANES
            y = _seg_norm(p[:, s * LANES:(s + 1) * LANES], lo_half, gain_ref[:, c:c + LANES])
            outs.append(y if scale is None else y * scale)
        return outs

    qa = normed(_C_QA, 512, QSCALE)
    for s in range(4):
        qa_ref[0, :, s * LANES:(s + 1) * LANES] = qa[s].astype(bf)
    ka = normed(_C_KA, 512)
    oh_hi = jnp.where(lane == HEAD_DIM + blk, 1.0, 0.0)
    oh_lo = jnp.where(lane == blk, 1.0, 0.0)
    for s in range(4):
        kae_ref[0, :, s * LANES:(s + 1) * LANES] = jnp.where(lo_half, ka[s], oh_hi).astype(bf)
        kao_ref[0, :, s * LANES:(s + 1) * LANES] = jnp.where(lo_half, oh_lo, ka[s]).astype(bf)
        for j in range(TM // MOBA_BLOCK):
            kmean_ref[0, j, :, s * LANES:(s + 1) * LANES] = jnp.broadcast_to(
                jnp.mean(ka[s][j * MOBA_BLOCK:(j + 1) * MOBA_BLOCK], axis=0, keepdims=True), (8, LANES))
    one_hi = jnp.where(lane == HEAD_DIM, 1.0, 0.0)
    one_lo = jnp.where(lane == 0, 1.0, 0.0)
    va = _nn(h, w_ref[:, _C_VA:_C_VA + 512])
    for s in range(4):
        vs = va[:, s * LANES:(s + 1) * LANES]
        vae_ref[0, :, s * LANES:(s + 1) * LANES] = jnp.where(lo_half, vs, one_hi).astype(bf)
        vao_ref[0, :, s * LANES:(s + 1) * LANES] = jnp.where(lo_half, one_lo, vs).astype(bf)
    qb = normed(_C_QB, 512, QSCALE)
    for s in range(4):
        qb_ref[0, :, s * LANES:(s + 1) * LANES] = qb[s].astype(bf)

    rest = _nn(h, w_ref[:, _C_KC:_C_GATE])
    gate_ref[0] = jnp.concatenate([_nn(h[0:TM // 2], wg_ref[...]), _nn(h[TM // 2:], wg_ref[...])], axis=0)

    def slab(c):
        return rest[:, c - _C_KC:c - _C_KC + LANES]

    def chunk_store(ref, stage_ref, y):
        stage_ref[...] = y
        lo = lax.broadcasted_iota(jnp.int32, (TM // CMP_STRIDE, LANES), 1) < HEAD_DIM
        for u in range(CMP_STRIDE // 2):
            t0 = stage_ref[pl.ds(2 * u, TM // CMP_STRIDE, stride=CMP_STRIDE), :]
            t1 = stage_ref[pl.ds(2 * u + 1, TM // CMP_STRIDE, stride=CMP_STRIDE), :]
            ref[0, 0, :, u * LANES:(u + 1) * LANES] = jnp.where(lo, t0, pltpu.roll(t1, HEAD_DIM, 1)).astype(bf)
            ref[0, 1, :, u * LANES:(u + 1) * LANES] = jnp.where(lo, pltpu.roll(t0, HEAD_DIM, 1), t1).astype(bf)

    chunk_store(kc_ref, stage_k_ref, slab(_C_KC))
    chunk_store(vc_ref, stage_v_ref, slab(_C_VC))

    def dup_store(ref, y):
        r = pltpu.roll(y, HEAD_DIM, 1)
        ref[0, 0] = jnp.where(lo_half, y, r).astype(bf)
        ref[0, 1] = jnp.where(lo_half, r, y).astype(bf)

    y = _seg_norm(slab(_C_KSL), lo_half, gain_ref[:, _C_KSL:_C_KSL + LANES])
    r = pltpu.roll(y, HEAD_DIM, 1)
    oh = jnp.where((lane & (HEAD_DIM - 1)) == sblk, 1.0, 0.0)
    kse_ref[0, 0] = jnp.where(lo_half, y, oh).astype(bf)
    kso_ref[0, 0] = jnp.where(lo_half, oh, r).astype(bf)
    kse_ref[0, 1] = jnp.where(lo_half, r, oh).astype(bf)
    kso_ref[0, 1] = jnp.where(lo_half, oh, y).astype(bf)
    def aug_store(ref, y):
        ref[0, 0] = jnp.where(lo_half, y, one_hi).astype(bf)
        ref[0, 1] = jnp.where(lo_half, pltpu.roll(y, HEAD_DIM, 1), one_hi).astype(bf)

    aug_store(vsl_ref, slab(_C_VSL))
    dup_store(kwn_ref, _seg_norm(slab(_C_KWN), lo_half, gain_ref[:, _C_KWN:_C_KWN + LANES]))
    aug_store(vwn_ref, slab(_C_VWN))


def _in_proj(x, norm_w, w1, wg, gain_row):
    B, S, _ = x.shape
    nt = S // TQ
    bf = jnp.bfloat16
    tok = lambda w, dt: jax.ShapeDtypeStruct((B, S, w), dt)
    grp = jax.ShapeDtypeStruct((B, NSA_GROUPS, S, LANES), bf)
    rows = PROJ_CHAINS * TM
    tok_spec = lambda w: pl.BlockSpec((1, rows, w), lambda b, i: (b, i, 0))
    grp_spec = pl.BlockSpec((1, NSA_GROUPS, rows, LANES), lambda b, i: (b, 0, i, 0))
    flat = CMP_STRIDE * HEAD_DIM
    chunk = jax.ShapeDtypeStruct((B, NSA_GROUPS, S // CMP_STRIDE, flat), bf)
    chunk_spec = pl.BlockSpec((1, NSA_GROUPS, rows // CMP_STRIDE, flat), lambda b, i: (b, 0, i, 0))
    full = lambda a: pl.BlockSpec(a.shape, lambda b, i: (0,) * a.ndim, pipeline_mode=pl.Buffered(1))
    nw = norm_w.reshape(1, D_MODEL)
    return pl.pallas_call(
        _in_proj_body,
        grid=(B, S // rows),
        in_specs=[tok_spec(D_MODEL), full(nw), full(w1), full(wg), full(gain_row)],
        out_specs=[tok_spec(512), tok_spec(512), tok_spec(512), tok_spec(512), tok_spec(512), tok_spec(512),
                   pl.BlockSpec((1, rows // MOBA_BLOCK, 8, 512), lambda b, i: (b, i, 0, 0)),
                   chunk_spec, chunk_spec,
                   grp_spec, grp_spec, grp_spec, grp_spec, grp_spec, tok_spec(LANES)],
        out_shape=[tok(512, bf), tok(512, bf), tok(512, bf), tok(512, bf), tok(512, bf), tok(512, bf),
                   jax.ShapeDtypeStruct((B, nt, 8, 512), jnp.float32),
                   chunk, chunk, grp, grp, grp, grp, grp, tok(LANES, jnp.float32)],
        scratch_shapes=[pltpu.VMEM((PROJ_CHAINS, TM, LANES), jnp.float32)] * 2,
        compiler_params=pltpu.CompilerParams(dimension_semantics=("parallel", "parallel"),
                                             vmem_limit_bytes=VMEM_LIMIT),
        name="in_proj",
    )(x, nw, w1, wg, gain_row)


def _gelu_tanh(x):
    return 0.5 * x * (1.0 + jnp.tanh(math.sqrt(2.0 / math.pi) * (x + 0.044715 * (x * x * x))))


def _compress_body(xk_ref, xv_ref, w1k_ref, w1v_ref, pk_ref, pv_ref,
                   w2k_ref, w2v_ref, gk_ref, kcmp_ref, vcmp_ref):
    nchunk = xk_ref.shape[2]
    half = CMP_STRIDE * HEAD_DIM

    def branch(x_ref, w1_ref, pos_ref, w2_ref, gain):
        pos = jnp.broadcast_to(pos_ref[...], (8, CMP_LEN * HEAD_DIM))
        posw = (_nn(pos[:, 0:half], w1_ref[...])[0:1, 0:CMP_HIDDEN]
                + _nn(pos[:, half:], w1_ref[...])[0:1, CMP_HIDDEN:])
        outs = []
        for g in range(NSA_GROUPS):
            ab = _nn(x_ref[0, g], w1_ref[...])
            top, bot = ab[:, 0:CMP_HIDDEN], ab[:, CMP_HIDDEN:]
            hid = top + pltpu.roll(bot, nchunk - 1, 0) + posw
            y = _nn(_gelu_tanh(hid).astype(jnp.bfloat16), w2_ref[...])
            if gain is not None:
                ms = jnp.sum(y * y, axis=-1, keepdims=True) * (0.5 / HEAD_DIM)
                y = y * lax.rsqrt(ms + EPS) * gain
            else:
                lane = lax.broadcasted_iota(jnp.int32, y.shape, 1)
                y = jnp.where(lane < HEAD_DIM, y, jnp.where(lane == HEAD_DIM, 1.0, 0.0))
            outs.append(y.astype(jnp.bfloat16))
        return outs

    k0, k1 = branch(xk_ref, w1k_ref, pk_ref, w2k_ref, gk_ref[...])
    kcmp_ref[0, 0], kcmp_ref[0, 1] = k0, k1
    v0, v1 = branch(xv_ref, w1v_ref, pv_ref, w2v_ref, None)
    vcmp_ref[0, 0], vcmp_ref[0, 1] = v0, v1


def _compress(xk, xv, pos_k, w1_k, w2_k, pos_v, w1_v, w2_v, gain_cmp):
    B, _, nchunk, half = xk.shape
    bf = jnp.bfloat16
    halves = lambda w1: jnp.concatenate([w1[:half], w1[half:]], axis=1).astype(bf)
    args = [xk, xv, halves(w1_k), halves(w1_v),
            pos_k.reshape(1, -1).astype(bf), pos_v.reshape(1, -1).astype(bf),
            jnp.concatenate([w2_k, w2_k], axis=1).astype(bf), jnp.concatenate([w2_v, w2_v], axis=1).astype(bf),
            jnp.concatenate([gain_cmp, gain_cmp]).reshape(1, LANES)]
    x_spec = pl.BlockSpec((1, NSA_GROUPS, nchunk, half), lambda b: (b, 0, 0, 0))
    full = lambda a: pl.BlockSpec(a.shape, lambda b: (0,) * a.ndim)
    out = jax.ShapeDtypeStruct((B, NSA_GROUPS, nchunk, LANES), bf)
    o_spec = pl.BlockSpec((1, NSA_GROUPS, nchunk, LANES), lambda b: (b, 0, 0, 0))
    return pl.pallas_call(
        _compress_body,
        grid=(B,),
        in_specs=[x_spec, x_spec] + [full(a) for a in args[2:]],
        out_specs=[o_spec, o_spec],
        out_shape=[out, out],
        compiler_params=pltpu.CompilerParams(dimension_semantics=("parallel",), vmem_limit_bytes=VMEM_LIMIT),
        name="compress",
    )(*args)


def _softmax_step(s, v_top, v_bot, m_ref, acc_ref, first):
    blocks = [s[:, c * LANES:(c + 1) * LANES] for c in range(s.shape[1] // LANES)]
    bmax = blocks[0]
    for blk in blocks[1:]:
        bmax = jnp.maximum(bmax, blk)
    m_new = jnp.broadcast_to(jnp.max(bmax, axis=-1, keepdims=True), m_ref.shape)
    if not first:
        m_old = m_ref[...]
        m_new = jnp.maximum(m_old, m_new)
        alpha = jnp.exp2(m_old - m_new)
    pb = jnp.concatenate([jnp.exp2(blk - m_new) for blk in blocks], axis=1).astype(jnp.bfloat16)
    half = s.shape[0] // 2
    pv = jnp.concatenate([_nn(pb[0:half], v_top), _nn(pb[half:], v_bot)], axis=0)
    acc_ref[...] = pv if first else alpha * acc_ref[...] + pv
    m_ref[...] = m_new


def _diag_chunk(qq, kk, v_top, v_bot, bias_of, m_ref, acc_ref):
    nh = qq.shape[0] // TR
    q_lo = jnp.concatenate([qq[h * TR:h * TR + TQ] for h in range(nh)], axis=0)
    q_hi = jnp.concatenate([qq[h * TR + TQ:(h + 1) * TR] for h in range(nh)], axis=0)
    s_lo = _nt(q_lo, kk[0:TQ]) + jnp.concatenate([bias_of(h, 0) for h in range(nh)], axis=0)
    s_hi = _nt(q_hi, kk) + jnp.concatenate(
        [jnp.concatenate([bias_of(h, 1), bias_of(h, 0)], axis=1) for h in range(nh)], axis=0)
    for s, vt, vb, off in ((s_lo, v_top[0:TQ], v_bot[0:TQ], 0), (s_hi, v_top, v_bot, TQ)):
        blocks = [s[:, c * LANES:(c + 1) * LANES] for c in range(s.shape[1] // LANES)]
        bmax = blocks[0]
        for blk in blocks[1:]:
            bmax = jnp.maximum(bmax, blk)
        m = jnp.broadcast_to(jnp.max(bmax, axis=-1, keepdims=True), (nh * TQ, LANES))
        pb = jnp.concatenate([jnp.exp2(blk - m) for blk in blocks], axis=1).astype(jnp.bfloat16)
        half = nh * TQ // 2
        pv = jnp.concatenate([_nn(pb[0:half], vt), _nn(pb[half:], vb)], axis=0)
        for h in range(nh):
            m_ref[h * TR + off:h * TR + off + TQ] = m[h * TQ:(h + 1) * TQ]
            acc_ref[h * TR + off:h * TR + off + TQ] = pv[h * TQ:(h + 1) * TQ]


def _finish(acc, ones_lane):
    lane = lax.broadcasted_iota(jnp.int32, acc.shape, 1)
    return acc / jnp.sum(jnp.where(lane == ones_lane, acc, 0.0), axis=-1, keepdims=True)


def _tile_kind(delta):
    return jnp.where(delta >= 0, jnp.minimum(delta, N_FAR), _K_MASKED)


def _moba_body(q_ref, ke_ref, ko_ref, ve_ref, vo_ref, kmean_ref, eye_ref, bias_ref, o_ref, m_ref, acc_ref):
    i = pl.program_id(2)
    nblk = kmean_ref.shape[1]
    lane = lax.broadcasted_iota(jnp.int32, (TR, LANES), 1)
    lo_half = lane < HEAD_DIM
    lane16 = lax.broadcasted_iota(jnp.int32, (nblk, LANES), 1)
    n_heads = 2 * MOBA_PAIRS
    jrow = lax.broadcasted_iota(jnp.int32, (nblk, n_heads * TR), 0)
    assert TR == 2 * TQ
    qblk = 2 * i + (lax.shift_right_logical(lax.broadcasted_iota(jnp.int32, (nblk, n_heads * TR), 1),
                                            TQ.bit_length() - 1) & 1)
    pad = jnp.zeros((HEAD_DIM - nblk, TR), jnp.float32)
    zero = jnp.zeros((TR, LANES), jnp.bfloat16)

    scores = []
    for pr in range(MOBA_PAIRS):
        q2 = q_ref[0, :, pr * LANES:(pr + 1) * LANES]
        kmean2 = kmean_ref[0, :, pr * LANES:(pr + 1) * LANES]
        for hh in range(2):
            inhead16 = (lane16 < HEAD_DIM) if hh == 0 else (lane16 >= HEAD_DIM)
            km_hi, km_lo = _split(jnp.where(inhead16, kmean2, 0.0))
            scores.append(_nt(km_hi, q2) + _nt(km_lo, q2))
    sc = jnp.where(jrow < qblk, jnp.concatenate(scores, axis=1), -jnp.inf)
    flag = jnp.where(jrow == qblk, 0.0, 1.0)
    for _ in range(MOBA_TOPK):
        best = jnp.max(sc, axis=0, keepdims=True)
        first = jnp.min(jnp.where(sc == best, jrow, nblk), axis=0, keepdims=True)
        hit = jrow == first
        flag = jnp.where(hit, 0.0, flag)
        sc = jnp.where(hit, -jnp.inf, sc)

    def augmented_q(pr):
        q2 = q_ref[0, :, pr * LANES:(pr + 1) * LANES]
        flags = [flag[:, (2 * pr + hh) * TR:(2 * pr + hh + 1) * TR] for hh in range(2)]
        placed = jnp.concatenate([flags[1], pad, flags[0], pad], axis=0).astype(jnp.bfloat16)
        maskpart = jnp.concatenate([_nt(eye_ref[...], placed[:, a * TQ:(a + 1) * TQ]) for a in range(TR // TQ)],
                                   axis=0) * NEG
        q2f = q2.astype(jnp.float32)
        return jnp.concatenate(
            [jnp.concatenate([jnp.where(lo_half, q2f, maskpart).astype(jnp.bfloat16), zero], axis=1),
             jnp.concatenate([zero, jnp.where(lo_half, maskpart, q2f).astype(jnp.bfloat16)], axis=1)], axis=0)

    qqs = [augmented_q(pr) for pr in range(MOBA_PAIRS)]

    def operands(c, pr):
        rows = pl.ds(pl.multiple_of(c * TK, TK), TK)
        cols = slice(pr * LANES, (pr + 1) * LANES)
        kk = jnp.concatenate([ke_ref[0, rows, cols], ko_ref[0, rows, cols]], axis=1)
        return kk, ve_ref[0, rows, cols], vo_ref[0, rows, cols]

    for pr in range(MOBA_PAIRS):
        kk, ve, vo = operands(i, pr)
        _diag_chunk(qqs[pr], kk, ve, vo, lambda hh, delta, pr=pr: bias_ref[2 * pr + hh, delta],
                    m_ref.at[pr], acc_ref.at[pr])

    def body(c, carry):
        d = 2 * (i - c)
        for pr in range(MOBA_PAIRS):
            kk, ve, vo = operands(c, pr)
            b = jnp.concatenate(
                [jnp.concatenate([bias_ref[2 * pr + hh, _tile_kind(d + a - t)] for t in range(2)], axis=1)
                 for hh in range(2) for a in range(2)], axis=0)
            _softmax_step(_nt(qqs[pr], kk) + b, ve, vo, m_ref.at[pr], acc_ref.at[pr], False)
        return carry

    lax.fori_loop(0, i, body, 0)
    for pr in range(MOBA_PAIRS):
        o_ref[0, :, pr * LANES:(pr + 1) * LANES] = jnp.where(
            lo_half, _finish(acc_ref[pr, 0:TR], HEAD_DIM), _finish(acc_ref[pr, TR:2 * TR], 0))


def _eye(n):
    return jnp.asarray(np.eye(n, dtype=np.float32), jnp.bfloat16)


def _moba(qa, kae, kao, vae, vao, kmean, bias):
    B, S, _ = qa.shape
    nt = S // TQ
    width = MOBA_PAIRS * LANES
    kv_spec = pl.BlockSpec((1, S, width), lambda b, p, i: (b, 0, p), pipeline_mode=pl.Buffered(1))
    return pl.pallas_call(
        _moba_body,
        grid=(B, MOBA_HEADS // (2 * MOBA_PAIRS), S // TR),
        in_specs=[pl.BlockSpec((1, TR, width), lambda b, p, i: (b, i, p)),
                  kv_spec, kv_spec, kv_spec, kv_spec,
                  pl.BlockSpec((1, nt, width), lambda b, p, i: (b, 0, p)),
                  pl.BlockSpec((TQ, TQ), lambda b, p, i: (0, 0)),
                  pl.BlockSpec((2 * MOBA_PAIRS, _N_TILE_KINDS, TQ, TQ), lambda b, p, i: (p, 0, 0, 0),
                               pipeline_mode=pl.Buffered(1))],
        out_specs=pl.BlockSpec((1, TR, width), lambda b, p, i: (b, i, p)),
        out_shape=jax.ShapeDtypeStruct((B, S, MOBA_HEADS * HEAD_DIM), jnp.float32),
        scratch_shapes=[pltpu.VMEM((MOBA_PAIRS, 2 * TR, LANES), jnp.float32)] * 2,
        compiler_params=pltpu.CompilerParams(dimension_semantics=("parallel", "parallel", "arbitrary"),
                                             vmem_limit_bytes=VMEM_LIMIT),
        name="moba",
    )(qa, kae, kao, vae, vao, kmean, _eye(TQ), bias)


def _nsa_body(q_ref, kcmp_ref, vcmp_ref, kse_ref, kso_ref, vsl_ref, kwn_ref, vwn_ref, gate_ref, ov_ref, eye_ref,
              gexp_ref, bias_ref, o_ref, m_ref, acc_ref):
    i = pl.program_id(1)

    def to_hi(o):
        return pltpu.roll(o, HEAD_DIM, 1)

    ncmp = kcmp_ref.shape[2]
    nsel = ov_ref.shape[0]
    R = NSA_HPG * TR
    bf = jnp.bfloat16
    lane = lax.broadcasted_iota(jnp.int32, (TR, LANES), 1)
    lo_half = lane < HEAD_DIM
    sel_heads = (0, 2, 1, 3)

    def compressed(g):
        qf = [q_ref[0, :, (2 * g + s) * LANES:(2 * g + s + 1) * LANES].astype(jnp.float32) for s in range(2)]
        q4 = jnp.concatenate([jnp.where(lo_half if hh % 2 == 0 else ~lo_half, qf[hh // 2], 0.0).astype(bf)
                              for hh in range(NSA_HPG)], axis=0)

        tpos = i * TR + lax.broadcasted_iota(jnp.int32, (TR, ncmp), 0)
        cidx = lax.broadcasted_iota(jnp.int32, (TR, ncmp), 1)
        cbias1 = jnp.where((cidx * CMP_STRIDE + (CMP_LEN - 1) <= tpos) & (cidx < ncmp - 1), 0.0, NEG)
        lc = _nt(q4, kcmp_ref[0, g]) + jnp.concatenate([cbias1] * NSA_HPG, axis=0)
        cblocks = [lc[:, c * LANES:(c + 1) * LANES] for c in range(ncmp // LANES)]
        cmax = cblocks[0]
        for blk in cblocks[1:]:
            cmax = jnp.maximum(cmax, blk)
        cm = jnp.maximum(jnp.broadcast_to(jnp.max(cmax, axis=-1, keepdims=True), (R, LANES)), 0.5 * NEG)
        ce = [jnp.exp2(blk - cm) for blk in cblocks]
        cden = ce[0]
        for eb in ce[1:]:
            cden = cden + eb
        cden = jnp.broadcast_to(jnp.sum(cden, axis=-1, keepdims=True), (R, LANES))
        crcp = 1.0 / jnp.where(cden > 0.0, cden, 1.0)
        pc = jnp.concatenate([eb * crcp for eb in ce], axis=1)
        pcb = pc.astype(bf)
        o_cmp = jnp.concatenate([_nn(pcb[0:R // 2], vcmp_ref[0, g]), _nn(pcb[R // 2:], vcmp_ref[0, g])], axis=0)
        psum = pc[0:TR] + pc[TR:2 * TR] + pc[2 * TR:3 * TR] + pc[3 * TR:4 * TR]
        p_hi, p_lo = _split(psum)
        ov = ov_ref[...]
        imp = _nt(ov, p_hi) + _nt(ov, p_lo)
        return qf, q4, o_cmp, imp

    def window(g, q4):
        o_win = []
        for a in range(2):
            qt = 2 * i + a
            t0 = jnp.maximum(qt - (N_WIN_TILES - 1), 0)
            wstart = pl.multiple_of(t0 * TQ, TQ)
            qh = jnp.concatenate([q4[hh * TR + a * TQ:hh * TR + (a + 1) * TQ] for hh in range(NSA_HPG)], axis=0)
            kinds = []
            for r in range(N_WIN_TILES):
                dd = qt - (t0 + r)
                kinds.append(jnp.where(dd >= 0, _K_WIN + dd, _K_MASKED))
            bw = jnp.concatenate([jnp.concatenate([bias_ref[NSA_HPG * g + hh, k] for k in kinds], axis=1)
                                  for hh in range(NSA_HPG)], axis=0)
            sw = _nt(qh, kwn_ref[0, g, pl.ds(wstart, N_WIN_TILES * TQ), :]) + bw
            wm, wacc = m_ref.at[g, 0:NSA_HPG * TQ], acc_ref.at[g, 0:NSA_HPG * TQ]
            vw = vwn_ref[0, g, pl.ds(wstart, N_WIN_TILES * TQ), :]
            _softmax_step(sw, vw, vw, wm, wacc, True)
            o_win.append(_finish(wacc[...], HEAD_DIM))
        return o_win

    def select(imps):
        width = NSA_GROUPS * TR
        jrow = lax.broadcasted_iota(jnp.int32, (nsel, width), 0)
        qblk = lax.shift_right_logical(i * TR + (lax.broadcasted_iota(jnp.int32, (nsel, width), 1) & (TR - 1)),
                                       SEL_BLOCK.bit_length() - 1)
        cand = (jrow >= 1) & (jrow < qblk)
        keys = jnp.where(cand, jnp.concatenate(imps, axis=1), -1.0)
        flag = jnp.where((jrow == 0) | (jrow == qblk), 0.0, 1.0)
        for _ in range(SEL_TOPN - 2):
            best = jnp.max(keys, axis=0, keepdims=True)
            first = jnp.min(jnp.where(keys == best, jrow, nsel), axis=0, keepdims=True)
            hit = jrow == first
            flag = jnp.where(hit, 0.0, flag)
            keys = jnp.where(hit, -1.0, keys)
        return flag

    def augmented_q(qf, flag):
        if nsel < HEAD_DIM:
            flag = jnp.concatenate([flag, jnp.zeros((HEAD_DIM - nsel, TR), jnp.float32)], axis=0)
        flag2 = jnp.concatenate([flag, flag], axis=0).astype(bf)
        maskpart = jnp.concatenate([_nt(eye_ref[...], flag2[:, a * TQ:(a + 1) * TQ]) for a in range(TR // TQ)],
                                   axis=0) * NEG

        zero = jnp.zeros((2 * TR, LANES), bf)
        qa2 = [jnp.concatenate([jnp.where(lo_half if par == 0 else ~lo_half, qf[s], maskpart).astype(bf)
                                for s in range(2)], axis=0) for par in range(2)]
        return jnp.concatenate([jnp.concatenate([qa2[0], zero], axis=1),
                                jnp.concatenate([zero, qa2[1]], axis=1)], axis=0)

    cmp_parts = [compressed(g) for g in range(NSA_GROUPS)]
    o_wins = [window(g, cmp_parts[g][1]) for g in range(NSA_GROUPS)]
    flags = select([part[3] for part in cmp_parts])
    state = [(augmented_q(cmp_parts[g][0], flags[:, g * TR:(g + 1) * TR]), cmp_parts[g][2], o_wins[g])
             for g in range(NSA_GROUPS)]

    def sel_operands(c, g):
        rows = pl.ds(pl.multiple_of(c * TK, TK), TK)
        kk = jnp.concatenate([kse_ref[0, g, rows, :], kso_ref[0, g, rows, :]], axis=1)
        return kk, vsl_ref[0, g, rows, :]

    for g in range(NSA_GROUPS):
        kk, v = sel_operands(i, g)
        _diag_chunk(state[g][0], kk, v, v, lambda hh, delta, g=g: bias_ref[NSA_HPG * g + sel_heads[hh], delta],
                    m_ref.at[g], acc_ref.at[g])

    def sel_body(c, carry):
        d = 2 * (i - c)
        for g in range(NSA_GROUPS):
            kk, v = sel_operands(c, g)
            b = jnp.concatenate(
                [jnp.concatenate([bias_ref[NSA_HPG * g + h, _tile_kind(d + a - t)] for t in range(2)], axis=1)
                 for h in sel_heads for a in range(2)], axis=0)
            _softmax_step(_nt(state[g][0], kk) + b, v, v, m_ref.at[g], acc_ref.at[g], False)
        return carry

    lax.fori_loop(0, i, sel_body, 0)

    gate = jax.nn.sigmoid(gate_ref[0])
    g_hi, g_lo = _split(gate)
    for g in range(NSA_GROUPS):
        _, o_cmp, o_win = state[g]
        o_all = _finish(acc_ref[g], HEAD_DIM)
        o_sel = [o_all[0:2 * TR], to_hi(o_all[2 * TR:4 * TR])]
        gx = _nn(g_hi, gexp_ref[g]) + _nn(g_lo, gexp_ref[g])
        for s in range(2):
            h_even, h_odd = 2 * s, 2 * s + 1
            cmp2 = jnp.where(lo_half, o_cmp[h_even * TR:(h_even + 1) * TR],
                             to_hi(o_cmp[h_odd * TR:(h_odd + 1) * TR]))
            sel2 = jnp.where(lo_half, o_sel[0][s * TR:(s + 1) * TR], o_sel[1][s * TR:(s + 1) * TR])
            win2 = jnp.where(lo_half,
                             jnp.concatenate([o_win[a][h_even * TQ:(h_even + 1) * TQ] for a in range(2)], axis=0),
                             to_hi(jnp.concatenate([o_win[a][h_odd * TQ:(h_odd + 1) * TQ] for a in range(2)], axis=0)))
            base = s * 3 * LANES
            o_ref[0, :, (2 * g + s) * LANES:(2 * g + s + 1) * LANES] = (
                gx[:, base:base + LANES] * cmp2 + gx[:, base + LANES:base + 2 * LANES] * sel2
                + gx[:, base + 2 * LANES:base + 3 * LANES] * win2)


def _gate_expand():
    e = np.zeros((NSA_GROUPS, LANES, 2 * 3 * LANES), np.float32)
    for g in range(NSA_GROUPS):
        for s in range(2):
            for c in range(3):
                for half in range(2):
                    h = g * NSA_HPG + 2 * s + half
                    col = (s * 3 + c) * LANES + half * HEAD_DIM
                    e[g, 3 * h + c, col:col + HEAD_DIM] = 1.0
    return e


def _nsa(qb, kcmp, vcmp, kse, kso, vsl, kwn, vwn, gate, bias):
    B, S, _ = qb.shape
    nt = S // TQ
    ncmp = kcmp.shape[2]
    nsel = S // SEL_BLOCK
    bf = jnp.bfloat16
    cs = np.arange(ncmp) * CMP_STRIDE
    bs = np.arange(nsel) * SEL_BLOCK
    ov = ((cs[None, :] < bs[:, None] + SEL_BLOCK) & (cs[None, :] + CMP_LEN > bs[:, None])
          & (np.arange(ncmp)[None, :] < ncmp - 1)).astype(np.float32)
    once = pl.Buffered(1)
    grp = lambda n: pl.BlockSpec((1, NSA_GROUPS, n, LANES), lambda b, i: (b, 0, 0, 0), pipeline_mode=once)
    width = NSA_HEADS * HEAD_DIM
    return pl.pallas_call(
        _nsa_body,
        grid=(B, S // TR),
        in_specs=[pl.BlockSpec((1, TR, width), lambda b, i: (b, i, 0)),
                  grp(ncmp), grp(ncmp), grp(S), grp(S), grp(S), grp(S), grp(S),
                  pl.BlockSpec((1, TR, LANES), lambda b, i: (b, i, 0)),
                  pl.BlockSpec((nsel, ncmp), lambda b, i: (0, 0), pipeline_mode=once),
                  pl.BlockSpec((TQ, TQ), lambda b, i: (0, 0), pipeline_mode=once),
                  pl.BlockSpec((NSA_GROUPS, LANES, 6 * LANES), lambda b, i: (0, 0, 0), pipeline_mode=once),
                  pl.BlockSpec((NSA_HEADS, _N_TILE_KINDS, TQ, TQ), lambda b, i: (1, 0, 0, 0), pipeline_mode=once)],
        out_specs=pl.BlockSpec((1, TR, width), lambda b, i: (b, i, 0)),
        out_shape=jax.ShapeDtypeStruct((B, S, width), jnp.float32),
        scratch_shapes=[pltpu.VMEM((NSA_GROUPS, NSA_HPG * TR, LANES), jnp.float32)] * 2,
        compiler_params=pltpu.CompilerParams(dimension_semantics=("parallel", "arbitrary"),
                                             vmem_limit_bytes=VMEM_LIMIT),
        name="nsa",
    )(qb, kcmp, vcmp, kse, kso, vsl, kwn, vwn, gate, jnp.asarray(ov, bf), _eye(TQ),
      jnp.asarray(_gate_expand(), bf), bias)


def _out_proj_body(x_ref, nw_ref, oa_ref, ob_ref, wz_ref, wzb_ref, wgm_ref, wa_ref, wb_ref, wo_ref, out_ref):
    for c in range(PROJ_CHAINS):
        rows = pl.ds(c * TM, TM)
        x = x_ref[0, rows, :]
        ms = jnp.mean(x * x, axis=-1, keepdims=True)
        h = (x * lax.rsqrt(ms + EPS) * nw_ref[...]).astype(jnp.bfloat16)
        ya = (oa_ref[0, rows, :] * jax.nn.silu(_nn(h, wz_ref[:, _C_ZA:_C_ZA + 512]))).astype(jnp.bfloat16)
        yb = (ob_ref[0, rows, :] * jax.nn.silu(_nn(h, wzb_ref[...]))).astype(jnp.bfloat16)
        gm = jax.nn.sigmoid(_nn(h, wgm_ref[...]))
        merged = gm[:, 0:D_MODEL] * _nn(ya, wa_ref[...]) + gm[:, D_MODEL:] * _nn(yb, wb_ref[...])
        out_ref[0, rows, :] = x + _nn(merged.astype(jnp.bfloat16), wo_ref[...])


def _out_proj(x, norm_w, oa, ob, wz, wzb, wgm, wa, wb, wo):
    B, S, _ = x.shape
    nw = norm_w.reshape(1, D_MODEL)
    tok = lambda w: pl.BlockSpec((1, PROJ_CHAINS * TM, w), lambda b, i: (b, i, 0))
    full = lambda a: pl.BlockSpec(a.shape, lambda b, i: (0,) * a.ndim, pipeline_mode=pl.Buffered(1))
    return pl.pallas_call(
        _out_proj_body,
        grid=(B, S // (PROJ_CHAINS * TM)),
        in_specs=[tok(D_MODEL), full(nw), tok(512), tok(512), full(wz), full(wzb), full(wgm),
                  full(wa), full(wb), full(wo)],
        out_specs=tok(D_MODEL),
        out_shape=jax.ShapeDtypeStruct((B, S, D_MODEL), jnp.float32),
        compiler_params=pltpu.CompilerParams(dimension_semantics=("parallel", "parallel"),
                                             vmem_limit_bytes=VMEM_LIMIT),
        name="out_proj",
    )(x, nw, oa, ob, wz, wzb, wgm, wa, wb, wo)


def _pack_in_weights(w_in, q_norm_a, k_norm_a, q_norm_b, k_norm_sel, k_norm_win):
    bf = jnp.bfloat16
    ng = 3 * NSA_HEADS
    cols = lambda a, b: lax.slice_in_dim(w_in, a, b, axis=2).astype(bf).reshape(D_MODEL, b - a)
    z_b = _C_GATE + ng
    w = (cols(0, _C_GATE),
         jnp.pad(cols(_C_GATE, z_b), ((0, 0), (0, LANES - ng))),
         cols(z_b, z_b + NSA_HEADS * HEAD_DIM),
         cols(z_b + NSA_HEADS * HEAD_DIM, w_in.shape[2]))
    ones = lambda n: jnp.ones((n,), jnp.float32)
    gain = jnp.concatenate([jnp.tile(q_norm_a, MOBA_HEADS), jnp.tile(k_norm_a, MOBA_HEADS), ones(1024),
                            jnp.tile(q_norm_b, NSA_HEADS), ones(256), jnp.tile(k_norm_sel, NSA_GROUPS), ones(128),
                            jnp.tile(k_norm_win, NSA_GROUPS), ones(256)]).reshape(1, _N1)
    return w, gain


def kernel(x, norm_w, w_in, q_norm_a, k_norm_a, q_norm_b, k_norm_cmp, k_norm_sel, k_norm_win, cmp_pos_k, cmp_w1_k, cmp_w2_k, cmp_pos_v, cmp_w1_v, cmp_w2_v, rel_bias, w_branch_a, w_branch_b, w_out):
    bf = jnp.bfloat16
    (w_main, w_gate, w_zb, w_gm), gain = _pack_in_weights(w_in, q_norm_a[0], k_norm_a[0], q_norm_b[0],
                                                           k_norm_sel[0], k_norm_win[0])
    bias = _bias_tiles(rel_bias)
    qa, kae, kao, vae, vao, qb, kmean, kc, vc, kse, kso, vsl, kwn, vwn, gate = _in_proj(x, norm_w[0], w_main, w_gate,
                                                                                        gain)
    kcmp, vcmp = _compress(kc, vc, cmp_pos_k[0], cmp_w1_k[0], cmp_w2_k[0],
                           cmp_pos_v[0], cmp_w1_v[0], cmp_w2_v[0], k_norm_cmp[0])
    oa = _moba(qa, kae, kao, vae, vao, kmean[:, :, 0, :], bias)
    ob = _nsa(qb, kcmp, vcmp, kse, kso, vsl, kwn, vwn, gate, bias)
    return _out_proj(x, norm_w[0], oa, ob, w_main, w_zb, w_gm, w_branch_a[0].astype(bf), w_branch_b[0].astype(bf),
                     w_out[0].astype(bf))
```

```python
import math

import jax
import jax.numpy as jnp
import numpy as np
from jax import lax
from jax.experimental import pallas as pl
from jax.experimental.pallas import tpu as pltpu

D_MODEL = 1024
HEAD_DIM = 64
MOBA_HEADS = 8
NSA_HEADS = 8
NSA_GROUPS = 2
NSA_HPG = NSA_HEADS // NSA_GROUPS
MOBA_BLOCK = 256
MOBA_TOPK = 3
CMP_LEN = 32
CMP_STRIDE = 16
CMP_HIDDEN = 256
SEL_BLOCK = 64
SEL_TOPN = 16
WINDOW = 512
NUM_BUCKETS = 32
MAX_DISTANCE = 1024
EPS = 1e-6
NEG = -1e30

LANES = 128
TQ = 256
TK = 2 * TQ
TR = 2 * TQ
TM = 2 * TQ
MOBA_PAIRS = 4
PROJ_CHAINS = 2
N_FAR = 5
N_WIN_TILES = WINDOW // TQ + 1
VMEM_LIMIT = 56 * 1024 * 1024
LOG2E = math.log2(math.e)
QSCALE = LOG2E / math.sqrt(HEAD_DIM)

_K_WIN = N_FAR + 1
_K_MASKED = _K_WIN + N_WIN_TILES
_N_TILE_KINDS = _K_MASKED + 1

_C_QA, _C_KA, _C_VA, _C_ZA, _C_QB = 0, 512, 1024, 1536, 2048
_C_KC, _C_VC, _C_KSL, _C_VSL, _C_KWN, _C_VWN, _C_GATE = 2560, 2688, 2816, 2944, 3072, 3200, 3328
_N1 = _C_GATE + LANES


def _nt(a, b):
    return lax.dot_general(a, b, (((1,), (1,)), ((), ())), preferred_element_type=jnp.float32)


def _nn(a, b):
    return jnp.dot(a, b, preferred_element_type=jnp.float32)


def _split(x):
    hi = x.astype(jnp.bfloat16)
    lo = (x - hi.astype(jnp.float32)).astype(jnp.bfloat16)
    return hi, lo


def _bucket_table(n):
    d = np.arange(n)
    nf = np.maximum(d, NUM_BUCKETS // 2).astype(np.float64)
    large = NUM_BUCKETS // 2 + (np.log(nf / (NUM_BUCKETS // 2)) / math.log(MAX_DISTANCE / (NUM_BUCKETS // 2))
                                * (NUM_BUCKETS - NUM_BUCKETS // 2)).astype(np.int64)
    return np.where(d < NUM_BUCKETS // 2, d, np.minimum(large, NUM_BUCKETS - 1))


def _bias_onehot():
    oh = np.zeros((40, _N_TILE_KINDS * 2 * TQ), np.float32)
    buckets = _bucket_table((N_FAR + 2) * TQ + 1)
    for t in range(_N_TILE_KINDS):
        window = _K_WIN <= t < _K_MASKED
        delta = t - _K_WIN if window else t
        for u in range(2 * TQ):
            d = TQ * delta + TQ - u
            masked = t == _K_MASKED or d < 0 or (window and d >= WINDOW)
            oh[NUM_BUCKETS if masked else buckets[d], t * 2 * TQ + u] = 1.0
    return oh


def _bias_body(relt_ref, oh_ref, out_ref):
    g = jnp.dot(jnp.broadcast_to(relt_ref[0], (8, 40)), oh_ref[...], preferred_element_type=jnp.float32,
                precision=lax.Precision.HIGHEST)
    for t in range(_N_TILE_KINDS):
        full = jnp.broadcast_to(g[0:1, t * 2 * TQ:(t + 1) * 2 * TQ], (TQ, 2 * TQ))
        rolled = pltpu.roll(full, TQ, 1, stride=1, stride_axis=0)
        out_ref[0, t] = rolled[:, :TQ]


def _bias_tiles(rel_bias):
    nh = MOBA_HEADS + NSA_HEADS
    relt = jnp.concatenate([rel_bias.T.astype(jnp.float32) * LOG2E, jnp.full((nh, 1), NEG, jnp.float32),
                            jnp.zeros((nh, 7), jnp.float32)], axis=1).reshape(nh, 1, 40)
    oh = jnp.asarray(_bias_onehot())
    return pl.pallas_call(
        _bias_body,
        grid=(nh,),
        in_specs=[pl.BlockSpec((1, 1, 40), lambda h: (h, 0, 0)), pl.BlockSpec(oh.shape, lambda h: (0, 0))],
        out_specs=pl.BlockSpec((1, _N_TILE_KINDS, TQ, TQ), lambda h: (h, 0, 0, 0)),
        out_shape=jax.ShapeDtypeStruct((nh, _N_TILE_KINDS, TQ, TQ), jnp.float32),
        compiler_params=pltpu.CompilerParams(dimension_semantics=("parallel",), vmem_limit_bytes=VMEM_LIMIT),
        name="bias_tiles",
    )(relt, oh)


def _seg_norm(x, lo_half, gain):
    x2 = x * x
    ss_lo = jnp.sum(jnp.where(lo_half, x2, 0.0), axis=-1, keepdims=True)
    ss_hi = jnp.sum(jnp.where(lo_half, 0.0, x2), axis=-1, keepdims=True)
    ss = jnp.where(lo_half, ss_lo, ss_hi)
    return x * lax.rsqrt(ss * (1.0 / HEAD_DIM) + EPS) * gain


def _in_proj_body(x_ref, nw_ref, w_ref, wg_ref, gain_ref, *refs):
    outs, stages = refs[:-2], refs[-2:]
    for c in range(PROJ_CHAINS):
        rows = pl.ds(c * TM, TM)
        tok = lambda r: r.at[:, rows, :]
        grp = lambda r: r.at[:, :, rows, :]
        chunk = lambda r: r.at[:, :, pl.ds(c * (TM // CMP_STRIDE), TM // CMP_STRIDE), :]
        (qa, kae, kao, vae, vao, qb, kmean, kc, vc, kse, kso, vsl, kwn, vwn, gate) = outs
        _in_proj_tile(pl.program_id(1) * PROJ_CHAINS + c, tok(x_ref), nw_ref, w_ref, wg_ref, gain_ref,
                      tok(qa), tok(kae), tok(kao), tok(vae), tok(vao), tok(qb),
                      kmean.at[:, pl.ds(c * (TM // MOBA_BLOCK), TM // MOBA_BLOCK)], chunk(kc), chunk(vc),
                      grp(kse), grp(kso), grp(vsl), grp(kwn), grp(vwn), tok(gate),
                      stages[0].at[c], stages[1].at[c])


def _in_proj_tile(i, x_ref, nw_ref, w_ref, wg_ref, gain_ref,
                  qa_ref, kae_ref, kao_ref, vae_ref, vao_ref, qb_ref, kmean_ref, kc_ref, vc_ref,
                  kse_ref, kso_ref, vsl_ref, kwn_ref, vwn_ref, gate_ref, stage_k_ref, stage_v_ref):
    x = x_ref[0]
    ms = jnp.mean(x * x, axis=-1, keepdims=True)
    h = (x * lax.rsqrt(ms + EPS) * nw_ref[...]).astype(jnp.bfloat16)
    bf = jnp.bfloat16
    lane = lax.broadcasted_iota(jnp.int32, (TM, LANES), 1)
    row = lax.broadcasted_iota(jnp.int32, (TM, LANES), 0)
    lo_half = lane < HEAD_DIM
    blk = i * (TM // MOBA_BLOCK) + lax.shift_right_logical(row, MOBA_BLOCK.bit_length() - 1)
    sblk = i * (TM // SEL_BLOCK) + lax.shift_right_logical(row, SEL_BLOCK.bit_length() - 1)

    def normed(col, width, scale=None):
        p = _nn(h, w_ref[:, col:col + width])
        outs = []
        for s in range(width // LANES):
            c = col + s * LANES
            y = _seg_norm(p[:, s * LANES:(s + 1) * LANES], lo_half, gain_ref[:, c:c + LANES])
            outs.append(y if scale is None else y * scale)
        return outs

    qa = normed(_C_QA, 512, QSCALE)
    for s in range(4):
        qa_ref[0, :, s * LANES:(s + 1) * LANES] = qa[s].astype(bf)
    ka = normed(_C_KA, 512)
    oh_hi = jnp.where(lane == HEAD_DIM + blk, 1.0, 0.0)
    oh_lo = jnp.where(lane == blk, 1.0, 0.0)
    for s in range(4):
        kae_ref[0, :, s * LANES:(s + 1) * LANES] = jnp.where(lo_half, ka[s], oh_hi).astype(bf)
        kao_ref[0, :, s * LANES:(s + 1) * LANES] = jnp.where(lo_half, oh_lo, ka[s]).astype(bf)
        for j in range(TM // MOBA_BLOCK):
            kmean_ref[0, j, :, s * LANES:(s + 1) * LANES] = jnp.broadcast_to(
                jnp.mean(ka[s][j * MOBA_BLOCK:(j + 1) * MOBA_BLOCK], axis=0, keepdims=True), (8, LANES))
    va = _nn(h, w_ref[:, _C_VA:_C_VA + 512])
    for s in range(4):
        vs = va[:, s * LANES:(s + 1) * LANES]
        vae_ref[0, :, s * LANES:(s + 1) * LANES] = jnp.where(lo_half, vs, 1.0).astype(bf)
        vao_ref[0, :, s * LANES:(s + 1) * LANES] = jnp.where(lo_half, 1.0, vs).astype(bf)
    qb = normed(_C_QB, 512, QSCALE)
    for s in range(4):
        qb_ref[0, :, s * LANES:(s + 1) * LANES] = qb[s].astype(bf)

    rest = _nn(h, w_ref[:, _C_KC:_C_GATE])
    gate_ref[0] = jnp.concatenate([_nn(h[0:TM // 2], wg_ref[...]), _nn(h[TM // 2:], wg_ref[...])], axis=0)

    def slab(c):
        return rest[:, c - _C_KC:c - _C_KC + LANES]

    def chunk_store(ref, stage_ref, y):
        stage_ref[...] = y
        lo = lax.broadcasted_iota(jnp.int32, (TM // CMP_STRIDE, LANES), 1) < HEAD_DIM
        for u in range(CMP_STRIDE // 2):
            t0 = stage_ref[pl.ds(2 * u, TM // CMP_STRIDE, stride=CMP_STRIDE), :]
            t1 = stage_ref[pl.ds(2 * u + 1, TM // CMP_STRIDE, stride=CMP_STRIDE), :]
            ref[0, 0, :, u * LANES:(u + 1) * LANES] = jnp.where(lo, t0, pltpu.roll(t1, HEAD_DIM, 1)).astype(bf)
            ref[0, 1, :, u * LANES:(u + 1) * LANES] = jnp.where(lo, pltpu.roll(t0, HEAD_DIM, 1), t1).astype(bf)

    chunk_store(kc_ref, stage_k_ref, slab(_C_KC))
    chunk_store(vc_ref, stage_v_ref, slab(_C_VC))

    def dup_store(ref, y):
        r = pltpu.roll(y, HEAD_DIM, 1)
        ref[0, 0] = jnp.where(lo_half, y, r).astype(bf)
        ref[0, 1] = jnp.where(lo_half, r, y).astype(bf)

    y = _seg_norm(slab(_C_KSL), lo_half, gain_ref[:, _C_KSL:_C_KSL + LANES])
    r = pltpu.roll(y, HEAD_DIM, 1)
    oh = jnp.where((lane & (HEAD_DIM - 1)) == sblk, 1.0, 0.0)
    kse_ref[0, 0] = jnp.where(lo_half, y, oh).astype(bf)
    kso_ref[0, 0] = jnp.where(lo_half, oh, r).astype(bf)
    kse_ref[0, 1] = jnp.where(lo_half, r, oh).astype(bf)
    kso_ref[0, 1] = jnp.where(lo_half, oh, y).astype(bf)
    def aug_store(ref, y):
        ref[0, 0] = jnp.where(lo_half, y, 1.0).astype(bf)
        ref[0, 1] = jnp.where(lo_half, pltpu.roll(y, HEAD_DIM, 1), 1.0).astype(bf)

    aug_store(vsl_ref, slab(_C_VSL))
    dup_store(kwn_ref, _seg_norm(slab(_C_KWN), lo_half, gain_ref[:, _C_KWN:_C_KWN + LANES]))
    aug_store(vwn_ref, slab(_C_VWN))


def _in_proj(x, norm_w, w1, wg, gain_row):
    B, S, _ = x.shape
    nt = S // TQ
    bf = jnp.bfloat16
    tok = lambda w, dt: jax.ShapeDtypeStruct((B, S, w), dt)
    grp = jax.ShapeDtypeStruct((B, NSA_GROUPS, S, LANES), bf)
    rows = PROJ_CHAINS * TM
    tok_spec = lambda w: pl.BlockSpec((1, rows, w), lambda b, i: (b, i, 0))
    grp_spec = pl.BlockSpec((1, NSA_GROUPS, rows, LANES), lambda b, i: (b, 0, i, 0))
    flat = CMP_STRIDE * HEAD_DIM
    chunk = jax.ShapeDtypeStruct((B, NSA_GROUPS, S // CMP_STRIDE, flat), bf)
    chunk_spec = pl.BlockSpec((1, NSA_GROUPS, rows // CMP_STRIDE, flat), lambda b, i: (b, 0, i, 0))
    full = lambda a: pl.BlockSpec(a.shape, lambda b, i: (0,) * a.ndim, pipeline_mode=pl.Buffered(1))
    nw = norm_w.reshape(1, D_MODEL)
    return pl.pallas_call(
        _in_proj_body,
        grid=(B, S // rows),
        in_specs=[tok_spec(D_MODEL), full(nw), full(w1), full(wg), full(gain_row)],
        out_specs=[tok_spec(512), tok_spec(512), tok_spec(512), tok_spec(512), tok_spec(512), tok_spec(512),
                   pl.BlockSpec((1, rows // MOBA_BLOCK, 8, 512), lambda b, i: (b, i, 0, 0)),
                   chunk_spec, chunk_spec,
                   grp_spec, grp_spec, grp_spec, grp_spec, grp_spec, tok_spec(LANES)],
        out_shape=[tok(512, bf), tok(512, bf), tok(512, bf), tok(512, bf), tok(512, bf), tok(512, bf),
                   jax.ShapeDtypeStruct((B, nt, 8, 512), jnp.float32),
                   chunk, chunk, grp, grp, grp, grp, grp, tok(LANES, jnp.float32)],
        scratch_shapes=[pltpu.VMEM((PROJ_CHAINS, TM, LANES), jnp.float32)] * 2,
        compiler_params=pltpu.CompilerParams(dimension_semantics=("parallel", "parallel"),
                                             vmem_limit_bytes=VMEM_LIMIT),
        name="in_proj",
    )(x, nw, w1, wg, gain_row)


def _gelu_tanh(x):
    return 0.5 * x * (1.0 + jnp.tanh(math.sqrt(2.0 / math.pi) * (x + 0.044715 * (x * x * x))))


def _compress_body(xk_ref, xv_ref, w1k_ref, w1v_ref, pk_ref, pv_ref,
                   w2k_ref, w2v_ref, gk_ref, kcmp_ref, vcmp_ref):
    nchunk = xk_ref.shape[2]
    half = CMP_STRIDE * HEAD_DIM

    def branch(x_ref, w1_ref, pos_ref, w2_ref, gain):
        pos = jnp.broadcast_to(pos_ref[...], (8, CMP_LEN * HEAD_DIM))
        posw = (_nn(pos[:, 0:half], w1_ref[...])[0:1, 0:CMP_HIDDEN]
                + _nn(pos[:, half:], w1_ref[...])[0:1, CMP_HIDDEN:])
        outs = []
        for g in range(NSA_GROUPS):
            ab = _nn(x_ref[0, g], w1_ref[...])
            top, bot = ab[:, 0:CMP_HIDDEN], ab[:, CMP_HIDDEN:]
            hid = top + pltpu.roll(bot, nchunk - 1, 0) + posw
            y = _nn(_gelu_tanh(hid).astype(jnp.bfloat16), w2_ref[...])
            if gain is not None:
                ms = jnp.sum(y * y, axis=-1, keepdims=True) * (0.5 / HEAD_DIM)
                y = y * lax.rsqrt(ms + EPS) * gain
            else:
                lane = lax.broadcasted_iota(jnp.int32, y.shape, 1)
                y = jnp.where(lane < HEAD_DIM, y, jnp.where(lane == HEAD_DIM, 1.0, 0.0))
            outs.append(y.astype(jnp.bfloat16))
        return outs

    k0, k1 = branch(xk_ref, w1k_ref, pk_ref, w2k_ref, gk_ref[...])
    kcmp_ref[0, 0], kcmp_ref[0, 1] = k0, k1
    v0, v1 = branch(xv_ref, w1v_ref, pv_ref, w2v_ref, None)
    vcmp_ref[0, 0], vcmp_ref[0, 1] = v0, v1


def _compress(xk, xv, pos_k, w1_k, w2_k, pos_v, w1_v, w2_v, gain_cmp):
    B, _, nchunk, half = xk.shape
    bf = jnp.bfloat16
    halves = lambda w1: jnp.concatenate([w1[:half], w1[half:]], axis=1).astype(bf)
    args = [xk, xv, halves(w1_k), halves(w1_v),
            pos_k.reshape(1, -1).astype(bf), pos_v.reshape(1, -1).astype(bf),
            jnp.concatenate([w2_k, w2_k], axis=1).astype(bf), jnp.concatenate([w2_v, w2_v], axis=1).astype(bf),
            jnp.concatenate([gain_cmp, gain_cmp]).reshape(1, LANES)]
    x_spec = pl.BlockSpec((1, NSA_GROUPS, nchunk, half), lambda b: (b, 0, 0, 0))
    full = lambda a: pl.BlockSpec(a.shape, lambda b: (0,) * a.ndim)
    out = jax.ShapeDtypeStruct((B, NSA_GROUPS, nchunk, LANES), bf)
    o_spec = pl.BlockSpec((1, NSA_GROUPS, nchunk, LANES), lambda b: (b, 0, 0, 0))
    return pl.pallas_call(
        _compress_body,
        grid=(B,),
        in_specs=[x_spec, x_spec] + [full(a) for a in args[2:]],
        out_specs=[o_spec, o_spec],
        out_shape=[out, out],
        compiler_params=pltpu.CompilerParams(dimension_semantics=("parallel",), vmem_limit_bytes=VMEM_LIMIT),
        name="compress",
    )(*args)


def _softmax_step(s, v_top, v_bot, m_ref, acc_ref, first):
    blocks = [s[:, c * LANES:(c + 1) * LANES] for c in range(s.shape[1] // LANES)]
    bmax = blocks[0]
    for blk in blocks[1:]:
        bmax = jnp.maximum(bmax, blk)
    m_new = jnp.broadcast_to(jnp.max(bmax, axis=-1, keepdims=True), m_ref.shape)
    if not first:
        m_old = m_ref[...]
        m_new = jnp.maximum(m_old, m_new)
        alpha = jnp.exp2(m_old - m_new)
    pb = jnp.concatenate([jnp.exp2(blk - m_new) for blk in blocks], axis=1).astype(jnp.bfloat16)
    half = s.shape[0] // 2
    pv = jnp.concatenate([_nn(pb[0:half], v_top), _nn(pb[half:], v_bot)], axis=0)
    acc_ref[...] = pv if first else alpha * acc_ref[...] + pv
    m_ref[...] = m_new


def _diag_chunk(qq, kk, v_top, v_bot, bias_of, m_ref, acc_ref):
    nh = qq.shape[0] // TR
    q_lo = jnp.concatenate([qq[h * TR:h * TR + TQ] for h in range(nh)], axis=0)
    q_hi = jnp.concatenate([qq[h * TR + TQ:(h + 1) * TR] for h in range(nh)], axis=0)
    s_lo = _nt(q_lo, kk[0:TQ]) + jnp.concatenate([bias_of(h, 0) for h in range(nh)], axis=0)
    s_hi = _nt(q_hi, kk) + jnp.concatenate(
        [jnp.concatenate([bias_of(h, 1), bias_of(h, 0)], axis=1) for h in range(nh)], axis=0)
    for s, vt, vb, off in ((s_lo, v_top[0:TQ], v_bot[0:TQ], 0), (s_hi, v_top, v_bot, TQ)):
        blocks = [s[:, c * LANES:(c + 1) * LANES] for c in range(s.shape[1] // LANES)]
        bmax = blocks[0]
        for blk in blocks[1:]:
            bmax = jnp.maximum(bmax, blk)
        m = jnp.broadcast_to(jnp.max(bmax, axis=-1, keepdims=True), (nh * TQ, LANES))
        pb = jnp.concatenate([jnp.exp2(blk - m) for blk in blocks], axis=1).astype(jnp.bfloat16)
        half = nh * TQ // 2
        pv = jnp.concatenate([_nn(pb[0:half], vt), _nn(pb[half:], vb)], axis=0)
        for h in range(nh):
            m_ref[h * TR + off:h * TR + off + TQ] = m[h * TQ:(h + 1) * TQ]
            acc_ref[h * TR + off:h * TR + off + TQ] = pv[h * TQ:(h + 1) * TQ]


def _finish_pair(acc_even, acc_odd, lo_half, odd_flipped):
    swap = lambda x: pltpu.roll(x, HEAD_DIM, 1)
    if odd_flipped:
        return jnp.where(lo_half, acc_even, acc_odd) / swap(jnp.where(lo_half, acc_odd, acc_even))
    return jnp.where(lo_half, acc_even, swap(acc_odd)) / jnp.where(lo_half, swap(acc_even), acc_odd)


def _tile_kind(delta):
    return jnp.where(delta >= 0, jnp.minimum(delta, N_FAR), _K_MASKED)


def _moba_body(q_ref, ke_ref, ko_ref, ve_ref, vo_ref, kmean_ref, eye_ref, bias_ref, o_ref, m_ref, acc_ref):
    i = pl.program_id(2)
    nblk = kmean_ref.shape[1]
    lane = lax.broadcasted_iota(jnp.int32, (TR, LANES), 1)
    lo_half = lane < HEAD_DIM
    lane16 = lax.broadcasted_iota(jnp.int32, (nblk, LANES), 1)
    n_heads = 2 * MOBA_PAIRS
    jrow = lax.broadcasted_iota(jnp.int32, (nblk, n_heads * TR), 0)
    assert TR == 2 * TQ
    qblk = 2 * i + (lax.shift_right_logical(lax.broadcasted_iota(jnp.int32, (nblk, n_heads * TR), 1),
                                            TQ.bit_length() - 1) & 1)
    pad = jnp.zeros((HEAD_DIM - nblk, TR), jnp.float32)
    zero = jnp.zeros((TR, LANES), jnp.bfloat16)

    scores = []
    for pr in range(MOBA_PAIRS):
        q2 = q_ref[0, :, pr * LANES:(pr + 1) * LANES]
        kmean2 = kmean_ref[0, :, pr * LANES:(pr + 1) * LANES]
        for hh in range(2):
            inhead16 = (lane16 < HEAD_DIM) if hh == 0 else (lane16 >= HEAD_DIM)
            km_hi, km_lo = _split(jnp.where(inhead16, kmean2, 0.0))
            scores.append(_nt(km_hi, q2) + _nt(km_lo, q2))
    sc = jnp.where(jrow < qblk, jnp.concatenate(scores, axis=1), -jnp.inf)
    flag = jnp.where(jrow == qblk, 0.0, 1.0)
    for _ in range(MOBA_TOPK):
        best = jnp.max(sc, axis=0, keepdims=True)
        first = jnp.min(jnp.where(sc == best, jrow, nblk), axis=0, keepdims=True)
        hit = jrow == first
        flag = jnp.where(hit, 0.0, flag)
        sc = jnp.where(hit, -jnp.inf, sc)

    def augmented_q(pr):
        q2 = q_ref[0, :, pr * LANES:(pr + 1) * LANES]
        flags = [flag[:, (2 * pr + hh) * TR:(2 * pr + hh + 1) * TR] for hh in range(2)]
        placed = jnp.concatenate([flags[1], pad, flags[0], pad], axis=0).astype(jnp.bfloat16)
        maskpart = jnp.concatenate([_nt(eye_ref[...], placed[:, a * TQ:(a + 1) * TQ]) for a in range(TR // TQ)],
                                   axis=0) * NEG
        q2f = q2.astype(jnp.float32)
        return jnp.concatenate(
            [jnp.concatenate([jnp.where(lo_half, q2f, maskpart).astype(jnp.bfloat16), zero], axis=1),
             jnp.concatenate([zero, jnp.where(lo_half, maskpart, q2f).astype(jnp.bfloat16)], axis=1)], axis=0)

    qqs = [augmented_q(pr) for pr in range(MOBA_PAIRS)]

    def operands(c, pr):
        rows = pl.ds(pl.multiple_of(c * TK, TK), TK)
        cols = slice(pr * LANES, (pr + 1) * LANES)
        kk = jnp.concatenate([ke_ref[0, rows, cols], ko_ref[0, rows, cols]], axis=1)
        return kk, ve_ref[0, rows, cols], vo_ref[0, rows, cols]

    for pr in range(MOBA_PAIRS):
        kk, ve, vo = operands(i, pr)
        _diag_chunk(qqs[pr], kk, ve, vo, lambda hh, delta, pr=pr: bias_ref[2 * pr + hh, delta],
                    m_ref.at[pr], acc_ref.at[pr])

    def body(c, carry):
        d = 2 * (i - c)
        for pr in range(MOBA_PAIRS):
            kk, ve, vo = operands(c, pr)
            b = jnp.concatenate(
                [jnp.concatenate([bias_ref[2 * pr + hh, _tile_kind(d + a - t)] for t in range(2)], axis=1)
                 for hh in range(2) for a in range(2)], axis=0)
            _softmax_step(_nt(qqs[pr], kk) + b, ve, vo, m_ref.at[pr], acc_ref.at[pr], False)
        return carry

    lax.fori_loop(0, i, body, 0)
    for pr in range(MOBA_PAIRS):
        o_ref[0, :, pr * LANES:(pr + 1) * LANES] = _finish_pair(acc_ref[pr, 0:TR], acc_ref[pr, TR:2 * TR],
                                                                 lo_half, True)


def _eye(n):
    return jnp.asarray(np.eye(n, dtype=np.float32), jnp.bfloat16)


def _moba(qa, kae, kao, vae, vao, kmean, bias):
    B, S, _ = qa.shape
    nt = S // TQ
    width = MOBA_PAIRS * LANES
    kv_spec = pl.BlockSpec((1, S, width), lambda b, p, i: (b, 0, p), pipeline_mode=pl.Buffered(1))
    return pl.pallas_call(
        _moba_body,
        grid=(B, MOBA_HEADS // (2 * MOBA_PAIRS), S // TR),
        in_specs=[pl.BlockSpec((1, TR, width), lambda b, p, i: (b, i, p)),
                  kv_spec, kv_spec, kv_spec, kv_spec,
                  pl.BlockSpec((1, nt, width), lambda b, p, i: (b, 0, p)),
                  pl.BlockSpec((TQ, TQ), lambda b, p, i: (0, 0)),
                  pl.BlockSpec((2 * MOBA_PAIRS, _N_TILE_KINDS, TQ, TQ), lambda b, p, i: (p, 0, 0, 0),
                               pipeline_mode=pl.Buffered(1))],
        out_specs=pl.BlockSpec((1, TR, width), lambda b, p, i: (b, i, p)),
        out_shape=jax.ShapeDtypeStruct((B, S, MOBA_HEADS * HEAD_DIM), jnp.float32),
        scratch_shapes=[pltpu.VMEM((MOBA_PAIRS, 2 * TR, LANES), jnp.float32)] * 2,
        compiler_params=pltpu.CompilerParams(dimension_semantics=("parallel", "parallel", "arbitrary"),
                                             vmem_limit_bytes=VMEM_LIMIT),
        name="moba",
    )(qa, kae, kao, vae, vao, kmean, _eye(TQ), bias)


def _nsa_body(q_ref, kcmp_ref, vcmp_ref, kse_ref, kso_ref, vsl_ref, kwn_ref, vwn_ref, gate_ref, ov_ref, eye_ref,
              gexp_ref, bias_ref, o_ref, m_ref, acc_ref):
    i = pl.program_id(1)

    def to_hi(o):
        return pltpu.roll(o, HEAD_DIM, 1)

    ncmp = kcmp_ref.shape[2]
    nsel = ov_ref.shape[0]
    R = NSA_HPG * TR
    bf = jnp.bfloat16
    lane = lax.broadcasted_iota(jnp.int32, (TR, LANES), 1)
    lo_half = lane < HEAD_DIM
    sel_heads = (0, 2, 1, 3)

    def compressed(g):
        qf = [q_ref[0, :, (2 * g + s) * LANES:(2 * g + s + 1) * LANES].astype(jnp.float32) for s in range(2)]
        q4 = jnp.concatenate([jnp.where(lo_half if hh % 2 == 0 else ~lo_half, qf[hh // 2], 0.0).astype(bf)
                              for hh in range(NSA_HPG)], axis=0)

        tpos = i * TR + lax.broadcasted_iota(jnp.int32, (TR, ncmp), 0)
        cidx = lax.broadcasted_iota(jnp.int32, (TR, ncmp), 1)
        cbias1 = jnp.where((cidx * CMP_STRIDE + (CMP_LEN - 1) <= tpos) & (cidx < ncmp - 1), 0.0, NEG)
        lc = _nt(q4, kcmp_ref[0, g]) + jnp.concatenate([cbias1] * NSA_HPG, axis=0)
        cblocks = [lc[:, c * LANES:(c + 1) * LANES] for c in range(ncmp // LANES)]
        cmax = cblocks[0]
        for blk in cblocks[1:]:
            cmax = jnp.maximum(cmax, blk)
        cm = jnp.maximum(jnp.broadcast_to(jnp.max(cmax, axis=-1, keepdims=True), (R, LANES)), 0.5 * NEG)
        ce = [jnp.exp2(blk - cm) for blk in cblocks]
        cden = ce[0]
        for eb in ce[1:]:
            cden = cden + eb
        cden = jnp.broadcast_to(jnp.sum(cden, axis=-1, keepdims=True), (R, LANES))
        crcp = 1.0 / jnp.where(cden > 0.0, cden, 1.0)
        pc = jnp.concatenate([eb * crcp for eb in ce], axis=1)
        pcb = pc.astype(bf)
        o_cmp = jnp.concatenate([_nn(pcb[0:R // 2], vcmp_ref[0, g]), _nn(pcb[R // 2:], vcmp_ref[0, g])], axis=0)
        psum = pc[0:TR] + pc[TR:2 * TR] + pc[2 * TR:3 * TR] + pc[3 * TR:4 * TR]
        p_hi, p_lo = _split(psum)
        ov = ov_ref[...]
        imp = _nt(ov, p_hi) + _nt(ov, p_lo)
        return qf, q4, o_cmp, imp

    def window(g, q4):
        o_win = []
        for a in range(2):
            qt = 2 * i + a
            t0 = jnp.maximum(qt - (N_WIN_TILES - 1), 0)
            wstart = pl.multiple_of(t0 * TQ, TQ)
            qh = jnp.concatenate([q4[hh * TR + a * TQ:hh * TR + (a + 1) * TQ] for hh in range(NSA_HPG)], axis=0)
            kinds = []
            for r in range(N_WIN_TILES):
                dd = qt - (t0 + r)
                kinds.append(jnp.where(dd >= 0, _K_WIN + dd, _K_MASKED))
            bw = jnp.concatenate([jnp.concatenate([bias_ref[NSA_HPG * g + hh, k] for k in kinds], axis=1)
                                  for hh in range(NSA_HPG)], axis=0)
            sw = _nt(qh, kwn_ref[0, g, pl.ds(wstart, N_WIN_TILES * TQ), :]) + bw
            wm, wacc = m_ref.at[g, 0:NSA_HPG * TQ], acc_ref.at[g, 0:NSA_HPG * TQ]
            vw = vwn_ref[0, g, pl.ds(wstart, N_WIN_TILES * TQ), :]
            _softmax_step(sw, vw, vw, wm, wacc, True)
            wv = wacc[...]
            o_win.append(wv / to_hi(wv))
        return o_win

    def select(imps):
        width = NSA_GROUPS * TR
        jrow = lax.broadcasted_iota(jnp.int32, (nsel, width), 0)
        qblk = lax.shift_right_logical(i * TR + (lax.broadcasted_iota(jnp.int32, (nsel, width), 1) & (TR - 1)),
                                       SEL_BLOCK.bit_length() - 1)
        cand = (jrow >= 1) & (jrow < qblk)
        keys = jnp.where(cand, jnp.concatenate(imps, axis=1), -1.0)
        flag = jnp.where((jrow == 0) | (jrow == qblk), 0.0, 1.0)
        for _ in range(SEL_TOPN - 2):
            best = jnp.max(keys, axis=0, keepdims=True)
            first = jnp.min(jnp.where(keys == best, jrow, nsel), axis=0, keepdims=True)
            hit = jrow == first
            flag = jnp.where(hit, 0.0, flag)
            keys = jnp.where(hit, -1.0, keys)
        return flag

    def augmented_q(qf, flag):
        if nsel < HEAD_DIM:
            flag = jnp.concatenate([flag, jnp.zeros((HEAD_DIM - nsel, TR), jnp.float32)], axis=0)
        flag2 = jnp.concatenate([flag, flag], axis=0).astype(bf)
        maskpart = jnp.concatenate([_nt(eye_ref[...], flag2[:, a * TQ:(a + 1) * TQ]) for a in range(TR // TQ)],
                                   axis=0) * NEG

        zero = jnp.zeros((2 * TR, LANES), bf)
        qa2 = [jnp.concatenate([jnp.where(lo_half if par == 0 else ~lo_half, qf[s], maskpart).astype(bf)
                                for s in range(2)], axis=0) for par in range(2)]
        return jnp.concatenate([jnp.concatenate([qa2[0], zero], axis=1),
                                jnp.concatenate([zero, qa2[1]], axis=1)], axis=0)

    cmp_parts = [compressed(g) for g in range(NSA_GROUPS)]
    o_wins = [window(g, cmp_parts[g][1]) for g in range(NSA_GROUPS)]
    flags = select([part[3] for part in cmp_parts])
    state = [(augmented_q(cmp_parts[g][0], flags[:, g * TR:(g + 1) * TR]), cmp_parts[g][2], o_wins[g])
             for g in range(NSA_GROUPS)]

    def sel_operands(c, g):
        rows = pl.ds(pl.multiple_of(c * TK, TK), TK)
        kk = jnp.concatenate([kse_ref[0, g, rows, :], kso_ref[0, g, rows, :]], axis=1)
        return kk, vsl_ref[0, g, rows, :]

    for g in range(NSA_GROUPS):
        kk, v = sel_operands(i, g)
        _diag_chunk(state[g][0], kk, v, v, lambda hh, delta, g=g: bias_ref[NSA_HPG * g + sel_heads[hh], delta],
                    m_ref.at[g], acc_ref.at[g])

    def sel_body(c, carry):
        d = 2 * (i - c)
        for g in range(NSA_GROUPS):
            kk, v = sel_operands(c, g)
            b = jnp.concatenate(
                [jnp.concatenate([bias_ref[NSA_HPG * g + h, _tile_kind(d + a - t)] for t in range(2)], axis=1)
                 for h in sel_heads for a in range(2)], axis=0)
            _softmax_step(_nt(state[g][0], kk) + b, v, v, m_ref.at[g], acc_ref.at[g], False)
        return carry

    lax.fori_loop(0, i, sel_body, 0)

    gate = jax.nn.sigmoid(gate_ref[0])
    g_hi, g_lo = _split(gate)
    for g in range(NSA_GROUPS):
        _, o_cmp, o_win = state[g]
        gx =_nn(g_hi, gexp_ref[g]) + _nn(g_lo, gexp_ref[g])
        for s in range(2):
            h_even, h_odd = 2 * s, 2 * s + 1
            cmp2 = jnp.where(lo_half, o_cmp[h_even * TR:(h_even + 1) * TR],
                             to_hi(o_cmp[h_odd * TR:(h_odd + 1) * TR]))
            sel2 = _finish_pair(acc_ref[g, s * TR:(s + 1) * TR], acc_ref[g, (2 + s) * TR:(3 + s) * TR],
                                lo_half, False)
            win2 = jnp.where(lo_half,
                             jnp.concatenate([o_win[a][h_even * TQ:(h_even + 1) * TQ] for a in range(2)], axis=0),
                             to_hi(jnp.concatenate([o_win[a][h_odd * TQ:(h_odd + 1) * TQ] for a in range(2)], axis=0)))
            base = s * 3 * LANES
            o_ref[0, :, (2 * g + s) * LANES:(2 * g + s + 1) * LANES] = (
                gx[:, base:base + LANES] * cmp2 + gx[:, base + LANES:base + 2 * LANES] * sel2
                + gx[:, base + 2 * LANES:base + 3 * LANES] * win2)


def _gate_expand():
    e = np.zeros((NSA_GROUPS, LANES, 2 * 3 * LANES), np.float32)
    for g in range(NSA_GROUPS):
        for s in range(2):
            for c in range(3):
                for half in range(2):
                    h = g * NSA_HPG + 2 * s + half
                    col = (s * 3 + c) * LANES + half * HEAD_DIM
                    e[g, 3 * h + c, col:col + HEAD_DIM] = 1.0
    return e


def _nsa(qb, kcmp, vcmp, kse, kso, vsl, kwn, vwn, gate, bias):
    B, S, _ = qb.shape
    nt = S // TQ
    ncmp = kcmp.shape[2]
    nsel = S // SEL_BLOCK
    bf = jnp.bfloat16
    cs = np.arange(ncmp) * CMP_STRIDE
    bs = np.arange(nsel) * SEL_BLOCK
    ov = ((cs[None, :] < bs[:, None] + SEL_BLOCK) & (cs[None, :] + CMP_LEN > bs[:, None])
          & (np.arange(ncmp)[None, :] < ncmp - 1)).astype(np.float32)
    once = pl.Buffered(1)
    grp = lambda n: pl.BlockSpec((1, NSA_GROUPS, n, LANES), lambda b, i: (b, 0, 0, 0), pipeline_mode=once)
    width = NSA_HEADS * HEAD_DIM
    return pl.pallas_call(
        _nsa_body,
        grid=(B, S // TR),
        in_specs=[pl.BlockSpec((1, TR, width), lambda b, i: (b, i, 0)),
                  grp(ncmp), grp(ncmp), grp(S), grp(S), grp(S), grp(S), grp(S),
                  pl.BlockSpec((1, TR, LANES), lambda b, i: (b, i, 0)),
                  pl.BlockSpec((nsel, ncmp), lambda b, i: (0, 0), pipeline_mode=once),
                  pl.BlockSpec((TQ, TQ), lambda b, i: (0, 0), pipeline_mode=once),
                  pl.BlockSpec((NSA_GROUPS, LANES, 6 * LANES), lambda b, i: (0, 0, 0), pipeline_mode=once),
                  pl.BlockSpec((NSA_HEADS, _N_TILE_KINDS, TQ, TQ), lambda b, i: (1, 0, 0, 0), pipeline_mode=once)],
        out_specs=pl.BlockSpec((1, TR, width), lambda b, i: (b, i, 0)),
        out_shape=jax.ShapeDtypeStruct((B, S, width), jnp.float32),
        scratch_shapes=[pltpu.VMEM((NSA_GROUPS, NSA_HPG * TR, LANES), jnp.float32)] * 2,
        compiler_params=pltpu.CompilerParams(dimension_semantics=("parallel", "arbitrary"),
                                             vmem_limit_bytes=VMEM_LIMIT),
        name="nsa",
    )(qb, kcmp, vcmp, kse, kso, vsl, kwn, vwn, gate, jnp.asarray(ov, bf), _eye(TQ),
      jnp.asarray(_gate_expand(), bf), bias)


def _out_proj_body(x_ref, nw_ref, oa_ref, ob_ref, wz_ref, wzb_ref, wgm_ref, wa_ref, wb_ref, wo_ref, out_ref):
    for c in range(PROJ_CHAINS):
        rows = pl.ds(c * TM, TM)
        x = x_ref[0, rows, :]
        ms = jnp.mean(x * x, axis=-1, keepdims=True)
        h = (x * lax.rsqrt(ms + EPS) * nw_ref[...]).astype(jnp.bfloat16)
        ya = (oa_ref[0, rows, :] * jax.nn.silu(_nn(h, wz_ref[:, _C_ZA:_C_ZA + 512]))).astype(jnp.bfloat16)
        yb = (ob_ref[0, rows, :] * jax.nn.silu(_nn(h, wzb_ref[...]))).astype(jnp.bfloat16)
        gm = jax.nn.sigmoid(_nn(h, wgm_ref[...]))
        merged = gm[:, 0:D_MODEL] * _nn(ya, wa_ref[...]) + gm[:, D_MODEL:] * _nn(yb, wb_ref[...])
        out_ref[0, rows, :] = x + _nn(merged.astype(jnp.bfloat16), wo_ref[...])


def _out_proj(x, norm_w, oa, ob, wz, wzb, wgm, wa, wb, wo):
    B, S, _ = x.shape
    nw = norm_w.reshape(1, D_MODEL)
    tok = lambda w: pl.BlockSpec((1, PROJ_CHAINS * TM, w), lambda b, i: (b, i, 0))
    full = lambda a: pl.BlockSpec(a.shape, lambda b, i: (0,) * a.ndim, pipeline_mode=pl.Buffered(1))
    return pl.pallas_call(
        _out_proj_body,
        grid=(B, S // (PROJ_CHAINS * TM)),
        in_specs=[tok(D_MODEL), full(nw), tok(512), tok(512), full(wz), full(wzb), full(wgm),
                  full(wa), full(wb), full(wo)],
        out_specs=tok(D_MODEL),
        out_shape=jax.ShapeDtypeStruct((B, S, D_MODEL), jnp.float32),
        compiler_params=pltpu.CompilerParams(dimension_semantics=("parallel", "parallel"),
                                             vmem_limit_bytes=VMEM_LIMIT),
        name="out_proj",
    )(x, nw, oa, ob, wz, wzb, wgm, wa, wb, wo)


def _pack_in_weights(w_in, q_norm_a, k_norm_a, q_norm_b, k_norm_sel, k_norm_win):
    bf = jnp.bfloat16
    ng = 3 * NSA_HEADS
    cols = lambda a, b: lax.slice_in_dim(w_in, a, b, axis=2).astype(bf).reshape(D_MODEL, b - a)
    z_b = _C_GATE + ng
    w = (cols(0, _C_GATE),
         jnp.pad(cols(_C_GATE, z_b), ((0, 0), (0, LANES - ng))),
         cols(z_b, z_b + NSA_HEADS * HEAD_DIM),
         cols(z_b + NSA_HEADS * HEAD_DIM, w_in.shape[2]))
    ones = lambda n: jnp.ones((n,), jnp.float32)
    gain = jnp.concatenate([jnp.tile(q_norm_a, MOBA_HEADS), jnp.tile(k_norm_a, MOBA_HEADS), ones(1024),
                            jnp.tile(q_norm_b, NSA_HEADS), ones(256), jnp.tile(k_norm_sel, NSA_GROUPS), ones(128),
                            jnp.tile(k_norm_win, NSA_GROUPS), ones(256)]).reshape(1, _N1)
    return w, gain


def kernel(x, norm_w, w_in, q_norm_a, k_norm_a, q_norm_b, k_norm_cmp, k_norm_sel, k_norm_win, cmp_pos_k, cmp_w1_k, cmp_w2_k, cmp_pos_v, cmp_w1_v, cmp_w2_v, rel_bias, w_branch_a, w_branch_b, w_out):
    bf = jnp.bfloat16
    (w_main, w_gate, w_zb, w_gm), gain = _pack_in_weights(w_in, q_norm_a[0], k_norm_a[0], q_norm_b[0],
                                                           k_norm_sel[0], k_norm_win[0])
    bias = _bias_tiles(rel_bias)
    qa, kae, kao, vae, vao, qb, kmean, kc, vc, kse, kso, vsl, kwn, vwn, gate = _in_proj(x, norm_w[0], w_main, w_gate,
                                                                                        gain)
    kcmp, vcmp = _compress(kc, vc, cmp_pos_k[0], cmp_w1_k[0], cmp_w2_k[0],
                           cmp_pos_v[0], cmp_w1_v[0], cmp_w2_v[0], k_norm_cmp[0])
    oa = _moba(qa, kae, kao, vae, vao, kmean[:, :, 0, :], bias)
    ob = _nsa(qb, kcmp, vcmp, kse, kso, vsl, kwn, vwn, gate, bias)
    return _out_proj(x, norm_w[0], oa, ob, w_main, w_zb, w_gm, w_branch_a[0].astype(bf), w_branch_b[0].astype(bf),
                     w_out[0].astype(bf))
```

```python
import math

import jax
import jax.numpy as jnp
import numpy as np
from jax import lax
from jax.experimental import pallas as pl
from jax.experimental.pallas import tpu as pltpu

D_MODEL = 1024
HEAD_DIM = 64
MOBA_HEADS = 8
NSA_HEADS = 8
NSA_GROUPS = 2
NSA_HPG = NSA_HEADS // NSA_GROUPS
MOBA_BLOCK = 256
MOBA_TOPK = 3
CMP_LEN = 32
CMP_STRIDE = 16
CMP_HIDDEN = 256
SEL_BLOCK = 64
SEL_TOPN = 16
WINDOW = 512
NUM_BUCKETS = 32
MAX_DISTANCE = 1024
EPS = 1e-6
NEG = -1e30

LANES = 128
TQ = 256
TK = 2 * TQ
TR = 2 * TQ
TM = 2 * TQ
MOBA_PAIRS = 4
PROJ_CHAINS = 2
N_FAR = 5
N_WIN_TILES = WINDOW // TQ + 1
VMEM_LIMIT = 56 * 1024 * 1024
LOG2E = math.log2(math.e)
QSCALE = LOG2E / math.sqrt(HEAD_DIM)

_K_WIN = N_FAR + 1
_K_MASKED = _K_WIN + N_WIN_TILES
_N_TILE_KINDS = _K_MASKED + 1

_C_QA, _C_KA, _C_VA, _C_ZA, _C_QB = 0, 512, 1024, 1536, 2048
_C_KC, _C_VC, _C_KSL, _C_VSL, _C_KWN, _C_VWN, _C_GATE = 2560, 2688, 2816, 2944, 3072, 3200, 3328
_N1 = _C_GATE + LANES


def _nt(a, b):
    return lax.dot_general(a, b, (((1,), (1,)), ((), ())), preferred_element_type=jnp.float32)


def _nn(a, b):
    return jnp.dot(a, b, preferred_element_type=jnp.float32)


def _split(x):
    hi = x.astype(jnp.bfloat16)
    lo = (x - hi.astype(jnp.float32)).astype(jnp.bfloat16)
    return hi, lo


def _bucket_table(n):
    d = np.arange(n)
    nf = np.maximum(d, NUM_BUCKETS // 2).astype(np.float64)
    large = NUM_BUCKETS // 2 + (np.log(nf / (NUM_BUCKETS // 2)) / math.log(MAX_DISTANCE / (NUM_BUCKETS // 2))
                                * (NUM_BUCKETS - NUM_BUCKETS // 2)).astype(np.int64)
    return np.where(d < NUM_BUCKETS // 2, d, np.minimum(large, NUM_BUCKETS - 1))


def _bias_onehot():
    oh = np.zeros((40, _N_TILE_KINDS * 2 * TQ), np.float32)
    buckets = _bucket_table((N_FAR + 2) * TQ + 1)
    for t in range(_N_TILE_KINDS):
        window = _K_WIN <= t < _K_MASKED
        delta = t - _K_WIN if window else t
        for u in range(2 * TQ):
            d = TQ * delta + TQ - u
            masked = t == _K_MASKED or d < 0 or (window and d >= WINDOW)
            oh[NUM_BUCKETS if masked else buckets[d], t * 2 * TQ + u] = 1.0
    return oh


def _bias_body(relt_ref, oh_ref, out_ref):
    g = jnp.dot(jnp.broadcast_to(relt_ref[0], (8, 40)), oh_ref[...], preferred_element_type=jnp.float32,
                precision=lax.Precision.HIGHEST)
    for t in range(_N_TILE_KINDS):
        full = jnp.broadcast_to(g[0:1, t * 2 * TQ:(t + 1) * 2 * TQ], (TQ, 2 * TQ))
        rolled = pltpu.roll(full, TQ, 1, stride=1, stride_axis=0)
        out_ref[0, t] = rolled[:, :TQ]


def _bias_tiles(rel_bias):
    nh = MOBA_HEADS + NSA_HEADS
    relt = jnp.concatenate([rel_bias.T.astype(jnp.float32) * LOG2E, jnp.full((nh, 1), NEG, jnp.float32),
                            jnp.zeros((nh, 7), jnp.float32)], axis=1).reshape(nh, 1, 40)
    oh = jnp.asarray(_bias_onehot())
    return pl.pallas_call(
        _bias_body,
        grid=(nh,),
        in_specs=[pl.BlockSpec((1, 1, 40), lambda h: (h, 0, 0)), pl.BlockSpec(oh.shape, lambda h: (0, 0))],
        out_specs=pl.BlockSpec((1, _N_TILE_KINDS, TQ, TQ), lambda h: (h, 0, 0, 0)),
        out_shape=jax.ShapeDtypeStruct((nh, _N_TILE_KINDS, TQ, TQ), jnp.float32),
        compiler_params=pltpu.CompilerParams(dimension_semantics=("parallel",), vmem_limit_bytes=VMEM_LIMIT),
        name="bias_tiles",
    )(relt, oh)


def _seg_norm(x, lo_half, gain):
    x2 = x * x
    ss_lo = jnp.sum(jnp.where(lo_half, x2, 0.0), axis=-1, keepdims=True)
    ss_hi = jnp.sum(jnp.where(lo_half, 0.0, x2), axis=-1, keepdims=True)
    ss = jnp.where(lo_half, ss_lo, ss_hi)
    return x * lax.rsqrt(ss * (1.0 / HEAD_DIM) + EPS) * gain


def _in_proj_body(x_ref, nw_ref, w_ref, wg_ref, gain_ref, *refs):
    outs, stages = refs[:-2], refs[-2:]
    for c in range(PROJ_CHAINS):
        rows = pl.ds(c * TM, TM)
        tok = lambda r: r.at[:, rows, :]
        grp = lambda r: r.at[:, :, rows, :]
        chunk = lambda r: r.at[:, :, pl.ds(c * (TM // CMP_STRIDE), TM // CMP_STRIDE), :]
        (qa, kae, kao, vae, vao, qb, kmean, kc, vc, kse, kso, vsl, kwn, vwn, gate) = outs
        _in_proj_tile(pl.program_id(1) * PROJ_CHAINS + c, tok(x_ref), nw_ref, w_ref, wg_ref, gain_ref,
                      tok(qa), tok(kae), tok(kao), tok(vae), tok(vao), tok(qb),
                      kmean.at[:, pl.ds(c * (TM // MOBA_BLOCK), TM // MOBA_BLOCK)], chunk(kc), chunk(vc),
                      grp(kse), grp(kso), grp(vsl), grp(kwn), grp(vwn), tok(gate),
                      stages[0].at[c], stages[1].at[c])


def _in_proj_tile(i, x_ref, nw_ref, w_ref, wg_ref, gain_ref,
                  qa_ref, kae_ref, kao_ref, vae_ref, vao_ref, qb_ref, kmean_ref, kc_ref, vc_ref,
                  kse_ref, kso_ref, vsl_ref, kwn_ref, vwn_ref, gate_ref, stage_k_ref, stage_v_ref):
    x = x_ref[0]
    ms = jnp.mean(x * x, axis=-1, keepdims=True)
    h = (x * lax.rsqrt(ms + EPS) * nw_ref[...]).astype(jnp.bfloat16)
    bf = jnp.bfloat16
    lane = lax.broadcasted_iota(jnp.int32, (TM, LANES), 1)
    row = lax.broadcasted_iota(jnp.int32, (TM, LANES), 0)
    lo_half = lane < HEAD_DIM
    blk = i * (TM // MOBA_BLOCK) + lax.shift_right_logical(row, MOBA_BLOCK.bit_length() - 1)
    sblk = i * (TM // SEL_BLOCK) + lax.shift_right_logical(row, SEL_BLOCK.bit_length() - 1)

    def normed(col, width, scale=None):
        p = _nn(h, w_ref[:, col:col + width])
        outs = []
        for s in range(width // LANES):
            c = col + s * LANES
            y = _seg_norm(p[:, s * LANES:(s + 1) * LANES], lo_half, gain_ref[:, c:c + LANES])
            outs.append(y if scale is None else y * scale)
        return outs

    qa = normed(_C_QA, 512, QSCALE)
    for s in range(4):
        qa_ref[0, :, s * LANES:(s + 1) * LANES] = qa[s].astype(bf)
    ka = normed(_C_KA, 512)
    oh_hi = jnp.where(lane == HEAD_DIM + blk, 1.0, 0.0)
    oh_lo = jnp.where(lane == blk, 1.0, 0.0)
    for s in range(4):
        kae_ref[0, :, s * LANES:(s + 1) * LANES] = jnp.where(lo_half, ka[s], oh_hi).astype(bf)
        kao_ref[0, :, s * LANES:(s + 1) * LANES] = jnp.where(lo_half, oh_lo, ka[s]).astype(bf)
        for j in range(TM // MOBA_BLOCK):
            kmean_ref[0, j, :, s * LANES:(s + 1) * LANES] = jnp.broadcast_to(
                jnp.mean(ka[s][j * MOBA_BLOCK:(j + 1) * MOBA_BLOCK], axis=0, keepdims=True), (8, LANES))
    va = _nn(h, w_ref[:, _C_VA:_C_VA + 512])
    for s in range(4):
        vs = va[:, s * LANES:(s + 1) * LANES]
        vae_ref[0, :, s * LANES:(s + 1) * LANES] = jnp.where(lo_half, vs, 1.0).astype(bf)
        vao_ref[0, :, s * LANES:(s + 1) * LANES] = jnp.where(lo_half, 1.0, vs).astype(bf)
    qb = normed(_C_QB, 512, QSCALE)
    for s in range(4):
        qb_ref[0, :, s * LANES:(s + 1) * LANES] = qb[s].astype(bf)

    rest = _nn(h, w_ref[:, _C_KC:_C_GATE])
    gate_ref[0] = jnp.concatenate([_nn(h[0:TM // 2], wg_ref[...]), _nn(h[TM // 2:], wg_ref[...])], axis=0)

    def slab(c):
        return rest[:, c - _C_KC:c - _C_KC + LANES]

    def chunk_store(ref, stage_ref, y):
        stage_ref[...] = y
        lo = lax.broadcasted_iota(jnp.int32, (TM // CMP_STRIDE, LANES), 1) < HEAD_DIM
        for u in range(CMP_STRIDE // 2):
            t0 = stage_ref[pl.ds(2 * u, TM // CMP_STRIDE, stride=CMP_STRIDE), :]
            t1 = stage_ref[pl.ds(2 * u + 1, TM // CMP_STRIDE, stride=CMP_STRIDE), :]
            ref[0, 0, :, u * LANES:(u + 1) * LANES] = jnp.where(lo, t0, pltpu.roll(t1, HEAD_DIM, 1)).astype(bf)
            ref[0, 1, :, u * LANES:(u + 1) * LANES] = jnp.where(lo, pltpu.roll(t0, HEAD_DIM, 1), t1).astype(bf)

    chunk_store(kc_ref, stage_k_ref, slab(_C_KC))
    chunk_store(vc_ref, stage_v_ref, slab(_C_VC))

    def dup_store(ref, y):
        r = pltpu.roll(y, HEAD_DIM, 1)
        ref[0, 0] = jnp.where(lo_half, y, r).astype(bf)
        ref[0, 1] = jnp.where(lo_half, r, y).astype(bf)

    y = _seg_norm(slab(_C_KSL), lo_half, gain_ref[:, _C_KSL:_C_KSL + LANES])
    r = pltpu.roll(y, HEAD_DIM, 1)
    oh = jnp.where((lane & (HEAD_DIM - 1)) == sblk, 1.0, 0.0)
    kse_ref[0, 0] = jnp.where(lo_half, y, oh).astype(bf)
    kso_ref[0, 0] = jnp.where(lo_half, oh, r).astype(bf)
    kse_ref[0, 1] = jnp.where(lo_half, r, oh).astype(bf)
    kso_ref[0, 1] = jnp.where(lo_half, oh, y).astype(bf)
    def aug_store(ref, y):
        ref[0, 0] = jnp.where(lo_half, y, 1.0).astype(bf)
        ref[0, 1] = jnp.where(lo_half, pltpu.roll(y, HEAD_DIM, 1), 1.0).astype(bf)

    aug_store(vsl_ref, slab(_C_VSL))
    dup_store(kwn_ref, _seg_norm(slab(_C_KWN), lo_half, gain_ref[:, _C_KWN:_C_KWN + LANES]))
    aug_store(vwn_ref, slab(_C_VWN))


def _in_proj(x, norm_w, w1, wg, gain_row):
    B, S, _ = x.shape
    nt = S // TQ
    bf = jnp.bfloat16
    tok = lambda w, dt: jax.ShapeDtypeStruct((B, S, w), dt)
    grp = jax.ShapeDtypeStruct((B, NSA_GROUPS, S, LANES), bf)
    rows = PROJ_CHAINS * TM
    tok_spec = lambda w: pl.BlockSpec((1, rows, w), lambda b, i: (b, i, 0))
    grp_spec = pl.BlockSpec((1, NSA_GROUPS, rows, LANES), lambda b, i: (b, 0, i, 0))
    flat = CMP_STRIDE * HEAD_DIM
    chunk = jax.ShapeDtypeStruct((B, NSA_GROUPS, S // CMP_STRIDE, flat), bf)
    chunk_spec = pl.BlockSpec((1, NSA_GROUPS, rows // CMP_STRIDE, flat), lambda b, i: (b, 0, i, 0))
    full = lambda a: pl.BlockSpec(a.shape, lambda b, i: (0,) * a.ndim, pipeline_mode=pl.Buffered(1))
    nw = norm_w.reshape(1, D_MODEL)
    return pl.pallas_call(
        _in_proj_body,
        grid=(B, S // rows),
        in_specs=[tok_spec(D_MODEL), full(nw), full(w1), full(wg), full(gain_row)],
        out_specs=[tok_spec(512), tok_spec(512), tok_spec(512), tok_spec(512), tok_spec(512), tok_spec(512),
                   pl.BlockSpec((1, rows // MOBA_BLOCK, 8, 512), lambda b, i: (b, i, 0, 0)),
                   chunk_spec, chunk_spec,
                   grp_spec, grp_spec, grp_spec, grp_spec, grp_spec, tok_spec(LANES)],
        out_shape=[tok(512, bf), tok(512, bf), tok(512, bf), tok(512, bf), tok(512, bf), tok(512, bf),
                   jax.ShapeDtypeStruct((B, nt, 8, 512), jnp.float32),
                   chunk, chunk, grp, grp, grp, grp, grp, tok(LANES, jnp.float32)],
        scratch_shapes=[pltpu.VMEM((PROJ_CHAINS, TM, LANES), jnp.float32)] * 2,
        compiler_params=pltpu.CompilerParams(dimension_semantics=("parallel", "parallel"),
                                             vmem_limit_bytes=VMEM_LIMIT),
        name="in_proj",
    )(x, nw, w1, wg, gain_row)


def _gelu_tanh(x):
    return 0.5 * x * (1.0 + jnp.tanh(math.sqrt(2.0 / math.pi) * (x + 0.044715 * (x * x * x))))


def _compress_body(xk_ref, xv_ref, w1k_ref, w1v_ref, pk_ref, pv_ref,
                   w2k_ref, w2v_ref, gk_ref, kcmp_ref, vcmp_ref):
    nchunk = xk_ref.shape[2]
    half = CMP_STRIDE * HEAD_DIM

    def branch(x_ref, w1_ref, pos_ref, w2_ref, gain):
        pos = jnp.broadcast_to(pos_ref[...], (8, CMP_LEN * HEAD_DIM))
        posw = (_nn(pos[:, 0:half], w1_ref[...])[0:1, 0:CMP_HIDDEN]
                + _nn(pos[:, half:], w1_ref[...])[0:1, CMP_HIDDEN:])
        outs = []
        for g in range(NSA_GROUPS):
            ab = _nn(x_ref[0, g], w1_ref[...])
            top, bot = ab[:, 0:CMP_HIDDEN], ab[:, CMP_HIDDEN:]
            hid = top + pltpu.roll(bot, nchunk - 1, 0) + posw
            y = _nn(_gelu_tanh(hid).astype(jnp.bfloat16), w2_ref[...])
            if gain is not None:
                ms = jnp.sum(y * y, axis=-1, keepdims=True) * (0.5 / HEAD_DIM)
                y = y * lax.rsqrt(ms + EPS) * gain
            outs.append(y.astype(jnp.bfloat16))
        return outs

    k0, k1 = branch(xk_ref, w1k_ref, pk_ref, w2k_ref, gk_ref[...])
    kcmp_ref[0, 0], kcmp_ref[0, 1] = k0, k1
    v0, v1 = branch(xv_ref, w1v_ref, pv_ref, w2v_ref, None)
    vcmp_ref[0, 0], vcmp_ref[0, 1] = v0, v1


def _compress(xk, xv, pos_k, w1_k, w2_k, pos_v, w1_v, w2_v, gain_cmp):
    B, _, nchunk, half = xk.shape
    bf = jnp.bfloat16
    halves = lambda w1: jnp.concatenate([w1[:half], w1[half:]], axis=1).astype(bf)
    args = [xk, xv, halves(w1_k), halves(w1_v),
            pos_k.reshape(1, -1).astype(bf), pos_v.reshape(1, -1).astype(bf),
            jnp.concatenate([w2_k, w2_k], axis=1).astype(bf), jnp.concatenate([w2_v, w2_v], axis=1).astype(bf),
            jnp.concatenate([gain_cmp, gain_cmp]).reshape(1, LANES)]
    x_spec = pl.BlockSpec((1, NSA_GROUPS, nchunk, half), lambda b: (b, 0, 0, 0))
    full = lambda a: pl.BlockSpec(a.shape, lambda b: (0,) * a.ndim)
    out = jax.ShapeDtypeStruct((B, NSA_GROUPS, nchunk, LANES), bf)
    o_spec = pl.BlockSpec((1, NSA_GROUPS, nchunk, LANES), lambda b: (b, 0, 0, 0))
    return pl.pallas_call(
        _compress_body,
        grid=(B,),
        in_specs=[x_spec, x_spec] + [full(a) for a in args[2:]],
        out_specs=[o_spec, o_spec],
        out_shape=[out, out],
        compiler_params=pltpu.CompilerParams(dimension_semantics=("parallel",), vmem_limit_bytes=VMEM_LIMIT),
        name="compress",
    )(*args)


def _softmax_step(s, v_top, v_bot, m_ref, acc_ref, first):
    blocks = [s[:, c * LANES:(c + 1) * LANES] for c in range(s.shape[1] // LANES)]
    bmax = blocks[0]
    for blk in blocks[1:]:
        bmax = jnp.maximum(bmax, blk)
    m_new = jnp.broadcast_to(jnp.max(bmax, axis=-1, keepdims=True), m_ref.shape)
    if not first:
        m_old = m_ref[...]
        m_new = jnp.maximum(m_old, m_new)
        alpha = jnp.exp2(m_old - m_new)
    pb = jnp.concatenate([jnp.exp2(blk - m_new) for blk in blocks], axis=1).astype(jnp.bfloat16)
    half = s.shape[0] // 2
    pv = jnp.concatenate([_nn(pb[0:half], v_top), _nn(pb[half:], v_bot)], axis=0)
    acc_ref[...] = pv if first else alpha * acc_ref[...] + pv
    m_ref[...] = m_new


def _diag_chunk(qq, kk, v_top, v_bot, bias_of, m_ref, acc_ref):
    nh = qq.shape[0] // TR
    q_lo = jnp.concatenate([qq[h * TR:h * TR + TQ] for h in range(nh)], axis=0)
    q_hi = jnp.concatenate([qq[h * TR + TQ:(h + 1) * TR] for h in range(nh)], axis=0)
    s_lo = _nt(q_lo, kk[0:TQ]) + jnp.concatenate([bias_of(h, 0) for h in range(nh)], axis=0)
    s_hi = _nt(q_hi, kk) + jnp.concatenate(
        [jnp.concatenate([bias_of(h, 1), bias_of(h, 0)], axis=1) for h in range(nh)], axis=0)
    for s, vt, vb, off in ((s_lo, v_top[0:TQ], v_bot[0:TQ], 0), (s_hi, v_top, v_bot, TQ)):
        blocks = [s[:, c * LANES:(c + 1) * LANES] for c in range(s.shape[1] // LANES)]
        bmax = blocks[0]
        for blk in blocks[1:]:
            bmax = jnp.maximum(bmax, blk)
        m = jnp.broadcast_to(jnp.max(bmax, axis=-1, keepdims=True), (nh * TQ, LANES))
        pb = jnp.concatenate([jnp.exp2(blk - m) for blk in blocks], axis=1).astype(jnp.bfloat16)
        half = nh * TQ // 2
        pv = jnp.concatenate([_nn(pb[0:half], vt), _nn(pb[half:], vb)], axis=0)
        for h in range(nh):
            m_ref[h * TR + off:h * TR + off + TQ] = m[h * TQ:(h + 1) * TQ]
            acc_ref[h * TR + off:h * TR + off + TQ] = pv[h * TQ:(h + 1) * TQ]


def _finish_pair(acc_even, acc_odd, lo_half, odd_flipped):
    swap = lambda x: pltpu.roll(x, HEAD_DIM, 1)
    if odd_flipped:
        return jnp.where(lo_half, acc_even, acc_odd) / swap(jnp.where(lo_half, acc_odd, acc_even))
    return jnp.where(lo_half, acc_even, swap(acc_odd)) / jnp.where(lo_half, swap(acc_even), acc_odd)


def _tile_kind(delta):
    return jnp.where(delta >= 0, jnp.minimum(delta, N_FAR), _K_MASKED)


def _moba_body(q_ref, ke_ref, ko_ref, ve_ref, vo_ref, kmean_ref, eye_ref, bias_ref, o_ref, m_ref, acc_ref):
    i = pl.program_id(2)
    nblk = kmean_ref.shape[1]
    lane = lax.broadcasted_iota(jnp.int32, (TR, LANES), 1)
    lo_half = lane < HEAD_DIM
    lane16 = lax.broadcasted_iota(jnp.int32, (nblk, LANES), 1)
    n_heads = 2 * MOBA_PAIRS
    jrow = lax.broadcasted_iota(jnp.int32, (nblk, n_heads * TR), 0)
    assert TR == 2 * TQ
    qblk = 2 * i + (lax.shift_right_logical(lax.broadcasted_iota(jnp.int32, (nblk, n_heads * TR), 1),
                                            TQ.bit_length() - 1) & 1)
    pad = jnp.zeros((HEAD_DIM - nblk, TR), jnp.float32)
    zero = jnp.zeros((TR, LANES), jnp.bfloat16)

    scores = []
    for pr in range(MOBA_PAIRS):
        q2 = q_ref[0, :, pr * LANES:(pr + 1) * LANES]
        kmean2 = kmean_ref[0, :, pr * LANES:(pr + 1) * LANES]
        for hh in range(2):
            inhead16 = (lane16 < HEAD_DIM) if hh == 0 else (lane16 >= HEAD_DIM)
            km_hi, km_lo = _split(jnp.where(inhead16, kmean2, 0.0))
            scores.append(_nt(km_hi, q2) + _nt(km_lo, q2))
    sc = jnp.where(jrow < qblk, jnp.concatenate(scores, axis=1), -jnp.inf)
    flag = jnp.where(jrow == qblk, 0.0, 1.0)
    for _ in range(MOBA_TOPK):
        best = jnp.max(sc, axis=0, keepdims=True)
        first = jnp.min(jnp.where(sc == best, jrow, nblk), axis=0, keepdims=True)
        hit = jrow == first
        flag = jnp.where(hit, 0.0, flag)
        sc = jnp.where(hit, -jnp.inf, sc)

    def augmented_q(pr):
        q2 = q_ref[0, :, pr * LANES:(pr + 1) * LANES]
        flags = [flag[:, (2 * pr + hh) * TR:(2 * pr + hh + 1) * TR] for hh in range(2)]
        placed = jnp.concatenate([flags[1], pad, flags[0], pad], axis=0).astype(jnp.bfloat16)
        maskpart = jnp.concatenate([_nt(eye_ref[...], placed[:, a * TQ:(a + 1) * TQ]) for a in range(TR // TQ)],
                                   axis=0) * NEG
        q2f = q2.astype(jnp.float32)
        return jnp.concatenate(
            [jnp.concatenate([jnp.where(lo_half, q2f, maskpart).astype(jnp.bfloat16), zero], axis=1),
             jnp.concatenate([zero, jnp.where(lo_half, maskpart, q2f).astype(jnp.bfloat16)], axis=1)], axis=0)

    qqs = [augmented_q(pr) for pr in range(MOBA_PAIRS)]

    def operands(c, pr):
        rows = pl.ds(pl.multiple_of(c * TK, TK), TK)
        cols = slice(pr * LANES, (pr + 1) * LANES)
        kk = jnp.concatenate([ke_ref[0, rows, cols], ko_ref[0, rows, cols]], axis=1)
        return kk, ve_ref[0, rows, cols], vo_ref[0, rows, cols]

    for pr in range(MOBA_PAIRS):
        kk, ve, vo = operands(i, pr)
        _diag_chunk(qqs[pr], kk, ve, vo, lambda hh, delta, pr=pr: bias_ref[2 * pr + hh, delta],
                    m_ref.at[pr], acc_ref.at[pr])

    def body(c, carry):
        d = 2 * (i - c)
        for pr in range(MOBA_PAIRS):
            kk, ve, vo = operands(c, pr)
            b = jnp.concatenate(
                [jnp.concatenate([bias_ref[2 * pr + hh, _tile_kind(d + a - t)] for t in range(2)], axis=1)
                 for hh in range(2) for a in range(2)], axis=0)
            _softmax_step(_nt(qqs[pr], kk) + b, ve, vo, m_ref.at[pr], acc_ref.at[pr], False)
        return carry

    lax.fori_loop(0, i, body, 0)
    for pr in range(MOBA_PAIRS):
        o_ref[0, :, pr * LANES:(pr + 1) * LANES] = _finish_pair(acc_ref[pr, 0:TR], acc_ref[pr, TR:2 * TR],
                                                                 lo_half, True)


def _eye(n):
    return jnp.asarray(np.eye(n, dtype=np.float32), jnp.bfloat16)


def _moba(qa, kae, kao, vae, vao, kmean, bias):
    B, S, _ = qa.shape
    nt = S // TQ
    width = MOBA_PAIRS * LANES
    kv_spec = pl.BlockSpec((1, S, width), lambda b, p, i: (b, 0, p), pipeline_mode=pl.Buffered(1))
    return pl.pallas_call(
        _moba_body,
        grid=(B, MOBA_HEADS // (2 * MOBA_PAIRS), S // TR),
        in_specs=[pl.BlockSpec((1, TR, width), lambda b, p, i: (b, i, p)),
                  kv_spec, kv_spec, kv_spec, kv_spec,
                  pl.BlockSpec((1, nt, width), lambda b, p, i: (b, 0, p)),
                  pl.BlockSpec((TQ, TQ), lambda b, p, i: (0, 0)),
                  pl.BlockSpec((2 * MOBA_PAIRS, _N_TILE_KINDS, TQ, TQ), lambda b, p, i: (p, 0, 0, 0),
                               pipeline_mode=pl.Buffered(1))],
        out_specs=pl.BlockSpec((1, TR, width), lambda b, p, i: (b, i, p)),
        out_shape=jax.ShapeDtypeStruct((B, S, MOBA_HEADS * HEAD_DIM), jnp.float32),
        scratch_shapes=[pltpu.VMEM((MOBA_PAIRS, 2 * TR, LANES), jnp.float32)] * 2,
        compiler_params=pltpu.CompilerParams(dimension_semantics=("parallel", "parallel", "arbitrary"),
                                             vmem_limit_bytes=VMEM_LIMIT),
        name="moba",
    )(qa, kae, kao, vae, vao, kmean, _eye(TQ), bias)


def _nsa_body(q_ref, kcmp_ref, vcmp_ref, kse_ref, kso_ref, vsl_ref, kwn_ref, vwn_ref, gate_ref, ov_ref, eye_ref,
              gexp_ref, bias_ref, o_ref, m_ref, acc_ref):
    i = pl.program_id(1)

    def to_hi(o):
        return pltpu.roll(o, HEAD_DIM, 1)

    ncmp = kcmp_ref.shape[2]
    nsel = ov_ref.shape[0]
    R = NSA_HPG * TR
    bf = jnp.bfloat16
    lane = lax.broadcasted_iota(jnp.int32, (TR, LANES), 1)
    lo_half = lane < HEAD_DIM
    sel_heads = (0, 2, 1, 3)

    def compressed(g):
        qf = [q_ref[0, :, (2 * g + s) * LANES:(2 * g + s + 1) * LANES].astype(jnp.float32) for s in range(2)]
        q4 = jnp.concatenate([jnp.where(lo_half if hh % 2 == 0 else ~lo_half, qf[hh // 2], 0.0).astype(bf)
                              for hh in range(NSA_HPG)], axis=0)

        tpos = i * TR + lax.broadcasted_iota(jnp.int32, (TR, ncmp), 0)
        cidx = lax.broadcasted_iota(jnp.int32, (TR, ncmp), 1)
        cbias1 = jnp.where((cidx * CMP_STRIDE + (CMP_LEN - 1) <= tpos) & (cidx < ncmp - 1), 0.0, NEG)
        lc = _nt(q4, kcmp_ref[0, g]) + jnp.concatenate([cbias1] * NSA_HPG, axis=0)
        cblocks = [lc[:, c * LANES:(c + 1) * LANES] for c in range(ncmp // LANES)]
        cmax = cblocks[0]
        for blk in cblocks[1:]:
            cmax = jnp.maximum(cmax, blk)
        cm = jnp.maximum(jnp.broadcast_to(jnp.max(cmax, axis=-1, keepdims=True), (R, LANES)), 0.5 * NEG)
        ce = [jnp.exp2(blk - cm) for blk in cblocks]
        cden = ce[0]
        for eb in ce[1:]:
            cden = cden + eb
        cden = jnp.broadcast_to(jnp.sum(cden, axis=-1, keepdims=True), (R, LANES))
        crcp = 1.0 / jnp.where(cden > 0.0, cden, 1.0)
        pc = jnp.concatenate([eb * crcp for eb in ce], axis=1)
        pcb = pc.astype(bf)
        o_cmp = jnp.concatenate([_nn(pcb[0:R // 2], vcmp_ref[0, g]), _nn(pcb[R // 2:], vcmp_ref[0, g])], axis=0)
        psum = pc[0:TR] + pc[TR:2 * TR] + pc[2 * TR:3 * TR] + pc[3 * TR:4 * TR]
        p_hi, p_lo = _split(psum)
        ov = ov_ref[...]
        imp = _nt(ov, p_hi) + _nt(ov, p_lo)
        return qf, q4, o_cmp, imp

    def window(g, q4):
        o_win = []
        for a in range(2):
            qt = 2 * i + a
            t0 = jnp.maximum(qt - (N_WIN_TILES - 1), 0)
            wstart = pl.multiple_of(t0 * TQ, TQ)
            qh = jnp.concatenate([q4[hh * TR + a * TQ:hh * TR + (a + 1) * TQ] for hh in range(NSA_HPG)], axis=0)
            kinds = []
            for r in range(N_WIN_TILES):
                dd = qt - (t0 + r)
                kinds.append(jnp.where(dd >= 0, _K_WIN + dd, _K_MASKED))
            bw = jnp.concatenate([jnp.concatenate([bias_ref[NSA_HPG * g + hh, k] for k in kinds], axis=1)
                                  for hh in range(NSA_HPG)], axis=0)
            sw = _nt(qh, kwn_ref[0, g, pl.ds(wstart, N_WIN_TILES * TQ), :]) + bw
            wm, wacc = m_ref.at[g, 0:NSA_HPG * TQ], acc_ref.at[g, 0:NSA_HPG * TQ]
            vw = vwn_ref[0, g, pl.ds(wstart, N_WIN_TILES * TQ), :]
            _softmax_step(sw, vw, vw, wm, wacc, True)
            wv = wacc[...]
            o_win.append(wv / to_hi(wv))
        return o_win

    def select(imps):
        width = NSA_GROUPS * TR
        jrow = lax.broadcasted_iota(jnp.int32, (nsel, width), 0)
        qblk = lax.shift_right_logical(i * TR + (lax.broadcasted_iota(jnp.int32, (nsel, width), 1) & (TR - 1)),
                                       SEL_BLOCK.bit_length() - 1)
        cand = (jrow >= 1) & (jrow < qblk)
        keys = jnp.where(cand, jnp.concatenate(imps, axis=1), -1.0)
        flag = jnp.where((jrow == 0) | (jrow == qblk), 0.0, 1.0)
        for _ in range(SEL_TOPN - 2):
            best = jnp.max(keys, axis=0, keepdims=True)
            first = jnp.min(jnp.where(keys == best, jrow, nsel), axis=0, keepdims=True)
            hit = jrow == first
            flag = jnp.where(hit, 0.0, flag)
            keys = jnp.where(hit, -1.0, keys)
        return flag

    def augmented_q(qf, flag):
        if nsel < HEAD_DIM:
            flag = jnp.concatenate([flag, jnp.zeros((HEAD_DIM - nsel, TR), jnp.float32)], axis=0)
        flag2 = jnp.concatenate([flag, flag], axis=0).astype(bf)
        maskpart = jnp.concatenate([_nt(eye_ref[...], flag2[:, a * TQ:(a + 1) * TQ]) for a in range(TR // TQ)],
                                   axis=0) * NEG

        zero = jnp.zeros((2 * TR, LANES), bf)
        qa2 = [jnp.concatenate([jnp.where(lo_half if par == 0 else ~lo_half, qf[s], maskpart).astype(bf)
                                for s in range(2)], axis=0) for par in range(2)]
        return jnp.concatenate([jnp.concatenate([qa2[0], zero], axis=1),
                                jnp.concatenate([zero, qa2[1]], axis=1)], axis=0)

    cmp_parts = [compressed(g) for g in range(NSA_GROUPS)]
    o_wins = [window(g, cmp_parts[g][1]) for g in range(NSA_GROUPS)]
    flags = select([part[3] for part in cmp_parts])
    state = [(augmented_q(cmp_parts[g][0], flags[:, g * TR:(g + 1) * TR]), cmp_parts[g][2], o_wins[g])
             for g in range(NSA_GROUPS)]

    def sel_operands(c, g):
        rows = pl.ds(pl.multiple_of(c * TK, TK), TK)
        kk = jnp.concatenate([kse_ref[0, g, rows, :], kso_ref[0, g, rows, :]], axis=1)
        return kk, vsl_ref[0, g, rows, :]

    for g in range(NSA_GROUPS):
        kk, v = sel_operands(i, g)
        _diag_chunk(state[g][0], kk, v, v, lambda hh, delta, g=g: bias_ref[NSA_HPG * g + sel_heads[hh], delta],
                    m_ref.at[g], acc_ref.at[g])

    def sel_body(c, carry):
        d = 2 * (i - c)
        for g in range(NSA_GROUPS):
            kk, v = sel_operands(c, g)
            b = jnp.concatenate(
                [jnp.concatenate([bias_ref[NSA_HPG * g + h, _tile_kind(d + a - t)] for t in range(2)], axis=1)
                 for h in sel_heads for a in range(2)], axis=0)
            _softmax_step(_nt(state[g][0], kk) + b, v, v, m_ref.at[g], acc_ref.at[g], False)
        return carry

    lax.fori_loop(0, i, sel_body, 0)

    gate = jax.nn.sigmoid(gate_ref[0])
    g_hi, g_lo = _split(gate)
    for g in range(NSA_GROUPS):
        _, o_cmp, o_win = state[g]
        gx =_nn(g_hi, gexp_ref[g]) + _nn(g_lo, gexp_ref[g])
        for s in range(2):
            h_even, h_odd = 2 * s, 2 * s + 1
            cmp2 = jnp.where(lo_half, o_cmp[h_even * TR:(h_even + 1) * TR], o_cmp[h_odd * TR:(h_odd + 1) * TR])
            sel2 = _finish_pair(acc_ref[g, s * TR:(s + 1) * TR], acc_ref[g, (2 + s) * TR:(3 + s) * TR],
                                lo_half, False)
            win2 = jnp.where(lo_half,
                             jnp.concatenate([o_win[a][h_even * TQ:(h_even + 1) * TQ] for a in range(2)], axis=0),
                             to_hi(jnp.concatenate([o_win[a][h_odd * TQ:(h_odd + 1) * TQ] for a in range(2)], axis=0)))
            base = s * 3 * LANES
            o_ref[0, :, (2 * g + s) * LANES:(2 * g + s + 1) * LANES] = (
                gx[:, base:base + LANES] * cmp2 + gx[:, base + LANES:base + 2 * LANES] * sel2
                + gx[:, base + 2 * LANES:base + 3 * LANES] * win2)


def _gate_expand():
    e = np.zeros((NSA_GROUPS, LANES, 2 * 3 * LANES), np.float32)
    for g in range(NSA_GROUPS):
        for s in range(2):
            for c in range(3):
                for half in range(2):
                    h = g * NSA_HPG + 2 * s + half
                    col = (s * 3 + c) * LANES + half * HEAD_DIM
                    e[g, 3 * h + c, col:col + HEAD_DIM] = 1.0
    return e


def _nsa(qb, kcmp, vcmp, kse, kso, vsl, kwn, vwn, gate, bias):
    B, S, _ = qb.shape
    nt = S // TQ
    ncmp = kcmp.shape[2]
    nsel = S // SEL_BLOCK
    bf = jnp.bfloat16
    cs = np.arange(ncmp) * CMP_STRIDE
    bs = np.arange(nsel) * SEL_BLOCK
    ov = ((cs[None, :] < bs[:, None] + SEL_BLOCK) & (cs[None, :] + CMP_LEN > bs[:, None])
          & (np.arange(ncmp)[None, :] < ncmp - 1)).astype(np.float32)
    once = pl.Buffered(1)
    grp = lambda n: pl.BlockSpec((1, NSA_GROUPS, n, LANES), lambda b, i: (b, 0, 0, 0), pipeline_mode=once)
    width = NSA_HEADS * HEAD_DIM
    return pl.pallas_call(
        _nsa_body,
        grid=(B, S // TR),
        in_specs=[pl.BlockSpec((1, TR, width), lambda b, i: (b, i, 0)),
                  grp(ncmp), grp(ncmp), grp(S), grp(S), grp(S), grp(S), grp(S),
                  pl.BlockSpec((1, TR, LANES), lambda b, i: (b, i, 0)),
                  pl.BlockSpec((nsel, ncmp), lambda b, i: (0, 0), pipeline_mode=once),
                  pl.BlockSpec((TQ, TQ), lambda b, i: (0, 0), pipeline_mode=once),
                  pl.BlockSpec((NSA_GROUPS, LANES, 6 * LANES), lambda b, i: (0, 0, 0), pipeline_mode=once),
                  pl.BlockSpec((NSA_HEADS, _N_TILE_KINDS, TQ, TQ), lambda b, i: (1, 0, 0, 0), pipeline_mode=once)],
        out_specs=pl.BlockSpec((1, TR, width), lambda b, i: (b, i, 0)),
        out_shape=jax.ShapeDtypeStruct((B, S, width), jnp.float32),
        scratch_shapes=[pltpu.VMEM((NSA_GROUPS, NSA_HPG * TR, LANES), jnp.float32)] * 2,
        compiler_params=pltpu.CompilerParams(dimension_semantics=("parallel", "arbitrary"),
                                             vmem_limit_bytes=VMEM_LIMIT),
        name="nsa",
    )(qb, kcmp, vcmp, kse, kso, vsl, kwn, vwn, gate, jnp.asarray(ov, bf), _eye(TQ),
      jnp.asarray(_gate_expand(), bf), bias)


def _out_proj_body(x_ref, nw_ref, oa_ref, ob_ref, wz_ref, wzb_ref, wgm_ref, wa_ref, wb_ref, wo_ref, out_ref):
    for c in range(PROJ_CHAINS):
        rows = pl.ds(c * TM, TM)
        x = x_ref[0, rows, :]
        ms = jnp.mean(x * x, axis=-1, keepdims=True)
        h = (x * lax.rsqrt(ms + EPS) * nw_ref[...]).astype(jnp.bfloat16)
        ya = (oa_ref[0, rows, :] * jax.nn.silu(_nn(h, wz_ref[:, _C_ZA:_C_ZA + 512]))).astype(jnp.bfloat16)
        yb = (ob_ref[0, rows, :] * jax.nn.silu(_nn(h, wzb_ref[...]))).astype(jnp.bfloat16)
        gm = jax.nn.sigmoid(_nn(h, wgm_ref[...]))
        merged = gm[:, 0:D_MODEL] * _nn(ya, wa_ref[...]) + gm[:, D_MODEL:] * _nn(yb, wb_ref[...])
        out_ref[0, rows, :] = x + _nn(merged.astype(jnp.bfloat16), wo_ref[...])


def _out_proj(x, norm_w, oa, ob, wz, wzb, wgm, wa, wb, wo):
    B, S, _ = x.shape
    nw = norm_w.reshape(1, D_MODEL)
    tok = lambda w: pl.BlockSpec((1, PROJ_CHAINS * TM, w), lambda b, i: (b, i, 0))
    full = lambda a: pl.BlockSpec(a.shape, lambda b, i: (0,) * a.ndim, pipeline_mode=pl.Buffered(1))
    return pl.pallas_call(
        _out_proj_body,
        grid=(B, S // (PROJ_CHAINS * TM)),
        in_specs=[tok(D_MODEL), full(nw), tok(512), tok(512), full(wz), full(wzb), full(wgm),
                  full(wa), full(wb), full(wo)],
        out_specs=tok(D_MODEL),
        out_shape=jax.ShapeDtypeStruct((B, S, D_MODEL), jnp.float32),
        compiler_params=pltpu.CompilerParams(dimension_semantics=("parallel", "parallel"),
                                             vmem_limit_bytes=VMEM_LIMIT),
        name="out_proj",
    )(x, nw, oa, ob, wz, wzb, wgm, wa, wb, wo)


def _pack_in_weights(w_in, q_norm_a, k_norm_a, q_norm_b, k_norm_sel, k_norm_win):
    bf = jnp.bfloat16
    ng = 3 * NSA_HEADS
    cols = lambda a, b: lax.slice_in_dim(w_in, a, b, axis=2).astype(bf).reshape(D_MODEL, b - a)
    z_b = _C_GATE + ng
    w = (cols(0, _C_GATE),
         jnp.pad(cols(_C_GATE, z_b), ((0, 0), (0, LANES - ng))),
         cols(z_b, z_b + NSA_HEADS * HEAD_DIM),
         cols(z_b + NSA_HEADS * HEAD_DIM, w_in.shape[2]))
    ones = lambda n: jnp.ones((n,), jnp.float32)
    gain = jnp.concatenate([jnp.tile(q_norm_a, MOBA_HEADS), jnp.tile(k_norm_a, MOBA_HEADS), ones(1024),
                            jnp.tile(q_norm_b, NSA_HEADS), ones(256), jnp.tile(k_norm_sel, NSA_GROUPS), ones(128),
                            jnp.tile(k_norm_win, NSA_GROUPS), ones(256)]).reshape(1, _N1)
    return w, gain


def kernel(x, norm_w, w_in, q_norm_a, k_norm_a, q_norm_b, k_norm_cmp, k_norm_sel, k_norm_win, cmp_pos_k, cmp_w1_k, cmp_w2_k, cmp_pos_v, cmp_w1_v, cmp_w2_v, rel_bias, w_branch_a, w_branch_b, w_out):
    bf = jnp.bfloat16
    (w_main, w_gate, w_zb, w_gm), gain = _pack_in_weights(w_in, q_norm_a[0], k_norm_a[0], q_norm_b[0],
                                                           k_norm_sel[0], k_norm_win[0])
    bias = _bias_tiles(rel_bias)
    qa, kae, kao, vae, vao, qb, kmean, kc, vc, kse, kso, vsl, kwn, vwn, gate = _in_proj(x, norm_w[0], w_main, w_gate,
                                                                                        gain)
    kcmp, vcmp = _compress(kc, vc, cmp_pos_k[0], cmp_w1_k[0], cmp_w2_k[0],
                           cmp_pos_v[0], cmp_w1_v[0], cmp_w2_v[0], k_norm_cmp[0])
    oa = _moba(qa, kae, kao, vae, vao, kmean[:, :, 0, :], bias)
    ob = _nsa(qb, kcmp, vcmp, kse, kso, vsl, kwn, vwn, gate, bias)
    return _out_proj(x, norm_w[0], oa, ob, w_main, w_zb, w_gm, w_branch_a[0].astype(bf), w_branch_b[0].astype(bf),
                     w_out[0].astype(bf))
```
